```python
import jax, jax.numpy as jnp
from jax import lax
import numpy as np

D_MODEL = 2048
BATCH = 4
SEQ = 2048
DEPTH = 1
DEC_BATCH = 128
DEC_SEQ = 4
PAST_LEN = 16384
PAGE_SIZE = 128

D_MIX = D_MODEL
D_CONV = D_MIX // 2
N_CONV_GROUPS = 8
CONV_A_WIDTH = 3
DN_HEADS = 8
DN_DK = 128
DN_DV = (D_MIX - D_CONV) // DN_HEADS
DN_QK = DN_HEADS * DN_DK
DN_V = DN_HEADS * DN_DV
DN_CONV_WIDTH = 4
DN_QKV = 2 * DN_QK + DN_V
CHUNK = 64
D_FF = 5632
D_PLE = 256
EPS = 1e-6
COL_SIZES = [D_CONV, D_CONV, D_CONV, DN_QK, DN_QK, DN_V, DN_V, DN_HEADS, DN_HEADS]
COL_OFFSETS = [int(o) for o in np.cumsum(COL_SIZES)[:-1]]
IN_COLS = int(sum(COL_SIZES))

kernel_name = "hymba_conv_gdn_macaron_step"


def rmsnorm(x, g):
    xf = x.astype(jnp.float32)
    y = xf * lax.rsqrt(jnp.mean(xf * xf, axis=-1, keepdims=True) + EPS)
    return (y * g.astype(jnp.float32)).astype(x.dtype)


def l2norm(x):
    return x * lax.rsqrt(jnp.sum(x * x, axis=-1, keepdims=True) + EPS)


def swiglu(x, wg, wu, wd):
    return (jax.nn.silu(x @ wg) * (x @ wu)) @ wd


def causal_dwconv(u, buf, w):
    T = u.shape[1]
    W = w.shape[0]
    full = jnp.concatenate([buf.astype(u.dtype), u], axis=1)
    y = full[:, 0:T] * w[0]
    for j in range(1, W):
        y = y + full[:, j:j + T] * w[j]
    return y, full[:, -(W - 1):]


def gated_delta_chunked(q, k, v, g, beta, s0):
    Bn, T, H, Dk = q.shape
    Dv = v.shape[-1]
    C = CHUNK if T >= CHUNK else T
    pad = (-T) % C
    if pad:
        pw = ((0, 0), (0, pad), (0, 0), (0, 0))
        q, k, v = jnp.pad(q, pw), jnp.pad(k, pw), jnp.pad(v, pw)
        g, beta = jnp.pad(g, pw[:3]), jnp.pad(beta, pw[:3])
    N = (T + pad) // C

    def chunks(a):
        a = a.reshape((Bn, N, C, H) + a.shape[3:])
        return jnp.moveaxis(a, (1, 3), (0, 2))

    qc, kc, vc = chunks(q), chunks(k), chunks(v)
    gc = jnp.cumsum(chunks(g), axis=-1)
    bc = chunks(beta)
    causal = jnp.tril(jnp.ones((C, C), dtype=bool))
    strict = jnp.tril(jnp.ones((C, C), dtype=bool), -1)
    decay = jnp.exp(jnp.where(causal, gc[..., :, None] - gc[..., None, :], -jnp.inf))
    kb = kc * bc[..., None]
    a_mat = jnp.where(strict, jnp.einsum('nbhid,nbhjd->nbhij', kb, kc) * decay, 0.0)
    eye = jnp.eye(C, dtype=jnp.float32)
    t_mat = lax.linalg.triangular_solve(a_mat + eye, jnp.broadcast_to(eye, a_mat.shape),
                                        left_side=True, lower=True, unit_diagonal=True)
    w = jnp.einsum('nbhij,nbhjd->nbhid', t_mat, kb * jnp.exp(gc)[..., None])
    u = jnp.einsum('nbhij,nbhjd->nbhid', t_mat, vc * bc[..., None])
    qk = jnp.where(causal, jnp.einsum('nbhid,nbhjd->nbhij', qc, kc) * decay, 0.0)

    def step(S, inp):
        q_i, k_i, u_i, w_i, qk_i, g_i = inp
        v_new = u_i - jnp.einsum('bhck,bhkv->bhcv', w_i, S)
        o_i = (jnp.einsum('bhck,bhkv->bhcv', q_i * jnp.exp(g_i)[..., None], S)
               + jnp.einsum('bhij,bhjv->bhiv', qk_i, v_new))
        g_last = g_i[..., -1:]
        S = (S * jnp.exp(g_last)[..., None]
             + jnp.einsum('bhck,bhcv->bhkv', k_i * jnp.exp(g_last - g_i)[..., None], v_new))
        return S, o_i

    s_fin, o = lax.scan(step, s0, (qc, kc, u, w, qk, gc))
    o = jnp.moveaxis(o, (0, 2), (1, 3)).reshape(Bn, N * C, H, Dv)[:, :T]
    return o, s_fin


def mixing(h, conv_a_buf, qkv_buf, s0, w_in, conv_a_w, conv_qkv_w, a_log, dt_bias, dn_norm, w_out):
    Bn, T, _ = h.shape
    proj = h @ w_in
    gb, gcv, hc, q, k, v, z, a, b = jnp.split(proj, COL_OFFSETS, axis=-1)
    cu, conv_a_new = causal_dwconv(gcv * hc, conv_a_buf, conv_a_w)
    y_a = gb * cu
    cqkv, qkv_new = causal_dwconv(jnp.concatenate([q, k, v], axis=-1), qkv_buf, conv_qkv_w)
    cqkv = jax.nn.silu(cqkv).astype(jnp.float32)
    q, k, v = jnp.split(cqkv, [DN_QK, 2 * DN_QK], axis=-1)
    q = l2norm(q.reshape(Bn, T, DN_HEADS, DN_DK)) * (DN_DK ** -0.5)
    k = l2norm(k.reshape(Bn, T, DN_HEADS, DN_DK))
    v = v.reshape(Bn, T, DN_HEADS, DN_DV)
    g = -jnp.exp(a_log.astype(jnp.float32)) * jax.nn.softplus(
        a.astype(jnp.float32) + dt_bias.astype(jnp.float32))
    beta = jax.nn.sigmoid(b.astype(jnp.float32))
    o, s_new = gated_delta_chunked(q, k, v, g, beta, s0.astype(jnp.float32))
    o = rmsnorm(o, dn_norm) * jax.nn.silu(z.astype(jnp.float32).reshape(Bn, T, DN_HEADS, DN_DV))
    y_b = o.reshape(Bn, T, DN_V).astype(h.dtype)
    out = jnp.concatenate([y_a, y_b], axis=-1) @ w_out
    return out, conv_a_new, qkv_new, s_new.astype(h.dtype)


def layer_forward(x, p, conv_a_buf, qkv_buf, s0, lw):
    (f1_pre, f1_post, f1_wg, f1_wu, f1_wd,
     mix_pre, mix_post, w_in, conv_a_w, conv_qkv_w, a_log, dt_bias, dn_norm, w_out,
     f2_pre, f2_post, f2_wg, f2_wu, f2_wd,
     ple_pre, ple_post, w_ple_gate, w_ple_proj) = lw
    x = x + 0.5 * rmsnorm(swiglu(rmsnorm(x, f1_pre), f1_wg, f1_wu, f1_wd), f1_post)
    m, ca, cq, s = mixing(rmsnorm(x, mix_pre), conv_a_buf, qkv_buf, s0, w_in, conv_a_w,
                          conv_qkv_w, a_log, dt_bias, dn_norm, w_out)
    x = x + rmsnorm(m, mix_post)
    x = x + 0.5 * rmsnorm(swiglu(rmsnorm(x, f2_pre), f2_wg, f2_wu, f2_wd), f2_post)
    gate = jax.nn.sigmoid(rmsnorm(x, ple_pre) @ w_ple_gate)
    x = x + rmsnorm(gate * (p.astype(x.dtype) @ w_ple_proj), ple_post)
    return x, ca, cq, s


def setup_inputs(seed: int = 0) -> dict:
    key = jax.random.key(seed)
    ks = iter(jax.random.split(key, 64))
    nrm = lambda shape, s=1.0: jax.random.normal(next(ks), shape, jnp.float32) * s
    gain = lambda n: 1.0 + 0.02 * jax.random.normal(next(ks), (DEPTH, n), jnp.float32)
    d = {}
    d["x_prompt"] = nrm((BATCH, SEQ, D_MODEL))
    d["x_sample"] = nrm((DEC_BATCH, DEC_SEQ, D_MODEL))
    d["state_conv_a"] = nrm((DEPTH, DEC_BATCH, CONV_A_WIDTH - 1, D_CONV))
    d["state_conv_qkv"] = nrm((DEPTH, DEC_BATCH, DN_CONV_WIDTH - 1, DN_QKV))
    d["state_delta"] = nrm((DEPTH, DEC_BATCH, DN_HEADS, DN_DK, DN_DV), 0.1)
    d["p_prompt"] = nrm((DEPTH, BATCH, SEQ, D_PLE))
    d["p_sample"] = nrm((DEPTH, DEC_BATCH, DEC_SEQ, D_PLE))
    for nm in ("f1",):
        d[nm + "_pre"] = gain(D_MODEL)
        d[nm + "_post"] = gain(D_MODEL)
        d[nm + "_wg"] = nrm((DEPTH, D_MODEL, D_FF), D_MODEL ** -0.5)
        d[nm + "_wu"] = nrm((DEPTH, D_MODEL, D_FF), D_MODEL ** -0.5)
        d[nm + "_wd"] = nrm((DEPTH, D_FF, D_MODEL), D_FF ** -0.5)
    d["mix_pre"] = gain(D_MODEL)
    d["mix_post"] = gain(D_MODEL)
    d["w_in"] = nrm((DEPTH, D_MODEL, IN_COLS), D_MODEL ** -0.5)
    d["conv_a_w"] = nrm((DEPTH, CONV_A_WIDTH, D_CONV), CONV_A_WIDTH ** -0.5)
    d["conv_qkv_w"] = nrm((DEPTH, DN_CONV_WIDTH, DN_QKV), DN_CONV_WIDTH ** -0.5)
    d["a_log"] = jnp.log(jax.random.uniform(next(ks), (DEPTH, DN_HEADS), jnp.float32, 1.0, 16.0))
    d["dt_bias"] = nrm((DEPTH, DN_HEADS), 0.1)
    d["dn_norm"] = gain(DN_DV)
    d["w_out"] = nrm((DEPTH, D_MIX, D_MODEL), D_MIX ** -0.5)
    for nm in ("f2",):
        d[nm + "_pre"] = gain(D_MODEL)
        d[nm + "_post"] = gain(D_MODEL)
        d[nm + "_wg"] = nrm((DEPTH, D_MODEL, D_FF), D_MODEL ** -0.5)
        d[nm + "_wu"] = nrm((DEPTH, D_MODEL, D_FF), D_MODEL ** -0.5)
        d[nm + "_wd"] = nrm((DEPTH, D_FF, D_MODEL), D_FF ** -0.5)
    d["ple_pre"] = gain(D_MODEL)
    d["ple_post"] = gain(D_MODEL)
    d["w_ple_gate"] = nrm((DEPTH, D_MODEL, D_MODEL), D_MODEL ** -0.5)
    d["w_ple_proj"] = nrm((DEPTH, D_PLE, D_MODEL), D_PLE ** -0.5)
    return d


def reference(x_prompt, x_sample, state_conv_a, state_conv_qkv, state_delta, p_prompt, p_sample,
              f1_pre, f1_post, f1_wg, f1_wu, f1_wd,
              mix_pre, mix_post, w_in, conv_a_w, conv_qkv_w, a_log, dt_bias, dn_norm, w_out,
              f2_pre, f2_post, f2_wg, f2_wu, f2_wd,
              ple_pre, ple_post, w_ple_gate, w_ple_proj):
    yp, ys = x_prompt, x_sample
    bp = x_prompt.shape[0]
    ca_p, cq_p, s_p, ca_s, cq_s, s_s = [], [], [], [], [], []
    for i in range(DEPTH):
        lw = (f1_pre[i], f1_post[i], f1_wg[i], f1_wu[i], f1_wd[i],
              mix_pre[i], mix_post[i], w_in[i], conv_a_w[i], conv_qkv_w[i], a_log[i], dt_bias[i],
              dn_norm[i], w_out[i],
              f2_pre[i], f2_post[i], f2_wg[i], f2_wu[i], f2_wd[i],
              ple_pre[i], ple_post[i], w_ple_gate[i], w_ple_proj[i])
        zero_a = jnp.zeros((bp, CONV_A_WIDTH - 1, D_CONV), yp.dtype)
        zero_q = jnp.zeros((bp, DN_CONV_WIDTH - 1, DN_QKV), yp.dtype)
        zero_s = jnp.zeros((bp, DN_HEADS, DN_DK, DN_DV), jnp.float32)
        yp, a1, q1, s1 = layer_forward(yp, p_prompt[i], zero_a, zero_q, zero_s, lw)
        ys, a2, q2, s2 = layer_forward(ys, p_sample[i], state_conv_a[i], state_conv_qkv[i],
                                       state_delta[i], lw)
        ca_p.append(a1); cq_p.append(q1); s_p.append(s1)
        ca_s.append(a2); cq_s.append(q2); s_s.append(s2)
    return (yp, ys, jnp.stack(ca_p), jnp.stack(cq_p), jnp.stack(s_p),
            jnp.stack(ca_s), jnp.stack(cq_s), jnp.stack(s_s))
```

```python
import functools

import jax
import jax.numpy as jnp
from jax import lax
from jax.experimental import pallas as pl
from jax.experimental.pallas import tpu as pltpu

F32 = jnp.float32
BF16 = jnp.bfloat16

D_MODEL = 2048
D_CONV = 1024
DN_HEADS = 8
DN_DK = 128
DN_DV = 128
DN_QKV = 3072
D_FF = 5632
D_PLE = 256
EPS = 1e-6
CHUNK = 64
SAMPLE_LEN = 4
SEQ_PAD = 8
SEQ_SHIFT = 3
IN_MAIN = 7168
IN_PAD = 7296
AB_BLOCK = IN_MAIN // 128
Z_BLOCK = 6

TM = 512
TF = 512
TN_IN = IN_PAD // 3
VMEM_LIMIT = 56 * 1024 * 1024


def _rms(x, g):
    ms = jnp.mean(x * x, axis=-1, keepdims=True)
    return x * lax.rsqrt(ms + EPS) * g


def _silu(x):
    return x * jax.nn.sigmoid(x)


def _mm(a, b):
    return jnp.dot(a.astype(BF16), b.astype(BF16), preferred_element_type=F32)


def _mm_nt(a, b):
    return lax.dot_general(a.astype(BF16), b.astype(BF16), (((1,), (1,)), ((), ())),
                           preferred_element_type=F32)


def _mm_tn(a, b):
    return jnp.dot(a.T.astype(BF16), b.astype(BF16), preferred_element_type=F32)


def _mm_exact(a, b):
    return jnp.dot(a, b, precision=lax.Precision.HIGHEST, preferred_element_type=F32)


def _ffn_kernel(x_ref, pre_ref, post_ref, wg_ref, wu_ref, wd_ref, o_ref, h_ref, acc_ref):
    j = pl.program_id(1)

    @pl.when(j == 0)
    def _():
        h_ref[...] = _rms(x_ref[...], pre_ref[...]).astype(BF16)
        acc_ref[...] = jnp.zeros_like(acc_ref)

    h = h_ref[...]
    g = jnp.dot(h, wg_ref[...], preferred_element_type=F32)
    u = jnp.dot(h, wu_ref[...], preferred_element_type=F32)
    a = (_silu(g) * u).astype(BF16)
    acc_ref[...] += jnp.dot(a, wd_ref[...], preferred_element_type=F32)

    @pl.when(j == pl.num_programs(1) - 1)
    def _():
        o_ref[...] = x_ref[...] + 0.5 * _rms(acc_ref[...], post_ref[...])


def _ffn(x, pre, post, wg, wu, wd):
    m = x.shape[0]
    return pl.pallas_call(
        _ffn_kernel,
        grid=(m // TM, D_FF // TF),
        in_specs=[
            pl.BlockSpec((TM, D_MODEL), lambda i, j: (i, 0)),
            pl.BlockSpec((1, D_MODEL), lambda i, j: (0, 0)),
            pl.BlockSpec((1, D_MODEL), lambda i, j: (0, 0)),
            pl.BlockSpec((D_MODEL, TF), lambda i, j: (0, j)),
            pl.BlockSpec((D_MODEL, TF), lambda i, j: (0, j)),
            pl.BlockSpec((TF, D_MODEL), lambda i, j: (j, 0)),
        ],
        out_specs=pl.BlockSpec((TM, D_MODEL), lambda i, j: (i, 0)),
        out_shape=jax.ShapeDtypeStruct((m, D_MODEL), F32),
        scratch_shapes=[pltpu.VMEM((TM, D_MODEL), BF16), pltpu.VMEM((TM, D_MODEL), F32)],
        compiler_params=pltpu.CompilerParams(
            dimension_semantics=("parallel", "arbitrary"), vmem_limit_bytes=VMEM_LIMIT),
        name="ffn",
    )(x, pre, post, wg, wu, wd)


def _inproj_kernel(x_ref, pre_ref, w_ref, o_ref, h_ref):
    @pl.when(pl.program_id(1) == 0)
    def _():
        h_ref[...] = _rms(x_ref[...], pre_ref[...]).astype(BF16)

    o_ref[...] = jnp.dot(h_ref[...], w_ref[...], preferred_element_type=F32)


def _inproj(x, pre, w):
    m = x.shape[0]
    return pl.pallas_call(
        _inproj_kernel,
        grid=(m // TM, IN_PAD // TN_IN),
        in_specs=[
            pl.BlockSpec((TM, D_MODEL), lambda i, j: (i, 0)),
            pl.BlockSpec((1, D_MODEL), lambda i, j: (0, 0)),
            pl.BlockSpec((D_MODEL, TN_IN), lambda i, j: (0, j)),
        ],
        out_specs=pl.BlockSpec((TM, TN_IN), lambda i, j: (i, j)),
        out_shape=jax.ShapeDtypeStruct((m, IN_PAD), F32),
        scratch_shapes=[pltpu.VMEM((TM, D_MODEL), BF16)],
        compiler_params=pltpu.CompilerParams(
            dimension_semantics=("parallel", "arbitrary"), vmem_limit_bytes=VMEM_LIMIT),
        name="inproj",
    )(x, pre, w)


def _conv_taps(ext_ref, w, width):
    acc = None
    for j in range(width):
        lo = 8 - (width - 1) + j
        term = ext_ref[lo:lo + CHUNK, :] * w[j:j + 1, :]
        acc = term if acc is None else acc + term
    return acc


def _unit_lower_inverse(a, eye, squarings):
    inv = eye - a
    power = a
    for _ in range(squarings):
        power = _mm(power, power)
        inv = _mm(inv, eye + power)
    return inv


def _head_intra(qh, kh, vh, g_col, beta_col, gc_col, causal, strict, low_f, strict_f, eye, squarings):
    qn = qh * lax.rsqrt(jnp.sum(qh * qh, axis=-1, keepdims=True) + EPS) * (DN_DK ** -0.5)
    kn = kh * lax.rsqrt(jnp.sum(kh * kh, axis=-1, keepdims=True) + EPS)
    gdiff = _mm_exact(low_f, g_col * strict_f)
    decay = jnp.where(causal, jnp.exp(gdiff), 0.0)
    kb = kn * beta_col
    gram = _mm_nt(jnp.concatenate([kb, qn], axis=0), kn)
    a_mat = jnp.where(strict, gram[:CHUNK] * decay, 0.0)
    qk = jnp.where(causal, gram[CHUNK:] * decay, 0.0)
    t_mat = _unit_lower_inverse(a_mat, eye, squarings)
    egc = jnp.exp(gc_col)
    wu = _mm(t_mat, jnp.concatenate([kb * egc, vh * beta_col], axis=1))
    return kn, wu[:, :DN_DK], wu[:, DN_DK:], qn * egc, qk


def _gates(ab, alog, dtb):
    x = ab + dtb
    softplus = jnp.maximum(x, 0.0) + jnp.log1p(jnp.exp(-jnp.abs(x)))
    return -jnp.exp(alog) * softplus, jax.nn.sigmoid(ab)


def _masks(same_seq):
    row = lax.broadcasted_iota(jnp.int32, (CHUNK, CHUNK), 0)
    col = lax.broadcasted_iota(jnp.int32, (CHUNK, CHUNK), 1)
    causal = row >= col
    strict = row > col
    if same_seq:
        same = (row >> SEQ_SHIFT) == (col >> SEQ_SHIFT)
        causal = causal & same
        strict = strict & same
    eye = jnp.where(row == col, 1.0, 0.0).astype(F32)
    return row, col, causal, strict, causal.astype(F32), strict.astype(F32), eye


def _mix_prompt_kernel(pa_ref, pq_ref, pz_ref, pab_ref, caw_ref, cqw_ref, alog_ref, dtb_ref, dnn_ref,
                       y_ref, ca_ref, cq_ref, so_ref, exta_ref, extq_ref, s_ref):
    c = pl.program_id(1)

    @pl.when(c == 0)
    def _():
        exta_ref[0:8, :] = jnp.zeros((8, D_CONV), F32)
        extq_ref[0:8, :] = jnp.zeros((8, DN_QKV), F32)
        s_ref[...] = jnp.zeros_like(s_ref)

    pa = pa_ref[...]
    exta_ref[8:8 + CHUNK, :] = pa[:, D_CONV:2 * D_CONV] * pa[:, 2 * D_CONV:]
    y_ref[:, :D_CONV] = (pa[:, :D_CONV] * _conv_taps(exta_ref, caw_ref[...], 3)).astype(y_ref.dtype)
    extq_ref[8:8 + CHUNK, :] = pq_ref[...]
    cqkv = _silu(_conv_taps(extq_ref, cqw_ref[...], 4))

    _, _, causal, strict, low_f, strict_f, eye = _masks(False)
    g_all, beta_all = _gates(pab_ref[...], alog_ref[...], dtb_ref[...])
    gc_all = _mm_exact(low_f, g_all)
    z = pz_ref[...]
    dnn = dnn_ref[...]
    for h in range(DN_HEADS):
        lo = h * DN_DK
        g_col = g_all[:, h:h + 1]
        gc_col = gc_all[:, h:h + 1]
        beta_col = beta_all[:, DN_HEADS + h:DN_HEADS + h + 1]
        kn, w, u, qdec, qk = _head_intra(
            cqkv[:, lo:lo + DN_DK], cqkv[:, 1024 + lo:1024 + lo + DN_DK],
            cqkv[:, 2048 + lo:2048 + lo + DN_DV], g_col, beta_col, gc_col,
            causal, strict, low_f, strict_f, eye, 5)
        s_old = s_ref[h]
        r = _mm(jnp.concatenate([w, qdec], axis=0), s_old)
        v_new = u - r[:CHUNK]
        o = r[CHUNK:] + _mm(qk, v_new)
        g_last = gc_col[CHUNK - 1:CHUNK, :]
        kdec = kn * jnp.exp(g_last - gc_col)
        s_ref[h] = s_old * jnp.exp(g_last) + _mm_tn(kdec, v_new)
        yb = _rms(o, dnn) * _silu(z[:, lo:lo + DN_DV])
        y_ref[:, D_CONV + lo:D_CONV + lo + DN_DV] = yb.astype(y_ref.dtype)

    exta_ref[0:8, :] = exta_ref[CHUNK:CHUNK + 8, :]
    extq_ref[0:8, :] = extq_ref[CHUNK:CHUNK + 8, :]

    @pl.when(c == pl.num_programs(1) - 1)
    def _():
        ca_ref[0] = exta_ref[CHUNK + 6:CHUNK + 8, :]
        cq_ref[0] = extq_ref[CHUNK + 5:CHUNK + 8, :]
        so_ref[0] = s_ref[...]


def _mix_prompt(p, n_seq, seq_len, caw, cqw, alog, dtb, dnn):
    n_chunks = seq_len // CHUNK
    row = lambda b, c: b * n_chunks + c
    full2 = lambda b, c: (0, 0)
    return pl.pallas_call(
        _mix_prompt_kernel,
        grid=(n_seq, n_chunks),
        in_specs=[
            pl.BlockSpec((CHUNK, 3 * D_CONV), lambda b, c: (row(b, c), 0)),
            pl.BlockSpec((CHUNK, DN_QKV), lambda b, c: (row(b, c), 1)),
            pl.BlockSpec((CHUNK, 1024), lambda b, c: (row(b, c), Z_BLOCK)),
            pl.BlockSpec((CHUNK, 128), lambda b, c: (row(b, c), AB_BLOCK)),
            pl.BlockSpec((3, D_CONV), full2),
            pl.BlockSpec((4, DN_QKV), full2),
            pl.BlockSpec((1, 128), full2),
            pl.BlockSpec((1, 128), full2),
            pl.BlockSpec((1, DN_DV), full2),
        ],
        out_specs=[
            pl.BlockSpec((CHUNK, D_MODEL), lambda b, c: (row(b, c), 0)),
            pl.BlockSpec((1, 2, D_CONV), lambda b, c: (b, 0, 0)),
            pl.BlockSpec((1, 3, DN_QKV), lambda b, c: (b, 0, 0)),
            pl.BlockSpec((1, DN_HEADS, DN_DK, DN_DV), lambda b, c: (b, 0, 0, 0)),
        ],
        out_shape=[
            jax.ShapeDtypeStruct((n_seq * seq_len, D_MODEL), BF16),
            jax.ShapeDtypeStruct((n_seq, 2, D_CONV), F32),
            jax.ShapeDtypeStruct((n_seq, 3, DN_QKV), F32),
            jax.ShapeDtypeStruct((n_seq, DN_HEADS, DN_DK, DN_DV), F32),
        ],
        scratch_shapes=[
            pltpu.VMEM((CHUNK + 8, D_CONV), F32),
            pltpu.VMEM((CHUNK + 8, DN_QKV), F32),
            pltpu.VMEM((DN_HEADS, DN_DK, DN_DV), F32),
        ],
        compiler_params=pltpu.CompilerParams(
            dimension_semantics=("arbitrary", "arbitrary"), vmem_limit_bytes=VMEM_LIMIT),
        name="mix_prompt",
    )(p, p, p, p, caw, cqw, alog, dtb, dnn)


SEQ_PER_STEP = CHUNK // SEQ_PAD


def _mix_sample_kernel(pa_ref, pq_ref, pz_ref, pab_ref, sa_ref, sq_ref, si_ref,
                       caw_ref, cqw_ref, alog_ref, dtb_ref, dnn_ref,
                       y_ref, ca_ref, cq_ref, so_ref, exta_ref, extq_ref):
    pa = pa_ref[...]
    exta_ref[8:8 + CHUNK, :] = pa[:, D_CONV:2 * D_CONV] * pa[:, 2 * D_CONV:]
    extq_ref[8:8 + CHUNK, :] = pq_ref[...]
    for b in range(SEQ_PER_STEP):
        exta_ref[SEQ_PAD * b + 6:SEQ_PAD * b + 8, :] = sa_ref[b]
        extq_ref[SEQ_PAD * b + 5:SEQ_PAD * b + 8, :] = sq_ref[b]
    y_ref[:, :D_CONV] = (pa[:, :D_CONV] * _conv_taps(exta_ref, caw_ref[...], 3)).astype(y_ref.dtype)
    row, _, causal, strict, low_f, strict_f, eye = _masks(True)
    valid = ((row[:, 0:1] & (SEQ_PAD - 1)) < SAMPLE_LEN).astype(F32)
    cqkv = _silu(_conv_taps(extq_ref, cqw_ref[...], 4)) * valid
    for b in range(SEQ_PER_STEP):
        ca_ref[b] = exta_ref[SEQ_PAD * b + 10:SEQ_PAD * b + 12, :]
        cq_ref[b] = extq_ref[SEQ_PAD * b + 9:SEQ_PAD * b + 12, :]

    g_all, beta_all = _gates(pab_ref[...], alog_ref[...], dtb_ref[...])
    g_all = g_all * valid
    beta_all = beta_all * valid
    gc_all = _mm_exact(low_f, g_all)
    col = lax.broadcasted_iota(jnp.int32, (CHUNK, CHUNK), 1)
    last_sel = (col == (row | (SEQ_PAD - 1))).astype(F32)
    gl_all = _mm_exact(last_sel, gc_all)
    z = pz_ref[...]
    dnn = dnn_ref[...]
    for h in range(DN_HEADS):
        lo = h * DN_DK
        g_col = g_all[:, h:h + 1]
        gc_col = gc_all[:, h:h + 1]
        gl_col = gl_all[:, h:h + 1]
        beta_col = beta_all[:, DN_HEADS + h:DN_HEADS + h + 1]
        kn, w, u, qdec, qk = _head_intra(
            cqkv[:, lo:lo + DN_DK], cqkv[:, 1024 + lo:1024 + lo + DN_DK],
            cqkv[:, 2048 + lo:2048 + lo + DN_DV], g_col, beta_col, gc_col,
            causal, strict, low_f, strict_f, eye, 1)
        kdec = kn * jnp.exp(gl_col - gc_col)
        s_scale = jnp.exp(gl_col)
        v_parts, o_parts = [], []
        for b in range(SEQ_PER_STEP):
            r0 = SEQ_PAD * b
            s_old = si_ref[b, h]
            r = _mm(jnp.concatenate([w[r0:r0 + SEQ_PAD], qdec[r0:r0 + SEQ_PAD]], axis=0), s_old)
            v_new = u[r0:r0 + SEQ_PAD] - r[:SEQ_PAD]
            so_ref[b, h] = s_old * s_scale[r0:r0 + 1, :] + _mm_tn(kdec[r0:r0 + SEQ_PAD], v_new)
            v_parts.append(v_new)
            o_parts.append(r[SEQ_PAD:])
        o = jnp.concatenate(o_parts, axis=0) + _mm(qk, jnp.concatenate(v_parts, axis=0))
        yb = _rms(o, dnn) * _silu(z[:, lo:lo + DN_DV])
        y_ref[:, D_CONV + lo:D_CONV + lo + DN_DV] = yb.astype(y_ref.dtype)


def _mix_sample(p8, state_a, state_q, state_s, caw, cqw, alog, dtb, dnn):
    n_seq = state_s.shape[0]
    full2 = lambda i: (0, 0)
    return pl.pallas_call(
        _mix_sample_kernel,
        grid=(n_seq // SEQ_PER_STEP,),
        in_specs=[
            pl.BlockSpec((CHUNK, 3 * D_CONV), lambda i: (i, 0)),
            pl.BlockSpec((CHUNK, DN_QKV), lambda i: (i, 1)),
            pl.BlockSpec((CHUNK, 1024), lambda i: (i, Z_BLOCK)),
            pl.BlockSpec((CHUNK, 128), lambda i: (i, AB_BLOCK)),
            pl.BlockSpec((SEQ_PER_STEP, 2, D_CONV), lambda i: (i, 0, 0)),
            pl.BlockSpec((SEQ_PER_STEP, 3, DN_QKV), lambda i: (i, 0, 0)),
            pl.BlockSpec((SEQ_PER_STEP, DN_HEADS, DN_DK, DN_DV), lambda i: (i, 0, 0, 0)),
            pl.BlockSpec((3, D_CONV), full2),
            pl.BlockSpec((4, DN_QKV), full2),
            pl.BlockSpec((1, 128), full2),
            pl.BlockSpec((1, 128), full2),
            pl.BlockSpec((1, DN_DV), full2),
        ],
        out_specs=[
            pl.BlockSpec((CHUNK, D_MODEL), lambda i: (i, 0)),
            pl.BlockSpec((SEQ_PER_STEP, 2, D_CONV), lambda i: (i, 0, 0)),
            pl.BlockSpec((SEQ_PER_STEP, 3, DN_QKV), lambda i: (i, 0, 0)),
            pl.BlockSpec((SEQ_PER_STEP, DN_HEADS, DN_DK, DN_DV), lambda i: (i, 0, 0, 0)),
        ],
        out_shape=[
            jax.ShapeDtypeStruct((n_seq * SEQ_PAD, D_MODEL), BF16),
            jax.ShapeDtypeStruct((n_seq, 2, D_CONV), F32),
            jax.ShapeDtypeStruct((n_seq, 3, DN_QKV), F32),
            jax.ShapeDtypeStruct((n_seq, DN_HEADS, DN_DK, DN_DV), F32),
        ],
        scratch_shapes=[
            pltpu.VMEM((CHUNK + 8, D_CONV), F32),
            pltpu.VMEM((CHUNK + 8, DN_QKV), F32),
        ],
        compiler_params=pltpu.CompilerParams(
            dimension_semantics=("parallel",), vmem_limit_bytes=VMEM_LIMIT),
        name="mix_sample",
    )(p8, p8, p8, p8, state_a, state_q, state_s, caw, cqw, alog, dtb, dnn)


def _outproj_kernel(y_ref, w_ref, post_ref, x_ref, o_ref):
    m = jnp.dot(y_ref[...], w_ref[...], preferred_element_type=F32)
    o_ref[...] = x_ref[...] + _rms(m, post_ref[...])


def _outproj(y, w, post, x):
    m = x.shape[0]
    return pl.pallas_call(
        _outproj_kernel,
        grid=(m // TM,),
        in_specs=[
            pl.BlockSpec((TM, D_MODEL), lambda i: (i, 0)),
            pl.BlockSpec((D_MODEL, D_MODEL), lambda i: (0, 0)),
            pl.BlockSpec((1, D_MODEL), lambda i: (0, 0)),
            pl.BlockSpec((TM, D_MODEL), lambda i: (i, 0)),
        ],
        out_specs=pl.BlockSpec((TM, D_MODEL), lambda i: (i, 0)),
        out_shape=jax.ShapeDtypeStruct((m, D_MODEL), F32),
        compiler_params=pltpu.CompilerParams(
            dimension_semantics=("parallel",), vmem_limit_bytes=VMEM_LIMIT),
        name="outproj",
    )(y, w, post, x)


def _ple_kernel(x_ref, p_ref, pre_ref, post_ref, wg_ref, wp_ref, o_ref):
    x = x_ref[...]
    h = _rms(x, pre_ref[...]).astype(BF16)
    gate = jax.nn.sigmoid(jnp.dot(h, wg_ref[...], preferred_element_type=F32))
    proj = jnp.dot(p_ref[...].astype(BF16), wp_ref[...], preferred_element_type=F32)
    o_ref[...] = x + _rms(gate * proj, post_ref[...])


def _ple(x, p, pre, post, wg, wp):
    m = x.shape[0]
    return pl.pallas_call(
        _ple_kernel,
        grid=(m // TM,),
        in_specs=[
            pl.BlockSpec((TM, D_MODEL), lambda i: (i, 0)),
            pl.BlockSpec((TM, D_PLE), lambda i: (i, 0)),
            pl.BlockSpec((1, D_MODEL), lambda i: (0, 0)),
            pl.BlockSpec((1, D_MODEL), lambda i: (0, 0)),
            pl.BlockSpec((D_MODEL, D_MODEL), lambda i: (0, 0)),
            pl.BlockSpec((D_PLE, D_MODEL), lambda i: (0, 0)),
        ],
        out_specs=pl.BlockSpec((TM, D_MODEL), lambda i: (i, 0)),
        out_shape=jax.ShapeDtypeStruct((m, D_MODEL), F32),
        compiler_params=pltpu.CompilerParams(
            dimension_semantics=("parallel",), vmem_limit_bytes=VMEM_LIMIT),
        name="ple",
    )(x, p, pre, post, wg, wp)


def _row(v):
    return v.reshape(1, -1).astype(F32)


def _pad_lanes(v):
    return jnp.pad(_row(v), ((0, 0), (0, 128 - v.shape[-1])))


def kernel(x_prompt, x_sample, state_conv_a, state_conv_qkv, state_delta, p_prompt, p_sample,
           f1_pre, f1_post, f1_wg, f1_wu, f1_wd,
           mix_pre, mix_post, w_in, conv_a_w, conv_qkv_w, a_log, dt_bias, dn_norm, w_out,
           f2_pre, f2_post, f2_wg, f2_wu, f2_wd,
           ple_pre, ple_post, w_ple_gate, w_ple_proj):
    depth = f1_pre.shape[0]
    n_p, seq_p, _ = x_prompt.shape
    n_s, seq_s, _ = x_sample.shape
    rows_p = n_p * seq_p
    rows_s = n_s * seq_s
    assert seq_p % CHUNK == 0 and seq_s == SAMPLE_LEN and n_s % SEQ_PER_STEP == 0
    assert (rows_p + rows_s) % TM == 0
    x = jnp.concatenate([x_prompt.reshape(rows_p, D_MODEL), x_sample.reshape(rows_s, D_MODEL)], axis=0)
    outs = [[] for _ in range(6)]
    for i in range(depth):
        x = _ffn(x, _row(f1_pre[i]), _row(f1_post[i]),
                 f1_wg[i].astype(BF16), f1_wu[i].astype(BF16), f1_wd[i].astype(BF16))
        w_in_pad = jnp.pad(w_in[i], ((0, 0), (0, IN_PAD - w_in.shape[-1]))).astype(BF16)
        proj = _inproj(x, _row(mix_pre[i]), w_in_pad)
        alog = _pad_lanes(a_log[i])
        dtb = _pad_lanes(dt_bias[i])
        dnn = _row(dn_norm[i])
        y_p, ca_p, cq_p, s_p = _mix_prompt(proj, n_p, seq_p, conv_a_w[i], conv_qkv_w[i], alog, dtb, dnn)
        proj_s = jnp.pad(proj[rows_p:].reshape(n_s, seq_s, IN_PAD), ((0, 0), (0, SEQ_PAD - seq_s), (0, 0)))
        y_s, ca_s, cq_s, s_s = _mix_sample(
            proj_s.reshape(n_s * SEQ_PAD, IN_PAD), state_conv_a[i], state_conv_qkv[i], state_delta[i],
            conv_a_w[i], conv_qkv_w[i], alog, dtb, dnn)
        y = jnp.concatenate(
            [y_p, y_s.reshape(n_s, SEQ_PAD, D_MODEL)[:, :seq_s].reshape(rows_s, D_MODEL)], axis=0)
        x = _outproj(y, w_out[i].astype(BF16), _row(mix_post[i]), x)
        x = _ffn(x, _row(f2_pre[i]), _row(f2_post[i]),
                 f2_wg[i].astype(BF16), f2_wu[i].astype(BF16), f2_wd[i].astype(BF16))
        p = jnp.concatenate([p_prompt[i].reshape(rows_p, D_PLE), p_sample[i].reshape(rows_s, D_PLE)], axis=0)
        x = _ple(x, p, _row(ple_pre[i]), _row(ple_post[i]),
                 w_ple_gate[i].astype(BF16), w_ple_proj[i].astype(BF16))
        for lst, val in zip(outs, (ca_p, cq_p, s_p, ca_s, cq_s, s_s)):
            lst.append(val)
    return (x[:rows_p].reshape(n_p, seq_p, D_MODEL), x[rows_p:].reshape(n_s, seq_s, D_MODEL),
            *[jnp.stack(lst) for lst in outs])
```

```python
import functools

import jax
import jax.numpy as jnp
from jax import lax
from jax.experimental import pallas as pl
from jax.experimental.pallas import tpu as pltpu

F32 = jnp.float32
BF16 = jnp.bfloat16

D_MODEL = 2048
D_CONV = 1024
DN_HEADS = 8
DN_DK = 128
DN_DV = 128
DN_QKV = 3072
D_FF = 5632
D_PLE = 256
EPS = 1e-6
CHUNK = 64
SAMPLE_LEN = 4
SEQ_PAD = 8
SEQ_SHIFT = 3
SEQ_PER_STEP = CHUNK // SEQ_PAD
IN_MAIN = 7168
Z_BLOCK = 6
AB_LANES = 128

TM = 512
TF = 512
TF_CAST = 256
TN_IN = 1024
VMEM_LIMIT = 56 * 1024 * 1024


def _rms(x, g):
    ms = jnp.mean(x * x, axis=-1, keepdims=True)
    return x * lax.rsqrt(ms + EPS) * g


def _silu(x):
    return x * jax.nn.sigmoid(x)


def _mm(a, b):
    return jnp.dot(a.astype(BF16), b.astype(BF16), preferred_element_type=F32)


def _mm_nt(a, b):
    return lax.dot_general(a.astype(BF16), b.astype(BF16), (((1,), (1,)), ((), ())),
                           preferred_element_type=F32)


def _mm_tn(a, b):
    return jnp.dot(a.T.astype(BF16), b.astype(BF16), preferred_element_type=F32)


def _mm_exact(a, b):
    return jnp.dot(a, b, precision=lax.Precision.HIGHEST, preferred_element_type=F32)


def _params(*semantics):
    return pltpu.CompilerParams(dimension_semantics=semantics, vmem_limit_bytes=VMEM_LIMIT)


def _ffn_kernel(emit_bf16, x_ref, pre_ref, post_ref, wg_ref, wu_ref, wd_ref, o_ref, *rest):
    h_ref, acc_ref = rest[-2:]
    j = pl.program_id(1)

    @pl.when(j == 0)
    def _():
        h_ref[...] = _rms(x_ref[...], pre_ref[...]).astype(BF16)
        acc_ref[...] = jnp.zeros_like(acc_ref)

    wg, wu, wd = wg_ref[...], wu_ref[...], wd_ref[...]
    if emit_bf16:
        wg, wu, wd = wg.astype(BF16), wu.astype(BF16), wd.astype(BF16)
        rest[0][...] = wg
        rest[1][...] = wu
        rest[2][...] = wd
    h = h_ref[...]
    g = jnp.dot(h, wg, preferred_element_type=F32)
    u = jnp.dot(h, wu, preferred_element_type=F32)
    a = (_silu(g) * u).astype(BF16)
    acc_ref[...] += jnp.dot(a, wd, preferred_element_type=F32)

    @pl.when(j == pl.num_programs(1) - 1)
    def _():
        o_ref[...] = x_ref[...] + 0.5 * _rms(acc_ref[...], post_ref[...])


def _ffn(x, pre, post, wg, wu, wd):
    m = x.shape[0]
    emit_bf16 = wg.dtype != BF16
    tf = TF_CAST if emit_bf16 else TF
    up_spec = pl.BlockSpec((D_MODEL, tf), lambda i, j: (0, j))
    down_spec = pl.BlockSpec((tf, D_MODEL), lambda i, j: (j, 0))
    out_specs = [pl.BlockSpec((TM, D_MODEL), lambda i, j: (i, 0))]
    out_shape = [jax.ShapeDtypeStruct((m, D_MODEL), F32)]
    if emit_bf16:
        assert m == TM, "weights are re-emitted once, by a single-row-tile call"
        out_specs += [up_spec, up_spec, down_spec]
        out_shape += [jax.ShapeDtypeStruct(w.shape, BF16) for w in (wg, wu, wd)]
    outs = pl.pallas_call(
        functools.partial(_ffn_kernel, emit_bf16),
        grid=(m // TM, D_FF // tf),
        in_specs=[
            pl.BlockSpec((TM, D_MODEL), lambda i, j: (i, 0)),
            pl.BlockSpec((1, D_MODEL), lambda i, j: (0, 0)),
            pl.BlockSpec((1, D_MODEL), lambda i, j: (0, 0)),
            up_spec, up_spec, down_spec,
        ],
        out_specs=out_specs,
        out_shape=out_shape,
        scratch_shapes=[pltpu.VMEM((TM, D_MODEL), BF16), pltpu.VMEM((TM, D_MODEL), F32)],
        compiler_params=_params("parallel", "arbitrary"),
        name="ffn_cast" if emit_bf16 else "ffn",
    )(x, pre, post, wg, wu, wd)
    return outs if emit_bf16 else (outs[0], wg, wu, wd)


def _inproj_kernel(emit_bf16, x_ref, pre_ref, w_ref, wab_ref, o_ref, ab_ref, *rest):
    h_ref = rest[-1]

    @pl.when(pl.program_id(1) == 0)
    def _():
        h = _rms(x_ref[...], pre_ref[...]).astype(BF16)
        h_ref[...] = h
        ab_ref[...] = jnp.dot(h, wab_ref[...], preferred_element_type=F32)

    w = w_ref[...]
    if emit_bf16:
        w = w.astype(BF16)
        rest[0][...] = w
    o_ref[...] = jnp.dot(h_ref[...], w, preferred_element_type=F32)


def _inproj(x, pre, w, wab):
    m = x.shape[0]
    emit_bf16 = w.dtype != BF16
    w_spec = pl.BlockSpec((D_MODEL, TN_IN), lambda i, j: (0, j))
    out_specs = [pl.BlockSpec((TM, TN_IN), lambda i, j: (i, j)),
                 pl.BlockSpec((TM, AB_LANES), lambda i, j: (i, 0))]
    out_shape = [jax.ShapeDtypeStruct((m, IN_MAIN), F32), jax.ShapeDtypeStruct((m, AB_LANES), F32)]
    if emit_bf16:
        assert m == TM, "weights are re-emitted once, by a single-row-tile call"
        out_specs.append(w_spec)
        out_shape.append(jax.ShapeDtypeStruct((D_MODEL, IN_MAIN), BF16))
    outs = pl.pallas_call(
        functools.partial(_inproj_kernel, emit_bf16),
        grid=(m // TM, IN_MAIN // TN_IN),
        in_specs=[
            pl.BlockSpec((TM, D_MODEL), lambda i, j: (i, 0)),
            pl.BlockSpec((1, D_MODEL), lambda i, j: (0, 0)),
            w_spec,
            pl.BlockSpec((D_MODEL, AB_LANES), lambda i, j: (0, 0)),
        ],
        out_specs=out_specs,
        out_shape=out_shape,
        scratch_shapes=[pltpu.VMEM((TM, D_MODEL), BF16)],
        compiler_params=_params("parallel", "arbitrary"),
        name="inproj_cast" if emit_bf16 else "inproj",
    )(x, pre, w, wab)
    return outs if emit_bf16 else (outs[0], outs[1], w)


def _causal_conv(x, w, carry_ref):
    width = w.shape[0]
    is_row0 = lax.broadcasted_iota(jnp.int32, (8, 1), 0) == 0
    acc = x * w[0:1, :]
    for j in range(1, width):
        delayed = pltpu.roll(acc, 1, axis=0)
        if carry_ref is not None:
            head = jnp.where(is_row0, carry_ref[j - 1:j, :], delayed[:8])
            carry_ref[j - 1:j, :] = acc[CHUNK - 1:CHUNK, :]
            delayed = jnp.concatenate([head, delayed[8:]], axis=0)
        acc = x * w[j:j + 1, :] + delayed
    return acc


def _heads_intra(cqkv, g_all, beta_all, gc_all, causal, strict, low_f, strict_f, eye, squarings):
    heads = range(DN_HEADS)
    g_col = [g_all[:, h:h + 1] for h in heads]
    gc_col = [gc_all[:, h:h + 1] for h in heads]
    beta_col = [beta_all[:, DN_HEADS + h:DN_HEADS + h + 1] for h in heads]
    qh = [cqkv[:, h * DN_DK:(h + 1) * DN_DK] for h in heads]
    kh = [cqkv[:, 1024 + h * DN_DK:1024 + (h + 1) * DN_DK] for h in heads]
    vh = [cqkv[:, 2048 + h * DN_DV:2048 + (h + 1) * DN_DV] for h in heads]
    qn = [x * (lax.rsqrt(jnp.sum(x * x, axis=-1, keepdims=True) + EPS) * (DN_DK ** -0.5)) for x in qh]
    kn = [x * lax.rsqrt(jnp.sum(x * x, axis=-1, keepdims=True) + EPS) for x in kh]
    gdiff = [_mm_exact(low_f, g_col[h] * strict_f) for h in heads]
    decay = [jnp.where(causal, jnp.exp(x), 0.0) for x in gdiff]
    kb = [kn[h] * beta_col[h] for h in heads]
    gram = [_mm_nt(jnp.concatenate([kb[h], qn[h]], axis=0), kn[h]) for h in heads]
    a_mat = [jnp.where(strict, gram[h][:CHUNK] * decay[h], 0.0) for h in heads]
    qk = [jnp.where(causal, gram[h][CHUNK:] * decay[h], 0.0) for h in heads]
    inv = [eye - a for a in a_mat]
    power = a_mat
    for _ in range(squarings):
        power = [_mm(x, x) for x in power]
        inv = [_mm(inv[h], eye + power[h]) for h in heads]
    egc = [jnp.exp(x) for x in gc_col]
    wu = [_mm(inv[h], jnp.concatenate([kb[h] * egc[h], vh[h] * beta_col[h]], axis=1)) for h in heads]
    w = [x[:, :DN_DK] for x in wu]
    u = [x[:, DN_DK:] for x in wu]
    qdec = [qn[h] * egc[h] for h in heads]
    return kn, w, u, qdec, qk, gc_col


def _gates(ab, alog, dtb):
    x = ab + dtb
    softplus = jnp.maximum(x, 0.0) + jnp.log1p(jnp.exp(-jnp.abs(x)))
    return -jnp.exp(alog) * softplus, jax.nn.sigmoid(ab)


def _masks(same_seq):
    row = lax.broadcasted_iota(jnp.int32, (CHUNK, CHUNK), 0)
    col = lax.broadcasted_iota(jnp.int32, (CHUNK, CHUNK), 1)
    causal = row >= col
    strict = row > col
    if same_seq:
        same = (row >> SEQ_SHIFT) == (col >> SEQ_SHIFT)
        causal = causal & same
        strict = strict & same
    eye = jnp.where(row == col, 1.0, 0.0).astype(F32)
    return row, col, causal, strict, causal.astype(F32), strict.astype(F32), eye


def _mix_prompt_kernel(pa_ref, pq_ref, pz_ref, pab_ref, caw_ref, cqw_ref, alog_ref, dtb_ref, dnn_ref,
                       y_ref, ca_ref, cq_ref, so_ref, carrya_ref, carryq_ref, s_ref):
    c = pl.program_id(1)

    @pl.when(c == 0)
    def _():
        carrya_ref[...] = jnp.zeros_like(carrya_ref)
        carryq_ref[...] = jnp.zeros_like(carryq_ref)
        s_ref[...] = jnp.zeros_like(s_ref)

    pa = pa_ref[...]
    ua = pa[:, D_CONV:2 * D_CONV] * pa[:, 2 * D_CONV:]
    y_ref[:, :D_CONV] = (pa[:, :D_CONV] * _causal_conv(ua, caw_ref[...], carrya_ref)).astype(y_ref.dtype)
    pq = pq_ref[...]
    cqkv = _silu(_causal_conv(pq, cqw_ref[...], carryq_ref))

    _, _, causal, strict, low_f, strict_f, eye = _masks(False)
    g_all, beta_all = _gates(pab_ref[...], alog_ref[...], dtb_ref[...])
    gc_all = _mm_exact(low_f, g_all)
    z = pz_ref[...]
    dnn = dnn_ref[...]
    heads = range(DN_HEADS)
    kn, w, u, qdec, qk, gc_col = _heads_intra(
        cqkv, g_all, beta_all, gc_all, causal, strict, low_f, strict_f, eye, 5)
    s_old = [s_ref[h] for h in heads]
    r = [_mm(jnp.concatenate([w[h], qdec[h]], axis=0), s_old[h]) for h in heads]
    v_new = [u[h] - r[h][:CHUNK] for h in heads]
    g_last = [x[CHUNK - 1:CHUNK, :] for x in gc_col]
    kdec = [kn[h] * jnp.exp(g_last[h] - gc_col[h]) for h in heads]
    s_new = [s_old[h] * jnp.exp(g_last[h]) + _mm_tn(kdec[h], v_new[h]) for h in heads]
    o = [r[h][CHUNK:] + _mm(qk[h], v_new[h]) for h in heads]
    for h in heads:
        s_ref[h] = s_new[h]
        lo = h * DN_DV
        yb = _rms(o[h], dnn) * _silu(z[:, lo:lo + DN_DV])
        y_ref[:, D_CONV + lo:D_CONV + lo + DN_DV] = yb.astype(y_ref.dtype)

    @pl.when(c == pl.num_programs(1) - 1)
    def _():
        ca_ref[0] = ua[CHUNK - 2:CHUNK, :]
        cq_ref[0] = pq[CHUNK - 3:CHUNK, :]
        so_ref[0] = s_ref[...]


def _mix_prompt(proj, ab, n_seq, seq_len, caw, cqw, alog, dtb, dnn):
    n_chunks = seq_len // CHUNK
    row = lambda b, c: b * n_chunks + c
    full2 = lambda b, c: (0, 0)
    return pl.pallas_call(
        _mix_prompt_kernel,
        grid=(n_seq, n_chunks),
        in_specs=[
            pl.BlockSpec((CHUNK, 3 * D_CONV), lambda b, c: (row(b, c), 0)),
            pl.BlockSpec((CHUNK, DN_QKV), lambda b, c: (row(b, c), 1)),
            pl.BlockSpec((CHUNK, 1024), lambda b, c: (row(b, c), Z_BLOCK)),
            pl.BlockSpec((CHUNK, AB_LANES), lambda b, c: (row(b, c), 0)),
            pl.BlockSpec((3, D_CONV), full2),
            pl.BlockSpec((4, DN_QKV), full2),
            pl.BlockSpec((1, AB_LANES), full2),
            pl.BlockSpec((1, AB_LANES), full2),
            pl.BlockSpec((1, DN_DV), full2),
        ],
        out_specs=[
            pl.BlockSpec((CHUNK, D_MODEL), lambda b, c: (row(b, c), 0)),
            pl.BlockSpec((1, 2, D_CONV), lambda b, c: (b, 0, 0)),
            pl.BlockSpec((1, 3, DN_QKV), lambda b, c: (b, 0, 0)),
            pl.BlockSpec((1, DN_HEADS, DN_DK, DN_DV), lambda b, c: (b, 0, 0, 0)),
        ],
        out_shape=[
            jax.ShapeDtypeStruct((n_seq * seq_len, D_MODEL), BF16),
            jax.ShapeDtypeStruct((n_seq, 2, D_CONV), F32),
            jax.ShapeDtypeStruct((n_seq, 3, DN_QKV), F32),
            jax.ShapeDtypeStruct((n_seq, DN_HEADS, DN_DK, DN_DV), F32),
        ],
        scratch_shapes=[
            pltpu.VMEM((8, D_CONV), F32),
            pltpu.VMEM((8, DN_QKV), F32),
            pltpu.VMEM((DN_HEADS, DN_DK, DN_DV), F32),
        ],
        compiler_params=_params("arbitrary", "arbitrary"),
        name="mix_prompt",
    )(proj, proj, proj, ab, caw, cqw, alog, dtb, dnn)


def _mix_sample_kernel(pa_ref, pq_ref, pz_ref, pab_ref, sa_ref, sq_ref, si_ref,
                       caw_ref, cqw_ref, alog_ref, dtb_ref, dnn_ref,
                       y_ref, ca_ref, cq_ref, so_ref,
                       gb_ref, ua_ref, xq_ref, z_ref, ab_ref, ybuf_ref):
    seqs = range(SEQ_PER_STEP)
    pad = SEQ_PAD - SAMPLE_LEN
    for b in seqs:
        r0 = SEQ_PAD * b
        nxt = (b + 1) % SEQ_PER_STEP
        src = slice(SAMPLE_LEN * b, SAMPLE_LEN * (b + 1))
        pa = pa_ref[src, :]
        gb_ref[r0:r0 + SAMPLE_LEN, :] = pa[:, :D_CONV]
        ua_ref[r0:r0 + SAMPLE_LEN, :] = pa[:, D_CONV:2 * D_CONV] * pa[:, 2 * D_CONV:]
        xq_ref[r0:r0 + SAMPLE_LEN, :] = pq_ref[src, :]
        z_ref[r0:r0 + SAMPLE_LEN, :] = pz_ref[src, :]
        ab_ref[r0:r0 + SAMPLE_LEN, :] = pab_ref[src, :]
        for ref in (gb_ref, ua_ref, xq_ref, z_ref, ab_ref):
            ref[r0 + SAMPLE_LEN:r0 + SEQ_PAD, :] = jnp.zeros((pad, ref.shape[1]), F32)
        ua_ref[r0 + SEQ_PAD - 2:r0 + SEQ_PAD, :] = sa_ref[nxt]
        xq_ref[r0 + SEQ_PAD - 3:r0 + SEQ_PAD, :] = sq_ref[nxt]
    ybuf_ref[:, :D_CONV] = gb_ref[...] * _causal_conv(ua_ref[...], caw_ref[...], None)
    row, col, causal, strict, low_f, strict_f, eye = _masks(True)
    valid = (row[:, 0:1] & (SEQ_PAD - 1)) < SAMPLE_LEN
    cqkv = jnp.where(valid, _silu(_causal_conv(xq_ref[...], cqw_ref[...], None)), 0.0)
    for b in seqs:
        r0 = SEQ_PAD * b
        ca_ref[b] = ua_ref[r0 + SAMPLE_LEN - 2:r0 + SAMPLE_LEN, :]
        cq_ref[b] = xq_ref[r0 + SAMPLE_LEN - 3:r0 + SAMPLE_LEN, :]

    g_all, beta_all = _gates(ab_ref[...], alog_ref[...], dtb_ref[...])
    g_all = jnp.where(valid, g_all, 0.0)
    beta_all = jnp.where(valid, beta_all, 0.0)
    gc_all = _mm_exact(low_f, g_all)
    last_sel = (col == (row | (SEQ_PAD - 1))).astype(F32)
    gl_all = _mm_exact(last_sel, gc_all)
    z = z_ref[...]
    dnn = dnn_ref[...]
    heads = range(DN_HEADS)
    kn, w, u, qdec, qk, gc_col = _heads_intra(
        cqkv, g_all, beta_all, gc_all, causal, strict, low_f, strict_f, eye, 1)
    gl_col = [gl_all[:, h:h + 1] for h in heads]
    kdec = [kn[h] * jnp.exp(gl_col[h] - gc_col[h]) for h in heads]
    s_scale = [jnp.exp(x) for x in gl_col]
    rows = [slice(SEQ_PAD * b, SEQ_PAD * (b + 1)) for b in seqs]
    for h in heads:
        s_old = [si_ref[b, h] for b in seqs]
        r = [_mm(jnp.concatenate([w[h][rows[b]], qdec[h][rows[b]]], axis=0), s_old[b]) for b in seqs]
        v_new = [u[h][rows[b]] - r[b][:SEQ_PAD] for b in seqs]
        upd = [_mm_tn(kdec[h][rows[b]], v_new[b]) for b in seqs]
        for b in seqs:
            so_ref[b, h] = s_old[b] * s_scale[h][SEQ_PAD * b:SEQ_PAD * b + 1, :] + upd[b]
        o = jnp.concatenate([x[SEQ_PAD:] for x in r], axis=0) + _mm(qk[h], jnp.concatenate(v_new, axis=0))
        lo = h * DN_DV
        ybuf_ref[:, D_CONV + lo:D_CONV + lo + DN_DV] = _rms(o, dnn) * _silu(z[:, lo:lo + DN_DV])
    for b in seqs:
        y_ref[SAMPLE_LEN * b:SAMPLE_LEN * (b + 1), :] = ybuf_ref[SEQ_PAD * b:SEQ_PAD * b + SAMPLE_LEN, :]


def _mix_sample(proj, ab, state_a, state_q, state_s, caw, cqw, alog, dtb, dnn):
    n_seq = state_s.shape[0]
    full2 = lambda i: (0, 0)
    seq_rows = lambda width, col: pl.BlockSpec((SEQ_PER_STEP * SAMPLE_LEN, width), lambda i: (i, col))
    return pl.pallas_call(
        _mix_sample_kernel,
        grid=(n_seq // SEQ_PER_STEP,),
        in_specs=[
            seq_rows(3 * D_CONV, 0), seq_rows(DN_QKV, 1), seq_rows(1024, Z_BLOCK), seq_rows(AB_LANES, 0),
            pl.BlockSpec((SEQ_PER_STEP, 2, D_CONV), lambda i: (i, 0, 0)),
            pl.BlockSpec((SEQ_PER_STEP, 3, DN_QKV), lambda i: (i, 0, 0)),
            pl.BlockSpec((SEQ_PER_STEP, DN_HEADS, DN_DK, DN_DV), lambda i: (i, 0, 0, 0)),
            pl.BlockSpec((3, D_CONV), full2),
            pl.BlockSpec((4, DN_QKV), full2),
            pl.BlockSpec((1, AB_LANES), full2),
            pl.BlockSpec((1, AB_LANES), full2),
            pl.BlockSpec((1, DN_DV), full2),
        ],
        out_specs=[
            seq_rows(D_MODEL, 0),
            pl.BlockSpec((SEQ_PER_STEP, 2, D_CONV), lambda i: (i, 0, 0)),
            pl.BlockSpec((SEQ_PER_STEP, 3, DN_QKV), lambda i: (i, 0, 0)),
            pl.BlockSpec((SEQ_PER_STEP, DN_HEADS, DN_DK, DN_DV), lambda i: (i, 0, 0, 0)),
        ],
        out_shape=[
            jax.ShapeDtypeStruct((n_seq * SAMPLE_LEN, D_MODEL), F32),
            jax.ShapeDtypeStruct((n_seq, 2, D_CONV), F32),
            jax.ShapeDtypeStruct((n_seq, 3, DN_QKV), F32),
            jax.ShapeDtypeStruct((n_seq, DN_HEADS, DN_DK, DN_DV), F32),
        ],
        scratch_shapes=[
            pltpu.VMEM((CHUNK, D_CONV), F32),
            pltpu.VMEM((CHUNK, D_CONV), F32),
            pltpu.VMEM((CHUNK, DN_QKV), F32),
            pltpu.VMEM((CHUNK, 1024), F32),
            pltpu.VMEM((CHUNK, AB_LANES), F32),
            pltpu.VMEM((CHUNK, D_MODEL), F32),
        ],
        compiler_params=_params("parallel"),
        name="mix_sample",
    )(proj, proj, proj, ab, state_a, state_q, state_s, caw, cqw, alog, dtb, dnn)


def _outproj_kernel(y_ref, w_ref, post_ref, x_ref, o_ref):
    m = jnp.dot(y_ref[...].astype(BF16), w_ref[...], preferred_element_type=F32)
    o_ref[...] = x_ref[...] + _rms(m, post_ref[...])


def _outproj(y, w, post, x):
    m = x.shape[0]
    return pl.pallas_call(
        _outproj_kernel,
        grid=(m // TM,),
        in_specs=[
            pl.BlockSpec((TM, D_MODEL), lambda i: (i, 0)),
            pl.BlockSpec((D_MODEL, D_MODEL), lambda i: (0, 0)),
            pl.BlockSpec((1, D_MODEL), lambda i: (0, 0)),
            pl.BlockSpec((TM, D_MODEL), lambda i: (i, 0)),
        ],
        out_specs=pl.BlockSpec((TM, D_MODEL), lambda i: (i, 0)),
        out_shape=jax.ShapeDtypeStruct((m, D_MODEL), F32),
        compiler_params=_params("parallel"),
        name="outproj",
    )(y, w, post, x)


def _ple_kernel(x_ref, p_ref, pre_ref, post_ref, wg_ref, wp_ref, o_ref):
    x = x_ref[...]
    h = _rms(x, pre_ref[...]).astype(BF16)
    gate = jax.nn.sigmoid(jnp.dot(h, wg_ref[...], preferred_element_type=F32))
    proj = jnp.dot(p_ref[...].astype(BF16), wp_ref[...], preferred_element_type=F32)
    o_ref[...] = x + _rms(gate * proj, post_ref[...])


def _ple(x, p, pre, post, wg, wp):
    m = x.shape[0]
    return pl.pallas_call(
        _ple_kernel,
        grid=(m // TM,),
        in_specs=[
            pl.BlockSpec((TM, D_MODEL), lambda i: (i, 0)),
            pl.BlockSpec((TM, D_PLE), lambda i: (i, 0)),
            pl.BlockSpec((1, D_MODEL), lambda i: (0, 0)),
            pl.BlockSpec((1, D_MODEL), lambda i: (0, 0)),
            pl.BlockSpec((D_MODEL, D_MODEL), lambda i: (0, 0)),
            pl.BlockSpec((D_PLE, D_MODEL), lambda i: (0, 0)),
        ],
        out_specs=pl.BlockSpec((TM, D_MODEL), lambda i: (i, 0)),
        out_shape=jax.ShapeDtypeStruct((m, D_MODEL), F32),
        compiler_params=_params("parallel"),
        name="ple",
    )(x, p, pre, post, wg, wp)


def _row(v):
    return v.reshape(1, -1).astype(F32)


def _pad_lanes(v):
    return jnp.pad(_row(v), ((0, 0), (0, AB_LANES - v.shape[-1])))


def kernel(x_prompt, x_sample, state_conv_a, state_conv_qkv, state_delta, p_prompt, p_sample,
           f1_pre, f1_post, f1_wg, f1_wu, f1_wd,
           mix_pre, mix_post, w_in, conv_a_w, conv_qkv_w, a_log, dt_bias, dn_norm, w_out,
           f2_pre, f2_post, f2_wg, f2_wu, f2_wd,
           ple_pre, ple_post, w_ple_gate, w_ple_proj):
    depth = f1_pre.shape[0]
    n_p, seq_p, _ = x_prompt.shape
    n_s, seq_s, _ = x_sample.shape
    rows_p = n_p * seq_p
    rows_s = n_s * seq_s
    assert seq_p % CHUNK == 0 and seq_s == SAMPLE_LEN and n_s % SEQ_PER_STEP == 0
    assert rows_p % TM == 0 and rows_s == TM
    xp = x_prompt.reshape(rows_p, D_MODEL)
    xs = x_sample.reshape(rows_s, D_MODEL)
    outs = [[] for _ in range(6)]
    for i in range(depth):
        pre, post = _row(f1_pre[i]), _row(f1_post[i])
        xs, wg, wu, wd = _ffn(xs, pre, post, f1_wg[i], f1_wu[i], f1_wd[i])
        xp = _ffn(xp, pre, post, wg, wu, wd)[0]

        pre = _row(mix_pre[i])
        wab = jnp.pad(w_in[i][:, IN_MAIN:], ((0, 0), (0, AB_LANES - (w_in.shape[-1] - IN_MAIN)))).astype(BF16)
        proj_s, ab_s, w_in_bf16 = _inproj(xs, pre, w_in[i], wab)
        proj_p, ab_p, _ = _inproj(xp, pre, w_in_bf16, wab)

        alog, dtb, dnn = _pad_lanes(a_log[i]), _pad_lanes(dt_bias[i]), _row(dn_norm[i])
        y_p, ca_p, cq_p, s_p = _mix_prompt(proj_p, ab_p, n_p, seq_p, conv_a_w[i], conv_qkv_w[i], alog, dtb, dnn)
        y_s, ca_s, cq_s, s_s = _mix_sample(
            proj_s, ab_s, state_conv_a[i], state_conv_qkv[i], state_delta[i],
            conv_a_w[i], conv_qkv_w[i], alog, dtb, dnn)

        post = _row(mix_post[i])
        w_out_bf16 = w_out[i].astype(BF16)
        xs = _outproj(y_s, w_out_bf16, post, xs)
        xp = _outproj(y_p, w_out_bf16, post, xp)

        pre, post = _row(f2_pre[i]), _row(f2_post[i])
        xs, wg, wu, wd = _ffn(xs, pre, post, f2_wg[i], f2_wu[i], f2_wd[i])
        xp = _ffn(xp, pre, post, wg, wu, wd)[0]

        pre, post = _row(ple_pre[i]), _row(ple_post[i])
        wg, wp = w_ple_gate[i].astype(BF16), w_ple_proj[i].astype(BF16)
        xs = _ple(xs, p_sample[i].reshape(rows_s, D_PLE), pre, post, wg, wp)
        xp = _ple(xp, p_prompt[i].reshape(rows_p, D_PLE), pre, post, wg, wp)
        for lst, val in zip(outs, (ca_p, cq_p, s_p, ca_s, cq_s, s_s)):
            lst.append(val)
    return (xp.reshape(n_p, seq_p, D_MODEL), xs.reshape(n_s, seq_s, D_MODEL),
            *[jnp.stack(lst) for lst in outs])
```

```python
import functools

import jax
import jax.numpy as jnp
from jax import lax
from jax.experimental import pallas as pl
from jax.experimental.pallas import tpu as pltpu

F32 = jnp.float32
BF16 = jnp.bfloat16

D_MODEL = 2048
D_CONV = 1024
DN_HEADS = 8
DN_DK = 128
DN_DV = 128
DN_QKV = 3072
D_FF = 5632
D_PLE = 256
EPS = 1e-6
CHUNK = 64
SAMPLE_LEN = 4
SEQ_PAD = 8
SEQ_SHIFT = 3
SEQ_PER_STEP = CHUNK // SEQ_PAD
IN_MAIN = 7168
Z_BLOCK = 6
N_AB = 16
AB_LANES = 128

TM = 512
TF = 512
TF_CAST = 256
TN_IN = 1024
TM_IN = 1024
ROW_CHUNK = 32
VMEM_LIMIT = 56 * 1024 * 1024


def _rms(x, g):
    ms = jnp.mean(x * x, axis=-1, keepdims=True)
    return x * lax.rsqrt(ms + EPS) * g


def _silu(x):
    return x * jax.nn.sigmoid(x)


def _mm(a, b):
    return jnp.dot(a.astype(BF16), b.astype(BF16), preferred_element_type=F32)


def _mm_nt(a, b):
    return lax.dot_general(a.astype(BF16), b.astype(BF16), (((1,), (1,)), ((), ())),
                           preferred_element_type=F32)


def _mm_tn(a, b):
    return jnp.dot(a.T.astype(BF16), b.astype(BF16), preferred_element_type=F32)


def _mm_exact(a, b):
    return jnp.dot(a, b, precision=lax.Precision.HIGHEST, preferred_element_type=F32)


def _for_row_chunks(n_rows, body):
    def step(k, carry):
        body(pl.ds(pl.multiple_of(k * ROW_CHUNK, ROW_CHUNK), ROW_CHUNK))
        return carry
    lax.fori_loop(0, n_rows // ROW_CHUNK, step, 0, unroll=4)


def _params(*semantics):
    return pltpu.CompilerParams(dimension_semantics=semantics, vmem_limit_bytes=VMEM_LIMIT)


def _ffn_kernel(emit_bf16, x_ref, pre_ref, post_ref, wg_ref, wu_ref, wd_ref, o_ref, *rest):
    h_ref, acc_ref = rest[-2:]
    j = pl.program_id(1)

    @pl.when(j == 0)
    def _():
        def norm_rows(rows):
            h_ref[rows, :] = _rms(x_ref[rows, :], pre_ref[...]).astype(BF16)
            acc_ref[rows, :] = jnp.zeros((ROW_CHUNK, D_MODEL), F32)
        _for_row_chunks(x_ref.shape[0], norm_rows)

    wg, wu, wd = wg_ref[...], wu_ref[...], wd_ref[...]
    if emit_bf16:
        wg, wu, wd = wg.astype(BF16), wu.astype(BF16), wd.astype(BF16)
        rest[0][...] = wg
        rest[1][...] = wu
        rest[2][...] = wd
    h = h_ref[...]
    g = jnp.dot(h, wg, preferred_element_type=F32)
    u = jnp.dot(h, wu, preferred_element_type=F32)
    a = (_silu(g) * u).astype(BF16)
    acc_ref[...] += jnp.dot(a, wd, preferred_element_type=F32)

    @pl.when(j == pl.num_programs(1) - 1)
    def _():
        def finish_rows(rows):
            o_ref[rows, :] = x_ref[rows, :] + 0.5 * _rms(acc_ref[rows, :], post_ref[...])
        _for_row_chunks(x_ref.shape[0], finish_rows)


def _ffn(x, pre, post, wg, wu, wd):
    m = x.shape[0]
    emit_bf16 = wg.dtype != BF16
    tf = TF_CAST if emit_bf16 else TF
    up_spec = pl.BlockSpec((D_MODEL, tf), lambda i, j: (0, j))
    down_spec = pl.BlockSpec((tf, D_MODEL), lambda i, j: (j, 0))
    out_specs = [pl.BlockSpec((TM, D_MODEL), lambda i, j: (i, 0))]
    out_shape = [jax.ShapeDtypeStruct((m, D_MODEL), F32)]
    if emit_bf16:
        assert m == TM, "weights are re-emitted once, by a single-row-tile call"
        out_specs += [up_spec, up_spec, down_spec]
        out_shape += [jax.ShapeDtypeStruct(w.shape, BF16) for w in (wg, wu, wd)]
    outs = pl.pallas_call(
        functools.partial(_ffn_kernel, emit_bf16),
        grid=(m // TM, D_FF // tf),
        in_specs=[
            pl.BlockSpec((TM, D_MODEL), lambda i, j: (i, 0)),
            pl.BlockSpec((1, D_MODEL), lambda i, j: (0, 0)),
            pl.BlockSpec((1, D_MODEL), lambda i, j: (0, 0)),
            up_spec, up_spec, down_spec,
        ],
        out_specs=out_specs,
        out_shape=out_shape,
        scratch_shapes=[pltpu.VMEM((TM, D_MODEL), BF16), pltpu.VMEM((TM, D_MODEL), F32)],
        compiler_params=_params("parallel", "arbitrary"),
        name="ffn_cast" if emit_bf16 else "ffn",
    )(x, pre, post, wg, wu, wd)
    return outs if emit_bf16 else (outs[0], wg, wu, wd)


def _nt_dot(a, bt):
    return lax.dot_general(a, bt, (((1,), (1,)), ((), ())), preferred_element_type=F32)


def _inproj_kernel(emit_bf16, x_ref, pre_ref, wt_ref, wabt_ref, o_ref, ab_ref, *rest):
    h_ref = rest[-1]

    @pl.when(pl.program_id(1) == 0)
    def _():
        h = _rms(x_ref[...], pre_ref[...]).astype(BF16)
        h_ref[...] = h
        wabt = wabt_ref[...]
        if emit_bf16:
            wabt = jnp.concatenate(
                [wabt.astype(BF16), jnp.zeros((AB_LANES - N_AB, D_MODEL), BF16)], axis=0)
            rest[1][...] = wabt
        ab_ref[...] = _nt_dot(h, wabt)

    wt = wt_ref[...]
    if emit_bf16:
        wt = wt.astype(BF16)
        rest[0][...] = wt
    o_ref[...] = _nt_dot(h_ref[...], wt)


def _inproj(x, pre, wt, wabt, tm):
    m = x.shape[0]
    emit_bf16 = wt.dtype != BF16
    wt_spec = pl.BlockSpec((TN_IN, D_MODEL), lambda i, j: (j, 0))
    if emit_bf16:
        wabt = wt
        wabt_spec = pl.BlockSpec((N_AB, D_MODEL), lambda i, j: (IN_MAIN // N_AB, 0))
    else:
        wabt_spec = pl.BlockSpec((AB_LANES, D_MODEL), lambda i, j: (0, 0))
    out_specs = [pl.BlockSpec((tm, TN_IN), lambda i, j: (i, j)),
                 pl.BlockSpec((tm, AB_LANES), lambda i, j: (i, 0))]
    out_shape = [jax.ShapeDtypeStruct((m, IN_MAIN), F32), jax.ShapeDtypeStruct((m, AB_LANES), F32)]
    if emit_bf16:
        assert m == tm, "weights are re-emitted once, by a single-row-tile call"
        out_specs += [wt_spec, pl.BlockSpec((AB_LANES, D_MODEL), lambda i, j: (0, 0))]
        out_shape += [jax.ShapeDtypeStruct((IN_MAIN, D_MODEL), BF16),
                      jax.ShapeDtypeStruct((AB_LANES, D_MODEL), BF16)]
    outs = pl.pallas_call(
        functools.partial(_inproj_kernel, emit_bf16),
        grid=(m // tm, IN_MAIN // TN_IN),
        in_specs=[
            pl.BlockSpec((tm, D_MODEL), lambda i, j: (i, 0)),
            pl.BlockSpec((1, D_MODEL), lambda i, j: (0, 0)),
            wt_spec,
            wabt_spec,
        ],
        out_specs=out_specs,
        out_shape=out_shape,
        scratch_shapes=[pltpu.VMEM((tm, D_MODEL), BF16)],
        compiler_params=_params("parallel", "arbitrary"),
        name="inproj_cast" if emit_bf16 else "inproj",
    )(x, pre, wt, wabt)
    return outs if emit_bf16 else (outs[0], outs[1], wt, wabt)


def _causal_conv(x, w, carry_ref):
    width = w.shape[0]
    is_row0 = lax.broadcasted_iota(jnp.int32, (8, 1), 0) == 0
    acc = x * w[0:1, :]
    for j in range(1, width):
        delayed = pltpu.roll(acc, 1, axis=0)
        if carry_ref is not None:
            head = jnp.where(is_row0, carry_ref[j - 1:j, :], delayed[:8])
            carry_ref[j - 1:j, :] = acc[CHUNK - 1:CHUNK, :]
            delayed = jnp.concatenate([head, delayed[8:]], axis=0)
        acc = x * w[j:j + 1, :] + delayed
    return acc


def _heads_intra(cqkv, beta_all, gc_all, causal, strict, eye, squarings):
    heads = range(DN_HEADS)
    gc_col = [gc_all[:, h:h + 1] for h in heads]
    gc_t = gc_all.T
    gc_row = [gc_t[h:h + 1, :] for h in heads]
    beta_col = [beta_all[:, DN_HEADS + h:DN_HEADS + h + 1] for h in heads]
    qh = [cqkv[:, h * DN_DK:(h + 1) * DN_DK] for h in heads]
    kh = [cqkv[:, 1024 + h * DN_DK:1024 + (h + 1) * DN_DK] for h in heads]
    vh = [cqkv[:, 2048 + h * DN_DV:2048 + (h + 1) * DN_DV] for h in heads]
    qn = [x * (lax.rsqrt(jnp.sum(x * x, axis=-1, keepdims=True) + EPS) * (DN_DK ** -0.5)) for x in qh]
    kn = [x * lax.rsqrt(jnp.sum(x * x, axis=-1, keepdims=True) + EPS) for x in kh]
    decay = [jnp.exp(jnp.where(causal, gc_col[h] - gc_row[h], -jnp.inf)) for h in heads]
    kb = [kn[h] * beta_col[h] for h in heads]
    gram = [_mm_nt(jnp.concatenate([kb[h], qn[h]], axis=0), kn[h]) for h in heads]
    a_mat = [jnp.where(strict, gram[h][:CHUNK] * decay[h], 0.0) for h in heads]
    qk = [jnp.where(causal, gram[h][CHUNK:] * decay[h], 0.0) for h in heads]
    inv = [eye - a for a in a_mat]
    power = a_mat
    for _ in range(squarings):
        power = [_mm(x, x) for x in power]
        inv = [_mm(inv[h], eye + power[h]) for h in heads]
    egc = [jnp.exp(x) for x in gc_col]
    wu = [_mm(inv[h], jnp.concatenate([kb[h] * egc[h], vh[h] * beta_col[h]], axis=1)) for h in heads]
    w = [x[:, :DN_DK] for x in wu]
    u = [x[:, DN_DK:] for x in wu]
    qdec = [qn[h] * egc[h] for h in heads]
    return kn, w, u, qdec, qk, gc_col


def _gates(ab, alog, dtb):
    x = ab + dtb
    softplus = jnp.maximum(x, 0.0) + jnp.log1p(jnp.exp(-jnp.abs(x)))
    return -jnp.exp(alog) * softplus, jax.nn.sigmoid(ab)


def _masks(same_seq):
    row = lax.broadcasted_iota(jnp.int32, (CHUNK, CHUNK), 0)
    col = lax.broadcasted_iota(jnp.int32, (CHUNK, CHUNK), 1)
    causal = row >= col
    strict = row > col
    if same_seq:
        same = (row >> SEQ_SHIFT) == (col >> SEQ_SHIFT)
        causal = causal & same
        strict = strict & same
    eye = jnp.where(row == col, 1.0, 0.0).astype(F32)
    return row, col, causal, strict, causal.astype(F32), eye


def _mix_prompt_kernel(pa_ref, pq_ref, pz_ref, pab_ref, caw_ref, cqw_ref, alog_ref, dtb_ref, dnn_ref,
                       y_ref, ca_ref, cq_ref, so_ref, carrya_ref, carryq_ref, s_ref):
    c = pl.program_id(1)

    @pl.when(c == 0)
    def _():
        carrya_ref[...] = jnp.zeros_like(carrya_ref)
        carryq_ref[...] = jnp.zeros_like(carryq_ref)
        s_ref[...] = jnp.zeros_like(s_ref)

    pa = pa_ref[...]
    ua = pa[:, D_CONV:2 * D_CONV] * pa[:, 2 * D_CONV:]
    y_ref[:, :D_CONV] = (pa[:, :D_CONV] * _causal_conv(ua, caw_ref[...], carrya_ref)).astype(y_ref.dtype)
    pq = pq_ref[...]
    cqkv = _silu(_causal_conv(pq, cqw_ref[...], carryq_ref))

    _, _, causal, strict, low_f, eye = _masks(False)
    g_all, beta_all = _gates(pab_ref[...], alog_ref[...], dtb_ref[...])
    gc_all = _mm_exact(low_f, g_all)
    z = pz_ref[...]
    dnn = dnn_ref[...]
    heads = range(DN_HEADS)
    kn, w, u, qdec, qk, gc_col = _heads_intra(
        cqkv, beta_all, gc_all, causal, strict, eye, 5)
    s_old = [s_ref[h] for h in heads]
    r = [_mm(jnp.concatenate([w[h], qdec[h]], axis=0), s_old[h]) for h in heads]
    v_new = [u[h] - r[h][:CHUNK] for h in heads]
    g_last = [x[CHUNK - 1:CHUNK, :] for x in gc_col]
    kdec = [kn[h] * jnp.exp(g_last[h] - gc_col[h]) for h in heads]
    s_new = [s_old[h] * jnp.exp(g_last[h]) + _mm_tn(kdec[h], v_new[h]) for h in heads]
    o = [r[h][CHUNK:] + _mm(qk[h], v_new[h]) for h in heads]
    for h in heads:
        s_ref[h] = s_new[h]
        lo = h * DN_DV
        yb = _rms(o[h], dnn) * _silu(z[:, lo:lo + DN_DV])
        y_ref[:, D_CONV + lo:D_CONV + lo + DN_DV] = yb.astype(y_ref.dtype)

    @pl.when(c == pl.num_programs(1) - 1)
    def _():
        ca_ref[0] = ua[CHUNK - 2:CHUNK, :]
        cq_ref[0] = pq[CHUNK - 3:CHUNK, :]
        so_ref[0] = s_ref[...]


def _mix_prompt(proj, ab, n_seq, seq_len, caw, cqw, alog, dtb, dnn):
    n_chunks = seq_len // CHUNK
    row = lambda b, c: b * n_chunks + c
    full2 = lambda b, c: (0, 0)
    return pl.pallas_call(
        _mix_prompt_kernel,
        grid=(n_seq, n_chunks),
        in_specs=[
            pl.BlockSpec((CHUNK, 3 * D_CONV), lambda b, c: (row(b, c), 0)),
            pl.BlockSpec((CHUNK, DN_QKV), lambda b, c: (row(b, c), 1)),
            pl.BlockSpec((CHUNK, 1024), lambda b, c: (row(b, c), Z_BLOCK)),
            pl.BlockSpec((CHUNK, AB_LANES), lambda b, c: (row(b, c), 0)),
            pl.BlockSpec((3, D_CONV), full2),
            pl.BlockSpec((4, DN_QKV), full2),
            pl.BlockSpec((1, AB_LANES), full2),
            pl.BlockSpec((1, AB_LANES), full2),
            pl.BlockSpec((1, DN_DV), full2),
        ],
        out_specs=[
            pl.BlockSpec((CHUNK, D_MODEL), lambda b, c: (row(b, c), 0)),
            pl.BlockSpec((1, 2, D_CONV), lambda b, c: (b, 0, 0)),
            pl.BlockSpec((1, 3, DN_QKV), lambda b, c: (b, 0, 0)),
            pl.BlockSpec((1, DN_HEADS, DN_DK, DN_DV), lambda b, c: (b, 0, 0, 0)),
        ],
        out_shape=[
            jax.ShapeDtypeStruct((n_seq * seq_len, D_MODEL), BF16),
            jax.ShapeDtypeStruct((n_seq, 2, D_CONV), F32),
            jax.ShapeDtypeStruct((n_seq, 3, DN_QKV), F32),
            jax.ShapeDtypeStruct((n_seq, DN_HEADS, DN_DK, DN_DV), F32),
        ],
        scratch_shapes=[
            pltpu.VMEM((8, D_CONV), F32),
            pltpu.VMEM((8, DN_QKV), F32),
            pltpu.VMEM((DN_HEADS, DN_DK, DN_DV), F32),
        ],
        compiler_params=_params("arbitrary", "arbitrary"),
        name="mix_prompt",
    )(proj, proj, proj, ab, caw, cqw, alog, dtb, dnn)


def _mix_sample_kernel(pa_ref, pq_ref, pz_ref, pab_ref, sa_ref, sq_ref, si_ref,
                       caw_ref, cqw_ref, alog_ref, dtb_ref, dnn_ref,
                       y_ref, ca_ref, cq_ref, so_ref,
                       gb_ref, ua_ref, xq_ref, z_ref, ab_ref, ybuf_ref):
    seqs = range(SEQ_PER_STEP)
    pad = SEQ_PAD - SAMPLE_LEN
    for b in seqs:
        r0 = SEQ_PAD * b
        nxt = (b + 1) % SEQ_PER_STEP
        src = slice(SAMPLE_LEN * b, SAMPLE_LEN * (b + 1))
        pa = pa_ref[src, :]
        gb_ref[r0:r0 + SAMPLE_LEN, :] = pa[:, :D_CONV]
        ua_ref[r0:r0 + SAMPLE_LEN, :] = pa[:, D_CONV:2 * D_CONV] * pa[:, 2 * D_CONV:]
        xq_ref[r0:r0 + SAMPLE_LEN, :] = pq_ref[src, :]
        z_ref[r0:r0 + SAMPLE_LEN, :] = pz_ref[src, :]
        ab_ref[r0:r0 + SAMPLE_LEN, :] = pab_ref[src, :]
        for ref in (gb_ref, ua_ref, xq_ref, z_ref, ab_ref):
            ref[r0 + SAMPLE_LEN:r0 + SEQ_PAD, :] = jnp.zeros((pad, ref.shape[1]), F32)
        ua_ref[r0 + SEQ_PAD - 2:r0 + SEQ_PAD, :] = sa_ref[nxt]
        xq_ref[r0 + SEQ_PAD - 3:r0 + SEQ_PAD, :] = sq_ref[nxt]
    ybuf_ref[:, :D_CONV] = gb_ref[...] * _causal_conv(ua_ref[...], caw_ref[...], None)
    row, col, causal, strict, low_f, eye = _masks(True)
    valid = (row[:, 0:1] & (SEQ_PAD - 1)) < SAMPLE_LEN
    cqkv = jnp.where(valid, _silu(_causal_conv(xq_ref[...], cqw_ref[...], None)), 0.0)
    for b in seqs:
        r0 = SEQ_PAD * b
        ca_ref[b] = ua_ref[r0 + SAMPLE_LEN - 2:r0 + SAMPLE_LEN, :]
        cq_ref[b] = xq_ref[r0 + SAMPLE_LEN - 3:r0 + SAMPLE_LEN, :]

    g_all, beta_all = _gates(ab_ref[...], alog_ref[...], dtb_ref[...])
    g_all = jnp.where(valid, g_all, 0.0)
    beta_all = jnp.where(valid, beta_all, 0.0)
    gc_all = _mm_exact(low_f, g_all)
    last_sel = (col == (row | (SEQ_PAD - 1))).astype(F32)
    gl_all = _mm_exact(last_sel, gc_all)
    z = z_ref[...]
    dnn = dnn_ref[...]
    heads = range(DN_HEADS)
    kn, w, u, qdec, qk, gc_col = _heads_intra(
        cqkv, beta_all, gc_all, causal, strict, eye, 1)
    gl_col = [gl_all[:, h:h + 1] for h in heads]
    kdec = [kn[h] * jnp.exp(gl_col[h] - gc_col[h]) for h in heads]
    s_scale = [jnp.exp(x) for x in gl_col]
    rows = [slice(SEQ_PAD * b, SEQ_PAD * (b + 1)) for b in seqs]
    for h in heads:
        s_old = [si_ref[b, h] for b in seqs]
        r = [_mm(jnp.concatenate([w[h][rows[b]], qdec[h][rows[b]]], axis=0), s_old[b]) for b in seqs]
        v_new = [u[h][rows[b]] - r[b][:SEQ_PAD] for b in seqs]
        upd = [_mm_tn(kdec[h][rows[b]], v_new[b]) for b in seqs]
        for b in seqs:
            so_ref[b, h] = s_old[b] * s_scale[h][SEQ_PAD * b:SEQ_PAD * b + 1, :] + upd[b]
        o = jnp.concatenate([x[SEQ_PAD:] for x in r], axis=0) + _mm(qk[h], jnp.concatenate(v_new, axis=0))
        lo = h * DN_DV
        ybuf_ref[:, D_CONV + lo:D_CONV + lo + DN_DV] = _rms(o, dnn) * _silu(z[:, lo:lo + DN_DV])
    for b in seqs:
        y_ref[SAMPLE_LEN * b:SAMPLE_LEN * (b + 1), :] = ybuf_ref[SEQ_PAD * b:SEQ_PAD * b + SAMPLE_LEN, :]


def _mix_sample(proj, ab, state_a, state_q, state_s, caw, cqw, alog, dtb, dnn):
    n_seq = state_s.shape[0]
    full2 = lambda i: (0, 0)
    seq_rows = lambda width, col: pl.BlockSpec((SEQ_PER_STEP * SAMPLE_LEN, width), lambda i: (i, col))
    return pl.pallas_call(
        _mix_sample_kernel,
        grid=(n_seq // SEQ_PER_STEP,),
        in_specs=[
            seq_rows(3 * D_CONV, 0), seq_rows(DN_QKV, 1), seq_rows(1024, Z_BLOCK), seq_rows(AB_LANES, 0),
            pl.BlockSpec((SEQ_PER_STEP, 2, D_CONV), lambda i: (i, 0, 0)),
            pl.BlockSpec((SEQ_PER_STEP, 3, DN_QKV), lambda i: (i, 0, 0)),
            pl.BlockSpec((SEQ_PER_STEP, DN_HEADS, DN_DK, DN_DV), lambda i: (i, 0, 0, 0)),
            pl.BlockSpec((3, D_CONV), full2),
            pl.BlockSpec((4, DN_QKV), full2),
            pl.BlockSpec((1, AB_LANES), full2),
            pl.BlockSpec((1, AB_LANES), full2),
            pl.BlockSpec((1, DN_DV), full2),
        ],
        out_specs=[
            seq_rows(D_MODEL, 0),
            pl.BlockSpec((SEQ_PER_STEP, 2, D_CONV), lambda i: (i, 0, 0)),
            pl.BlockSpec((SEQ_PER_STEP, 3, DN_QKV), lambda i: (i, 0, 0)),
            pl.BlockSpec((SEQ_PER_STEP, DN_HEADS, DN_DK, DN_DV), lambda i: (i, 0, 0, 0)),
        ],
        out_shape=[
            jax.ShapeDtypeStruct((n_seq * SAMPLE_LEN, D_MODEL), F32),
            jax.ShapeDtypeStruct((n_seq, 2, D_CONV), F32),
            jax.ShapeDtypeStruct((n_seq, 3, DN_QKV), F32),
            jax.ShapeDtypeStruct((n_seq, DN_HEADS, DN_DK, DN_DV), F32),
        ],
        scratch_shapes=[
            pltpu.VMEM((CHUNK, D_CONV), F32),
            pltpu.VMEM((CHUNK, D_CONV), F32),
            pltpu.VMEM((CHUNK, DN_QKV), F32),
            pltpu.VMEM((CHUNK, 1024), F32),
            pltpu.VMEM((CHUNK, AB_LANES), F32),
            pltpu.VMEM((CHUNK, D_MODEL), F32),
        ],
        compiler_params=_params("parallel"),
        name="mix_sample",
    )(proj, proj, proj, ab, state_a, state_q, state_s, caw, cqw, alog, dtb, dnn)


def _outproj_kernel(y_ref, w_ref, post_ref, x_ref, o_ref):
    m = jnp.dot(y_ref[...].astype(BF16), w_ref[...], preferred_element_type=F32)
    o_ref[...] = x_ref[...] + _rms(m, post_ref[...])


def _outproj(y, w, post, x):
    m = x.shape[0]
    return pl.pallas_call(
        _outproj_kernel,
        grid=(m // TM,),
        in_specs=[
            pl.BlockSpec((TM, D_MODEL), lambda i: (i, 0)),
            pl.BlockSpec((D_MODEL, D_MODEL), lambda i: (0, 0)),
            pl.BlockSpec((1, D_MODEL), lambda i: (0, 0)),
            pl.BlockSpec((TM, D_MODEL), lambda i: (i, 0)),
        ],
        out_specs=pl.BlockSpec((TM, D_MODEL), lambda i: (i, 0)),
        out_shape=jax.ShapeDtypeStruct((m, D_MODEL), F32),
        compiler_params=_params("parallel"),
        name="outproj",
    )(y, w, post, x)


def _ple_kernel(x_ref, p_ref, pre_ref, post_ref, wg_ref, wp_ref, o_ref):
    x = x_ref[...]
    h = _rms(x, pre_ref[...]).astype(BF16)
    gate = jax.nn.sigmoid(jnp.dot(h, wg_ref[...], preferred_element_type=F32))
    proj = jnp.dot(p_ref[...].astype(BF16), wp_ref[...], preferred_element_type=F32)
    o_ref[...] = x + _rms(gate * proj, post_ref[...])


def _ple(x, p, pre, post, wg, wp):
    m = x.shape[0]
    return pl.pallas_call(
        _ple_kernel,
        grid=(m // TM,),
        in_specs=[
            pl.BlockSpec((TM, D_MODEL), lambda i: (i, 0)),
            pl.BlockSpec((TM, D_PLE), lambda i: (i, 0)),
            pl.BlockSpec((1, D_MODEL), lambda i: (0, 0)),
            pl.BlockSpec((1, D_MODEL), lambda i: (0, 0)),
            pl.BlockSpec((D_MODEL, D_MODEL), lambda i: (0, 0)),
            pl.BlockSpec((D_PLE, D_MODEL), lambda i: (0, 0)),
        ],
        out_specs=pl.BlockSpec((TM, D_MODEL), lambda i: (i, 0)),
        out_shape=jax.ShapeDtypeStruct((m, D_MODEL), F32),
        compiler_params=_params("parallel"),
        name="ple",
    )(x, p, pre, post, wg, wp)


def _row(v):
    return v.reshape(1, -1).astype(F32)


def _pad_lanes(v):
    return jnp.pad(_row(v), ((0, 0), (0, AB_LANES - v.shape[-1])))


def kernel(x_prompt, x_sample, state_conv_a, state_conv_qkv, state_delta, p_prompt, p_sample,
           f1_pre, f1_post, f1_wg, f1_wu, f1_wd,
           mix_pre, mix_post, w_in, conv_a_w, conv_qkv_w, a_log, dt_bias, dn_norm, w_out,
           f2_pre, f2_post, f2_wg, f2_wu, f2_wd,
           ple_pre, ple_post, w_ple_gate, w_ple_proj):
    depth = f1_pre.shape[0]
    n_p, seq_p, _ = x_prompt.shape
    n_s, seq_s, _ = x_sample.shape
    rows_p = n_p * seq_p
    rows_s = n_s * seq_s
    assert seq_p % CHUNK == 0 and seq_s == SAMPLE_LEN and n_s % SEQ_PER_STEP == 0
    assert rows_p % TM == 0 and rows_p % TM_IN == 0 and rows_s == TM
    assert w_in.shape[-1] == IN_MAIN + N_AB
    xp = x_prompt.reshape(rows_p, D_MODEL)
    xs = x_sample.reshape(rows_s, D_MODEL)
    outs = [[] for _ in range(6)]
    for i in range(depth):
        pre, post = _row(f1_pre[i]), _row(f1_post[i])
        xs, wg, wu, wd = _ffn(xs, pre, post, f1_wg[i], f1_wu[i], f1_wd[i])
        xp = _ffn(xp, pre, post, wg, wu, wd)[0]

        pre = _row(mix_pre[i])
        proj_s, ab_s, wt_bf16, wabt_bf16 = _inproj(xs, pre, w_in[i].T, None, TM)
        proj_p, ab_p, _, _ = _inproj(xp, pre, wt_bf16, wabt_bf16, TM_IN)

        alog, dtb, dnn = _pad_lanes(a_log[i]), _pad_lanes(dt_bias[i]), _row(dn_norm[i])
        y_p, ca_p, cq_p, s_p = _mix_prompt(proj_p, ab_p, n_p, seq_p, conv_a_w[i], conv_qkv_w[i], alog, dtb, dnn)
        y_s, ca_s, cq_s, s_s = _mix_sample(
            proj_s, ab_s, state_conv_a[i], state_conv_qkv[i], state_delta[i],
            conv_a_w[i], conv_qkv_w[i], alog, dtb, dnn)

        post = _row(mix_post[i])
        w_out_bf16 = w_out[i].astype(BF16)
        xs = _outproj(y_s, w_out_bf16, post, xs)
        xp = _outproj(y_p, w_out_bf16, post, xp)

        pre, post = _row(f2_pre[i]), _row(f2_post[i])
        xs, wg, wu, wd = _ffn(xs, pre, post, f2_wg[i], f2_wu[i], f2_wd[i])
        xp = _ffn(xp, pre, post, wg, wu, wd)[0]

        pre, post = _row(ple_pre[i]), _row(ple_post[i])
        wg, wp = w_ple_gate[i].astype(BF16), w_ple_proj[i].astype(BF16)
        xs = _ple(xs, p_sample[i].reshape(rows_s, D_PLE), pre, post, wg, wp)
        xp = _ple(xp, p_prompt[i].reshape(rows_p, D_PLE), pre, post, wg, wp)
        for lst, val in zip(outs, (ca_p, cq_p, s_p, ca_s, cq_s, s_s)):
            lst.append(val)
    return (xp.reshape(n_p, seq_p, D_MODEL), xs.reshape(n_s, seq_s, D_MODEL),
            *[jnp.stack(lst) for lst in outs])
```

```python
import functools

import jax
import jax.numpy as jnp
from jax import lax
from jax.experimental import pallas as pl
from jax.experimental.pallas import tpu as pltpu

F32 = jnp.float32
BF16 = jnp.bfloat16

D_MODEL = 2048
D_CONV = 1024
DN_HEADS = 8
DN_DK = 128
DN_DV = 128
DN_QKV = 3072
D_FF = 5632
D_PLE = 256
EPS = 1e-6
CHUNK = 64
SAMPLE_LEN = 4
SEQ_PAD = 8
SEQ_SHIFT = 3
SEQ_PER_STEP = CHUNK // SEQ_PAD
IN_MAIN = 7168
Z_BLOCK = 6
N_AB = 16
AB_LANES = 128

TM = 512
TF = 512
TF_CAST = 256
TN_IN = 1024
TM_BIG = 1024
ROW_CHUNK = 32
VMEM_LIMIT = 56 * 1024 * 1024


def _rms(x, g):
    ms = jnp.mean(x * x, axis=-1, keepdims=True)
    return x * lax.rsqrt(ms + EPS) * g


def _silu(x):
    return x * jax.nn.sigmoid(x)


def _mm(a, b):
    return jnp.dot(a.astype(BF16), b.astype(BF16), preferred_element_type=F32)


def _mm_nt(a, b):
    return lax.dot_general(a.astype(BF16), b.astype(BF16), (((1,), (1,)), ((), ())),
                           preferred_element_type=F32)


def _mm_tn(a, b):
    return jnp.dot(a.T.astype(BF16), b.astype(BF16), preferred_element_type=F32)


def _mm_exact(a, b):
    return jnp.dot(a, b, precision=lax.Precision.HIGHEST, preferred_element_type=F32)


def _for_row_chunks(n_rows, body):
    def step(k, carry):
        body(pl.ds(pl.multiple_of(k * ROW_CHUNK, ROW_CHUNK), ROW_CHUNK))
        return carry
    lax.fori_loop(0, n_rows // ROW_CHUNK, step, 0, unroll=4)


def _params(*semantics):
    return pltpu.CompilerParams(dimension_semantics=semantics, vmem_limit_bytes=VMEM_LIMIT)


def _ffn_kernel(emit_bf16, x_ref, pre_ref, post_ref, wg_ref, wu_ref, wd_ref, o_ref, *rest):
    h_ref = rest[-1]
    j = pl.program_id(1)

    @pl.when(j == 0)
    def _():
        def norm_rows(rows):
            h_ref[rows, :] = _rms(x_ref[rows, :], pre_ref[...]).astype(BF16)
            o_ref[rows, :] = jnp.zeros((ROW_CHUNK, D_MODEL), F32)
        _for_row_chunks(x_ref.shape[0], norm_rows)

    wg, wu, wd = wg_ref[...], wu_ref[...], wd_ref[...]
    if emit_bf16:
        wg, wu, wd = wg.astype(BF16), wu.astype(BF16), wd.astype(BF16)
        rest[0][...] = wg
        rest[1][...] = wu
        rest[2][...] = wd
    h = h_ref[...]
    g = jnp.dot(h, wg, preferred_element_type=F32)
    u = jnp.dot(h, wu, preferred_element_type=F32)
    a = (_silu(g) * u).astype(BF16)
    o_ref[...] += jnp.dot(a, wd, preferred_element_type=F32)

    @pl.when(j == pl.num_programs(1) - 1)
    def _():
        group = 4 * ROW_CHUNK

        def finish_rows(k, carry):
            r0 = pl.multiple_of(k * group, group)
            chunks = [pl.ds(r0 + s * ROW_CHUNK, ROW_CHUNK) for s in range(group // ROW_CHUNK)]
            scale = [0.5 * lax.rsqrt(jnp.mean(jnp.square(o_ref[c, :]), axis=-1, keepdims=True) + EPS)
                     for c in chunks]
            for c, sc in zip(chunks, scale):
                o_ref[c, :] = x_ref[c, :] + o_ref[c, :] * sc * post_ref[...]
            return carry
        lax.fori_loop(0, x_ref.shape[0] // group, finish_rows, 0)


def _ffn(x, pre, post, wg, wu, wd, tm):
    m = x.shape[0]
    emit_bf16 = wg.dtype != BF16
    tf = TF_CAST if emit_bf16 else TF
    up_spec = pl.BlockSpec((D_MODEL, tf), lambda i, j: (0, j))
    down_spec = pl.BlockSpec((tf, D_MODEL), lambda i, j: (j, 0))
    out_specs = [pl.BlockSpec((tm, D_MODEL), lambda i, j: (i, 0))]
    out_shape = [jax.ShapeDtypeStruct((m, D_MODEL), F32)]
    if emit_bf16:
        assert m == tm, "weights are re-emitted once, by a single-row-tile call"
        out_specs += [up_spec, up_spec, down_spec]
        out_shape += [jax.ShapeDtypeStruct(w.shape, BF16) for w in (wg, wu, wd)]
    outs = pl.pallas_call(
        functools.partial(_ffn_kernel, emit_bf16),
        grid=(m // tm, D_FF // tf),
        in_specs=[
            pl.BlockSpec((tm, D_MODEL), lambda i, j: (i, 0)),
            pl.BlockSpec((1, D_MODEL), lambda i, j: (0, 0)),
            pl.BlockSpec((1, D_MODEL), lambda i, j: (0, 0)),
            up_spec, up_spec, down_spec,
        ],
        out_specs=out_specs,
        out_shape=out_shape,
        scratch_shapes=[pltpu.VMEM((tm, D_MODEL), BF16)],
        compiler_params=_params("parallel", "arbitrary"),
        name="ffn_cast" if emit_bf16 else "ffn",
    )(x, pre, post, wg, wu, wd)
    return outs if emit_bf16 else (outs[0], wg, wu, wd)


def _nt_dot(a, bt):
    return lax.dot_general(a, bt, (((1,), (1,)), ((), ())), preferred_element_type=F32)


def _inproj_kernel(emit_bf16, x_ref, pre_ref, wt_ref, wabt_ref, o_ref, ab_ref, *rest):
    h_ref = rest[-1]

    @pl.when(pl.program_id(1) == 0)
    def _():
        h = _rms(x_ref[...], pre_ref[...]).astype(BF16)
        h_ref[...] = h
        wabt = wabt_ref[...]
        if emit_bf16:
            wabt = jnp.concatenate(
                [wabt.astype(BF16), jnp.zeros((AB_LANES - N_AB, D_MODEL), BF16)], axis=0)
            rest[1][...] = wabt
        ab_ref[...] = _nt_dot(h, wabt)

    wt = wt_ref[...]
    if emit_bf16:
        wt = wt.astype(BF16)
        rest[0][...] = wt
    o_ref[...] = _nt_dot(h_ref[...], wt)


def _inproj(x, pre, wt, wabt, tm):
    m = x.shape[0]
    emit_bf16 = wt.dtype != BF16
    wt_spec = pl.BlockSpec((TN_IN, D_MODEL), lambda i, j: (j, 0))
    if emit_bf16:
        wabt = wt
        wabt_spec = pl.BlockSpec((N_AB, D_MODEL), lambda i, j: (IN_MAIN // N_AB, 0))
    else:
        wabt_spec = pl.BlockSpec((AB_LANES, D_MODEL), lambda i, j: (0, 0))
    out_specs = [pl.BlockSpec((tm, TN_IN), lambda i, j: (i, j)),
                 pl.BlockSpec((tm, AB_LANES), lambda i, j: (i, 0))]
    out_shape = [jax.ShapeDtypeStruct((m, IN_MAIN), F32), jax.ShapeDtypeStruct((m, AB_LANES), F32)]
    if emit_bf16:
        assert m == tm, "weights are re-emitted once, by a single-row-tile call"
        out_specs += [wt_spec, pl.BlockSpec((AB_LANES, D_MODEL), lambda i, j: (0, 0))]
        out_shape += [jax.ShapeDtypeStruct((IN_MAIN, D_MODEL), BF16),
                      jax.ShapeDtypeStruct((AB_LANES, D_MODEL), BF16)]
    outs = pl.pallas_call(
        functools.partial(_inproj_kernel, emit_bf16),
        grid=(m // tm, IN_MAIN // TN_IN),
        in_specs=[
            pl.BlockSpec((tm, D_MODEL), lambda i, j: (i, 0)),
            pl.BlockSpec((1, D_MODEL), lambda i, j: (0, 0)),
            wt_spec,
            wabt_spec,
        ],
        out_specs=out_specs,
        out_shape=out_shape,
        scratch_shapes=[pltpu.VMEM((tm, D_MODEL), BF16)],
        compiler_params=_params("parallel", "arbitrary"),
        name="inproj_cast" if emit_bf16 else "inproj",
    )(x, pre, wt, wabt)
    return outs if emit_bf16 else (outs[0], outs[1], wt, wabt)


def _causal_conv(x, w, carry_ref):
    width = w.shape[0]
    is_row0 = lax.broadcasted_iota(jnp.int32, (8, 1), 0) == 0
    acc = x * w[0:1, :]
    for j in range(1, width):
        delayed = pltpu.roll(acc, 1, axis=0)
        if carry_ref is not None:
            head = jnp.where(is_row0, carry_ref[j - 1:j, :], delayed[:8])
            carry_ref[j - 1:j, :] = acc[CHUNK - 1:CHUNK, :]
            delayed = jnp.concatenate([head, delayed[8:]], axis=0)
        acc = x * w[j:j + 1, :] + delayed
    return acc


def _heads_intra(cqkv, beta_all, gc_all, causal, strict, eye, squarings):
    heads = range(DN_HEADS)
    gc_col = [gc_all[:, h:h + 1] for h in heads]
    gc_t = gc_all.T
    gc_row = [gc_t[h:h + 1, :] for h in heads]
    beta_col = [beta_all[:, DN_HEADS + h:DN_HEADS + h + 1] for h in heads]
    qh = [cqkv[:, h * DN_DK:(h + 1) * DN_DK] for h in heads]
    kh = [cqkv[:, 1024 + h * DN_DK:1024 + (h + 1) * DN_DK] for h in heads]
    vh = [cqkv[:, 2048 + h * DN_DV:2048 + (h + 1) * DN_DV] for h in heads]
    qn = [x * (lax.rsqrt(jnp.sum(x * x, axis=-1, keepdims=True) + EPS) * (DN_DK ** -0.5)) for x in qh]
    kn = [x * lax.rsqrt(jnp.sum(x * x, axis=-1, keepdims=True) + EPS) for x in kh]
    decay = [jnp.exp(jnp.where(causal, gc_col[h] - gc_row[h], -jnp.inf)) for h in heads]
    kb = [kn[h] * beta_col[h] for h in heads]
    gram = [_mm_nt(jnp.concatenate([kb[h], qn[h]], axis=0), kn[h]) for h in heads]
    a_mat = [jnp.where(strict, gram[h][:CHUNK] * decay[h], 0.0) for h in heads]
    qk = [jnp.where(causal, gram[h][CHUNK:] * decay[h], 0.0) for h in heads]
    inv = [eye - a for a in a_mat]
    power = a_mat
    for _ in range(squarings):
        power = [_mm(x, x) for x in power]
        inv = [_mm(inv[h], eye + power[h]) for h in heads]
    egc = [jnp.exp(x) for x in gc_col]
    wu = [_mm(inv[h], jnp.concatenate([kb[h] * egc[h], vh[h] * beta_col[h]], axis=1)) for h in heads]
    w = [x[:, :DN_DK] for x in wu]
    u = [x[:, DN_DK:] for x in wu]
    qdec = [qn[h] * egc[h] for h in heads]
    return kn, w, u, qdec, qk, gc_col


def _gates(ab, alog, dtb):
    x = ab + dtb
    softplus = jnp.maximum(x, 0.0) + jnp.log1p(jnp.exp(-jnp.abs(x)))
    return -jnp.exp(alog) * softplus, jax.nn.sigmoid(ab)


def _masks(same_seq):
    row = lax.broadcasted_iota(jnp.int32, (CHUNK, CHUNK), 0)
    col = lax.broadcasted_iota(jnp.int32, (CHUNK, CHUNK), 1)
    causal = row >= col
    strict = row > col
    if same_seq:
        same = (row >> SEQ_SHIFT) == (col >> SEQ_SHIFT)
        causal = causal & same
        strict = strict & same
    eye = jnp.where(row == col, 1.0, 0.0).astype(F32)
    return row, col, causal, strict, causal.astype(F32), eye


def _mix_prompt_kernel(pa_ref, pq_ref, pz_ref, pab_ref, caw_ref, cqw_ref, alog_ref, dtb_ref, dnn_ref,
                       y_ref, ca_ref, cq_ref, so_ref, carrya_ref, carryq_ref, s_ref):
    c = pl.program_id(1)

    @pl.when(c == 0)
    def _():
        carrya_ref[...] = jnp.zeros_like(carrya_ref)
        carryq_ref[...] = jnp.zeros_like(carryq_ref)
        s_ref[...] = jnp.zeros_like(s_ref)

    pa = pa_ref[...]
    ua = pa[:, D_CONV:2 * D_CONV] * pa[:, 2 * D_CONV:]
    y_ref[:, :D_CONV] = (pa[:, :D_CONV] * _causal_conv(ua, caw_ref[...], carrya_ref)).astype(y_ref.dtype)
    pq = pq_ref[...]
    cqkv = _silu(_causal_conv(pq, cqw_ref[...], carryq_ref))

    _, _, causal, strict, low_f, eye = _masks(False)
    g_all, beta_all = _gates(pab_ref[...], alog_ref[...], dtb_ref[...])
    gc_all = _mm_exact(low_f, g_all)
    z = pz_ref[...]
    dnn = dnn_ref[...]
    heads = range(DN_HEADS)
    kn, w, u, qdec, qk, gc_col = _heads_intra(
        cqkv, beta_all, gc_all, causal, strict, eye, 5)
    s_old = [s_ref[h] for h in heads]
    r = [_mm(jnp.concatenate([w[h], qdec[h]], axis=0), s_old[h]) for h in heads]
    v_new = [u[h] - r[h][:CHUNK] for h in heads]
    g_last = [x[CHUNK - 1:CHUNK, :] for x in gc_col]
    kdec = [kn[h] * jnp.exp(g_last[h] - gc_col[h]) for h in heads]
    s_new = [s_old[h] * jnp.exp(g_last[h]) + _mm_tn(kdec[h], v_new[h]) for h in heads]
    o = [r[h][CHUNK:] + _mm(qk[h], v_new[h]) for h in heads]
    for h in heads:
        s_ref[h] = s_new[h]
        lo = h * DN_DV
        yb = _rms(o[h], dnn) * _silu(z[:, lo:lo + DN_DV])
        y_ref[:, D_CONV + lo:D_CONV + lo + DN_DV] = yb.astype(y_ref.dtype)

    @pl.when(c == pl.num_programs(1) - 1)
    def _():
        ca_ref[0] = ua[CHUNK - 2:CHUNK, :]
        cq_ref[0] = pq[CHUNK - 3:CHUNK, :]
        so_ref[0] = s_ref[...]


def _mix_prompt(proj, ab, n_seq, seq_len, caw, cqw, alog, dtb, dnn):
    n_chunks = seq_len // CHUNK
    row = lambda b, c: b * n_chunks + c
    full2 = lambda b, c: (0, 0)
    return pl.pallas_call(
        _mix_prompt_kernel,
        grid=(n_seq, n_chunks),
        in_specs=[
            pl.BlockSpec((CHUNK, 3 * D_CONV), lambda b, c: (row(b, c), 0)),
            pl.BlockSpec((CHUNK, DN_QKV), lambda b, c: (row(b, c), 1)),
            pl.BlockSpec((CHUNK, 1024), lambda b, c: (row(b, c), Z_BLOCK)),
            pl.BlockSpec((CHUNK, AB_LANES), lambda b, c: (row(b, c), 0)),
            pl.BlockSpec((3, D_CONV), full2),
            pl.BlockSpec((4, DN_QKV), full2),
            pl.BlockSpec((1, AB_LANES), full2),
            pl.BlockSpec((1, AB_LANES), full2),
            pl.BlockSpec((1, DN_DV), full2),
        ],
        out_specs=[
            pl.BlockSpec((CHUNK, D_MODEL), lambda b, c: (row(b, c), 0)),
            pl.BlockSpec((1, 2, D_CONV), lambda b, c: (b, 0, 0)),
            pl.BlockSpec((1, 3, DN_QKV), lambda b, c: (b, 0, 0)),
            pl.BlockSpec((1, DN_HEADS, DN_DK, DN_DV), lambda b, c: (b, 0, 0, 0)),
        ],
        out_shape=[
            jax.ShapeDtypeStruct((n_seq * seq_len, D_MODEL), BF16),
            jax.ShapeDtypeStruct((n_seq, 2, D_CONV), F32),
            jax.ShapeDtypeStruct((n_seq, 3, DN_QKV), F32),
            jax.ShapeDtypeStruct((n_seq, DN_HEADS, DN_DK, DN_DV), F32),
        ],
        scratch_shapes=[
            pltpu.VMEM((8, D_CONV), F32),
            pltpu.VMEM((8, DN_QKV), F32),
            pltpu.VMEM((DN_HEADS, DN_DK, DN_DV), F32),
        ],
        compiler_params=_params("arbitrary", "arbitrary"),
        name="mix_prompt",
    )(proj, proj, proj, ab, caw, cqw, alog, dtb, dnn)


def _mix_sample_kernel(pa_ref, pq_ref, pz_ref, pab_ref, sa_ref, sq_ref, si_ref,
                       caw_ref, cqw_ref, alog_ref, dtb_ref, dnn_ref,
                       y_ref, ca_ref, cq_ref, so_ref,
                       gb_ref, ua_ref, xq_ref, z_ref, ab_ref, ybuf_ref):
    seqs = range(SEQ_PER_STEP)
    pad = SEQ_PAD - SAMPLE_LEN
    for b in seqs:
        r0 = SEQ_PAD * b
        nxt = (b + 1) % SEQ_PER_STEP
        src = slice(SAMPLE_LEN * b, SAMPLE_LEN * (b + 1))
        pa = pa_ref[src, :]
        gb_ref[r0:r0 + SAMPLE_LEN, :] = pa[:, :D_CONV]
        ua_ref[r0:r0 + SAMPLE_LEN, :] = pa[:, D_CONV:2 * D_CONV] * pa[:, 2 * D_CONV:]
        xq_ref[r0:r0 + SAMPLE_LEN, :] = pq_ref[src, :]
        z_ref[r0:r0 + SAMPLE_LEN, :] = pz_ref[src, :]
        ab_ref[r0:r0 + SAMPLE_LEN, :] = pab_ref[src, :]
        for ref in (gb_ref, ua_ref, xq_ref, z_ref, ab_ref):
            ref[r0 + SAMPLE_LEN:r0 + SEQ_PAD, :] = jnp.zeros((pad, ref.shape[1]), F32)
        ua_ref[r0 + SEQ_PAD - 2:r0 + SEQ_PAD, :] = sa_ref[nxt]
        xq_ref[r0 + SEQ_PAD - 3:r0 + SEQ_PAD, :] = sq_ref[nxt]
    ybuf_ref[:, :D_CONV] = gb_ref[...] * _causal_conv(ua_ref[...], caw_ref[...], None)
    row, col, causal, strict, low_f, eye = _masks(True)
    valid = (row[:, 0:1] & (SEQ_PAD - 1)) < SAMPLE_LEN
    cqkv = jnp.where(valid, _silu(_causal_conv(xq_ref[...], cqw_ref[...], None)), 0.0)
    for b in seqs:
        r0 = SEQ_PAD * b
        ca_ref[b] = ua_ref[r0 + SAMPLE_LEN - 2:r0 + SAMPLE_LEN, :]
        cq_ref[b] = xq_ref[r0 + SAMPLE_LEN - 3:r0 + SAMPLE_LEN, :]

    g_all, beta_all = _gates(ab_ref[...], alog_ref[...], dtb_ref[...])
    g_all = jnp.where(valid, g_all, 0.0)
    beta_all = jnp.where(valid, beta_all, 0.0)
    gc_all = _mm_exact(low_f, g_all)
    last_sel = (col == (row | (SEQ_PAD - 1))).astype(F32)
    gl_all = _mm_exact(last_sel, gc_all)
    z = z_ref[...]
    dnn = dnn_ref[...]
    heads = range(DN_HEADS)
    kn, w, u, qdec, qk, gc_col = _heads_intra(
        cqkv, beta_all, gc_all, causal, strict, eye, 1)
    gl_col = [gl_all[:, h:h + 1] for h in heads]
    kdec = [kn[h] * jnp.exp(gl_col[h] - gc_col[h]) for h in heads]
    s_scale = [jnp.exp(x) for x in gl_col]
    rows = [slice(SEQ_PAD * b, SEQ_PAD * (b + 1)) for b in seqs]
    for h in heads:
        s_old = [si_ref[b, h] for b in seqs]
        r = [_mm(jnp.concatenate([w[h][rows[b]], qdec[h][rows[b]]], axis=0), s_old[b]) for b in seqs]
        v_new = [u[h][rows[b]] - r[b][:SEQ_PAD] for b in seqs]
        upd = [_mm_tn(kdec[h][rows[b]], v_new[b]) for b in seqs]
        for b in seqs:
            so_ref[b, h] = s_old[b] * s_scale[h][SEQ_PAD * b:SEQ_PAD * b + 1, :] + upd[b]
        o = jnp.concatenate([x[SEQ_PAD:] for x in r], axis=0) + _mm(qk[h], jnp.concatenate(v_new, axis=0))
        lo = h * DN_DV
        ybuf_ref[:, D_CONV + lo:D_CONV + lo + DN_DV] = _rms(o, dnn) * _silu(z[:, lo:lo + DN_DV])
    for b in seqs:
        y_ref[SAMPLE_LEN * b:SAMPLE_LEN * (b + 1), :] = ybuf_ref[SEQ_PAD * b:SEQ_PAD * b + SAMPLE_LEN, :]


def _mix_sample(proj, ab, state_a, state_q, state_s, caw, cqw, alog, dtb, dnn):
    n_seq = state_s.shape[0]
    full2 = lambda i: (0, 0)
    seq_rows = lambda width, col: pl.BlockSpec((SEQ_PER_STEP * SAMPLE_LEN, width), lambda i: (i, col))
    return pl.pallas_call(
        _mix_sample_kernel,
        grid=(n_seq // SEQ_PER_STEP,),
        in_specs=[
            seq_rows(3 * D_CONV, 0), seq_rows(DN_QKV, 1), seq_rows(1024, Z_BLOCK), seq_rows(AB_LANES, 0),
            pl.BlockSpec((SEQ_PER_STEP, 2, D_CONV), lambda i: (i, 0, 0)),
            pl.BlockSpec((SEQ_PER_STEP, 3, DN_QKV), lambda i: (i, 0, 0)),
            pl.BlockSpec((SEQ_PER_STEP, DN_HEADS, DN_DK, DN_DV), lambda i: (i, 0, 0, 0)),
            pl.BlockSpec((3, D_CONV), full2),
            pl.BlockSpec((4, DN_QKV), full2),
            pl.BlockSpec((1, AB_LANES), full2),
            pl.BlockSpec((1, AB_LANES), full2),
            pl.BlockSpec((1, DN_DV), full2),
        ],
        out_specs=[
            seq_rows(D_MODEL, 0),
            pl.BlockSpec((SEQ_PER_STEP, 2, D_CONV), lambda i: (i, 0, 0)),
            pl.BlockSpec((SEQ_PER_STEP, 3, DN_QKV), lambda i: (i, 0, 0)),
            pl.BlockSpec((SEQ_PER_STEP, DN_HEADS, DN_DK, DN_DV), lambda i: (i, 0, 0, 0)),
        ],
        out_shape=[
            jax.ShapeDtypeStruct((n_seq * SAMPLE_LEN, D_MODEL), F32),
            jax.ShapeDtypeStruct((n_seq, 2, D_CONV), F32),
            jax.ShapeDtypeStruct((n_seq, 3, DN_QKV), F32),
            jax.ShapeDtypeStruct((n_seq, DN_HEADS, DN_DK, DN_DV), F32),
        ],
        scratch_shapes=[
            pltpu.VMEM((CHUNK, D_CONV), F32),
            pltpu.VMEM((CHUNK, D_CONV), F32),
            pltpu.VMEM((CHUNK, DN_QKV), F32),
            pltpu.VMEM((CHUNK, 1024), F32),
            pltpu.VMEM((CHUNK, AB_LANES), F32),
            pltpu.VMEM((CHUNK, D_MODEL), F32),
        ],
        compiler_params=_params("parallel"),
        name="mix_sample",
    )(proj, proj, proj, ab, state_a, state_q, state_s, caw, cqw, alog, dtb, dnn)


def _outproj_kernel(y_ref, w_ref, post_ref, x_ref, o_ref):
    m = jnp.dot(y_ref[...].astype(BF16), w_ref[...], preferred_element_type=F32)
    o_ref[...] = x_ref[...] + _rms(m, post_ref[...])


def _outproj(y, w, post, x):
    m = x.shape[0]
    return pl.pallas_call(
        _outproj_kernel,
        grid=(m // TM,),
        in_specs=[
            pl.BlockSpec((TM, D_MODEL), lambda i: (i, 0)),
            pl.BlockSpec((D_MODEL, D_MODEL), lambda i: (0, 0)),
            pl.BlockSpec((1, D_MODEL), lambda i: (0, 0)),
            pl.BlockSpec((TM, D_MODEL), lambda i: (i, 0)),
        ],
        out_specs=pl.BlockSpec((TM, D_MODEL), lambda i: (i, 0)),
        out_shape=jax.ShapeDtypeStruct((m, D_MODEL), F32),
        compiler_params=_params("parallel"),
        name="outproj",
    )(y, w, post, x)


def _ple_kernel(x_ref, p_ref, pre_ref, post_ref, wg_ref, wp_ref, o_ref):
    x = x_ref[...]
    h = _rms(x, pre_ref[...]).astype(BF16)
    gate = jax.nn.sigmoid(jnp.dot(h, wg_ref[...], preferred_element_type=F32))
    proj = jnp.dot(p_ref[...].astype(BF16), wp_ref[...], preferred_element_type=F32)
    o_ref[...] = x + _rms(gate * proj, post_ref[...])


def _ple(x, p, pre, post, wg, wp):
    m = x.shape[0]
    return pl.pallas_call(
        _ple_kernel,
        grid=(m // TM,),
        in_specs=[
            pl.BlockSpec((TM, D_MODEL), lambda i: (i, 0)),
            pl.BlockSpec((TM, D_PLE), lambda i: (i, 0)),
            pl.BlockSpec((1, D_MODEL), lambda i: (0, 0)),
            pl.BlockSpec((1, D_MODEL), lambda i: (0, 0)),
            pl.BlockSpec((D_MODEL, D_MODEL), lambda i: (0, 0)),
            pl.BlockSpec((D_PLE, D_MODEL), lambda i: (0, 0)),
        ],
        out_specs=pl.BlockSpec((TM, D_MODEL), lambda i: (i, 0)),
        out_shape=jax.ShapeDtypeStruct((m, D_MODEL), F32),
        compiler_params=_params("parallel"),
        name="ple",
    )(x, p, pre, post, wg, wp)


def _row(v):
    return v.reshape(1, -1).astype(F32)


def _pad_lanes(v):
    return jnp.pad(_row(v), ((0, 0), (0, AB_LANES - v.shape[-1])))


def kernel(x_prompt, x_sample, state_conv_a, state_conv_qkv, state_delta, p_prompt, p_sample,
           f1_pre, f1_post, f1_wg, f1_wu, f1_wd,
           mix_pre, mix_post, w_in, conv_a_w, conv_qkv_w, a_log, dt_bias, dn_norm, w_out,
           f2_pre, f2_post, f2_wg, f2_wu, f2_wd,
           ple_pre, ple_post, w_ple_gate, w_ple_proj):
    depth = f1_pre.shape[0]
    n_p, seq_p, _ = x_prompt.shape
    n_s, seq_s, _ = x_sample.shape
    rows_p = n_p * seq_p
    rows_s = n_s * seq_s
    assert seq_p % CHUNK == 0 and seq_s == SAMPLE_LEN and n_s % SEQ_PER_STEP == 0
    assert rows_p % TM == 0 and rows_p % TM_BIG == 0 and rows_s == TM
    assert w_in.shape[-1] == IN_MAIN + N_AB
    xp = x_prompt.reshape(rows_p, D_MODEL)
    xs = x_sample.reshape(rows_s, D_MODEL)
    outs = [[] for _ in range(6)]
    for i in range(depth):
        pre, post = _row(f1_pre[i]), _row(f1_post[i])
        xs, wg, wu, wd = _ffn(xs, pre, post, f1_wg[i], f1_wu[i], f1_wd[i], TM)
        xp = _ffn(xp, pre, post, wg, wu, wd, TM_BIG)[0]

        pre = _row(mix_pre[i])
        proj_s, ab_s, wt_bf16, wabt_bf16 = _inproj(xs, pre, w_in[i].T, None, TM)
        proj_p, ab_p, _, _ = _inproj(xp, pre, wt_bf16, wabt_bf16, TM_BIG)

        alog, dtb, dnn = _pad_lanes(a_log[i]), _pad_lanes(dt_bias[i]), _row(dn_norm[i])
        y_p, ca_p, cq_p, s_p = _mix_prompt(proj_p, ab_p, n_p, seq_p, conv_a_w[i], conv_qkv_w[i], alog, dtb, dnn)
        y_s, ca_s, cq_s, s_s = _mix_sample(
            proj_s, ab_s, state_conv_a[i], state_conv_qkv[i], state_delta[i],
            conv_a_w[i], conv_qkv_w[i], alog, dtb, dnn)

        post = _row(mix_post[i])
        w_out_bf16 = w_out[i].astype(BF16)
        xs = _outproj(y_s, w_out_bf16, post, xs)
        xp = _outproj(y_p, w_out_bf16, post, xp)

        pre, post = _row(f2_pre[i]), _row(f2_post[i])
        xs, wg, wu, wd = _ffn(xs, pre, post, f2_wg[i], f2_wu[i], f2_wd[i], TM)
        xp = _ffn(xp, pre, post, wg, wu, wd, TM_BIG)[0]

        pre, post = _row(ple_pre[i]), _row(ple_post[i])
        wg, wp = w_ple_gate[i].astype(BF16), w_ple_proj[i].astype(BF16)
        xs = _ple(xs, p_sample[i].reshape(rows_s, D_PLE), pre, post, wg, wp)
        xp = _ple(xp, p_prompt[i].reshape(rows_p, D_PLE), pre, post, wg, wp)
        for lst, val in zip(outs, (ca_p, cq_p, s_p, ca_s, cq_s, s_s)):
            lst.append(val)
    return (xp.reshape(n_p, seq_p, D_MODEL), xs.reshape(n_s, seq_s, D_MODEL),
            *[jnp.stack(lst) for lst in outs])
```

```python
import functools

import jax
import jax.numpy as jnp
from jax import lax
from jax.experimental import pallas as pl
from jax.experimental.pallas import tpu as pltpu

F32 = jnp.float32
BF16 = jnp.bfloat16

D_MODEL = 2048
D_CONV = 1024
DN_HEADS = 8
DN_DK = 128
DN_DV = 128
DN_QKV = 3072
D_FF = 5632
D_PLE = 256
EPS = 1e-6
CHUNK = 64
SAMPLE_LEN = 4
SEQ_PAD = 8
SEQ_SHIFT = 3
SEQ_PER_STEP = CHUNK // SEQ_PAD
IN_MAIN = 7168
Z_BLOCK = 6
N_AB = 16
AB_LANES = 128

TM = 512
TF = 512
TF_CAST = 256
TN_IN = 1024
TM_BIG = 1024
TN_A = 512
SUB_COLS = 512
ROW_CHUNK = 32
VMEM_LIMIT = 56 * 1024 * 1024


def _rms(x, g):
    ms = jnp.mean(x * x, axis=-1, keepdims=True)
    return x * lax.rsqrt(ms + EPS) * g


def _silu(x):
    return x * jax.nn.sigmoid(x)


def _mm(a, b):
    return jnp.dot(a.astype(BF16), b.astype(BF16), preferred_element_type=F32)


def _mm_nt(a, b):
    return lax.dot_general(a.astype(BF16), b.astype(BF16), (((1,), (1,)), ((), ())),
                           preferred_element_type=F32)


def _mm_tn(a, b):
    return jnp.dot(a.T.astype(BF16), b.astype(BF16), preferred_element_type=F32)


def _mm_exact(a, b):
    return jnp.dot(a, b, precision=lax.Precision.HIGHEST, preferred_element_type=F32)


def _nt_dot(a, bt):
    return lax.dot_general(a, bt, (((1,), (1,)), ((), ())), preferred_element_type=F32)


def _for_row_chunks(n_rows, body):
    def step(k, carry):
        body(pl.ds(pl.multiple_of(k * ROW_CHUNK, ROW_CHUNK), ROW_CHUNK))
        return carry
    lax.fori_loop(0, n_rows // ROW_CHUNK, step, 0, unroll=4)


def _params(*semantics):
    return pltpu.CompilerParams(dimension_semantics=semantics, vmem_limit_bytes=VMEM_LIMIT)


def _ffn_kernel(emit_bf16, x_ref, pre_ref, post_ref, wg_ref, wu_ref, wd_ref, o_ref, *rest):
    h_ref = rest[-1]
    j = pl.program_id(1)

    @pl.when(j == 0)
    def _():
        def norm_rows(rows):
            h_ref[rows, :] = _rms(x_ref[rows, :], pre_ref[...]).astype(BF16)
            o_ref[rows, :] = jnp.zeros((ROW_CHUNK, D_MODEL), F32)
        _for_row_chunks(x_ref.shape[0], norm_rows)

    wg, wu, wd = wg_ref[...], wu_ref[...], wd_ref[...]
    if emit_bf16:
        wg, wu, wd = wg.astype(BF16), wu.astype(BF16), wd.astype(BF16)
        rest[0][...] = wg
        rest[1][...] = wu
        rest[2][...] = wd
    h = h_ref[...]
    g = jnp.dot(h, wg, preferred_element_type=F32)
    u = jnp.dot(h, wu, preferred_element_type=F32)
    a = (_silu(g) * u).astype(BF16)
    o_ref[...] += jnp.dot(a, wd, preferred_element_type=F32)

    @pl.when(j == pl.num_programs(1) - 1)
    def _():
        group = 4 * ROW_CHUNK

        def finish_rows(k, carry):
            r0 = pl.multiple_of(k * group, group)
            chunks = [pl.ds(r0 + s * ROW_CHUNK, ROW_CHUNK) for s in range(group // ROW_CHUNK)]
            scale = [0.5 * lax.rsqrt(jnp.mean(jnp.square(o_ref[c, :]), axis=-1, keepdims=True) + EPS)
                     for c in chunks]
            for c, sc in zip(chunks, scale):
                o_ref[c, :] = x_ref[c, :] + o_ref[c, :] * sc * post_ref[...]
            return carry
        lax.fori_loop(0, x_ref.shape[0] // group, finish_rows, 0)


def _ffn(x, pre, post, wg, wu, wd, tm):
    m = x.shape[0]
    emit_bf16 = wg.dtype != BF16
    tf = TF_CAST if emit_bf16 else TF
    up_spec = pl.BlockSpec((D_MODEL, tf), lambda i, j: (0, j))
    down_spec = pl.BlockSpec((tf, D_MODEL), lambda i, j: (j, 0))
    out_specs = [pl.BlockSpec((tm, D_MODEL), lambda i, j: (i, 0))]
    out_shape = [jax.ShapeDtypeStruct((m, D_MODEL), F32)]
    if emit_bf16:
        assert m == tm, "weights are re-emitted once, by a single-row-tile call"
        out_specs += [up_spec, up_spec, down_spec]
        out_shape += [jax.ShapeDtypeStruct(w.shape, BF16) for w in (wg, wu, wd)]
    outs = pl.pallas_call(
        functools.partial(_ffn_kernel, emit_bf16),
        grid=(m // tm, D_FF // tf),
        in_specs=[
            pl.BlockSpec((tm, D_MODEL), lambda i, j: (i, 0)),
            pl.BlockSpec((1, D_MODEL), lambda i, j: (0, 0)),
            pl.BlockSpec((1, D_MODEL), lambda i, j: (0, 0)),
            up_spec, up_spec, down_spec,
        ],
        out_specs=out_specs,
        out_shape=out_shape,
        scratch_shapes=[pltpu.VMEM((tm, D_MODEL), BF16)],
        compiler_params=_params("parallel", "arbitrary"),
        name="ffn_cast" if emit_bf16 else "ffn",
    )(x, pre, post, wg, wu, wd)
    return outs if emit_bf16 else (outs[0], wg, wu, wd)


def _inproj_cast_kernel(x_ref, pre_ref, wt_ref, wabt_ref, o_ref, ab_ref, wt16_ref, wabt16_ref, h_ref):
    @pl.when(pl.program_id(1) == 0)
    def _():
        h = _rms(x_ref[...], pre_ref[...]).astype(BF16)
        h_ref[...] = h
        wabt = jnp.concatenate(
            [wabt_ref[...].astype(BF16), jnp.zeros((AB_LANES - N_AB, D_MODEL), BF16)], axis=0)
        wabt16_ref[...] = wabt
        ab_ref[...] = _nt_dot(h, wabt)

    wt = wt_ref[...].astype(BF16)
    wt16_ref[...] = wt
    o_ref[...] = _nt_dot(h_ref[...], wt)


def _inproj_cast(x, pre, wt):
    m = x.shape[0]
    assert m == TM, "weights are re-emitted once, by a single-row-tile call"
    wt_spec = pl.BlockSpec((TN_IN, D_MODEL), lambda i, j: (j, 0))
    return pl.pallas_call(
        _inproj_cast_kernel,
        grid=(1, IN_MAIN // TN_IN),
        in_specs=[
            pl.BlockSpec((TM, D_MODEL), lambda i, j: (0, 0)),
            pl.BlockSpec((1, D_MODEL), lambda i, j: (0, 0)),
            wt_spec,
            pl.BlockSpec((N_AB, D_MODEL), lambda i, j: (IN_MAIN // N_AB, 0)),
        ],
        out_specs=[
            pl.BlockSpec((TM, TN_IN), lambda i, j: (0, j)),
            pl.BlockSpec((TM, AB_LANES), lambda i, j: (0, 0)),
            wt_spec,
            pl.BlockSpec((AB_LANES, D_MODEL), lambda i, j: (0, 0)),
        ],
        out_shape=[
            jax.ShapeDtypeStruct((m, IN_MAIN), F32),
            jax.ShapeDtypeStruct((m, AB_LANES), F32),
            jax.ShapeDtypeStruct((IN_MAIN, D_MODEL), BF16),
            jax.ShapeDtypeStruct((AB_LANES, D_MODEL), BF16),
        ],
        scratch_shapes=[pltpu.VMEM((TM, D_MODEL), BF16)],
        compiler_params=_params("parallel", "arbitrary"),
        name="inproj_cast",
    )(x, pre, wt, wt)


def _causal_conv(x, w, carry_in):
    width = w.shape[0]
    rows = x.shape[0]
    is_row0 = lax.broadcasted_iota(jnp.int32, (8, 1), 0) == 0
    acc = x * w[0:1, :]
    carry_out = []
    for j in range(1, width):
        delayed = pltpu.roll(acc, 1, axis=0)
        if carry_in is not None:
            carry_out.append(acc[rows - 1:rows, :])
            head = jnp.where(is_row0, carry_in[j - 1], delayed[:8])
            delayed = jnp.concatenate([head, delayed[8:]], axis=0)
        acc = x * w[j:j + 1, :] + delayed
    return acc, carry_out


def _load_carry(carry_ref, j, cols, n, seq_start):
    return [jnp.where(seq_start, 0.0, carry_ref[j, r:r + 1, cols]) for r in range(n)]


def _store_carry(carry_ref, j, cols, rows):
    for r, row in enumerate(rows):
        carry_ref[j, r:r + 1, cols] = row


def _proj_a_kernel(tiles_per_seq, x_ref, pre_ref, wgb_ref, wgcv_ref, whc_ref, caw_ref,
                   ya_ref, h_ref, tail_ref, carry_ref):
    i, j = pl.program_id(0), pl.program_id(1)
    tm = x_ref.shape[0]

    @pl.when(j == 0)
    def _():
        def norm_rows(rows):
            h_ref[rows, :] = _rms(x_ref[rows, :], pre_ref[...]).astype(BF16)
        _for_row_chunks(tm, norm_rows)

    @pl.when((i == 0) & (j == 0))
    def _():
        carry_ref[...] = jnp.zeros_like(carry_ref)

    seq_start = i % tiles_per_seq == 0
    h = h_ref[...]
    for c in range(TN_A // SUB_COLS):
        cols = slice(c * SUB_COLS, (c + 1) * SUB_COLS)
        u = _nt_dot(h, wgcv_ref[cols, :]) * _nt_dot(h, whc_ref[cols, :])
        tail_ref[0, :, cols] = u[tm - 8:tm, :]
        conv, carry = _causal_conv(u, caw_ref[:, cols], _load_carry(carry_ref, j, cols, 2, seq_start))
        _store_carry(carry_ref, j, cols, carry)
        ya_ref[:, cols] = (_nt_dot(h, wgb_ref[cols, :]) * conv).astype(BF16)


def _proj_a(x, pre, wt, caw, tm, tiles_per_seq):
    m = x.shape[0]
    n_col = D_CONV // TN_A
    w_spec = lambda part: pl.BlockSpec((TN_A, D_MODEL), lambda i, j: (part * n_col + j, 0))
    return pl.pallas_call(
        functools.partial(_proj_a_kernel, tiles_per_seq),
        grid=(m // tm, n_col),
        in_specs=[
            pl.BlockSpec((tm, D_MODEL), lambda i, j: (i, 0)),
            pl.BlockSpec((1, D_MODEL), lambda i, j: (0, 0)),
            w_spec(0), w_spec(1), w_spec(2),
            pl.BlockSpec((3, TN_A), lambda i, j: (0, j)),
        ],
        out_specs=[
            pl.BlockSpec((tm, TN_A), lambda i, j: (i, j)),
            pl.BlockSpec((tm, D_MODEL), lambda i, j: (i, 0)),
            pl.BlockSpec((1, 8, TN_A), lambda i, j: (i, 0, j)),
        ],
        out_shape=[
            jax.ShapeDtypeStruct((m, D_CONV), BF16),
            jax.ShapeDtypeStruct((m, D_MODEL), BF16),
            jax.ShapeDtypeStruct((m // tm, 8, D_CONV), F32),
        ],
        scratch_shapes=[pltpu.VMEM((n_col, 8, TN_A), F32)],
        compiler_params=_params("arbitrary", "arbitrary"),
        name="proj_a",
    )(x, pre, wt, wt, wt, caw)


def _proj_kernel(conv, norm, with_ab, tiles_per_seq, *refs):
    refs = list(refs)
    h_ref, wt_ref = refs[:2]
    del refs[:2]
    cqw_ref = refs.pop(0) if conv else None
    wabt_ref = refs.pop(0) if with_ab else None
    o_ref = refs.pop(0)
    tail_ref = refs.pop(0) if conv else None
    ab_ref = refs.pop(0) if with_ab else None
    carry_ref = refs.pop(0) if conv else None
    i, j = pl.program_id(0), pl.program_id(1)
    tm = h_ref.shape[0]
    if conv:
        @pl.when((i == 0) & (j == 0))
        def _():
            carry_ref[...] = jnp.zeros_like(carry_ref)

    seq_start = i % tiles_per_seq == 0
    h = h_ref[...]
    if with_ab:
        ab_ref[...] = _nt_dot(h, wabt_ref[...])
    for c in range(TN_IN // SUB_COLS):
        cols = slice(c * SUB_COLS, (c + 1) * SUB_COLS)
        p = _nt_dot(h, wt_ref[cols, :])
        if conv:
            tail_ref[0, :, cols] = p[tm - 8:tm, :]
            p, carry = _causal_conv(p, cqw_ref[:, cols], _load_carry(carry_ref, j, cols, 3, seq_start))
            _store_carry(carry_ref, j, cols, carry)
        p = _silu(p)
        if norm:
            scale = jnp.where(j == 0, DN_DK ** -0.5, 1.0)
            for k in range(SUB_COLS // DN_DK):
                blk = p[:, k * DN_DK:(k + 1) * DN_DK]
                inv = lax.rsqrt(jnp.sum(blk * blk, axis=-1, keepdims=True) + EPS) * scale
                o_ref[:, c * SUB_COLS + k * DN_DK:c * SUB_COLS + (k + 1) * DN_DK] = blk * inv
        else:
            o_ref[:, cols] = p


def _proj(h, wt, first_tile, n_tiles, tm, tiles_per_seq, cqw=None, cqw_tile=0, norm=False, wabt=None):
    m = h.shape[0]
    conv, with_ab = cqw is not None, wabt is not None
    operands = [h, wt]
    in_specs = [pl.BlockSpec((tm, D_MODEL), lambda i, j: (i, 0)),
                pl.BlockSpec((TN_IN, D_MODEL), lambda i, j: (first_tile + j, 0))]
    out_specs = [pl.BlockSpec((tm, TN_IN), lambda i, j: (i, j))]
    out_shape = [jax.ShapeDtypeStruct((m, n_tiles * TN_IN), F32)]
    scratch = []
    if conv:
        operands.append(cqw)
        in_specs.append(pl.BlockSpec((4, TN_IN), lambda i, j: (0, cqw_tile + j)))
        out_specs.append(pl.BlockSpec((1, 8, TN_IN), lambda i, j: (i, 0, j)))
        out_shape.append(jax.ShapeDtypeStruct((m // tm, 8, n_tiles * TN_IN), F32))
        scratch.append(pltpu.VMEM((n_tiles, 8, TN_IN), F32))
    if with_ab:
        assert n_tiles == 1
        operands.append(wabt)
        in_specs.append(pl.BlockSpec((AB_LANES, D_MODEL), lambda i, j: (0, 0)))
        out_specs.append(pl.BlockSpec((tm, AB_LANES), lambda i, j: (i, 0)))
        out_shape.append(jax.ShapeDtypeStruct((m, AB_LANES), F32))
    return pl.pallas_call(
        functools.partial(_proj_kernel, conv, norm, with_ab, tiles_per_seq),
        grid=(m // tm, n_tiles),
        in_specs=in_specs,
        out_specs=out_specs,
        out_shape=out_shape,
        scratch_shapes=scratch,
        compiler_params=_params("arbitrary", "arbitrary"),
        name="proj_" + ("conv" if conv else "plain") + ("_norm" if norm else ""),
    )(*operands)


def _l2_normalize(x, scale):
    return x * (lax.rsqrt(jnp.sum(x * x, axis=-1, keepdims=True) + EPS) * scale)


def _heads_intra(qn, kn, vh, beta_all, gc_all, causal, strict, eye, squarings):
    heads = range(DN_HEADS)
    gc_col = [gc_all[:, h:h + 1] for h in heads]
    gc_t = gc_all.T
    gc_row = [gc_t[h:h + 1, :] for h in heads]
    beta_col = [beta_all[:, DN_HEADS + h:DN_HEADS + h + 1] for h in heads]
    decay = [jnp.exp(jnp.where(causal, gc_col[h] - gc_row[h], -jnp.inf)) for h in heads]
    kb = [kn[h] * beta_col[h] for h in heads]
    gram = [_mm_nt(jnp.concatenate([kb[h], qn[h]], axis=0), kn[h]) for h in heads]
    a_mat = [jnp.where(strict, gram[h][:CHUNK] * decay[h], 0.0) for h in heads]
    qk = [jnp.where(causal, gram[h][CHUNK:] * decay[h], 0.0) for h in heads]
    inv = [eye - a for a in a_mat]
    power = a_mat
    for _ in range(squarings):
        power = [_mm(x, x) for x in power]
        inv = [_mm(inv[h], eye + power[h]) for h in heads]
    egc = [jnp.exp(x) for x in gc_col]
    wu = [_mm(inv[h], jnp.concatenate([kb[h] * egc[h], vh[h] * beta_col[h]], axis=1)) for h in heads]
    w = [x[:, :DN_DK] for x in wu]
    u = [x[:, DN_DK:] for x in wu]
    qdec = [qn[h] * egc[h] for h in heads]
    return w, u, qdec, qk, gc_col


def _gates(ab, alog, dtb):
    x = ab + dtb
    softplus = jnp.maximum(x, 0.0) + jnp.log1p(jnp.exp(-jnp.abs(x)))
    return -jnp.exp(alog) * softplus, jax.nn.sigmoid(ab)


def _masks(same_seq):
    row = lax.broadcasted_iota(jnp.int32, (CHUNK, CHUNK), 0)
    col = lax.broadcasted_iota(jnp.int32, (CHUNK, CHUNK), 1)
    causal = row >= col
    strict = row > col
    if same_seq:
        same = (row >> SEQ_SHIFT) == (col >> SEQ_SHIFT)
        causal = causal & same
        strict = strict & same
    eye = jnp.where(row == col, 1.0, 0.0).astype(F32)
    return row, col, causal, strict, causal.astype(F32), eye


def _mix_prompt_kernel(qk_ref, v_ref, zs_ref, ab_ref, alog_ref, dtb_ref, dnn_ref,
                       y_ref, so_ref, s_ref):
    c = pl.program_id(1)

    @pl.when(c == 0)
    def _():
        s_ref[...] = jnp.zeros_like(s_ref)

    heads = range(DN_HEADS)
    _, _, causal, strict, low_f, eye = _masks(False)
    g_all, beta_all = _gates(ab_ref[...], alog_ref[...], dtb_ref[...])
    gc_all = _mm_exact(low_f, g_all)
    qn = [qk_ref[:, h * DN_DK:(h + 1) * DN_DK] for h in heads]
    kn = [qk_ref[:, DN_HEADS * DN_DK + h * DN_DK:DN_HEADS * DN_DK + (h + 1) * DN_DK] for h in heads]
    vh = [v_ref[:, h * DN_DV:(h + 1) * DN_DV] for h in heads]
    dnn = dnn_ref[...]
    w, u, qdec, qk, gc_col = _heads_intra(qn, kn, vh, beta_all, gc_all, causal, strict, eye, 5)
    s_old = [s_ref[h] for h in heads]
    r = [_mm(jnp.concatenate([w[h], qdec[h]], axis=0), s_old[h]) for h in heads]
    v_new = [u[h] - r[h][:CHUNK] for h in heads]
    g_last = [x[CHUNK - 1:CHUNK, :] for x in gc_col]
    kdec = [kn[h] * jnp.exp(g_last[h] - gc_col[h]) for h in heads]
    s_new = [s_old[h] * jnp.exp(g_last[h]) + _mm_tn(kdec[h], v_new[h]) for h in heads]
    o = [r[h][CHUNK:] + _mm(qk[h], v_new[h]) for h in heads]
    for h in heads:
        s_ref[h] = s_new[h]
        lo = h * DN_DV
        y_ref[:, lo:lo + DN_DV] = (_rms(o[h], dnn) * zs_ref[:, lo:lo + DN_DV]).astype(y_ref.dtype)

    @pl.when(c == pl.num_programs(1) - 1)
    def _():
        so_ref[0] = s_ref[...]


def _mix_prompt(qk, v, zs, ab, n_seq, seq_len, alog, dtb, dnn):
    n_chunks = seq_len // CHUNK
    rows = lambda width: pl.BlockSpec((CHUNK, width), lambda b, c: (b * n_chunks + c, 0))
    full2 = lambda b, c: (0, 0)
    return pl.pallas_call(
        _mix_prompt_kernel,
        grid=(n_seq, n_chunks),
        in_specs=[
            rows(2 * DN_HEADS * DN_DK), rows(DN_HEADS * DN_DV), rows(DN_HEADS * DN_DV), rows(AB_LANES),
            pl.BlockSpec((1, AB_LANES), full2),
            pl.BlockSpec((1, AB_LANES), full2),
            pl.BlockSpec((1, DN_DV), full2),
        ],
        out_specs=[
            rows(DN_HEADS * DN_DV),
            pl.BlockSpec((1, DN_HEADS, DN_DK, DN_DV), lambda b, c: (b, 0, 0, 0)),
        ],
        out_shape=[
            jax.ShapeDtypeStruct((n_seq * seq_len, DN_HEADS * DN_DV), BF16),
            jax.ShapeDtypeStruct((n_seq, DN_HEADS, DN_DK, DN_DV), F32),
        ],
        scratch_shapes=[pltpu.VMEM((DN_HEADS, DN_DK, DN_DV), F32)],
        compiler_params=_params("arbitrary", "arbitrary"),
        name="mix_prompt",
    )(qk, v, zs, ab, alog, dtb, dnn)


def _mix_sample_kernel(pa_ref, pq_ref, pz_ref, pab_ref, sa_ref, sq_ref, si_ref,
                       caw_ref, cqw_ref, alog_ref, dtb_ref, dnn_ref,
                       y_ref, ca_ref, cq_ref, so_ref,
                       gb_ref, ua_ref, xq_ref, z_ref, ab_ref, ybuf_ref):
    seqs = range(SEQ_PER_STEP)
    pad = SEQ_PAD - SAMPLE_LEN
    for b in seqs:
        r0 = SEQ_PAD * b
        nxt = (b + 1) % SEQ_PER_STEP
        src = slice(SAMPLE_LEN * b, SAMPLE_LEN * (b + 1))
        pa = pa_ref[src, :]
        gb_ref[r0:r0 + SAMPLE_LEN, :] = pa[:, :D_CONV]
        ua_ref[r0:r0 + SAMPLE_LEN, :] = pa[:, D_CONV:2 * D_CONV] * pa[:, 2 * D_CONV:]
        xq_ref[r0:r0 + SAMPLE_LEN, :] = pq_ref[src, :]
        z_ref[r0:r0 + SAMPLE_LEN, :] = pz_ref[src, :]
        ab_ref[r0:r0 + SAMPLE_LEN, :] = pab_ref[src, :]
        for ref in (gb_ref, ua_ref, xq_ref, z_ref, ab_ref):
            ref[r0 + SAMPLE_LEN:r0 + SEQ_PAD, :] = jnp.zeros((pad, ref.shape[1]), F32)
        ua_ref[r0 + SEQ_PAD - 2:r0 + SEQ_PAD, :] = sa_ref[nxt]
        xq_ref[r0 + SEQ_PAD - 3:r0 + SEQ_PAD, :] = sq_ref[nxt]
    ybuf_ref[:, :D_CONV] = gb_ref[...] * _causal_conv(ua_ref[...], caw_ref[...], None)[0]
    row, col, causal, strict, low_f, eye = _masks(True)
    valid = (row[:, 0:1] & (SEQ_PAD - 1)) < SAMPLE_LEN
    cqkv = jnp.where(valid, _silu(_causal_conv(xq_ref[...], cqw_ref[...], None)[0]), 0.0)
    for b in seqs:
        r0 = SEQ_PAD * b
        ca_ref[b] = ua_ref[r0 + SAMPLE_LEN - 2:r0 + SAMPLE_LEN, :]
        cq_ref[b] = xq_ref[r0 + SAMPLE_LEN - 3:r0 + SAMPLE_LEN, :]

    g_all, beta_all = _gates(ab_ref[...], alog_ref[...], dtb_ref[...])
    g_all = jnp.where(valid, g_all, 0.0)
    beta_all = jnp.where(valid, beta_all, 0.0)
    gc_all = _mm_exact(low_f, g_all)
    last_sel = (col == (row | (SEQ_PAD - 1))).astype(F32)
    gl_all = _mm_exact(last_sel, gc_all)
    z = z_ref[...]
    dnn = dnn_ref[...]
    heads = range(DN_HEADS)
    qn = [_l2_normalize(cqkv[:, h * DN_DK:(h + 1) * DN_DK], DN_DK ** -0.5) for h in heads]
    kn = [_l2_normalize(cqkv[:, 1024 + h * DN_DK:1024 + (h + 1) * DN_DK], 1.0) for h in heads]
    vh = [cqkv[:, 2048 + h * DN_DV:2048 + (h + 1) * DN_DV] for h in heads]
    w, u, qdec, qk, gc_col = _heads_intra(qn, kn, vh, beta_all, gc_all, causal, strict, eye, 1)
    gl_col = [gl_all[:, h:h + 1] for h in heads]
    kdec = [kn[h] * jnp.exp(gl_col[h] - gc_col[h]) for h in heads]
    s_scale = [jnp.exp(x) for x in gl_col]
    rows = [slice(SEQ_PAD * b, SEQ_PAD * (b + 1)) for b in seqs]
    for h in heads:
        s_old = [si_ref[b, h] for b in seqs]
        r = [_mm(jnp.concatenate([w[h][rows[b]], qdec[h][rows[b]]], axis=0), s_old[b]) for b in seqs]
        v_new = [u[h][rows[b]] - r[b][:SEQ_PAD] for b in seqs]
        upd = [_mm_tn(kdec[h][rows[b]], v_new[b]) for b in seqs]
        for b in seqs:
            so_ref[b, h] = s_old[b] * s_scale[h][SEQ_PAD * b:SEQ_PAD * b + 1, :] + upd[b]
        o = jnp.concatenate([x[SEQ_PAD:] for x in r], axis=0) + _mm(qk[h], jnp.concatenate(v_new, axis=0))
        lo = h * DN_DV
        ybuf_ref[:, D_CONV + lo:D_CONV + lo + DN_DV] = _rms(o, dnn) * _silu(z[:, lo:lo + DN_DV])
    for b in seqs:
        y_ref[SAMPLE_LEN * b:SAMPLE_LEN * (b + 1), :] = ybuf_ref[SEQ_PAD * b:SEQ_PAD * b + SAMPLE_LEN, :]


def _mix_sample(proj, ab, state_a, state_q, state_s, caw, cqw, alog, dtb, dnn):
    n_seq = state_s.shape[0]
    full2 = lambda i: (0, 0)
    seq_rows = lambda width, col: pl.BlockSpec((SEQ_PER_STEP * SAMPLE_LEN, width), lambda i: (i, col))
    return pl.pallas_call(
        _mix_sample_kernel,
        grid=(n_seq // SEQ_PER_STEP,),
        in_specs=[
            seq_rows(3 * D_CONV, 0), seq_rows(DN_QKV, 1), seq_rows(1024, Z_BLOCK), seq_rows(AB_LANES, 0),
            pl.BlockSpec((SEQ_PER_STEP, 2, D_CONV), lambda i: (i, 0, 0)),
            pl.BlockSpec((SEQ_PER_STEP, 3, DN_QKV), lambda i: (i, 0, 0)),
            pl.BlockSpec((SEQ_PER_STEP, DN_HEADS, DN_DK, DN_DV), lambda i: (i, 0, 0, 0)),
            pl.BlockSpec((3, D_CONV), full2),
            pl.BlockSpec((4, DN_QKV), full2),
            pl.BlockSpec((1, AB_LANES), full2),
            pl.BlockSpec((1, AB_LANES), full2),
            pl.BlockSpec((1, DN_DV), full2),
        ],
        out_specs=[
            seq_rows(D_MODEL, 0),
            pl.BlockSpec((SEQ_PER_STEP, 2, D_CONV), lambda i: (i, 0, 0)),
            pl.BlockSpec((SEQ_PER_STEP, 3, DN_QKV), lambda i: (i, 0, 0)),
            pl.BlockSpec((SEQ_PER_STEP, DN_HEADS, DN_DK, DN_DV), lambda i: (i, 0, 0, 0)),
        ],
        out_shape=[
            jax.ShapeDtypeStruct((n_seq * SAMPLE_LEN, D_MODEL), F32),
            jax.ShapeDtypeStruct((n_seq, 2, D_CONV), F32),
            jax.ShapeDtypeStruct((n_seq, 3, DN_QKV), F32),
            jax.ShapeDtypeStruct((n_seq, DN_HEADS, DN_DK, DN_DV), F32),
        ],
        scratch_shapes=[
            pltpu.VMEM((CHUNK, D_CONV), F32),
            pltpu.VMEM((CHUNK, D_CONV), F32),
            pltpu.VMEM((CHUNK, DN_QKV), F32),
            pltpu.VMEM((CHUNK, 1024), F32),
            pltpu.VMEM((CHUNK, AB_LANES), F32),
            pltpu.VMEM((CHUNK, D_MODEL), F32),
        ],
        compiler_params=_params("parallel"),
        name="mix_sample",
    )(proj, proj, proj, ab, state_a, state_q, state_s, caw, cqw, alog, dtb, dnn)


def _outproj_kernel(ya_ref, yb_ref, wa_ref, wb_ref, post_ref, x_ref, o_ref):
    m = (jnp.dot(ya_ref[...].astype(BF16), wa_ref[...], preferred_element_type=F32)
         + jnp.dot(yb_ref[...].astype(BF16), wb_ref[...], preferred_element_type=F32))
    o_ref[...] = x_ref[...] + _rms(m, post_ref[...])


def _outproj(ya, yb, yb_block, w, post, x):
    m = x.shape[0]
    half = D_MODEL // 2
    return pl.pallas_call(
        _outproj_kernel,
        grid=(m // TM,),
        in_specs=[
            pl.BlockSpec((TM, half), lambda i: (i, 0)),
            pl.BlockSpec((TM, half), lambda i: (i, yb_block)),
            pl.BlockSpec((half, D_MODEL), lambda i: (0, 0)),
            pl.BlockSpec((half, D_MODEL), lambda i: (1, 0)),
            pl.BlockSpec((1, D_MODEL), lambda i: (0, 0)),
            pl.BlockSpec((TM, D_MODEL), lambda i: (i, 0)),
        ],
        out_specs=pl.BlockSpec((TM, D_MODEL), lambda i: (i, 0)),
        out_shape=jax.ShapeDtypeStruct((m, D_MODEL), F32),
        compiler_params=_params("parallel"),
        name="outproj",
    )(ya, yb, w, w, post, x)


def _ple_kernel(x_ref, p_ref, pre_ref, post_ref, wg_ref, wp_ref, o_ref):
    x = x_ref[...]
    h = _rms(x, pre_ref[...]).astype(BF16)
    gate = jax.nn.sigmoid(jnp.dot(h, wg_ref[...], preferred_element_type=F32))
    proj = jnp.dot(p_ref[...].astype(BF16), wp_ref[...], preferred_element_type=F32)
    o_ref[...] = x + _rms(gate * proj, post_ref[...])


def _ple(x, p, pre, post, wg, wp):
    m = x.shape[0]
    return pl.pallas_call(
        _ple_kernel,
        grid=(m // TM,),
        in_specs=[
            pl.BlockSpec((TM, D_MODEL), lambda i: (i, 0)),
            pl.BlockSpec((TM, D_PLE), lambda i: (i, 0)),
            pl.BlockSpec((1, D_MODEL), lambda i: (0, 0)),
            pl.BlockSpec((1, D_MODEL), lambda i: (0, 0)),
            pl.BlockSpec((D_MODEL, D_MODEL), lambda i: (0, 0)),
            pl.BlockSpec((D_PLE, D_MODEL), lambda i: (0, 0)),
        ],
        out_specs=pl.BlockSpec((TM, D_MODEL), lambda i: (i, 0)),
        out_shape=jax.ShapeDtypeStruct((m, D_MODEL), F32),
        compiler_params=_params("parallel"),
        name="ple",
    )(x, p, pre, post, wg, wp)


def _row(v):
    return v.reshape(1, -1).astype(F32)


def _pad_lanes(v):
    return jnp.pad(_row(v), ((0, 0), (0, AB_LANES - v.shape[-1])))


def _seq_tails(tails, tiles_per_seq, n_rows):
    return tails[tiles_per_seq - 1::tiles_per_seq, 8 - n_rows:, :]


def kernel(x_prompt, x_sample, state_conv_a, state_conv_qkv, state_delta, p_prompt, p_sample,
           f1_pre, f1_post, f1_wg, f1_wu, f1_wd,
           mix_pre, mix_post, w_in, conv_a_w, conv_qkv_w, a_log, dt_bias, dn_norm, w_out,
           f2_pre, f2_post, f2_wg, f2_wu, f2_wd,
           ple_pre, ple_post, w_ple_gate, w_ple_proj):
    depth = f1_pre.shape[0]
    n_p, seq_p, _ = x_prompt.shape
    n_s, seq_s, _ = x_sample.shape
    rows_p = n_p * seq_p
    rows_s = n_s * seq_s
    assert seq_p % TM_BIG == 0 and seq_s == SAMPLE_LEN and n_s % SEQ_PER_STEP == 0
    assert rows_s == TM
    assert w_in.shape[-1] == IN_MAIN + N_AB
    tiles_per_seq = seq_p // TM_BIG
    xp = x_prompt.reshape(rows_p, D_MODEL)
    xs = x_sample.reshape(rows_s, D_MODEL)
    outs = [[] for _ in range(6)]
    for i in range(depth):
        pre, post = _row(f1_pre[i]), _row(f1_post[i])
        xs, wg, wu, wd = _ffn(xs, pre, post, f1_wg[i], f1_wu[i], f1_wd[i], TM)
        xp = _ffn(xp, pre, post, wg, wu, wd, TM_BIG)[0]

        pre = _row(mix_pre[i])
        caw, cqw = conv_a_w[i], conv_qkv_w[i]
        proj_s, ab_s, wt, wabt = _inproj_cast(xs, pre, w_in[i].T)
        y_a, h, tail_a = _proj_a(xp, pre, wt, caw, TM_BIG, tiles_per_seq)
        qk, tail_qk = _proj(h, wt, 3, 2, TM_BIG, tiles_per_seq, cqw=cqw, cqw_tile=0, norm=True)
        v, tail_v, ab_p = _proj(h, wt, 5, 1, TM_BIG, tiles_per_seq, cqw=cqw, cqw_tile=2, wabt=wabt)
        zs, = _proj(h, wt, 6, 1, TM_BIG, tiles_per_seq)

        alog, dtb, dnn = _pad_lanes(a_log[i]), _pad_lanes(dt_bias[i]), _row(dn_norm[i])
        y_b, s_p = _mix_prompt(qk, v, zs, ab_p, n_p, seq_p, alog, dtb, dnn)
        ca_p = _seq_tails(tail_a, tiles_per_seq, 2)
        cq_p = _seq_tails(jnp.concatenate([tail_qk, tail_v], axis=-1), tiles_per_seq, 3)
        y_s, ca_s, cq_s, s_s = _mix_sample(
            proj_s, ab_s, state_conv_a[i], state_conv_qkv[i], state_delta[i], caw, cqw, alog, dtb, dnn)

        post = _row(mix_post[i])
        w_out_bf16 = w_out[i].astype(BF16)
        xs = _outproj(y_s, y_s, 1, w_out_bf16, post, xs)
        xp = _outproj(y_a, y_b, 0, w_out_bf16, post, xp)

        pre, post = _row(f2_pre[i]), _row(f2_post[i])
        xs, wg, wu, wd = _ffn(xs, pre, post, f2_wg[i], f2_wu[i], f2_wd[i], TM)
        xp = _ffn(xp, pre, post, wg, wu, wd, TM_BIG)[0]

        pre, post = _row(ple_pre[i]), _row(ple_post[i])
        wg, wp = w_ple_gate[i].astype(BF16), w_ple_proj[i].astype(BF16)
        xs = _ple(xs, p_sample[i].reshape(rows_s, D_PLE), pre, post, wg, wp)
        xp = _ple(xp, p_prompt[i].reshape(rows_p, D_PLE), pre, post, wg, wp)
        for lst, val in zip(outs, (ca_p, cq_p, s_p, ca_s, cq_s, s_s)):
            lst.append(val)
    return (xp.reshape(n_p, seq_p, D_MODEL), xs.reshape(n_s, seq_s, D_MODEL),
            *[jnp.stack(lst) for lst in outs])
```

```python
import functools

import jax
import jax.numpy as jnp
from jax import lax
from jax.experimental import pallas as pl
from jax.experimental.pallas import tpu as pltpu

F32 = jnp.float32
BF16 = jnp.bfloat16

D_MODEL = 2048
D_CONV = 1024
DN_HEADS = 8
DN_DK = 128
DN_DV = 128
DN_QKV = 3072
D_FF = 5632
D_PLE = 256
EPS = 1e-6
CHUNK = 64
CHUNKS_PER_STEP = 8
SAMPLE_LEN = 4
SEQ_PAD = 8
SEQ_SHIFT = 3
SEQ_PER_STEP = CHUNK // SEQ_PAD
IN_MAIN = 7168
Z_BLOCK = 6
N_AB = 16
AB_LANES = 128

TM = 512
TF = 512
TF_CAST = 256
TN_IN = 1024
TM_BIG = 1024
TN_A = 512
SUB_COLS = 512
ROW_CHUNK = 32
VMEM_LIMIT = 56 * 1024 * 1024


def _rms(x, g):
    ms = jnp.mean(x * x, axis=-1, keepdims=True)
    return x * lax.rsqrt(ms + EPS) * g


def _silu(x):
    return x * jax.nn.sigmoid(x)


def _mm(a, b):
    return jnp.dot(a.astype(BF16), b.astype(BF16), preferred_element_type=F32)


def _mm_nt(a, b):
    return lax.dot_general(a.astype(BF16), b.astype(BF16), (((1,), (1,)), ((), ())),
                           preferred_element_type=F32)


def _mm_tn(a, b):
    return jnp.dot(a.T.astype(BF16), b.astype(BF16), preferred_element_type=F32)


def _mm_exact(a, b):
    return jnp.dot(a, b, precision=lax.Precision.HIGHEST, preferred_element_type=F32)


def _nt_dot(a, bt):
    return lax.dot_general(a, bt, (((1,), (1,)), ((), ())), preferred_element_type=F32)


def _for_row_chunks(n_rows, body):
    def step(k, carry):
        body(pl.ds(pl.multiple_of(k * ROW_CHUNK, ROW_CHUNK), ROW_CHUNK))
        return carry
    lax.fori_loop(0, n_rows // ROW_CHUNK, step, 0, unroll=4)


def _params(*semantics):
    return pltpu.CompilerParams(dimension_semantics=semantics, vmem_limit_bytes=VMEM_LIMIT)


def _ffn_kernel(emit_bf16, x_ref, pre_ref, post_ref, wg_ref, wu_ref, wd_ref, o_ref, *rest):
    h_ref = rest[-1]
    j = pl.program_id(1)

    @pl.when(j == 0)
    def _():
        def norm_rows(rows):
            h_ref[rows, :] = _rms(x_ref[rows, :], pre_ref[...]).astype(BF16)
            o_ref[rows, :] = jnp.zeros((ROW_CHUNK, D_MODEL), F32)
        _for_row_chunks(x_ref.shape[0], norm_rows)

    wg, wu, wd = wg_ref[...], wu_ref[...], wd_ref[...]
    if emit_bf16:
        wg, wu, wd = wg.astype(BF16), wu.astype(BF16), wd.astype(BF16)
        rest[0][...] = wg
        rest[1][...] = wu
        rest[2][...] = wd
    h = h_ref[...]
    g = jnp.dot(h, wg, preferred_element_type=F32)
    u = jnp.dot(h, wu, preferred_element_type=F32)
    a = (_silu(g) * u).astype(BF16)
    o_ref[...] += jnp.dot(a, wd, preferred_element_type=F32)

    @pl.when(j == pl.num_programs(1) - 1)
    def _():
        group = 4 * ROW_CHUNK

        def finish_rows(k, carry):
            r0 = pl.multiple_of(k * group, group)
            chunks = [pl.ds(r0 + s * ROW_CHUNK, ROW_CHUNK) for s in range(group // ROW_CHUNK)]
            scale = [0.5 * lax.rsqrt(jnp.mean(jnp.square(o_ref[c, :]), axis=-1, keepdims=True) + EPS)
                     for c in chunks]
            for c, sc in zip(chunks, scale):
                o_ref[c, :] = x_ref[c, :] + o_ref[c, :] * sc * post_ref[...]
            return carry
        lax.fori_loop(0, x_ref.shape[0] // group, finish_rows, 0)


def _ffn(x, pre, post, wg, wu, wd, tm):
    m = x.shape[0]
    emit_bf16 = wg.dtype != BF16
    tf = TF_CAST if emit_bf16 else TF
    up_spec = pl.BlockSpec((D_MODEL, tf), lambda i, j: (0, j))
    down_spec = pl.BlockSpec((tf, D_MODEL), lambda i, j: (j, 0))
    out_specs = [pl.BlockSpec((tm, D_MODEL), lambda i, j: (i, 0))]
    out_shape = [jax.ShapeDtypeStruct((m, D_MODEL), F32)]
    if emit_bf16:
        assert m == tm, "weights are re-emitted once, by a single-row-tile call"
        out_specs += [up_spec, up_spec, down_spec]
        out_shape += [jax.ShapeDtypeStruct(w.shape, BF16) for w in (wg, wu, wd)]
    outs = pl.pallas_call(
        functools.partial(_ffn_kernel, emit_bf16),
        grid=(m // tm, D_FF // tf),
        in_specs=[
            pl.BlockSpec((tm, D_MODEL), lambda i, j: (i, 0)),
            pl.BlockSpec((1, D_MODEL), lambda i, j: (0, 0)),
            pl.BlockSpec((1, D_MODEL), lambda i, j: (0, 0)),
            up_spec, up_spec, down_spec,
        ],
        out_specs=out_specs,
        out_shape=out_shape,
        scratch_shapes=[pltpu.VMEM((tm, D_MODEL), BF16)],
        compiler_params=_params("parallel", "arbitrary"),
        name="ffn_cast" if emit_bf16 else "ffn",
    )(x, pre, post, wg, wu, wd)
    return outs if emit_bf16 else (outs[0], wg, wu, wd)


def _inproj_cast_kernel(x_ref, pre_ref, wt_ref, wabt_ref, o_ref, ab_ref, wt16_ref, wabt16_ref, h_ref):
    @pl.when(pl.program_id(1) == 0)
    def _():
        h = _rms(x_ref[...], pre_ref[...]).astype(BF16)
        h_ref[...] = h
        wabt = jnp.concatenate(
            [wabt_ref[...].astype(BF16), jnp.zeros((AB_LANES - N_AB, D_MODEL), BF16)], axis=0)
        wabt16_ref[...] = wabt
        ab_ref[...] = _nt_dot(h, wabt)

    wt = wt_ref[...].astype(BF16)
    wt16_ref[...] = wt
    o_ref[...] = _nt_dot(h_ref[...], wt)


def _inproj_cast(x, pre, wt):
    m = x.shape[0]
    assert m == TM, "weights are re-emitted once, by a single-row-tile call"
    wt_spec = pl.BlockSpec((TN_IN, D_MODEL), lambda i, j: (j, 0))
    return pl.pallas_call(
        _inproj_cast_kernel,
        grid=(1, IN_MAIN // TN_IN),
        in_specs=[
            pl.BlockSpec((TM, D_MODEL), lambda i, j: (0, 0)),
            pl.BlockSpec((1, D_MODEL), lambda i, j: (0, 0)),
            wt_spec,
            pl.BlockSpec((N_AB, D_MODEL), lambda i, j: (IN_MAIN // N_AB, 0)),
        ],
        out_specs=[
            pl.BlockSpec((TM, TN_IN), lambda i, j: (0, j)),
            pl.BlockSpec((TM, AB_LANES), lambda i, j: (0, 0)),
            wt_spec,
            pl.BlockSpec((AB_LANES, D_MODEL), lambda i, j: (0, 0)),
        ],
        out_shape=[
            jax.ShapeDtypeStruct((m, IN_MAIN), F32),
            jax.ShapeDtypeStruct((m, AB_LANES), F32),
            jax.ShapeDtypeStruct((IN_MAIN, D_MODEL), BF16),
            jax.ShapeDtypeStruct((AB_LANES, D_MODEL), BF16),
        ],
        scratch_shapes=[pltpu.VMEM((TM, D_MODEL), BF16)],
        compiler_params=_params("parallel", "arbitrary"),
        name="inproj_cast",
    )(x, pre, wt, wt)


def _causal_conv(x, w, carry_in):
    width = w.shape[0]
    rows = x.shape[0]
    is_row0 = lax.broadcasted_iota(jnp.int32, (8, 1), 0) == 0
    acc = x * w[0:1, :]
    carry_out = []
    for j in range(1, width):
        delayed = pltpu.roll(acc, 1, axis=0)
        if carry_in is not None:
            carry_out.append(acc[rows - 1:rows, :])
            head = jnp.where(is_row0, carry_in[j - 1], delayed[:8])
            delayed = jnp.concatenate([head, delayed[8:]], axis=0)
        acc = x * w[j:j + 1, :] + delayed
    return acc, carry_out


def _load_carry(carry_ref, j, cols, n, seq_start):
    return [jnp.where(seq_start, 0.0, carry_ref[j, r:r + 1, cols]) for r in range(n)]


def _store_carry(carry_ref, j, cols, rows):
    for r, row in enumerate(rows):
        carry_ref[j, r:r + 1, cols] = row


def _proj_a_kernel(tiles_per_seq, x_ref, pre_ref, wgb_ref, wgcv_ref, whc_ref, caw_ref,
                   ya_ref, h_ref, tail_ref, carry_ref):
    i, j = pl.program_id(0), pl.program_id(1)
    tm = x_ref.shape[0]

    @pl.when(j == 0)
    def _():
        def norm_rows(rows):
            h_ref[rows, :] = _rms(x_ref[rows, :], pre_ref[...]).astype(BF16)
        _for_row_chunks(tm, norm_rows)

    @pl.when((i == 0) & (j == 0))
    def _():
        carry_ref[...] = jnp.zeros_like(carry_ref)

    seq_start = i % tiles_per_seq == 0
    h = h_ref[...]
    for c in range(TN_A // SUB_COLS):
        cols = slice(c * SUB_COLS, (c + 1) * SUB_COLS)
        u = _nt_dot(h, wgcv_ref[cols, :]) * _nt_dot(h, whc_ref[cols, :])
        tail_ref[0, :, cols] = u[tm - 8:tm, :]
        conv, carry = _causal_conv(u, caw_ref[:, cols], _load_carry(carry_ref, j, cols, 2, seq_start))
        _store_carry(carry_ref, j, cols, carry)
        ya_ref[:, cols] = (_nt_dot(h, wgb_ref[cols, :]) * conv).astype(BF16)


def _proj_a(x, pre, wt, caw, tm, tiles_per_seq):
    m = x.shape[0]
    n_col = D_CONV // TN_A
    w_spec = lambda part: pl.BlockSpec((TN_A, D_MODEL), lambda i, j: (part * n_col + j, 0))
    return pl.pallas_call(
        functools.partial(_proj_a_kernel, tiles_per_seq),
        grid=(m // tm, n_col),
        in_specs=[
            pl.BlockSpec((tm, D_MODEL), lambda i, j: (i, 0)),
            pl.BlockSpec((1, D_MODEL), lambda i, j: (0, 0)),
            w_spec(0), w_spec(1), w_spec(2),
            pl.BlockSpec((3, TN_A), lambda i, j: (0, j)),
        ],
        out_specs=[
            pl.BlockSpec((tm, TN_A), lambda i, j: (i, j)),
            pl.BlockSpec((tm, D_MODEL), lambda i, j: (i, 0)),
            pl.BlockSpec((1, 8, TN_A), lambda i, j: (i, 0, j)),
        ],
        out_shape=[
            jax.ShapeDtypeStruct((m, D_CONV), BF16),
            jax.ShapeDtypeStruct((m, D_MODEL), BF16),
            jax.ShapeDtypeStruct((m // tm, 8, D_CONV), F32),
        ],
        scratch_shapes=[pltpu.VMEM((n_col, 8, TN_A), F32)],
        compiler_params=_params("arbitrary", "arbitrary"),
        name="proj_a",
    )(x, pre, wt, wt, wt, caw)


def _proj_kernel(conv, norm, with_ab, tiles_per_seq, *refs):
    refs = list(refs)
    h_ref, wt_ref = refs[:2]
    del refs[:2]
    cqw_ref = refs.pop(0) if conv else None
    wabt_ref = refs.pop(0) if with_ab else None
    o_ref = refs.pop(0)
    tail_ref = refs.pop(0) if conv else None
    ab_ref = refs.pop(0) if with_ab else None
    carry_ref = refs.pop(0) if conv else None
    i, j = pl.program_id(0), pl.program_id(1)
    tm = h_ref.shape[0]
    if conv:
        @pl.when((i == 0) & (j == 0))
        def _():
            carry_ref[...] = jnp.zeros_like(carry_ref)

    seq_start = i % tiles_per_seq == 0
    h = h_ref[...]
    if with_ab:
        ab_ref[...] = _nt_dot(h, wabt_ref[...])
    for c in range(TN_IN // SUB_COLS):
        cols = slice(c * SUB_COLS, (c + 1) * SUB_COLS)
        p = _nt_dot(h, wt_ref[cols, :])
        if conv:
            tail_ref[0, :, cols] = p[tm - 8:tm, :]
            p, carry = _causal_conv(p, cqw_ref[:, cols], _load_carry(carry_ref, j, cols, 3, seq_start))
            _store_carry(carry_ref, j, cols, carry)
        p = _silu(p)
        if norm:
            scale = jnp.where(j == 0, DN_DK ** -0.5, 1.0)
            for k in range(SUB_COLS // DN_DK):
                blk = p[:, k * DN_DK:(k + 1) * DN_DK]
                inv = lax.rsqrt(jnp.sum(blk * blk, axis=-1, keepdims=True) + EPS) * scale
                o_ref[:, c * SUB_COLS + k * DN_DK:c * SUB_COLS + (k + 1) * DN_DK] = blk * inv
        else:
            o_ref[:, cols] = p


def _proj(h, wt, first_tile, n_tiles, tm, tiles_per_seq, cqw=None, cqw_tile=0, norm=False, wabt=None):
    m = h.shape[0]
    conv, with_ab = cqw is not None, wabt is not None
    operands = [h, wt]
    in_specs = [pl.BlockSpec((tm, D_MODEL), lambda i, j: (i, 0)),
                pl.BlockSpec((TN_IN, D_MODEL), lambda i, j: (first_tile + j, 0))]
    out_specs = [pl.BlockSpec((tm, TN_IN), lambda i, j: (i, j))]
    out_shape = [jax.ShapeDtypeStruct((m, n_tiles * TN_IN), F32)]
    scratch = []
    if conv:
        operands.append(cqw)
        in_specs.append(pl.BlockSpec((4, TN_IN), lambda i, j: (0, cqw_tile + j)))
        out_specs.append(pl.BlockSpec((1, 8, TN_IN), lambda i, j: (i, 0, j)))
        out_shape.append(jax.ShapeDtypeStruct((m // tm, 8, n_tiles * TN_IN), F32))
        scratch.append(pltpu.VMEM((n_tiles, 8, TN_IN), F32))
    if with_ab:
        assert n_tiles == 1
        operands.append(wabt)
        in_specs.append(pl.BlockSpec((AB_LANES, D_MODEL), lambda i, j: (0, 0)))
        out_specs.append(pl.BlockSpec((tm, AB_LANES), lambda i, j: (i, 0)))
        out_shape.append(jax.ShapeDtypeStruct((m, AB_LANES), F32))
    return pl.pallas_call(
        functools.partial(_proj_kernel, conv, norm, with_ab, tiles_per_seq),
        grid=(m // tm, n_tiles),
        in_specs=in_specs,
        out_specs=out_specs,
        out_shape=out_shape,
        scratch_shapes=scratch,
        compiler_params=_params("arbitrary", "arbitrary"),
        name="proj_" + ("conv" if conv else "plain") + ("_norm" if norm else ""),
    )(*operands)


def _l2_normalize(x, scale):
    return x * (lax.rsqrt(jnp.sum(x * x, axis=-1, keepdims=True) + EPS) * scale)


def _gate_columns(beta_all, gc_all):
    heads = range(DN_HEADS)
    gc_t = gc_all.T
    return ([beta_all[:, DN_HEADS + h:DN_HEADS + h + 1] for h in heads],
            [gc_all[:, h:h + 1] for h in heads], [gc_t[h:h + 1, :] for h in heads])


def _heads_intra(qn, kn, vh, beta_col, gc_col, gc_row, causal, strict, eye, squarings):
    heads = range(len(qn))
    decay = [jnp.exp(jnp.where(causal, gc_col[h] - gc_row[h], -jnp.inf)) for h in heads]
    kb = [kn[h] * beta_col[h] for h in heads]
    gram = [_mm_nt(jnp.concatenate([kb[h], qn[h]], axis=0), kn[h]) for h in heads]
    a_mat = [jnp.where(strict, gram[h][:CHUNK] * decay[h], 0.0) for h in heads]
    qk = [jnp.where(causal, gram[h][CHUNK:] * decay[h], 0.0) for h in heads]
    inv = [eye - a for a in a_mat]
    power = a_mat
    for _ in range(squarings):
        power = [_mm(x, x) for x in power]
        inv = [_mm(inv[h], eye + power[h]) for h in heads]
    egc = [jnp.exp(x) for x in gc_col]
    wu = [_mm(inv[h], jnp.concatenate([kb[h] * egc[h], vh[h] * beta_col[h]], axis=1)) for h in heads]
    w = [x[:, :DN_DK] for x in wu]
    u = [x[:, DN_DK:] for x in wu]
    qdec = [qn[h] * egc[h] for h in heads]
    return w, u, qdec, qk


def _gates(ab, alog, dtb):
    x = ab + dtb
    softplus = jnp.maximum(x, 0.0) + jnp.log1p(jnp.exp(-jnp.abs(x)))
    return -jnp.exp(alog) * softplus, jax.nn.sigmoid(ab)


def _masks(same_seq):
    row = lax.broadcasted_iota(jnp.int32, (CHUNK, CHUNK), 0)
    col = lax.broadcasted_iota(jnp.int32, (CHUNK, CHUNK), 1)
    causal = row >= col
    strict = row > col
    if same_seq:
        same = (row >> SEQ_SHIFT) == (col >> SEQ_SHIFT)
        causal = causal & same
        strict = strict & same
    eye = jnp.where(row == col, 1.0, 0.0).astype(F32)
    return row, col, causal, strict, causal.astype(F32), eye


def _mix_prompt_kernel(qk_ref, v_ref, zs_ref, ab_ref, alog_ref, dtb_ref, dnn_ref,
                       y_ref, so_ref, s_ref):
    c = pl.program_id(1)

    @pl.when(c == 0)
    def _():
        s_ref[...] = jnp.zeros_like(s_ref)

    heads = range(DN_HEADS)
    _, _, causal, strict, low_f, eye = _masks(False)
    g_all, beta_all = _gates(ab_ref[...], alog_ref[...], dtb_ref[...])
    chunk_rows = [slice(g * CHUNK, (g + 1) * CHUNK) for g in range(CHUNKS_PER_STEP)]
    qn, kn, vh, beta_col, gc_col, gc_row = [], [], [], [], [], []
    for rows in chunk_rows:
        cols = _gate_columns(beta_all[rows], _mm_exact(low_f, g_all[rows]))
        beta_col += cols[0]
        gc_col += cols[1]
        gc_row += cols[2]
        qn += [qk_ref[rows, h * DN_DK:(h + 1) * DN_DK] for h in heads]
        kn += [qk_ref[rows, (DN_HEADS + h) * DN_DK:(DN_HEADS + h + 1) * DN_DK] for h in heads]
        vh += [v_ref[rows, h * DN_DV:(h + 1) * DN_DV] for h in heads]
    dnn = dnn_ref[...]
    w, u, qdec, qk = _heads_intra(qn, kn, vh, beta_col, gc_col, gc_row, causal, strict, eye, 5)
    g_last = [x[CHUNK - 1:CHUNK, :] for x in gc_col]
    kdec = [kn[n] * jnp.exp(g_last[n] - gc_col[n]) for n in range(len(kn))]
    s = [s_ref[h] for h in heads]
    for g, rows in enumerate(chunk_rows):
        unit = [g * DN_HEADS + h for h in heads]
        r = [_mm(jnp.concatenate([w[n], qdec[n]], axis=0), s[h]) for h, n in enumerate(unit)]
        v_new = [u[n] - r[h][:CHUNK] for h, n in enumerate(unit)]
        s = [s[h] * jnp.exp(g_last[n]) + _mm_tn(kdec[n], v_new[h]) for h, n in enumerate(unit)]
        o = [r[h][CHUNK:] + _mm(qk[n], v_new[h]) for h, n in enumerate(unit)]
        for h in heads:
            lo = h * DN_DV
            y_ref[rows, lo:lo + DN_DV] = (_rms(o[h], dnn) * zs_ref[rows, lo:lo + DN_DV]).astype(y_ref.dtype)
    for h in heads:
        s_ref[h] = s[h]

    @pl.when(c == pl.num_programs(1) - 1)
    def _():
        so_ref[0] = s_ref[...]


def _mix_prompt(qk, v, zs, ab, n_seq, seq_len, alog, dtb, dnn):
    step_rows = CHUNKS_PER_STEP * CHUNK
    n_steps = seq_len // step_rows
    rows = lambda width: pl.BlockSpec((step_rows, width), lambda b, c: (b * n_steps + c, 0))
    full2 = lambda b, c: (0, 0)
    return pl.pallas_call(
        _mix_prompt_kernel,
        grid=(n_seq, n_steps),
        in_specs=[
            rows(2 * DN_HEADS * DN_DK), rows(DN_HEADS * DN_DV), rows(DN_HEADS * DN_DV), rows(AB_LANES),
            pl.BlockSpec((1, AB_LANES), full2),
            pl.BlockSpec((1, AB_LANES), full2),
            pl.BlockSpec((1, DN_DV), full2),
        ],
        out_specs=[
            rows(DN_HEADS * DN_DV),
            pl.BlockSpec((1, DN_HEADS, DN_DK, DN_DV), lambda b, c: (b, 0, 0, 0)),
        ],
        out_shape=[
            jax.ShapeDtypeStruct((n_seq * seq_len, DN_HEADS * DN_DV), BF16),
            jax.ShapeDtypeStruct((n_seq, DN_HEADS, DN_DK, DN_DV), F32),
        ],
        scratch_shapes=[pltpu.VMEM((DN_HEADS, DN_DK, DN_DV), F32)],
        compiler_params=_params("arbitrary", "arbitrary"),
        name="mix_prompt",
    )(qk, v, zs, ab, alog, dtb, dnn)


def _mix_sample_kernel(pa_ref, pq_ref, pz_ref, pab_ref, sa_ref, sq_ref, si_ref,
                       caw_ref, cqw_ref, alog_ref, dtb_ref, dnn_ref,
                       y_ref, ca_ref, cq_ref, so_ref,
                       gb_ref, ua_ref, xq_ref, z_ref, ab_ref, ybuf_ref):
    seqs = range(SEQ_PER_STEP)
    pad = SEQ_PAD - SAMPLE_LEN
    for b in seqs:
        r0 = SEQ_PAD * b
        nxt = (b + 1) % SEQ_PER_STEP
        src = slice(SAMPLE_LEN * b, SAMPLE_LEN * (b + 1))
        pa = pa_ref[src, :]
        gb_ref[r0:r0 + SAMPLE_LEN, :] = pa[:, :D_CONV]
        ua_ref[r0:r0 + SAMPLE_LEN, :] = pa[:, D_CONV:2 * D_CONV] * pa[:, 2 * D_CONV:]
        xq_ref[r0:r0 + SAMPLE_LEN, :] = pq_ref[src, :]
        z_ref[r0:r0 + SAMPLE_LEN, :] = pz_ref[src, :]
        ab_ref[r0:r0 + SAMPLE_LEN, :] = pab_ref[src, :]
        for ref in (gb_ref, ua_ref, xq_ref, z_ref, ab_ref):
            ref[r0 + SAMPLE_LEN:r0 + SEQ_PAD, :] = jnp.zeros((pad, ref.shape[1]), F32)
        ua_ref[r0 + SEQ_PAD - 2:r0 + SEQ_PAD, :] = sa_ref[nxt]
        xq_ref[r0 + SEQ_PAD - 3:r0 + SEQ_PAD, :] = sq_ref[nxt]
    ybuf_ref[:, :D_CONV] = gb_ref[...] * _causal_conv(ua_ref[...], caw_ref[...], None)[0]
    row, col, causal, strict, low_f, eye = _masks(True)
    valid = (row[:, 0:1] & (SEQ_PAD - 1)) < SAMPLE_LEN
    cqkv = jnp.where(valid, _silu(_causal_conv(xq_ref[...], cqw_ref[...], None)[0]), 0.0)
    for b in seqs:
        r0 = SEQ_PAD * b
        ca_ref[b] = ua_ref[r0 + SAMPLE_LEN - 2:r0 + SAMPLE_LEN, :]
        cq_ref[b] = xq_ref[r0 + SAMPLE_LEN - 3:r0 + SAMPLE_LEN, :]

    g_all, beta_all = _gates(ab_ref[...], alog_ref[...], dtb_ref[...])
    g_all = jnp.where(valid, g_all, 0.0)
    beta_all = jnp.where(valid, beta_all, 0.0)
    gc_all = _mm_exact(low_f, g_all)
    last_sel = (col == (row | (SEQ_PAD - 1))).astype(F32)
    gl_all = _mm_exact(last_sel, gc_all)
    z = z_ref[...]
    dnn = dnn_ref[...]
    heads = range(DN_HEADS)
    qn = [_l2_normalize(cqkv[:, h * DN_DK:(h + 1) * DN_DK], DN_DK ** -0.5) for h in heads]
    kn = [_l2_normalize(cqkv[:, 1024 + h * DN_DK:1024 + (h + 1) * DN_DK], 1.0) for h in heads]
    vh = [cqkv[:, 2048 + h * DN_DV:2048 + (h + 1) * DN_DV] for h in heads]
    beta_col, gc_col, gc_row = _gate_columns(beta_all, gc_all)
    w, u, qdec, qk = _heads_intra(qn, kn, vh, beta_col, gc_col, gc_row, causal, strict, eye, 1)
    gl_col = [gl_all[:, h:h + 1] for h in heads]
    kdec = [kn[h] * jnp.exp(gl_col[h] - gc_col[h]) for h in heads]
    s_scale = [jnp.exp(x) for x in gl_col]
    rows = [slice(SEQ_PAD * b, SEQ_PAD * (b + 1)) for b in seqs]
    for h in heads:
        s_old = [si_ref[b, h] for b in seqs]
        r = [_mm(jnp.concatenate([w[h][rows[b]], qdec[h][rows[b]]], axis=0), s_old[b]) for b in seqs]
        v_new = [u[h][rows[b]] - r[b][:SEQ_PAD] for b in seqs]
        upd = [_mm_tn(kdec[h][rows[b]], v_new[b]) for b in seqs]
        for b in seqs:
            so_ref[b, h] = s_old[b] * s_scale[h][SEQ_PAD * b:SEQ_PAD * b + 1, :] + upd[b]
        o = jnp.concatenate([x[SEQ_PAD:] for x in r], axis=0) + _mm(qk[h], jnp.concatenate(v_new, axis=0))
        lo = h * DN_DV
        ybuf_ref[:, D_CONV + lo:D_CONV + lo + DN_DV] = _rms(o, dnn) * _silu(z[:, lo:lo + DN_DV])
    for b in seqs:
        y_ref[SAMPLE_LEN * b:SAMPLE_LEN * (b + 1), :] = ybuf_ref[SEQ_PAD * b:SEQ_PAD * b + SAMPLE_LEN, :]


def _mix_sample(proj, ab, state_a, state_q, state_s, caw, cqw, alog, dtb, dnn):
    n_seq = state_s.shape[0]
    full2 = lambda i: (0, 0)
    seq_rows = lambda width, col: pl.BlockSpec((SEQ_PER_STEP * SAMPLE_LEN, width), lambda i: (i, col))
    return pl.pallas_call(
        _mix_sample_kernel,
        grid=(n_seq // SEQ_PER_STEP,),
        in_specs=[
            seq_rows(3 * D_CONV, 0), seq_rows(DN_QKV, 1), seq_rows(1024, Z_BLOCK), seq_rows(AB_LANES, 0),
            pl.BlockSpec((SEQ_PER_STEP, 2, D_CONV), lambda i: (i, 0, 0)),
            pl.BlockSpec((SEQ_PER_STEP, 3, DN_QKV), lambda i: (i, 0, 0)),
            pl.BlockSpec((SEQ_PER_STEP, DN_HEADS, DN_DK, DN_DV), lambda i: (i, 0, 0, 0)),
            pl.BlockSpec((3, D_CONV), full2),
            pl.BlockSpec((4, DN_QKV), full2),
            pl.BlockSpec((1, AB_LANES), full2),
            pl.BlockSpec((1, AB_LANES), full2),
            pl.BlockSpec((1, DN_DV), full2),
        ],
        out_specs=[
            seq_rows(D_MODEL, 0),
            pl.BlockSpec((SEQ_PER_STEP, 2, D_CONV), lambda i: (i, 0, 0)),
            pl.BlockSpec((SEQ_PER_STEP, 3, DN_QKV), lambda i: (i, 0, 0)),
            pl.BlockSpec((SEQ_PER_STEP, DN_HEADS, DN_DK, DN_DV), lambda i: (i, 0, 0, 0)),
        ],
        out_shape=[
            jax.ShapeDtypeStruct((n_seq * SAMPLE_LEN, D_MODEL), F32),
            jax.ShapeDtypeStruct((n_seq, 2, D_CONV), F32),
            jax.ShapeDtypeStruct((n_seq, 3, DN_QKV), F32),
            jax.ShapeDtypeStruct((n_seq, DN_HEADS, DN_DK, DN_DV), F32),
        ],
        scratch_shapes=[
            pltpu.VMEM((CHUNK, D_CONV), F32),
            pltpu.VMEM((CHUNK, D_CONV), F32),
            pltpu.VMEM((CHUNK, DN_QKV), F32),
            pltpu.VMEM((CHUNK, 1024), F32),
            pltpu.VMEM((CHUNK, AB_LANES), F32),
            pltpu.VMEM((CHUNK, D_MODEL), F32),
        ],
        compiler_params=_params("parallel"),
        name="mix_sample",
    )(proj, proj, proj, ab, state_a, state_q, state_s, caw, cqw, alog, dtb, dnn)


def _outproj_kernel(ya_ref, yb_ref, wa_ref, wb_ref, post_ref, x_ref, o_ref):
    m = (jnp.dot(ya_ref[...].astype(BF16), wa_ref[...], preferred_element_type=F32)
         + jnp.dot(yb_ref[...].astype(BF16), wb_ref[...], preferred_element_type=F32))
    o_ref[...] = x_ref[...] + _rms(m, post_ref[...])


def _outproj(ya, yb, yb_block, w, post, x):
    m = x.shape[0]
    half = D_MODEL // 2
    return pl.pallas_call(
        _outproj_kernel,
        grid=(m // TM,),
        in_specs=[
            pl.BlockSpec((TM, half), lambda i: (i, 0)),
            pl.BlockSpec((TM, half), lambda i: (i, yb_block)),
            pl.BlockSpec((half, D_MODEL), lambda i: (0, 0)),
            pl.BlockSpec((half, D_MODEL), lambda i: (1, 0)),
            pl.BlockSpec((1, D_MODEL), lambda i: (0, 0)),
            pl.BlockSpec((TM, D_MODEL), lambda i: (i, 0)),
        ],
        out_specs=pl.BlockSpec((TM, D_MODEL), lambda i: (i, 0)),
        out_shape=jax.ShapeDtypeStruct((m, D_MODEL), F32),
        compiler_params=_params("parallel"),
        name="outproj",
    )(ya, yb, w, w, post, x)


def _ple_kernel(x_ref, p_ref, pre_ref, post_ref, wg_ref, wp_ref, o_ref):
    x = x_ref[...]
    h = _rms(x, pre_ref[...]).astype(BF16)
    gate = jax.nn.sigmoid(jnp.dot(h, wg_ref[...], preferred_element_type=F32))
    proj = jnp.dot(p_ref[...].astype(BF16), wp_ref[...], preferred_element_type=F32)
    o_ref[...] = x + _rms(gate * proj, post_ref[...])


def _ple(x, p, pre, post, wg, wp):
    m = x.shape[0]
    return pl.pallas_call(
        _ple_kernel,
        grid=(m // TM,),
        in_specs=[
            pl.BlockSpec((TM, D_MODEL), lambda i: (i, 0)),
            pl.BlockSpec((TM, D_PLE), lambda i: (i, 0)),
            pl.BlockSpec((1, D_MODEL), lambda i: (0, 0)),
            pl.BlockSpec((1, D_MODEL), lambda i: (0, 0)),
            pl.BlockSpec((D_MODEL, D_MODEL), lambda i: (0, 0)),
            pl.BlockSpec((D_PLE, D_MODEL), lambda i: (0, 0)),
        ],
        out_specs=pl.BlockSpec((TM, D_MODEL), lambda i: (i, 0)),
        out_shape=jax.ShapeDtypeStruct((m, D_MODEL), F32),
        compiler_params=_params("parallel"),
        name="ple",
    )(x, p, pre, post, wg, wp)


def _row(v):
    return v.reshape(1, -1).astype(F32)


def _pad_lanes(v):
    return jnp.pad(_row(v), ((0, 0), (0, AB_LANES - v.shape[-1])))


def _seq_tails(tails, tiles_per_seq, n_rows):
    return tails[tiles_per_seq - 1::tiles_per_seq, 8 - n_rows:, :]


def kernel(x_prompt, x_sample, state_conv_a, state_conv_qkv, state_delta, p_prompt, p_sample,
           f1_pre, f1_post, f1_wg, f1_wu, f1_wd,
           mix_pre, mix_post, w_in, conv_a_w, conv_qkv_w, a_log, dt_bias, dn_norm, w_out,
           f2_pre, f2_post, f2_wg, f2_wu, f2_wd,
           ple_pre, ple_post, w_ple_gate, w_ple_proj):
    depth = f1_pre.shape[0]
    n_p, seq_p, _ = x_prompt.shape
    n_s, seq_s, _ = x_sample.shape
    rows_p = n_p * seq_p
    rows_s = n_s * seq_s
    assert seq_p % TM_BIG == 0 and seq_s == SAMPLE_LEN and n_s % SEQ_PER_STEP == 0
    assert rows_s == TM
    assert w_in.shape[-1] == IN_MAIN + N_AB
    tiles_per_seq = seq_p // TM_BIG
    xp = x_prompt.reshape(rows_p, D_MODEL)
    xs = x_sample.reshape(rows_s, D_MODEL)
    outs = [[] for _ in range(6)]
    for i in range(depth):
        pre, post = _row(f1_pre[i]), _row(f1_post[i])
        xs, wg, wu, wd = _ffn(xs, pre, post, f1_wg[i], f1_wu[i], f1_wd[i], TM)
        xp = _ffn(xp, pre, post, wg, wu, wd, TM_BIG)[0]

        pre = _row(mix_pre[i])
        caw, cqw = conv_a_w[i], conv_qkv_w[i]
        proj_s, ab_s, wt, wabt = _inproj_cast(xs, pre, w_in[i].T)
        y_a, h, tail_a = _proj_a(xp, pre, wt, caw, TM_BIG, tiles_per_seq)
        qk, tail_qk = _proj(h, wt, 3, 2, TM_BIG, tiles_per_seq, cqw=cqw, cqw_tile=0, norm=True)
        v, tail_v, ab_p = _proj(h, wt, 5, 1, TM_BIG, tiles_per_seq, cqw=cqw, cqw_tile=2, wabt=wabt)
        zs, = _proj(h, wt, 6, 1, TM_BIG, tiles_per_seq)

        alog, dtb, dnn = _pad_lanes(a_log[i]), _pad_lanes(dt_bias[i]), _row(dn_norm[i])
        y_b, s_p = _mix_prompt(qk, v, zs, ab_p, n_p, seq_p, alog, dtb, dnn)
        ca_p = _seq_tails(tail_a, tiles_per_seq, 2)
        cq_p = _seq_tails(jnp.concatenate([tail_qk, tail_v], axis=-1), tiles_per_seq, 3)
        y_s, ca_s, cq_s, s_s = _mix_sample(
            proj_s, ab_s, state_conv_a[i], state_conv_qkv[i], state_delta[i], caw, cqw, alog, dtb, dnn)

        post = _row(mix_post[i])
        w_out_bf16 = w_out[i].astype(BF16)
        xs = _outproj(y_s, y_s, 1, w_out_bf16, post, xs)
        xp = _outproj(y_a, y_b, 0, w_out_bf16, post, xp)

        pre, post = _row(f2_pre[i]), _row(f2_post[i])
        xs, wg, wu, wd = _ffn(xs, pre, post, f2_wg[i], f2_wu[i], f2_wd[i], TM)
        xp = _ffn(xp, pre, post, wg, wu, wd, TM_BIG)[0]

        pre, post = _row(ple_pre[i]), _row(ple_post[i])
        wg, wp = w_ple_gate[i].astype(BF16), w_ple_proj[i].astype(BF16)
        xs = _ple(xs, p_sample[i].reshape(rows_s, D_PLE), pre, post, wg, wp)
        xp = _ple(xp, p_prompt[i].reshape(rows_p, D_PLE), pre, post, wg, wp)
        for lst, val in zip(outs, (ca_p, cq_p, s_p, ca_s, cq_s, s_s)):
            lst.append(val)
    return (xp.reshape(n_p, seq_p, D_MODEL), xs.reshape(n_s, seq_s, D_MODEL),
            *[jnp.stack(lst) for lst in outs])
```

```python
import functools

import jax
import jax.numpy as jnp
from jax import lax
from jax.experimental import pallas as pl
from jax.experimental.pallas import tpu as pltpu

F32 = jnp.float32
BF16 = jnp.bfloat16

D_MODEL = 2048
D_CONV = 1024
DN_HEADS = 8
DN_DK = 128
DN_DV = 128
DN_QKV = 3072
D_FF = 5632
D_PLE = 256
EPS = 1e-6
CHUNK = 64
CHUNKS_PER_STEP = 8
SAMPLE_LEN = 4
SEQ_PAD = 8
SEQ_SHIFT = 3
SEQ_PER_STEP = CHUNK // SEQ_PAD
IN_MAIN = 7168
Z_BLOCK = 6
N_AB = 16
AB_LANES = 128

TM = 512
TF = 512
TF_CAST = 256
TN_IN = 1024
TM_BIG = 1024
TN_A = 512
SUB_COLS = 512
ROW_CHUNK = 32
VMEM_LIMIT = 56 * 1024 * 1024


def _rms(x, g):
    ms = jnp.mean(x * x, axis=-1, keepdims=True)
    return x * lax.rsqrt(ms + EPS) * g


def _silu(x):
    return x * jax.nn.sigmoid(x)


def _mm(a, b):
    return jnp.dot(a.astype(BF16), b.astype(BF16), preferred_element_type=F32)


def _mm_nt(a, b):
    return lax.dot_general(a.astype(BF16), b.astype(BF16), (((1,), (1,)), ((), ())),
                           preferred_element_type=F32)


def _mm_tn(a, b):
    return jnp.dot(a.T.astype(BF16), b.astype(BF16), preferred_element_type=F32)


def _mm_exact(a, b):
    return jnp.dot(a, b, precision=lax.Precision.HIGHEST, preferred_element_type=F32)


def _nt_dot(a, bt):
    return lax.dot_general(a, bt, (((1,), (1,)), ((), ())), preferred_element_type=F32)


def _for_row_chunks(n_rows, body):
    def step(k, carry):
        body(pl.ds(pl.multiple_of(k * ROW_CHUNK, ROW_CHUNK), ROW_CHUNK))
        return carry
    lax.fori_loop(0, n_rows // ROW_CHUNK, step, 0, unroll=4)


def _params(*semantics):
    return pltpu.CompilerParams(dimension_semantics=semantics, vmem_limit_bytes=VMEM_LIMIT)


def _ffn_kernel(emit_bf16, x_ref, pre_ref, post_ref, wg_ref, wu_ref, wd_ref, o_ref, *rest):
    h_ref = rest[-1]
    j = pl.program_id(1)

    @pl.when(j == 0)
    def _():
        def norm_rows(rows):
            h_ref[rows, :] = _rms(x_ref[rows, :], pre_ref[...]).astype(BF16)
            o_ref[rows, :] = jnp.zeros((ROW_CHUNK, D_MODEL), F32)
        _for_row_chunks(x_ref.shape[0], norm_rows)

    wg, wu, wd = wg_ref[...], wu_ref[...], wd_ref[...]
    if emit_bf16:
        wg, wu, wd = wg.astype(BF16), wu.astype(BF16), wd.astype(BF16)
        rest[0][...] = wg
        rest[1][...] = wu
        rest[2][...] = wd
    h = h_ref[...]
    g = jnp.dot(h, wg, preferred_element_type=F32)
    u = jnp.dot(h, wu, preferred_element_type=F32)
    a = (_silu(g) * u).astype(BF16)
    o_ref[...] += jnp.dot(a, wd, preferred_element_type=F32)

    @pl.when(j == pl.num_programs(1) - 1)
    def _():
        group = 4 * ROW_CHUNK

        def finish_rows(k, carry):
            r0 = pl.multiple_of(k * group, group)
            chunks = [pl.ds(r0 + s * ROW_CHUNK, ROW_CHUNK) for s in range(group // ROW_CHUNK)]
            scale = [0.5 * lax.rsqrt(jnp.mean(jnp.square(o_ref[c, :]), axis=-1, keepdims=True) + EPS)
                     for c in chunks]
            for c, sc in zip(chunks, scale):
                o_ref[c, :] = x_ref[c, :] + o_ref[c, :] * sc * post_ref[...]
            return carry
        lax.fori_loop(0, x_ref.shape[0] // group, finish_rows, 0)


def _ffn(x, pre, post, wg, wu, wd, tm):
    m = x.shape[0]
    emit_bf16 = wg.dtype != BF16
    tf = TF_CAST if emit_bf16 else TF
    up_spec = pl.BlockSpec((D_MODEL, tf), lambda i, j: (0, j))
    down_spec = pl.BlockSpec((tf, D_MODEL), lambda i, j: (j, 0))
    out_specs = [pl.BlockSpec((tm, D_MODEL), lambda i, j: (i, 0))]
    out_shape = [jax.ShapeDtypeStruct((m, D_MODEL), F32)]
    if emit_bf16:
        assert m == tm, "weights are re-emitted once, by a single-row-tile call"
        out_specs += [up_spec, up_spec, down_spec]
        out_shape += [jax.ShapeDtypeStruct(w.shape, BF16) for w in (wg, wu, wd)]
    outs = pl.pallas_call(
        functools.partial(_ffn_kernel, emit_bf16),
        grid=(m // tm, D_FF // tf),
        in_specs=[
            pl.BlockSpec((tm, D_MODEL), lambda i, j: (i, 0)),
            pl.BlockSpec((1, D_MODEL), lambda i, j: (0, 0)),
            pl.BlockSpec((1, D_MODEL), lambda i, j: (0, 0)),
            up_spec, up_spec, down_spec,
        ],
        out_specs=out_specs,
        out_shape=out_shape,
        scratch_shapes=[pltpu.VMEM((tm, D_MODEL), BF16)],
        compiler_params=_params("parallel", "arbitrary"),
        name="ffn_cast" if emit_bf16 else "ffn",
    )(x, pre, post, wg, wu, wd)
    return outs if emit_bf16 else (outs[0], wg, wu, wd)


def _inproj_cast_kernel(x_ref, pre_ref, wt_ref, wabt_ref, o_ref, ab_ref, wt16_ref, wabt16_ref, h_ref):
    @pl.when(pl.program_id(1) == 0)
    def _():
        h = _rms(x_ref[...], pre_ref[...]).astype(BF16)
        h_ref[...] = h
        wabt = jnp.concatenate(
            [wabt_ref[...].astype(BF16), jnp.zeros((AB_LANES - N_AB, D_MODEL), BF16)], axis=0)
        wabt16_ref[...] = wabt
        ab_ref[...] = _nt_dot(h, wabt)

    wt = wt_ref[...].astype(BF16)
    wt16_ref[...] = wt
    o_ref[...] = _nt_dot(h_ref[...], wt)


def _inproj_cast(x, pre, wt):
    m = x.shape[0]
    assert m == TM, "weights are re-emitted once, by a single-row-tile call"
    wt_spec = pl.BlockSpec((TN_IN, D_MODEL), lambda i, j: (j, 0))
    return pl.pallas_call(
        _inproj_cast_kernel,
        grid=(1, IN_MAIN // TN_IN),
        in_specs=[
            pl.BlockSpec((TM, D_MODEL), lambda i, j: (0, 0)),
            pl.BlockSpec((1, D_MODEL), lambda i, j: (0, 0)),
            wt_spec,
            pl.BlockSpec((N_AB, D_MODEL), lambda i, j: (IN_MAIN // N_AB, 0)),
        ],
        out_specs=[
            pl.BlockSpec((TM, TN_IN), lambda i, j: (0, j)),
            pl.BlockSpec((TM, AB_LANES), lambda i, j: (0, 0)),
            wt_spec,
            pl.BlockSpec((AB_LANES, D_MODEL), lambda i, j: (0, 0)),
        ],
        out_shape=[
            jax.ShapeDtypeStruct((m, IN_MAIN), F32),
            jax.ShapeDtypeStruct((m, AB_LANES), F32),
            jax.ShapeDtypeStruct((IN_MAIN, D_MODEL), BF16),
            jax.ShapeDtypeStruct((AB_LANES, D_MODEL), BF16),
        ],
        scratch_shapes=[pltpu.VMEM((TM, D_MODEL), BF16)],
        compiler_params=_params("parallel", "arbitrary"),
        name="inproj_cast",
    )(x, pre, wt, wt)


def _causal_conv(x, w, carry_in):
    width = w.shape[0]
    rows = x.shape[0]
    is_row0 = lax.broadcasted_iota(jnp.int32, (8, 1), 0) == 0
    acc = x * w[0:1, :]
    carry_out = []
    for j in range(1, width):
        delayed = pltpu.roll(acc, 1, axis=0)
        if carry_in is not None:
            carry_out.append(acc[rows - 1:rows, :])
            head = jnp.where(is_row0, carry_in[j - 1], delayed[:8])
            delayed = jnp.concatenate([head, delayed[8:]], axis=0)
        acc = x * w[j:j + 1, :] + delayed
    return acc, carry_out


def _load_carry(carry_ref, j, cols, n, seq_start):
    return [jnp.where(seq_start, 0.0, carry_ref[j, r:r + 1, cols]) for r in range(n)]


def _store_carry(carry_ref, j, cols, rows):
    for r, row in enumerate(rows):
        carry_ref[j, r:r + 1, cols] = row


def _proj_a_kernel(tiles_per_seq, x_ref, pre_ref, wgb_ref, wgcv_ref, whc_ref, caw_ref,
                   ya_ref, h_ref, tail_ref, carry_ref):
    i, j = pl.program_id(0), pl.program_id(1)
    tm = x_ref.shape[0]

    @pl.when(j == 0)
    def _():
        def norm_rows(rows):
            h_ref[rows, :] = _rms(x_ref[rows, :], pre_ref[...]).astype(BF16)
        _for_row_chunks(tm, norm_rows)

    @pl.when((i == 0) & (j == 0))
    def _():
        carry_ref[...] = jnp.zeros_like(carry_ref)

    seq_start = i % tiles_per_seq == 0
    h = h_ref[...]
    for c in range(TN_A // SUB_COLS):
        cols = slice(c * SUB_COLS, (c + 1) * SUB_COLS)
        u = _nt_dot(h, wgcv_ref[cols, :]) * _nt_dot(h, whc_ref[cols, :])
        tail_ref[0, :, cols] = u[tm - 8:tm, :]
        conv, carry = _causal_conv(u, caw_ref[:, cols], _load_carry(carry_ref, j, cols, 2, seq_start))
        _store_carry(carry_ref, j, cols, carry)
        ya_ref[:, cols] = (_nt_dot(h, wgb_ref[cols, :]) * conv).astype(BF16)


def _proj_a(x, pre, wt, caw, tm, tiles_per_seq):
    m = x.shape[0]
    n_col = D_CONV // TN_A
    w_spec = lambda part: pl.BlockSpec((TN_A, D_MODEL), lambda i, j: (part * n_col + j, 0))
    return pl.pallas_call(
        functools.partial(_proj_a_kernel, tiles_per_seq),
        grid=(m // tm, n_col),
        in_specs=[
            pl.BlockSpec((tm, D_MODEL), lambda i, j: (i, 0)),
            pl.BlockSpec((1, D_MODEL), lambda i, j: (0, 0)),
            w_spec(0), w_spec(1), w_spec(2),
            pl.BlockSpec((3, TN_A), lambda i, j: (0, j)),
        ],
        out_specs=[
            pl.BlockSpec((tm, TN_A), lambda i, j: (i, j)),
            pl.BlockSpec((tm, D_MODEL), lambda i, j: (i, 0)),
            pl.BlockSpec((1, 8, TN_A), lambda i, j: (i, 0, j)),
        ],
        out_shape=[
            jax.ShapeDtypeStruct((m, D_CONV), BF16),
            jax.ShapeDtypeStruct((m, D_MODEL), BF16),
            jax.ShapeDtypeStruct((m // tm, 8, D_CONV), F32),
        ],
        scratch_shapes=[pltpu.VMEM((n_col, 8, TN_A), F32)],
        compiler_params=_params("arbitrary", "arbitrary"),
        name="proj_a",
    )(x, pre, wt, wt, wt, caw)


def _proj_kernel(conv, norm, with_ab, tiles_per_seq, *refs):
    refs = list(refs)
    h_ref, wt_ref = refs[:2]
    del refs[:2]
    cqw_ref = refs.pop(0) if conv else None
    wabt_ref = refs.pop(0) if with_ab else None
    o_ref = refs.pop(0)
    tail_ref = refs.pop(0) if conv else None
    ab_ref = refs.pop(0) if with_ab else None
    carry_ref = refs.pop(0) if conv else None
    i, j = pl.program_id(0), pl.program_id(1)
    tm = h_ref.shape[0]
    if conv:
        @pl.when((i == 0) & (j == 0))
        def _():
            carry_ref[...] = jnp.zeros_like(carry_ref)

    seq_start = i % tiles_per_seq == 0
    h = h_ref[...]
    if with_ab:
        ab_ref[...] = _nt_dot(h, wabt_ref[...])
    for c in range(TN_IN // SUB_COLS):
        cols = slice(c * SUB_COLS, (c + 1) * SUB_COLS)
        p = _nt_dot(h, wt_ref[cols, :])
        if conv:
            tail_ref[0, :, cols] = p[tm - 8:tm, :]
            p, carry = _causal_conv(p, cqw_ref[:, cols], _load_carry(carry_ref, j, cols, 3, seq_start))
            _store_carry(carry_ref, j, cols, carry)
        p = _silu(p)
        if norm:
            scale = jnp.where(j == 0, DN_DK ** -0.5, 1.0)
            for k in range(SUB_COLS // DN_DK):
                blk = p[:, k * DN_DK:(k + 1) * DN_DK]
                inv = lax.rsqrt(jnp.sum(blk * blk, axis=-1, keepdims=True) + EPS) * scale
                o_ref[:, c * SUB_COLS + k * DN_DK:c * SUB_COLS + (k + 1) * DN_DK] = blk * inv
        else:
            o_ref[:, cols] = p


def _proj(h, wt, first_tile, n_tiles, tm, tiles_per_seq, cqw=None, cqw_tile=0, norm=False, wabt=None):
    m = h.shape[0]
    conv, with_ab = cqw is not None, wabt is not None
    operands = [h, wt]
    in_specs = [pl.BlockSpec((tm, D_MODEL), lambda i, j: (i, 0)),
                pl.BlockSpec((TN_IN, D_MODEL), lambda i, j: (first_tile + j, 0))]
    out_specs = [pl.BlockSpec((tm, TN_IN), lambda i, j: (i, j))]
    out_shape = [jax.ShapeDtypeStruct((m, n_tiles * TN_IN), F32)]
    scratch = []
    if conv:
        operands.append(cqw)
        in_specs.append(pl.BlockSpec((4, TN_IN), lambda i, j: (0, cqw_tile + j)))
        out_specs.append(pl.BlockSpec((1, 8, TN_IN), lambda i, j: (i, 0, j)))
        out_shape.append(jax.ShapeDtypeStruct((m // tm, 8, n_tiles * TN_IN), F32))
        scratch.append(pltpu.VMEM((n_tiles, 8, TN_IN), F32))
    if with_ab:
        assert n_tiles == 1
        operands.append(wabt)
        in_specs.append(pl.BlockSpec((AB_LANES, D_MODEL), lambda i, j: (0, 0)))
        out_specs.append(pl.BlockSpec((tm, AB_LANES), lambda i, j: (i, 0)))
        out_shape.append(jax.ShapeDtypeStruct((m, AB_LANES), F32))
    return pl.pallas_call(
        functools.partial(_proj_kernel, conv, norm, with_ab, tiles_per_seq),
        grid=(m // tm, n_tiles),
        in_specs=in_specs,
        out_specs=out_specs,
        out_shape=out_shape,
        scratch_shapes=scratch,
        compiler_params=_params("arbitrary", "arbitrary"),
        name="proj_" + ("conv" if conv else "plain") + ("_norm" if norm else ""),
    )(*operands)


def _l2_normalize(x, scale):
    return x * (lax.rsqrt(jnp.sum(x * x, axis=-1, keepdims=True) + EPS) * scale)


def _gate_columns(beta_all, gc_all):
    heads = range(DN_HEADS)
    gc_t = gc_all.T
    return ([beta_all[:, DN_HEADS + h:DN_HEADS + h + 1] for h in heads],
            [gc_all[:, h:h + 1] for h in heads], [gc_t[h:h + 1, :] for h in heads])


def _heads_intra(qn, kn, vh, beta_col, gc_col, gc_row, causal, strict, eye, squarings):
    heads = range(len(qn))
    decay = [jnp.exp(jnp.where(causal, gc_col[h] - gc_row[h], -jnp.inf)) for h in heads]
    kb = [kn[h] * beta_col[h] for h in heads]
    gram = [_mm_nt(jnp.concatenate([kb[h], qn[h]], axis=0), kn[h]) for h in heads]
    a_mat = [jnp.where(strict, gram[h][:CHUNK] * decay[h], 0.0) for h in heads]
    qk = [jnp.where(causal, gram[h][CHUNK:] * decay[h], 0.0) for h in heads]
    inv = [eye - a for a in a_mat]
    power = a_mat
    for _ in range(squarings):
        power = [_mm(x, x) for x in power]
        inv = [_mm(inv[h], eye + power[h]) for h in heads]
    egc = [jnp.exp(x) for x in gc_col]
    wu = [_mm(inv[h], jnp.concatenate([kb[h] * egc[h], vh[h] * beta_col[h]], axis=1)) for h in heads]
    w = [x[:, :DN_DK] for x in wu]
    u = [x[:, DN_DK:] for x in wu]
    qdec = [qn[h] * egc[h] for h in heads]
    return w, u, qdec, qk


def _gates(ab, alog, dtb):
    x = ab + dtb
    softplus = jnp.maximum(x, 0.0) + jnp.log1p(jnp.exp(-jnp.abs(x)))
    return -jnp.exp(alog) * softplus, jax.nn.sigmoid(ab)


def _masks(same_seq):
    row = lax.broadcasted_iota(jnp.int32, (CHUNK, CHUNK), 0)
    col = lax.broadcasted_iota(jnp.int32, (CHUNK, CHUNK), 1)
    causal = row >= col
    strict = row > col
    if same_seq:
        same = (row >> SEQ_SHIFT) == (col >> SEQ_SHIFT)
        causal = causal & same
        strict = strict & same
    eye = jnp.where(row == col, 1.0, 0.0).astype(F32)
    return row, col, causal, strict, causal.astype(F32), eye


def _mix_prompt_kernel(qk_ref, v_ref, zs_ref, ab_ref, alog_ref, dtb_ref, dnn_ref,
                       y_ref, so_ref, s_ref):
    c = pl.program_id(1)

    @pl.when(c == 0)
    def _():
        s_ref[...] = jnp.zeros_like(s_ref)

    heads = range(DN_HEADS)
    _, _, causal, strict, low_f, eye = _masks(False)
    g_all, beta_all = _gates(ab_ref[...], alog_ref[...], dtb_ref[...])
    chunk_rows = [slice(g * CHUNK, (g + 1) * CHUNK) for g in range(CHUNKS_PER_STEP)]
    qn, kn, vh, beta_col, gc_col, gc_row = [], [], [], [], [], []
    for rows in chunk_rows:
        cols = _gate_columns(beta_all[rows], _mm_exact(low_f, g_all[rows]))
        beta_col += cols[0]
        gc_col += cols[1]
        gc_row += cols[2]
        qn += [qk_ref[rows, h * DN_DK:(h + 1) * DN_DK] for h in heads]
        kn += [qk_ref[rows, (DN_HEADS + h) * DN_DK:(DN_HEADS + h + 1) * DN_DK] for h in heads]
        vh += [v_ref[rows, h * DN_DV:(h + 1) * DN_DV] for h in heads]
    dnn = dnn_ref[...]
    w, u, qdec, qk = _heads_intra(qn, kn, vh, beta_col, gc_col, gc_row, causal, strict, eye, 5)
    g_last = [x[CHUNK - 1:CHUNK, :] for x in gc_col]
    kdec = [kn[n] * jnp.exp(g_last[n] - gc_col[n]) for n in range(len(kn))]
    s = [s_ref[h] for h in heads]
    for g, rows in enumerate(chunk_rows):
        unit = [g * DN_HEADS + h for h in heads]
        r = [_mm(jnp.concatenate([w[n], qdec[n]], axis=0), s[h]) for h, n in enumerate(unit)]
        v_new = [u[n] - r[h][:CHUNK] for h, n in enumerate(unit)]
        s = [s[h] * jnp.exp(g_last[n]) + _mm_tn(kdec[n], v_new[h]) for h, n in enumerate(unit)]
        o = [r[h][CHUNK:] + _mm(qk[n], v_new[h]) for h, n in enumerate(unit)]
        for h in heads:
            lo = h * DN_DV
            y_ref[rows, lo:lo + DN_DV] = (_rms(o[h], dnn) * zs_ref[rows, lo:lo + DN_DV]).astype(y_ref.dtype)
    for h in heads:
        s_ref[h] = s[h]

    @pl.when(c == pl.num_programs(1) - 1)
    def _():
        so_ref[0] = s_ref[...]


def _mix_prompt(qk, v, zs, ab, n_seq, seq_len, alog, dtb, dnn):
    step_rows = CHUNKS_PER_STEP * CHUNK
    n_steps = seq_len // step_rows
    rows = lambda width: pl.BlockSpec((step_rows, width), lambda b, c: (b * n_steps + c, 0))
    full2 = lambda b, c: (0, 0)
    return pl.pallas_call(
        _mix_prompt_kernel,
        grid=(n_seq, n_steps),
        in_specs=[
            rows(2 * DN_HEADS * DN_DK), rows(DN_HEADS * DN_DV), rows(DN_HEADS * DN_DV), rows(AB_LANES),
            pl.BlockSpec((1, AB_LANES), full2),
            pl.BlockSpec((1, AB_LANES), full2),
            pl.BlockSpec((1, DN_DV), full2),
        ],
        out_specs=[
            rows(DN_HEADS * DN_DV),
            pl.BlockSpec((1, DN_HEADS, DN_DK, DN_DV), lambda b, c: (b, 0, 0, 0)),
        ],
        out_shape=[
            jax.ShapeDtypeStruct((n_seq * seq_len, DN_HEADS * DN_DV), BF16),
            jax.ShapeDtypeStruct((n_seq, DN_HEADS, DN_DK, DN_DV), F32),
        ],
        scratch_shapes=[pltpu.VMEM((DN_HEADS, DN_DK, DN_DV), F32)],
        compiler_params=_params("arbitrary", "arbitrary"),
        name="mix_prompt",
    )(qk, v, zs, ab, alog, dtb, dnn)


def _proj_qkv_kernel(tiles_per_seq, h_ref, wq_ref, wk_ref, wv_ref, wabt_ref, cqw_ref, alog_ref, dtb_ref,
                     w_ref, qd_ref, kd_ref, u_ref, qk_ref, egl_ref, tail_ref, carry_ref, qkv_ref):
    i = pl.program_id(0)
    tm = h_ref.shape[0]

    @pl.when(i == 0)
    def _():
        carry_ref[...] = jnp.zeros_like(carry_ref)

    seq_start = i % tiles_per_seq == 0
    h = h_ref[...]
    g_all, beta_all = _gates(_nt_dot(h, wabt_ref[...]), alog_ref[...], dtb_ref[...])
    for t, wt_ref in enumerate((wq_ref, wk_ref, wv_ref)):
        for c in range(TN_IN // SUB_COLS):
            cols = slice(c * SUB_COLS, (c + 1) * SUB_COLS)
            lo = t * TN_IN + c * SUB_COLS
            p = _nt_dot(h, wt_ref[cols, :])
            tail_ref[0, :, lo:lo + SUB_COLS] = p[tm - 8:tm, :]
            p, carry = _causal_conv(p, cqw_ref[:, lo:lo + SUB_COLS],
                                    _load_carry(carry_ref, t, cols, 3, seq_start))
            _store_carry(carry_ref, t, cols, carry)
            p = _silu(p)
            if t == 2:
                qkv_ref[:, lo:lo + SUB_COLS] = p
            else:
                for k in range(SUB_COLS // DN_DK):
                    blk = p[:, k * DN_DK:(k + 1) * DN_DK]
                    qkv_ref[:, lo + k * DN_DK:lo + (k + 1) * DN_DK] = _l2_normalize(
                        blk, DN_DK ** -0.5 if t == 0 else 1.0)

    heads = range(DN_HEADS)
    _, _, causal, strict, low_f, eye = _masks(False)
    chunk_rows = [slice(g * CHUNK, (g + 1) * CHUNK) for g in range(tm // CHUNK)]
    qn, kn, vh, beta_col, gc_col, gc_row = [], [], [], [], [], []
    for rows in chunk_rows:
        gc_all = _mm_exact(low_f, g_all[rows])
        egl_ref[rows, :] = jnp.broadcast_to(jnp.exp(gc_all[CHUNK - 1:CHUNK, :]), (CHUNK, AB_LANES))
        cols = _gate_columns(beta_all[rows], gc_all)
        beta_col += cols[0]
        gc_col += cols[1]
        gc_row += cols[2]
        qn += [qkv_ref[rows, h * DN_DK:(h + 1) * DN_DK] for h in heads]
        kn += [qkv_ref[rows, (DN_HEADS + h) * DN_DK:(DN_HEADS + h + 1) * DN_DK] for h in heads]
        vh += [qkv_ref[rows, (2 * DN_HEADS + h) * DN_DV:(2 * DN_HEADS + h + 1) * DN_DV] for h in heads]
    w, u, qdec, qk = _heads_intra(qn, kn, vh, beta_col, gc_col, gc_row, causal, strict, eye, 5)
    for n in range(len(w)):
        rows, hd = chunk_rows[n // DN_HEADS], n % DN_HEADS
        lo = hd * DN_DK
        g_last = gc_col[n][CHUNK - 1:CHUNK, :]
        w_ref[rows, lo:lo + DN_DK] = w[n].astype(BF16)
        qd_ref[rows, lo:lo + DN_DK] = qdec[n].astype(BF16)
        kd_ref[rows, lo:lo + DN_DK] = (kn[n] * jnp.exp(g_last - gc_col[n])).astype(BF16)
        u_ref[rows, lo:lo + DN_DV] = u[n]
        qk_ref[rows, hd * CHUNK:(hd + 1) * CHUNK] = qk[n].astype(BF16)


def _proj_qkv(h, wt, wabt, cqw, alog, dtb, tiles_per_seq):
    m = h.shape[0]
    hk, hv = DN_HEADS * DN_DK, DN_HEADS * DN_DV
    rows = lambda width: pl.BlockSpec((TM, width), lambda i: (i, 0))
    whole = lambda shape: pl.BlockSpec(shape, lambda i: (0,) * len(shape))
    w_tile = lambda t: pl.BlockSpec((TN_IN, D_MODEL), lambda i: (3 + t, 0))
    return pl.pallas_call(
        functools.partial(_proj_qkv_kernel, tiles_per_seq),
        grid=(m // TM,),
        in_specs=[rows(D_MODEL), w_tile(0), w_tile(1), w_tile(2), whole((AB_LANES, D_MODEL)),
                  whole((4, DN_QKV)), whole((1, AB_LANES)), whole((1, AB_LANES))],
        out_specs=[rows(hk), rows(hk), rows(hk), rows(hv), rows(DN_HEADS * CHUNK), rows(AB_LANES),
                   pl.BlockSpec((1, 8, DN_QKV), lambda i: (i, 0, 0))],
        out_shape=[
            jax.ShapeDtypeStruct((m, hk), BF16), jax.ShapeDtypeStruct((m, hk), BF16),
            jax.ShapeDtypeStruct((m, hk), BF16), jax.ShapeDtypeStruct((m, hv), F32),
            jax.ShapeDtypeStruct((m, DN_HEADS * CHUNK), BF16), jax.ShapeDtypeStruct((m, AB_LANES), F32),
            jax.ShapeDtypeStruct((m // TM, 8, DN_QKV), F32),
        ],
        scratch_shapes=[pltpu.VMEM((3, 8, TN_IN), F32), pltpu.VMEM((TM, DN_QKV), F32)],
        compiler_params=_params("arbitrary"),
        name="proj_qkv",
    )(h, wt, wt, wt, wabt, cqw, alog, dtb)


def _delta_scan_kernel(w_ref, qd_ref, kd_ref, u_ref, qk_ref, egl_ref, zs_ref, dnn_ref,
                       y_ref, so_ref, s_ref):
    c = pl.program_id(1)

    @pl.when(c == 0)
    def _():
        s_ref[...] = jnp.zeros_like(s_ref)

    heads = range(DN_HEADS)
    dnn = dnn_ref[...]
    s = [s_ref[h] for h in heads]
    for g in range(w_ref.shape[0] // CHUNK):
        rows = slice(g * CHUNK, (g + 1) * CHUNK)
        col = [slice(h * DN_DK, (h + 1) * DN_DK) for h in heads]
        lhs = [jnp.concatenate([w_ref[rows, col[h]], qd_ref[rows, col[h]]], axis=0) for h in heads]
        r = [jnp.dot(lhs[h], s[h].astype(BF16), preferred_element_type=F32) for h in heads]
        v_new = [(u_ref[rows, col[h]] - r[h][:CHUNK]).astype(BF16) for h in heads]
        upd = [lax.dot_general(kd_ref[rows, col[h]], v_new[h], (((0,), (0,)), ((), ())),
                               preferred_element_type=F32) for h in heads]
        s = [s[h] * egl_ref[g * CHUNK:g * CHUNK + 1, h:h + 1] + upd[h] for h in heads]
        o = [r[h][CHUNK:] + jnp.dot(qk_ref[rows, h * CHUNK:(h + 1) * CHUNK], v_new[h],
                                    preferred_element_type=F32) for h in heads]
        for h in heads:
            y_ref[rows, col[h]] = (_rms(o[h], dnn) * zs_ref[rows, col[h]]).astype(y_ref.dtype)
    for h in heads:
        s_ref[h] = s[h]

    @pl.when(c == pl.num_programs(1) - 1)
    def _():
        so_ref[0] = s_ref[...]


def _delta_scan(w, qd, kd, u, qk, egl, zs, dnn, n_seq, seq_len):
    n_steps = seq_len // TM
    hk, hv = DN_HEADS * DN_DK, DN_HEADS * DN_DV
    rows = lambda width: pl.BlockSpec((TM, width), lambda b, c: (b * n_steps + c, 0))
    return pl.pallas_call(
        _delta_scan_kernel,
        grid=(n_seq, n_steps),
        in_specs=[rows(hk), rows(hk), rows(hk), rows(hv), rows(DN_HEADS * CHUNK), rows(AB_LANES), rows(hv),
                  pl.BlockSpec((1, DN_DV), lambda b, c: (0, 0))],
        out_specs=[rows(hv), pl.BlockSpec((1, DN_HEADS, DN_DK, DN_DV), lambda b, c: (b, 0, 0, 0))],
        out_shape=[jax.ShapeDtypeStruct((n_seq * seq_len, hv), BF16),
                   jax.ShapeDtypeStruct((n_seq, DN_HEADS, DN_DK, DN_DV), F32)],
        scratch_shapes=[pltpu.VMEM((DN_HEADS, DN_DK, DN_DV), F32)],
        compiler_params=_params("arbitrary", "arbitrary"),
        name="delta_scan",
    )(w, qd, kd, u, qk, egl, zs, dnn)


def _mix_sample_kernel(pa_ref, pq_ref, pz_ref, pab_ref, sa_ref, sq_ref, si_ref,
                       caw_ref, cqw_ref, alog_ref, dtb_ref, dnn_ref,
                       y_ref, ca_ref, cq_ref, so_ref,
                       gb_ref, ua_ref, xq_ref, z_ref, ab_ref, ybuf_ref):
    seqs = range(SEQ_PER_STEP)
    pad = SEQ_PAD - SAMPLE_LEN
    for b in seqs:
        r0 = SEQ_PAD * b
        nxt = (b + 1) % SEQ_PER_STEP
        src = slice(SAMPLE_LEN * b, SAMPLE_LEN * (b + 1))
        pa = pa_ref[src, :]
        gb_ref[r0:r0 + SAMPLE_LEN, :] = pa[:, :D_CONV]
        ua_ref[r0:r0 + SAMPLE_LEN, :] = pa[:, D_CONV:2 * D_CONV] * pa[:, 2 * D_CONV:]
        xq_ref[r0:r0 + SAMPLE_LEN, :] = pq_ref[src, :]
        z_ref[r0:r0 + SAMPLE_LEN, :] = pz_ref[src, :]
        ab_ref[r0:r0 + SAMPLE_LEN, :] = pab_ref[src, :]
        for ref in (gb_ref, ua_ref, xq_ref, z_ref, ab_ref):
            ref[r0 + SAMPLE_LEN:r0 + SEQ_PAD, :] = jnp.zeros((pad, ref.shape[1]), F32)
        ua_ref[r0 + SEQ_PAD - 2:r0 + SEQ_PAD, :] = sa_ref[nxt]
        xq_ref[r0 + SEQ_PAD - 3:r0 + SEQ_PAD, :] = sq_ref[nxt]
    ybuf_ref[:, :D_CONV] = gb_ref[...] * _causal_conv(ua_ref[...], caw_ref[...], None)[0]
    row, col, causal, strict, low_f, eye = _masks(True)
    valid = (row[:, 0:1] & (SEQ_PAD - 1)) < SAMPLE_LEN
    cqkv = jnp.where(valid, _silu(_causal_conv(xq_ref[...], cqw_ref[...], None)[0]), 0.0)
    for b in seqs:
        r0 = SEQ_PAD * b
        ca_ref[b] = ua_ref[r0 + SAMPLE_LEN - 2:r0 + SAMPLE_LEN, :]
        cq_ref[b] = xq_ref[r0 + SAMPLE_LEN - 3:r0 + SAMPLE_LEN, :]

    g_all, beta_all = _gates(ab_ref[...], alog_ref[...], dtb_ref[...])
    g_all = jnp.where(valid, g_all, 0.0)
    beta_all = jnp.where(valid, beta_all, 0.0)
    gc_all = _mm_exact(low_f, g_all)
    last_sel = (col == (row | (SEQ_PAD - 1))).astype(F32)
    gl_all = _mm_exact(last_sel, gc_all)
    z = z_ref[...]
    dnn = dnn_ref[...]
    heads = range(DN_HEADS)
    qn = [_l2_normalize(cqkv[:, h * DN_DK:(h + 1) * DN_DK], DN_DK ** -0.5) for h in heads]
    kn = [_l2_normalize(cqkv[:, 1024 + h * DN_DK:1024 + (h + 1) * DN_DK], 1.0) for h in heads]
    vh = [cqkv[:, 2048 + h * DN_DV:2048 + (h + 1) * DN_DV] for h in heads]
    beta_col, gc_col, gc_row = _gate_columns(beta_all, gc_all)
    w, u, qdec, qk = _heads_intra(qn, kn, vh, beta_col, gc_col, gc_row, causal, strict, eye, 1)
    gl_col = [gl_all[:, h:h + 1] for h in heads]
    kdec = [kn[h] * jnp.exp(gl_col[h] - gc_col[h]) for h in heads]
    s_scale = [jnp.exp(x) for x in gl_col]
    rows = [slice(SEQ_PAD * b, SEQ_PAD * (b + 1)) for b in seqs]
    for h in heads:
        s_old = [si_ref[b, h] for b in seqs]
        r = [_mm(jnp.concatenate([w[h][rows[b]], qdec[h][rows[b]]], axis=0), s_old[b]) for b in seqs]
        v_new = [u[h][rows[b]] - r[b][:SEQ_PAD] for b in seqs]
        upd = [_mm_tn(kdec[h][rows[b]], v_new[b]) for b in seqs]
        for b in seqs:
            so_ref[b, h] = s_old[b] * s_scale[h][SEQ_PAD * b:SEQ_PAD * b + 1, :] + upd[b]
        o = jnp.concatenate([x[SEQ_PAD:] for x in r], axis=0) + _mm(qk[h], jnp.concatenate(v_new, axis=0))
        lo = h * DN_DV
        ybuf_ref[:, D_CONV + lo:D_CONV + lo + DN_DV] = _rms(o, dnn) * _silu(z[:, lo:lo + DN_DV])
    for b in seqs:
        y_ref[SAMPLE_LEN * b:SAMPLE_LEN * (b + 1), :] = ybuf_ref[SEQ_PAD * b:SEQ_PAD * b + SAMPLE_LEN, :]


def _mix_sample(proj, ab, state_a, state_q, state_s, caw, cqw, alog, dtb, dnn):
    n_seq = state_s.shape[0]
    full2 = lambda i: (0, 0)
    seq_rows = lambda width, col: pl.BlockSpec((SEQ_PER_STEP * SAMPLE_LEN, width), lambda i: (i, col))
    return pl.pallas_call(
        _mix_sample_kernel,
        grid=(n_seq // SEQ_PER_STEP,),
        in_specs=[
            seq_rows(3 * D_CONV, 0), seq_rows(DN_QKV, 1), seq_rows(1024, Z_BLOCK), seq_rows(AB_LANES, 0),
            pl.BlockSpec((SEQ_PER_STEP, 2, D_CONV), lambda i: (i, 0, 0)),
            pl.BlockSpec((SEQ_PER_STEP, 3, DN_QKV), lambda i: (i, 0, 0)),
            pl.BlockSpec((SEQ_PER_STEP, DN_HEADS, DN_DK, DN_DV), lambda i: (i, 0, 0, 0)),
            pl.BlockSpec((3, D_CONV), full2),
            pl.BlockSpec((4, DN_QKV), full2),
            pl.BlockSpec((1, AB_LANES), full2),
            pl.BlockSpec((1, AB_LANES), full2),
            pl.BlockSpec((1, DN_DV), full2),
        ],
        out_specs=[
            seq_rows(D_MODEL, 0),
            pl.BlockSpec((SEQ_PER_STEP, 2, D_CONV), lambda i: (i, 0, 0)),
            pl.BlockSpec((SEQ_PER_STEP, 3, DN_QKV), lambda i: (i, 0, 0)),
            pl.BlockSpec((SEQ_PER_STEP, DN_HEADS, DN_DK, DN_DV), lambda i: (i, 0, 0, 0)),
        ],
        out_shape=[
            jax.ShapeDtypeStruct((n_seq * SAMPLE_LEN, D_MODEL), F32),
            jax.ShapeDtypeStruct((n_seq, 2, D_CONV), F32),
            jax.ShapeDtypeStruct((n_seq, 3, DN_QKV), F32),
            jax.ShapeDtypeStruct((n_seq, DN_HEADS, DN_DK, DN_DV), F32),
        ],
        scratch_shapes=[
            pltpu.VMEM((CHUNK, D_CONV), F32),
            pltpu.VMEM((CHUNK, D_CONV), F32),
            pltpu.VMEM((CHUNK, DN_QKV), F32),
            pltpu.VMEM((CHUNK, 1024), F32),
            pltpu.VMEM((CHUNK, AB_LANES), F32),
            pltpu.VMEM((CHUNK, D_MODEL), F32),
        ],
        compiler_params=_params("parallel"),
        name="mix_sample",
    )(proj, proj, proj, ab, state_a, state_q, state_s, caw, cqw, alog, dtb, dnn)


def _outproj_kernel(ya_ref, yb_ref, wa_ref, wb_ref, post_ref, x_ref, o_ref):
    m = (jnp.dot(ya_ref[...].astype(BF16), wa_ref[...], preferred_element_type=F32)
         + jnp.dot(yb_ref[...].astype(BF16), wb_ref[...], preferred_element_type=F32))
    o_ref[...] = x_ref[...] + _rms(m, post_ref[...])


def _outproj(ya, yb, yb_block, w, post, x):
    m = x.shape[0]
    half = D_MODEL // 2
    return pl.pallas_call(
        _outproj_kernel,
        grid=(m // TM,),
        in_specs=[
            pl.BlockSpec((TM, half), lambda i: (i, 0)),
            pl.BlockSpec((TM, half), lambda i: (i, yb_block)),
            pl.BlockSpec((half, D_MODEL), lambda i: (0, 0)),
            pl.BlockSpec((half, D_MODEL), lambda i: (1, 0)),
            pl.BlockSpec((1, D_MODEL), lambda i: (0, 0)),
            pl.BlockSpec((TM, D_MODEL), lambda i: (i, 0)),
        ],
        out_specs=pl.BlockSpec((TM, D_MODEL), lambda i: (i, 0)),
        out_shape=jax.ShapeDtypeStruct((m, D_MODEL), F32),
        compiler_params=_params("parallel"),
        name="outproj",
    )(ya, yb, w, w, post, x)


def _ple_kernel(x_ref, p_ref, pre_ref, post_ref, wg_ref, wp_ref, o_ref):
    x = x_ref[...]
    h = _rms(x, pre_ref[...]).astype(BF16)
    gate = jax.nn.sigmoid(jnp.dot(h, wg_ref[...], preferred_element_type=F32))
    proj = jnp.dot(p_ref[...].astype(BF16), wp_ref[...], preferred_element_type=F32)
    o_ref[...] = x + _rms(gate * proj, post_ref[...])


def _ple(x, p, pre, post, wg, wp):
    m = x.shape[0]
    return pl.pallas_call(
        _ple_kernel,
        grid=(m // TM,),
        in_specs=[
            pl.BlockSpec((TM, D_MODEL), lambda i: (i, 0)),
            pl.BlockSpec((TM, D_PLE), lambda i: (i, 0)),
            pl.BlockSpec((1, D_MODEL), lambda i: (0, 0)),
            pl.BlockSpec((1, D_MODEL), lambda i: (0, 0)),
            pl.BlockSpec((D_MODEL, D_MODEL), lambda i: (0, 0)),
            pl.BlockSpec((D_PLE, D_MODEL), lambda i: (0, 0)),
        ],
        out_specs=pl.BlockSpec((TM, D_MODEL), lambda i: (i, 0)),
        out_shape=jax.ShapeDtypeStruct((m, D_MODEL), F32),
        compiler_params=_params("parallel"),
        name="ple",
    )(x, p, pre, post, wg, wp)


def _row(v):
    return v.reshape(1, -1).astype(F32)


def _pad_lanes(v):
    return jnp.pad(_row(v), ((0, 0), (0, AB_LANES - v.shape[-1])))


def _seq_tails(tails, tiles_per_seq, n_rows):
    return tails[tiles_per_seq - 1::tiles_per_seq, 8 - n_rows:, :]


def kernel(x_prompt, x_sample, state_conv_a, state_conv_qkv, state_delta, p_prompt, p_sample,
           f1_pre, f1_post, f1_wg, f1_wu, f1_wd,
           mix_pre, mix_post, w_in, conv_a_w, conv_qkv_w, a_log, dt_bias, dn_norm, w_out,
           f2_pre, f2_post, f2_wg, f2_wu, f2_wd,
           ple_pre, ple_post, w_ple_gate, w_ple_proj):
    depth = f1_pre.shape[0]
    n_p, seq_p, _ = x_prompt.shape
    n_s, seq_s, _ = x_sample.shape
    rows_p = n_p * seq_p
    rows_s = n_s * seq_s
    assert seq_p % TM_BIG == 0 and seq_s == SAMPLE_LEN and n_s % SEQ_PER_STEP == 0
    assert rows_s == TM
    assert w_in.shape[-1] == IN_MAIN + N_AB
    tiles_per_seq = seq_p // TM_BIG
    xp = x_prompt.reshape(rows_p, D_MODEL)
    xs = x_sample.reshape(rows_s, D_MODEL)
    outs = [[] for _ in range(6)]
    for i in range(depth):
        pre, post = _row(f1_pre[i]), _row(f1_post[i])
        xs, wg, wu, wd = _ffn(xs, pre, post, f1_wg[i], f1_wu[i], f1_wd[i], TM)
        xp = _ffn(xp, pre, post, wg, wu, wd, TM_BIG)[0]

        pre = _row(mix_pre[i])
        caw, cqw = conv_a_w[i], conv_qkv_w[i]
        proj_s, ab_s, wt, wabt = _inproj_cast(xs, pre, w_in[i].T)
        y_a, h, tail_a = _proj_a(xp, pre, wt, caw, TM_BIG, tiles_per_seq)
        alog, dtb, dnn = _pad_lanes(a_log[i]), _pad_lanes(dt_bias[i]), _row(dn_norm[i])
        wy_w, qdec, kdec, wy_u, qk, egl, tail_q = _proj_qkv(h, wt, wabt, cqw, alog, dtb, seq_p // TM)
        zs, = _proj(h, wt, 6, 1, TM_BIG, tiles_per_seq)
        y_b, s_p = _delta_scan(wy_w, qdec, kdec, wy_u, qk, egl, zs, dnn, n_p, seq_p)
        ca_p = _seq_tails(tail_a, tiles_per_seq, 2)
        cq_p = _seq_tails(tail_q, seq_p // TM, 3)
        y_s, ca_s, cq_s, s_s = _mix_sample(
            proj_s, ab_s, state_conv_a[i], state_conv_qkv[i], state_delta[i], caw, cqw, alog, dtb, dnn)

        post = _row(mix_post[i])
        w_out_bf16 = w_out[i].astype(BF16)
        xs = _outproj(y_s, y_s, 1, w_out_bf16, post, xs)
        xp = _outproj(y_a, y_b, 0, w_out_bf16, post, xp)

        pre, post = _row(f2_pre[i]), _row(f2_post[i])
        xs, wg, wu, wd = _ffn(xs, pre, post, f2_wg[i], f2_wu[i], f2_wd[i], TM)
        xp = _ffn(xp, pre, post, wg, wu, wd, TM_BIG)[0]

        pre, post = _row(ple_pre[i]), _row(ple_post[i])
        wg, wp = w_ple_gate[i].astype(BF16), w_ple_proj[i].astype(BF16)
        xs = _ple(xs, p_sample[i].reshape(rows_s, D_PLE), pre, post, wg, wp)
        xp = _ple(xp, p_prompt[i].reshape(rows_p, D_PLE), pre, post, wg, wp)
        for lst, val in zip(outs, (ca_p, cq_p, s_p, ca_s, cq_s, s_s)):
            lst.append(val)
    return (xp.reshape(n_p, seq_p, D_MODEL), xs.reshape(n_s, seq_s, D_MODEL),
            *[jnp.stack(lst) for lst in outs])
```

```python
import functools

import jax
import jax.numpy as jnp
from jax import lax
from jax.experimental import pallas as pl
from jax.experimental.pallas import tpu as pltpu

F32 = jnp.float32
BF16 = jnp.bfloat16

D_MODEL = 2048
D_CONV = 1024
DN_HEADS = 8
DN_DK = 128
DN_DV = 128
DN_QKV = 3072
D_FF = 5632
D_PLE = 256
EPS = 1e-6
CHUNK = 64
SAMPLE_LEN = 4
SEQ_PAD = 8
SEQ_SHIFT = 3
SEQ_PER_STEP = CHUNK // SEQ_PAD
IN_MAIN = 7168
Z_BLOCK = 6
N_AB = 16
AB_LANES = 128

TM = 512
TF = 512
TF_CAST = 256
TN_IN = 1024
TM_BIG = 1024
TN_A = 512
SUB_COLS = 512
ROW_CHUNK = 32
VMEM_LIMIT = 56 * 1024 * 1024


def _rms(x, g):
    ms = jnp.mean(x * x, axis=-1, keepdims=True)
    return x * lax.rsqrt(ms + EPS) * g


def _silu(x):
    return x * jax.nn.sigmoid(x)


def _mm(a, b):
    return jnp.dot(a.astype(BF16), b.astype(BF16), preferred_element_type=F32)


def _mm_nt(a, b):
    return lax.dot_general(a.astype(BF16), b.astype(BF16), (((1,), (1,)), ((), ())),
                           preferred_element_type=F32)


def _mm_tn(a, b):
    return jnp.dot(a.T.astype(BF16), b.astype(BF16), preferred_element_type=F32)


def _mm_exact(a, b):
    return jnp.dot(a, b, precision=lax.Precision.HIGHEST, preferred_element_type=F32)


def _nt_dot(a, bt):
    return lax.dot_general(a, bt, (((1,), (1,)), ((), ())), preferred_element_type=F32)


def _for_row_chunks(n_rows, body):
    def step(k, carry):
        body(pl.ds(pl.multiple_of(k * ROW_CHUNK, ROW_CHUNK), ROW_CHUNK))
        return carry
    lax.fori_loop(0, n_rows // ROW_CHUNK, step, 0, unroll=4)


def _params(*semantics):
    return pltpu.CompilerParams(dimension_semantics=semantics, vmem_limit_bytes=VMEM_LIMIT)


def _ffn_kernel(emit_bf16, x_ref, pre_ref, post_ref, wg_ref, wu_ref, wd_ref, o_ref, *rest):
    h_ref = rest[-1]
    j = pl.program_id(1)

    @pl.when(j == 0)
    def _():
        def norm_rows(rows):
            h_ref[rows, :] = _rms(x_ref[rows, :], pre_ref[...]).astype(BF16)
            o_ref[rows, :] = jnp.zeros((ROW_CHUNK, D_MODEL), F32)
        _for_row_chunks(x_ref.shape[0], norm_rows)

    wg, wu, wd = wg_ref[...], wu_ref[...], wd_ref[...]
    if emit_bf16:
        wg, wu, wd = wg.astype(BF16), wu.astype(BF16), wd.astype(BF16)
        rest[0][...] = wg
        rest[1][...] = wu
        rest[2][...] = wd
    h = h_ref[...]
    g = jnp.dot(h, wg, preferred_element_type=F32)
    u = jnp.dot(h, wu, preferred_element_type=F32)
    a = (_silu(g) * u).astype(BF16)
    o_ref[...] += jnp.dot(a, wd, preferred_element_type=F32)

    @pl.when(j == pl.num_programs(1) - 1)
    def _():
        group = 4 * ROW_CHUNK

        def finish_rows(k, carry):
            r0 = pl.multiple_of(k * group, group)
            chunks = [pl.ds(r0 + s * ROW_CHUNK, ROW_CHUNK) for s in range(group // ROW_CHUNK)]
            scale = [0.5 * lax.rsqrt(jnp.mean(jnp.square(o_ref[c, :]), axis=-1, keepdims=True) + EPS)
                     for c in chunks]
            for c, sc in zip(chunks, scale):
                o_ref[c, :] = x_ref[c, :] + o_ref[c, :] * sc * post_ref[...]
            return carry
        lax.fori_loop(0, x_ref.shape[0] // group, finish_rows, 0)


def _ffn(x, pre, post, wg, wu, wd, tm):
    m = x.shape[0]
    emit_bf16 = wg.dtype != BF16
    tf = TF_CAST if emit_bf16 else TF
    up_spec = pl.BlockSpec((D_MODEL, tf), lambda i, j: (0, j))
    down_spec = pl.BlockSpec((tf, D_MODEL), lambda i, j: (j, 0))
    out_specs = [pl.BlockSpec((tm, D_MODEL), lambda i, j: (i, 0))]
    out_shape = [jax.ShapeDtypeStruct((m, D_MODEL), F32)]
    if emit_bf16:
        assert m == tm, "weights are re-emitted once, by a single-row-tile call"
        out_specs += [up_spec, up_spec, down_spec]
        out_shape += [jax.ShapeDtypeStruct(w.shape, BF16) for w in (wg, wu, wd)]
    outs = pl.pallas_call(
        functools.partial(_ffn_kernel, emit_bf16),
        grid=(m // tm, D_FF // tf),
        in_specs=[
            pl.BlockSpec((tm, D_MODEL), lambda i, j: (i, 0)),
            pl.BlockSpec((1, D_MODEL), lambda i, j: (0, 0)),
            pl.BlockSpec((1, D_MODEL), lambda i, j: (0, 0)),
            up_spec, up_spec, down_spec,
        ],
        out_specs=out_specs,
        out_shape=out_shape,
        scratch_shapes=[pltpu.VMEM((tm, D_MODEL), BF16)],
        compiler_params=_params("parallel", "arbitrary"),
        name="ffn_cast" if emit_bf16 else "ffn",
    )(x, pre, post, wg, wu, wd)
    return outs if emit_bf16 else (outs[0], wg, wu, wd)


def _inproj_cast_kernel(x_ref, pre_ref, wt_ref, wabt_ref, o_ref, ab_ref, wt16_ref, wabt16_ref, h_ref):
    @pl.when(pl.program_id(1) == 0)
    def _():
        h = _rms(x_ref[...], pre_ref[...]).astype(BF16)
        h_ref[...] = h
        wabt = jnp.concatenate(
            [wabt_ref[...].astype(BF16), jnp.zeros((AB_LANES - N_AB, D_MODEL), BF16)], axis=0)
        wabt16_ref[...] = wabt
        ab_ref[...] = _nt_dot(h, wabt)

    wt = wt_ref[...].astype(BF16)
    wt16_ref[...] = wt
    o_ref[...] = _nt_dot(h_ref[...], wt)


def _inproj_cast(x, pre, wt):
    m = x.shape[0]
    assert m == TM, "weights are re-emitted once, by a single-row-tile call"
    wt_spec = pl.BlockSpec((TN_IN, D_MODEL), lambda i, j: (j, 0))
    return pl.pallas_call(
        _inproj_cast_kernel,
        grid=(1, IN_MAIN // TN_IN),
        in_specs=[
            pl.BlockSpec((TM, D_MODEL), lambda i, j: (0, 0)),
            pl.BlockSpec((1, D_MODEL), lambda i, j: (0, 0)),
            wt_spec,
            pl.BlockSpec((N_AB, D_MODEL), lambda i, j: (IN_MAIN // N_AB, 0)),
        ],
        out_specs=[
            pl.BlockSpec((TM, TN_IN), lambda i, j: (0, j)),
            pl.BlockSpec((TM, AB_LANES), lambda i, j: (0, 0)),
            wt_spec,
            pl.BlockSpec((AB_LANES, D_MODEL), lambda i, j: (0, 0)),
        ],
        out_shape=[
            jax.ShapeDtypeStruct((m, IN_MAIN), F32),
            jax.ShapeDtypeStruct((m, AB_LANES), F32),
            jax.ShapeDtypeStruct((IN_MAIN, D_MODEL), BF16),
            jax.ShapeDtypeStruct((AB_LANES, D_MODEL), BF16),
        ],
        scratch_shapes=[pltpu.VMEM((TM, D_MODEL), BF16)],
        compiler_params=_params("parallel", "arbitrary"),
        name="inproj_cast",
    )(x, pre, wt, wt)


def _causal_conv(x, w, carry_in):
    width = w.shape[0]
    rows = x.shape[0]
    is_row0 = lax.broadcasted_iota(jnp.int32, (8, 1), 0) == 0
    acc = x * w[0:1, :]
    carry_out = []
    for j in range(1, width):
        delayed = pltpu.roll(acc, 1, axis=0)
        if carry_in is not None:
            carry_out.append(acc[rows - 1:rows, :])
            head = jnp.where(is_row0, carry_in[j - 1], delayed[:8])
            delayed = jnp.concatenate([head, delayed[8:]], axis=0)
        acc = x * w[j:j + 1, :] + delayed
    return acc, carry_out


def _load_carry(carry_ref, j, cols, n, seq_start):
    return [jnp.where(seq_start, 0.0, carry_ref[j, r:r + 1, cols]) for r in range(n)]


def _store_carry(carry_ref, j, cols, rows):
    for r, row in enumerate(rows):
        carry_ref[j, r:r + 1, cols] = row


def _proj_a_kernel(tiles_per_seq, x_ref, pre_ref, wgb_ref, wgcv_ref, whc_ref, caw_ref,
                   ya_ref, h_ref, tail_ref, carry_ref):
    i, j = pl.program_id(0), pl.program_id(1)
    tm = x_ref.shape[0]

    @pl.when(j == 0)
    def _():
        def norm_rows(rows):
            h_ref[rows, :] = _rms(x_ref[rows, :], pre_ref[...]).astype(BF16)
        _for_row_chunks(tm, norm_rows)

    @pl.when((i == 0) & (j == 0))
    def _():
        carry_ref[...] = jnp.zeros_like(carry_ref)

    seq_start = i % tiles_per_seq == 0
    h = h_ref[...]
    for c in range(TN_A // SUB_COLS):
        cols = slice(c * SUB_COLS, (c + 1) * SUB_COLS)
        u = _nt_dot(h, wgcv_ref[cols, :]) * _nt_dot(h, whc_ref[cols, :])
        tail_ref[0, :, cols] = u[tm - 8:tm, :]
        conv, carry = _causal_conv(u, caw_ref[:, cols], _load_carry(carry_ref, j, cols, 2, seq_start))
        _store_carry(carry_ref, j, cols, carry)
        ya_ref[:, cols] = (_nt_dot(h, wgb_ref[cols, :]) * conv).astype(BF16)


def _proj_a(x, pre, wt, caw, tm, tiles_per_seq):
    m = x.shape[0]
    n_col = D_CONV // TN_A
    w_spec = lambda part: pl.BlockSpec((TN_A, D_MODEL), lambda i, j: (part * n_col + j, 0))
    return pl.pallas_call(
        functools.partial(_proj_a_kernel, tiles_per_seq),
        grid=(m // tm, n_col),
        in_specs=[
            pl.BlockSpec((tm, D_MODEL), lambda i, j: (i, 0)),
            pl.BlockSpec((1, D_MODEL), lambda i, j: (0, 0)),
            w_spec(0), w_spec(1), w_spec(2),
            pl.BlockSpec((3, TN_A), lambda i, j: (0, j)),
        ],
        out_specs=[
            pl.BlockSpec((tm, TN_A), lambda i, j: (i, j)),
            pl.BlockSpec((tm, D_MODEL), lambda i, j: (i, 0)),
            pl.BlockSpec((1, 8, TN_A), lambda i, j: (i, 0, j)),
        ],
        out_shape=[
            jax.ShapeDtypeStruct((m, D_CONV), BF16),
            jax.ShapeDtypeStruct((m, D_MODEL), BF16),
            jax.ShapeDtypeStruct((m // tm, 8, D_CONV), F32),
        ],
        scratch_shapes=[pltpu.VMEM((n_col, 8, TN_A), F32)],
        compiler_params=_params("arbitrary", "arbitrary"),
        name="proj_a",
    )(x, pre, wt, wt, wt, caw)


def _proj_z_kernel(h_ref, wt_ref, o_ref):
    o_ref[...] = _silu(_nt_dot(h_ref[...], wt_ref[...]))


def _proj_z(h, wt, tm):
    m = h.shape[0]
    return pl.pallas_call(
        _proj_z_kernel,
        grid=(m // tm,),
        in_specs=[pl.BlockSpec((tm, D_MODEL), lambda i: (i, 0)),
                  pl.BlockSpec((TN_IN, D_MODEL), lambda i: (Z_BLOCK, 0))],
        out_specs=pl.BlockSpec((tm, TN_IN), lambda i: (i, 0)),
        out_shape=jax.ShapeDtypeStruct((m, TN_IN), F32),
        compiler_params=_params("parallel"),
        name="proj_z",
    )(h, wt)


def _l2_normalize(x, scale):
    return x * (lax.rsqrt(jnp.sum(x * x, axis=-1, keepdims=True) + EPS) * scale)


def _gate_columns(beta_all, gc_all):
    heads = range(DN_HEADS)
    gc_t = gc_all.T
    return ([beta_all[:, DN_HEADS + h:DN_HEADS + h + 1] for h in heads],
            [gc_all[:, h:h + 1] for h in heads], [gc_t[h:h + 1, :] for h in heads])


def _heads_intra(qn, kn, vh, beta_col, gc_col, gc_row, causal, strict, eye, squarings):
    heads = range(len(qn))
    decay = [jnp.exp(jnp.where(causal, gc_col[h] - gc_row[h], -jnp.inf)) for h in heads]
    kb = [kn[h] * beta_col[h] for h in heads]
    gram = [_mm_nt(jnp.concatenate([kb[h], qn[h]], axis=0), kn[h]) for h in heads]
    a_mat = [jnp.where(strict, gram[h][:CHUNK] * decay[h], 0.0) for h in heads]
    qk = [jnp.where(causal, gram[h][CHUNK:] * decay[h], 0.0) for h in heads]
    inv = [eye - a for a in a_mat]
    power = a_mat
    for _ in range(squarings):
        power = [_mm(x, x) for x in power]
        inv = [_mm(inv[h], eye + power[h]) for h in heads]
    egc = [jnp.exp(x) for x in gc_col]
    wu = [_mm(inv[h], jnp.concatenate([kb[h] * egc[h], vh[h] * beta_col[h]], axis=1)) for h in heads]
    w = [x[:, :DN_DK] for x in wu]
    u = [x[:, DN_DK:] for x in wu]
    qdec = [qn[h] * egc[h] for h in heads]
    return w, u, qdec, qk


def _gates(ab, alog, dtb):
    x = ab + dtb
    softplus = jnp.maximum(x, 0.0) + jnp.log1p(jnp.exp(-jnp.abs(x)))
    return -jnp.exp(alog) * softplus, jax.nn.sigmoid(ab)


def _masks(same_seq):
    row = lax.broadcasted_iota(jnp.int32, (CHUNK, CHUNK), 0)
    col = lax.broadcasted_iota(jnp.int32, (CHUNK, CHUNK), 1)
    causal = row >= col
    strict = row > col
    if same_seq:
        same = (row >> SEQ_SHIFT) == (col >> SEQ_SHIFT)
        causal = causal & same
        strict = strict & same
    eye = jnp.where(row == col, 1.0, 0.0).astype(F32)
    return row, col, causal, strict, causal.astype(F32), eye


def _proj_qkv_kernel(tiles_per_seq, h_ref, wq_ref, wk_ref, wv_ref, wabt_ref, cqw_ref, alog_ref, dtb_ref,
                     w_ref, qd_ref, kd_ref, u_ref, qk_ref, egl_ref, tail_ref, carry_ref, qkv_ref):
    i = pl.program_id(0)
    tm = h_ref.shape[0]

    @pl.when(i == 0)
    def _():
        carry_ref[...] = jnp.zeros_like(carry_ref)

    seq_start = i % tiles_per_seq == 0
    h = h_ref[...]
    g_all, beta_all = _gates(_nt_dot(h, wabt_ref[...]), alog_ref[...], dtb_ref[...])
    for t, wt_ref in enumerate((wq_ref, wk_ref, wv_ref)):
        for c in range(TN_IN // SUB_COLS):
            cols = slice(c * SUB_COLS, (c + 1) * SUB_COLS)
            lo = t * TN_IN + c * SUB_COLS
            p = _nt_dot(h, wt_ref[cols, :])
            tail_ref[0, :, lo:lo + SUB_COLS] = p[tm - 8:tm, :]
            p, carry = _causal_conv(p, cqw_ref[:, lo:lo + SUB_COLS],
                                    _load_carry(carry_ref, t, cols, 3, seq_start))
            _store_carry(carry_ref, t, cols, carry)
            p = _silu(p)
            if t == 2:
                qkv_ref[:, lo:lo + SUB_COLS] = p
            else:
                for k in range(SUB_COLS // DN_DK):
                    blk = p[:, k * DN_DK:(k + 1) * DN_DK]
                    qkv_ref[:, lo + k * DN_DK:lo + (k + 1) * DN_DK] = _l2_normalize(
                        blk, DN_DK ** -0.5 if t == 0 else 1.0)

    heads = range(DN_HEADS)
    _, _, causal, strict, low_f, eye = _masks(False)
    chunk_rows = [slice(g * CHUNK, (g + 1) * CHUNK) for g in range(tm // CHUNK)]
    qn, kn, vh, beta_col, gc_col, gc_row = [], [], [], [], [], []
    for rows in chunk_rows:
        gc_all = _mm_exact(low_f, g_all[rows])
        egl_ref[rows, :] = jnp.broadcast_to(jnp.exp(gc_all[CHUNK - 1:CHUNK, :]), (CHUNK, AB_LANES))
        cols = _gate_columns(beta_all[rows], gc_all)
        beta_col += cols[0]
        gc_col += cols[1]
        gc_row += cols[2]
        qn += [qkv_ref[rows, h * DN_DK:(h + 1) * DN_DK] for h in heads]
        kn += [qkv_ref[rows, (DN_HEADS + h) * DN_DK:(DN_HEADS + h + 1) * DN_DK] for h in heads]
        vh += [qkv_ref[rows, (2 * DN_HEADS + h) * DN_DV:(2 * DN_HEADS + h + 1) * DN_DV] for h in heads]
    w, u, qdec, qk = _heads_intra(qn, kn, vh, beta_col, gc_col, gc_row, causal, strict, eye, 5)
    for n in range(len(w)):
        rows, hd = chunk_rows[n // DN_HEADS], n % DN_HEADS
        lo = hd * DN_DK
        g_last = gc_col[n][CHUNK - 1:CHUNK, :]
        w_ref[rows, lo:lo + DN_DK] = w[n].astype(BF16)
        qd_ref[rows, lo:lo + DN_DK] = qdec[n].astype(BF16)
        kd_ref[rows, lo:lo + DN_DK] = (kn[n] * jnp.exp(g_last - gc_col[n])).astype(BF16)
        u_ref[rows, lo:lo + DN_DV] = u[n]
        qk_ref[rows, hd * CHUNK:(hd + 1) * CHUNK] = qk[n].astype(BF16)


def _proj_qkv(h, wt, wabt, cqw, alog, dtb, tiles_per_seq):
    m = h.shape[0]
    hk, hv = DN_HEADS * DN_DK, DN_HEADS * DN_DV
    rows = lambda width: pl.BlockSpec((TM, width), lambda i: (i, 0))
    whole = lambda shape: pl.BlockSpec(shape, lambda i: (0,) * len(shape))
    w_tile = lambda t: pl.BlockSpec((TN_IN, D_MODEL), lambda i: (3 + t, 0))
    return pl.pallas_call(
        functools.partial(_proj_qkv_kernel, tiles_per_seq),
        grid=(m // TM,),
        in_specs=[rows(D_MODEL), w_tile(0), w_tile(1), w_tile(2), whole((AB_LANES, D_MODEL)),
                  whole((4, DN_QKV)), whole((1, AB_LANES)), whole((1, AB_LANES))],
        out_specs=[rows(hk), rows(hk), rows(hk), rows(hv), rows(DN_HEADS * CHUNK), rows(AB_LANES),
                   pl.BlockSpec((1, 8, DN_QKV), lambda i: (i, 0, 0))],
        out_shape=[
            jax.ShapeDtypeStruct((m, hk), BF16), jax.ShapeDtypeStruct((m, hk), BF16),
            jax.ShapeDtypeStruct((m, hk), BF16), jax.ShapeDtypeStruct((m, hv), F32),
            jax.ShapeDtypeStruct((m, DN_HEADS * CHUNK), BF16), jax.ShapeDtypeStruct((m, AB_LANES), F32),
            jax.ShapeDtypeStruct((m // TM, 8, DN_QKV), F32),
        ],
        scratch_shapes=[pltpu.VMEM((3, 8, TN_IN), F32), pltpu.VMEM((TM, DN_QKV), F32)],
        compiler_params=_params("arbitrary"),
        name="proj_qkv",
    )(h, wt, wt, wt, wabt, cqw, alog, dtb)


def _delta_scan_kernel(w_ref, qd_ref, kd_ref, u_ref, qk_ref, egl_ref, zs_ref, dnn_ref,
                       y_ref, so_ref, s_ref):
    c = pl.program_id(1)

    @pl.when(c == 0)
    def _():
        s_ref[...] = jnp.zeros_like(s_ref)

    heads = range(DN_HEADS)
    dnn = dnn_ref[...]
    s = [s_ref[h] for h in heads]
    for g in range(w_ref.shape[0] // CHUNK):
        rows = slice(g * CHUNK, (g + 1) * CHUNK)
        col = [slice(h * DN_DK, (h + 1) * DN_DK) for h in heads]
        lhs = [jnp.concatenate([w_ref[rows, col[h]], qd_ref[rows, col[h]]], axis=0) for h in heads]
        r = [jnp.dot(lhs[h], s[h].astype(BF16), preferred_element_type=F32) for h in heads]
        v_new = [(u_ref[rows, col[h]] - r[h][:CHUNK]).astype(BF16) for h in heads]
        upd = [lax.dot_general(kd_ref[rows, col[h]], v_new[h], (((0,), (0,)), ((), ())),
                               preferred_element_type=F32) for h in heads]
        s = [s[h] * egl_ref[g * CHUNK:g * CHUNK + 1, h:h + 1] + upd[h] for h in heads]
        o = [r[h][CHUNK:] + jnp.dot(qk_ref[rows, h * CHUNK:(h + 1) * CHUNK], v_new[h],
                                    preferred_element_type=F32) for h in heads]
        for h in heads:
            y_ref[rows, col[h]] = (_rms(o[h], dnn) * zs_ref[rows, col[h]]).astype(y_ref.dtype)
    for h in heads:
        s_ref[h] = s[h]

    @pl.when(c == pl.num_programs(1) - 1)
    def _():
        so_ref[0] = s_ref[...]


def _delta_scan(w, qd, kd, u, qk, egl, zs, dnn, n_seq, seq_len):
    n_steps = seq_len // TM
    hk, hv = DN_HEADS * DN_DK, DN_HEADS * DN_DV
    rows = lambda width: pl.BlockSpec((TM, width), lambda b, c: (b * n_steps + c, 0))
    return pl.pallas_call(
        _delta_scan_kernel,
        grid=(n_seq, n_steps),
        in_specs=[rows(hk), rows(hk), rows(hk), rows(hv), rows(DN_HEADS * CHUNK), rows(AB_LANES), rows(hv),
                  pl.BlockSpec((1, DN_DV), lambda b, c: (0, 0))],
        out_specs=[rows(hv), pl.BlockSpec((1, DN_HEADS, DN_DK, DN_DV), lambda b, c: (b, 0, 0, 0))],
        out_shape=[jax.ShapeDtypeStruct((n_seq * seq_len, hv), BF16),
                   jax.ShapeDtypeStruct((n_seq, DN_HEADS, DN_DK, DN_DV), F32)],
        scratch_shapes=[pltpu.VMEM((DN_HEADS, DN_DK, DN_DV), F32)],
        compiler_params=_params("arbitrary", "arbitrary"),
        name="delta_scan",
    )(w, qd, kd, u, qk, egl, zs, dnn)


def _mix_sample_kernel(pa_ref, pq_ref, pz_ref, pab_ref, sa_ref, sq_ref, si_ref,
                       caw_ref, cqw_ref, alog_ref, dtb_ref, dnn_ref,
                       y_ref, ca_ref, cq_ref, so_ref,
                       gb_ref, ua_ref, xq_ref, z_ref, ab_ref, ybuf_ref):
    seqs = range(SEQ_PER_STEP)
    pad = SEQ_PAD - SAMPLE_LEN
    for b in seqs:
        r0 = SEQ_PAD * b
        nxt = (b + 1) % SEQ_PER_STEP
        src = slice(SAMPLE_LEN * b, SAMPLE_LEN * (b + 1))
        pa = pa_ref[src, :]
        gb_ref[r0:r0 + SAMPLE_LEN, :] = pa[:, :D_CONV]
        ua_ref[r0:r0 + SAMPLE_LEN, :] = pa[:, D_CONV:2 * D_CONV] * pa[:, 2 * D_CONV:]
        xq_ref[r0:r0 + SAMPLE_LEN, :] = pq_ref[src, :]
        z_ref[r0:r0 + SAMPLE_LEN, :] = pz_ref[src, :]
        ab_ref[r0:r0 + SAMPLE_LEN, :] = pab_ref[src, :]
        for ref in (gb_ref, ua_ref, xq_ref, z_ref, ab_ref):
            ref[r0 + SAMPLE_LEN:r0 + SEQ_PAD, :] = jnp.zeros((pad, ref.shape[1]), F32)
        ua_ref[r0 + SEQ_PAD - 2:r0 + SEQ_PAD, :] = sa_ref[nxt]
        xq_ref[r0 + SEQ_PAD - 3:r0 + SEQ_PAD, :] = sq_ref[nxt]
    ybuf_ref[:, :D_CONV] = gb_ref[...] * _causal_conv(ua_ref[...], caw_ref[...], None)[0]
    row, col, causal, strict, low_f, eye = _masks(True)
    valid = (row[:, 0:1] & (SEQ_PAD - 1)) < SAMPLE_LEN
    cqkv = jnp.where(valid, _silu(_causal_conv(xq_ref[...], cqw_ref[...], None)[0]), 0.0)
    for b in seqs:
        r0 = SEQ_PAD * b
        ca_ref[b] = ua_ref[r0 + SAMPLE_LEN - 2:r0 + SAMPLE_LEN, :]
        cq_ref[b] = xq_ref[r0 + SAMPLE_LEN - 3:r0 + SAMPLE_LEN, :]

    g_all, beta_all = _gates(ab_ref[...], alog_ref[...], dtb_ref[...])
    g_all = jnp.where(valid, g_all, 0.0)
    beta_all = jnp.where(valid, beta_all, 0.0)
    gc_all = _mm_exact(low_f, g_all)
    last_sel = (col == (row | (SEQ_PAD - 1))).astype(F32)
    gl_all = _mm_exact(last_sel, gc_all)
    z = z_ref[...]
    dnn = dnn_ref[...]
    heads = range(DN_HEADS)
    qn = [_l2_normalize(cqkv[:, h * DN_DK:(h + 1) * DN_DK], DN_DK ** -0.5) for h in heads]
    kn = [_l2_normalize(cqkv[:, 1024 + h * DN_DK:1024 + (h + 1) * DN_DK], 1.0) for h in heads]
    vh = [cqkv[:, 2048 + h * DN_DV:2048 + (h + 1) * DN_DV] for h in heads]
    beta_col, gc_col, gc_row = _gate_columns(beta_all, gc_all)
    w, u, qdec, qk = _heads_intra(qn, kn, vh, beta_col, gc_col, gc_row, causal, strict, eye, 1)
    gl_col = [gl_all[:, h:h + 1] for h in heads]
    kdec = [kn[h] * jnp.exp(gl_col[h] - gc_col[h]) for h in heads]
    s_scale = [jnp.exp(x) for x in gl_col]
    rows = [slice(SEQ_PAD * b, SEQ_PAD * (b + 1)) for b in seqs]
    pairs = [(h, b) for h in heads for b in seqs]
    s_old = [si_ref[b, h] for h, b in pairs]
    r = [_mm(jnp.concatenate([w[h][rows[b]], qdec[h][rows[b]]], axis=0), s_old[n])
         for n, (h, b) in enumerate(pairs)]
    v_new = [u[h][rows[b]] - r[n][:SEQ_PAD] for n, (h, b) in enumerate(pairs)]
    upd = [_mm_tn(kdec[h][rows[b]], v_new[n]) for n, (h, b) in enumerate(pairs)]
    for n, (h, b) in enumerate(pairs):
        so_ref[b, h] = s_old[n] * s_scale[h][SEQ_PAD * b:SEQ_PAD * b + 1, :] + upd[n]
    for h in heads:
        mine = range(h * SEQ_PER_STEP, (h + 1) * SEQ_PER_STEP)
        o = (jnp.concatenate([r[n][SEQ_PAD:] for n in mine], axis=0)
             + _mm(qk[h], jnp.concatenate([v_new[n] for n in mine], axis=0)))
        lo = h * DN_DV
        ybuf_ref[:, D_CONV + lo:D_CONV + lo + DN_DV] = _rms(o, dnn) * _silu(z[:, lo:lo + DN_DV])
    for b in seqs:
        y_ref[SAMPLE_LEN * b:SAMPLE_LEN * (b + 1), :] = ybuf_ref[SEQ_PAD * b:SEQ_PAD * b + SAMPLE_LEN, :]


def _mix_sample(proj, ab, state_a, state_q, state_s, caw, cqw, alog, dtb, dnn):
    n_seq = state_s.shape[0]
    full2 = lambda i: (0, 0)
    seq_rows = lambda width, col: pl.BlockSpec((SEQ_PER_STEP * SAMPLE_LEN, width), lambda i: (i, col))
    return pl.pallas_call(
        _mix_sample_kernel,
        grid=(n_seq // SEQ_PER_STEP,),
        in_specs=[
            seq_rows(3 * D_CONV, 0), seq_rows(DN_QKV, 1), seq_rows(1024, Z_BLOCK), seq_rows(AB_LANES, 0),
            pl.BlockSpec((SEQ_PER_STEP, 2, D_CONV), lambda i: (i, 0, 0)),
            pl.BlockSpec((SEQ_PER_STEP, 3, DN_QKV), lambda i: (i, 0, 0)),
            pl.BlockSpec((SEQ_PER_STEP, DN_HEADS, DN_DK, DN_DV), lambda i: (i, 0, 0, 0)),
            pl.BlockSpec((3, D_CONV), full2),
            pl.BlockSpec((4, DN_QKV), full2),
            pl.BlockSpec((1, AB_LANES), full2),
            pl.BlockSpec((1, AB_LANES), full2),
            pl.BlockSpec((1, DN_DV), full2),
        ],
        out_specs=[
            seq_rows(D_MODEL, 0),
            pl.BlockSpec((SEQ_PER_STEP, 2, D_CONV), lambda i: (i, 0, 0)),
            pl.BlockSpec((SEQ_PER_STEP, 3, DN_QKV), lambda i: (i, 0, 0)),
            pl.BlockSpec((SEQ_PER_STEP, DN_HEADS, DN_DK, DN_DV), lambda i: (i, 0, 0, 0)),
        ],
        out_shape=[
            jax.ShapeDtypeStruct((n_seq * SAMPLE_LEN, D_MODEL), F32),
            jax.ShapeDtypeStruct((n_seq, 2, D_CONV), F32),
            jax.ShapeDtypeStruct((n_seq, 3, DN_QKV), F32),
            jax.ShapeDtypeStruct((n_seq, DN_HEADS, DN_DK, DN_DV), F32),
        ],
        scratch_shapes=[
            pltpu.VMEM((CHUNK, D_CONV), F32),
            pltpu.VMEM((CHUNK, D_CONV), F32),
            pltpu.VMEM((CHUNK, DN_QKV), F32),
            pltpu.VMEM((CHUNK, 1024), F32),
            pltpu.VMEM((CHUNK, AB_LANES), F32),
            pltpu.VMEM((CHUNK, D_MODEL), F32),
        ],
        compiler_params=_params("parallel"),
        name="mix_sample",
    )(proj, proj, proj, ab, state_a, state_q, state_s, caw, cqw, alog, dtb, dnn)


def _outproj_kernel(ya_ref, yb_ref, wa_ref, wb_ref, post_ref, x_ref, o_ref):
    m = (jnp.dot(ya_ref[...].astype(BF16), wa_ref[...], preferred_element_type=F32)
         + jnp.dot(yb_ref[...].astype(BF16), wb_ref[...], preferred_element_type=F32))
    o_ref[...] = x_ref[...] + _rms(m, post_ref[...])


def _outproj(ya, yb, yb_block, w, post, x):
    m = x.shape[0]
    half = D_MODEL // 2
    return pl.pallas_call(
        _outproj_kernel,
        grid=(m // TM,),
        in_specs=[
            pl.BlockSpec((TM, half), lambda i: (i, 0)),
            pl.BlockSpec((TM, half), lambda i: (i, yb_block)),
            pl.BlockSpec((half, D_MODEL), lambda i: (0, 0)),
            pl.BlockSpec((half, D_MODEL), lambda i: (1, 0)),
            pl.BlockSpec((1, D_MODEL), lambda i: (0, 0)),
            pl.BlockSpec((TM, D_MODEL), lambda i: (i, 0)),
        ],
        out_specs=pl.BlockSpec((TM, D_MODEL), lambda i: (i, 0)),
        out_shape=jax.ShapeDtypeStruct((m, D_MODEL), F32),
        compiler_params=_params("parallel"),
        name="outproj",
    )(ya, yb, w, w, post, x)


def _ple_kernel(x_ref, p_ref, pre_ref, post_ref, wg_ref, wp_ref, o_ref):
    x = x_ref[...]
    h = _rms(x, pre_ref[...]).astype(BF16)
    gate = jax.nn.sigmoid(jnp.dot(h, wg_ref[...], preferred_element_type=F32))
    proj = jnp.dot(p_ref[...].astype(BF16), wp_ref[...], preferred_element_type=F32)
    o_ref[...] = x + _rms(gate * proj, post_ref[...])


def _ple(x, p, pre, post, wg, wp):
    m = x.shape[0]
    return pl.pallas_call(
        _ple_kernel,
        grid=(m // TM,),
        in_specs=[
            pl.BlockSpec((TM, D_MODEL), lambda i: (i, 0)),
            pl.BlockSpec((TM, D_PLE), lambda i: (i, 0)),
            pl.BlockSpec((1, D_MODEL), lambda i: (0, 0)),
            pl.BlockSpec((1, D_MODEL), lambda i: (0, 0)),
            pl.BlockSpec((D_MODEL, D_MODEL), lambda i: (0, 0)),
            pl.BlockSpec((D_PLE, D_MODEL), lambda i: (0, 0)),
        ],
        out_specs=pl.BlockSpec((TM, D_MODEL), lambda i: (i, 0)),
        out_shape=jax.ShapeDtypeStruct((m, D_MODEL), F32),
        compiler_params=_params("parallel"),
        name="ple",
    )(x, p, pre, post, wg, wp)


def _row(v):
    return v.reshape(1, -1).astype(F32)


def _pad_lanes(v):
    return jnp.pad(_row(v), ((0, 0), (0, AB_LANES - v.shape[-1])))


def _seq_tails(tails, tiles_per_seq, n_rows):
    return tails[tiles_per_seq - 1::tiles_per_seq, 8 - n_rows:, :]


def kernel(x_prompt, x_sample, state_conv_a, state_conv_qkv, state_delta, p_prompt, p_sample,
           f1_pre, f1_post, f1_wg, f1_wu, f1_wd,
           mix_pre, mix_post, w_in, conv_a_w, conv_qkv_w, a_log, dt_bias, dn_norm, w_out,
           f2_pre, f2_post, f2_wg, f2_wu, f2_wd,
           ple_pre, ple_post, w_ple_gate, w_ple_proj):
    depth = f1_pre.shape[0]
    n_p, seq_p, _ = x_prompt.shape
    n_s, seq_s, _ = x_sample.shape
    rows_p = n_p * seq_p
    rows_s = n_s * seq_s
    assert seq_p % TM_BIG == 0 and seq_s == SAMPLE_LEN and n_s % SEQ_PER_STEP == 0
    assert rows_s == TM
    assert w_in.shape[-1] == IN_MAIN + N_AB
    tiles_per_seq = seq_p // TM_BIG
    xp = x_prompt.reshape(rows_p, D_MODEL)
    xs = x_sample.reshape(rows_s, D_MODEL)
    outs = [[] for _ in range(6)]
    for i in range(depth):
        pre, post = _row(f1_pre[i]), _row(f1_post[i])
        xs, wg, wu, wd = _ffn(xs, pre, post, f1_wg[i], f1_wu[i], f1_wd[i], TM)
        xp = _ffn(xp, pre, post, wg, wu, wd, TM_BIG)[0]

        pre = _row(mix_pre[i])
        caw, cqw = conv_a_w[i], conv_qkv_w[i]
        proj_s, ab_s, wt, wabt = _inproj_cast(xs, pre, w_in[i].T)
        y_a, h, tail_a = _proj_a(xp, pre, wt, caw, TM_BIG, tiles_per_seq)
        alog, dtb, dnn = _pad_lanes(a_log[i]), _pad_lanes(dt_bias[i]), _row(dn_norm[i])
        wy_w, qdec, kdec, wy_u, qk, egl, tail_q = _proj_qkv(h, wt, wabt, cqw, alog, dtb, seq_p // TM)
        zs = _proj_z(h, wt, TM_BIG)
        y_b, s_p = _delta_scan(wy_w, qdec, kdec, wy_u, qk, egl, zs, dnn, n_p, seq_p)
        ca_p = _seq_tails(tail_a, tiles_per_seq, 2)
        cq_p = _seq_tails(tail_q, seq_p // TM, 3)
        y_s, ca_s, cq_s, s_s = _mix_sample(
            proj_s, ab_s, state_conv_a[i], state_conv_qkv[i], state_delta[i], caw, cqw, alog, dtb, dnn)

        post = _row(mix_post[i])
        w_out_bf16 = w_out[i].astype(BF16)
        xs = _outproj(y_s, y_s, 1, w_out_bf16, post, xs)
        xp = _outproj(y_a, y_b, 0, w_out_bf16, post, xp)

        pre, post = _row(f2_pre[i]), _row(f2_post[i])
        xs, wg, wu, wd = _ffn(xs, pre, post, f2_wg[i], f2_wu[i], f2_wd[i], TM)
        xp = _ffn(xp, pre, post, wg, wu, wd, TM_BIG)[0]

        pre, post = _row(ple_pre[i]), _row(ple_post[i])
        wg, wp = w_ple_gate[i].astype(BF16), w_ple_proj[i].astype(BF16)
        xs = _ple(xs, p_sample[i].reshape(rows_s, D_PLE), pre, post, wg, wp)
        xp = _ple(xp, p_prompt[i].reshape(rows_p, D_PLE), pre, post, wg, wp)
        for lst, val in zip(outs, (ca_p, cq_p, s_p, ca_s, cq_s, s_s)):
            lst.append(val)
    return (xp.reshape(n_p, seq_p, D_MODEL), xs.reshape(n_s, seq_s, D_MODEL),
            *[jnp.stack(lst) for lst in outs])
```

```python
import functools

import jax
import jax.numpy as jnp
from jax import lax
from jax.experimental import pallas as pl
from jax.experimental.pallas import tpu as pltpu

F32 = jnp.float32
BF16 = jnp.bfloat16

D_MODEL = 2048
D_CONV = 1024
DN_HEADS = 8
DN_DK = 128
DN_DV = 128
DN_QKV = 3072
D_FF = 5632
D_PLE = 256
EPS = 1e-6
CHUNK = 64
SAMPLE_LEN = 4
SEQ_PAD = 8
SEQ_SHIFT = 3
SEQ_PER_STEP = CHUNK // SEQ_PAD
IN_MAIN = 7168
Z_BLOCK = 6
N_AB = 16
AB_LANES = 128

TM = 512
TF = 512
TF_CAST = 512
TN_IN = 1024
TM_BIG = 1024
TN_A = 512
SUB_COLS = 512
ROW_CHUNK = 32
VMEM_LIMIT = 56 * 1024 * 1024


def _rms(x, g):
    ms = jnp.mean(x * x, axis=-1, keepdims=True)
    return x * lax.rsqrt(ms + EPS) * g


def _silu(x):
    return x * jax.nn.sigmoid(x)


def _mm(a, b):
    return jnp.dot(a.astype(BF16), b.astype(BF16), preferred_element_type=F32)


def _mm_nt(a, b):
    return lax.dot_general(a.astype(BF16), b.astype(BF16), (((1,), (1,)), ((), ())),
                           preferred_element_type=F32)


def _mm_tn(a, b):
    return jnp.dot(a.T.astype(BF16), b.astype(BF16), preferred_element_type=F32)


def _mm_exact(a, b):
    return jnp.dot(a, b, precision=lax.Precision.HIGHEST, preferred_element_type=F32)


def _nt_dot(a, bt):
    return lax.dot_general(a, bt, (((1,), (1,)), ((), ())), preferred_element_type=F32)


def _for_row_chunks(n_rows, body):
    def step(k, carry):
        body(pl.ds(pl.multiple_of(k * ROW_CHUNK, ROW_CHUNK), ROW_CHUNK))
        return carry
    lax.fori_loop(0, n_rows // ROW_CHUNK, step, 0, unroll=4)


def _params(*semantics):
    return pltpu.CompilerParams(dimension_semantics=semantics, vmem_limit_bytes=VMEM_LIMIT)


def _ffn_kernel(emit_bf16, x_ref, pre_ref, post_ref, wg_ref, wu_ref, wd_ref, o_ref, *rest):
    h_ref = rest[-1]
    j = pl.program_id(1)

    @pl.when(j == 0)
    def _():
        def norm_rows(rows):
            h_ref[rows, :] = _rms(x_ref[rows, :], pre_ref[...]).astype(BF16)
            o_ref[rows, :] = jnp.zeros((ROW_CHUNK, D_MODEL), F32)
        _for_row_chunks(x_ref.shape[0], norm_rows)

    wg, wu, wd = wg_ref[...], wu_ref[...], wd_ref[...]
    if emit_bf16:
        wg, wu, wd = wg.astype(BF16), wu.astype(BF16), wd.astype(BF16)
        rest[0][...] = wg
        rest[1][...] = wu
        rest[2][...] = wd
    h = h_ref[...]
    g = jnp.dot(h, wg, preferred_element_type=F32)
    u = jnp.dot(h, wu, preferred_element_type=F32)
    a = (_silu(g) * u).astype(BF16)
    o_ref[...] += jnp.dot(a, wd, preferred_element_type=F32)

    @pl.when(j == pl.num_programs(1) - 1)
    def _():
        group = 4 * ROW_CHUNK

        def finish_rows(k, carry):
            r0 = pl.multiple_of(k * group, group)
            chunks = [pl.ds(r0 + s * ROW_CHUNK, ROW_CHUNK) for s in range(group // ROW_CHUNK)]
            scale = [0.5 * lax.rsqrt(jnp.mean(jnp.square(o_ref[c, :]), axis=-1, keepdims=True) + EPS)
                     for c in chunks]
            for c, sc in zip(chunks, scale):
                o_ref[c, :] = x_ref[c, :] + o_ref[c, :] * sc * post_ref[...]
            return carry
        lax.fori_loop(0, x_ref.shape[0] // group, finish_rows, 0)


def _ffn(x, pre, post, wg, wu, wd, tm):
    m = x.shape[0]
    emit_bf16 = wg.dtype != BF16
    tf = TF_CAST if emit_bf16 else TF
    up_spec = pl.BlockSpec((D_MODEL, tf), lambda i, j: (0, j))
    down_spec = pl.BlockSpec((tf, D_MODEL), lambda i, j: (j, 0))
    out_specs = [pl.BlockSpec((tm, D_MODEL), lambda i, j: (i, 0))]
    out_shape = [jax.ShapeDtypeStruct((m, D_MODEL), F32)]
    if emit_bf16:
        assert m == tm, "weights are re-emitted once, by a single-row-tile call"
        out_specs += [up_spec, up_spec, down_spec]
        out_shape += [jax.ShapeDtypeStruct(w.shape, BF16) for w in (wg, wu, wd)]
    outs = pl.pallas_call(
        functools.partial(_ffn_kernel, emit_bf16),
        grid=(m // tm, D_FF // tf),
        in_specs=[
            pl.BlockSpec((tm, D_MODEL), lambda i, j: (i, 0)),
            pl.BlockSpec((1, D_MODEL), lambda i, j: (0, 0)),
            pl.BlockSpec((1, D_MODEL), lambda i, j: (0, 0)),
            up_spec, up_spec, down_spec,
        ],
        out_specs=out_specs,
        out_shape=out_shape,
        scratch_shapes=[pltpu.VMEM((tm, D_MODEL), BF16)],
        compiler_params=_params("parallel", "arbitrary"),
        name="ffn_cast" if emit_bf16 else "ffn",
    )(x, pre, post, wg, wu, wd)
    return outs if emit_bf16 else (outs[0], wg, wu, wd)


def _inproj_cast_kernel(x_ref, pre_ref, wt_ref, wabt_ref, o_ref, ab_ref, wt16_ref, wabt16_ref, h_ref):
    @pl.when(pl.program_id(1) == 0)
    def _():
        h = _rms(x_ref[...], pre_ref[...]).astype(BF16)
        h_ref[...] = h
        wabt = jnp.concatenate(
            [wabt_ref[...].astype(BF16), jnp.zeros((AB_LANES - N_AB, D_MODEL), BF16)], axis=0)
        wabt16_ref[...] = wabt
        ab_ref[...] = _nt_dot(h, wabt)

    wt = wt_ref[...].astype(BF16)
    wt16_ref[...] = wt
    o_ref[...] = _nt_dot(h_ref[...], wt)


def _inproj_cast(x, pre, wt):
    m = x.shape[0]
    assert m == TM, "weights are re-emitted once, by a single-row-tile call"
    wt_spec = pl.BlockSpec((TN_IN, D_MODEL), lambda i, j: (j, 0))
    return pl.pallas_call(
        _inproj_cast_kernel,
        grid=(1, IN_MAIN // TN_IN),
        in_specs=[
            pl.BlockSpec((TM, D_MODEL), lambda i, j: (0, 0)),
            pl.BlockSpec((1, D_MODEL), lambda i, j: (0, 0)),
            wt_spec,
            pl.BlockSpec((N_AB, D_MODEL), lambda i, j: (IN_MAIN // N_AB, 0)),
        ],
        out_specs=[
            pl.BlockSpec((TM, TN_IN), lambda i, j: (0, j)),
            pl.BlockSpec((TM, AB_LANES), lambda i, j: (0, 0)),
            wt_spec,
            pl.BlockSpec((AB_LANES, D_MODEL), lambda i, j: (0, 0)),
        ],
        out_shape=[
            jax.ShapeDtypeStruct((m, IN_MAIN), F32),
            jax.ShapeDtypeStruct((m, AB_LANES), F32),
            jax.ShapeDtypeStruct((IN_MAIN, D_MODEL), BF16),
            jax.ShapeDtypeStruct((AB_LANES, D_MODEL), BF16),
        ],
        scratch_shapes=[pltpu.VMEM((TM, D_MODEL), BF16)],
        compiler_params=_params("parallel", "arbitrary"),
        name="inproj_cast",
    )(x, pre, wt, wt)


def _causal_conv(x, w, carry_in):
    width = w.shape[0]
    rows = x.shape[0]
    is_row0 = lax.broadcasted_iota(jnp.int32, (8, 1), 0) == 0
    acc = x * w[0:1, :]
    carry_out = []
    for j in range(1, width):
        delayed = pltpu.roll(acc, 1, axis=0)
        if carry_in is not None:
            carry_out.append(acc[rows - 1:rows, :])
            head = jnp.where(is_row0, carry_in[j - 1], delayed[:8])
            delayed = jnp.concatenate([head, delayed[8:]], axis=0)
        acc = x * w[j:j + 1, :] + delayed
    return acc, carry_out


def _load_carry(carry_ref, j, cols, n, seq_start):
    return [jnp.where(seq_start, 0.0, carry_ref[j, r:r + 1, cols]) for r in range(n)]


def _store_carry(carry_ref, j, cols, rows):
    for r, row in enumerate(rows):
        carry_ref[j, r:r + 1, cols] = row


def _proj_a_kernel(tiles_per_seq, x_ref, pre_ref, wgb_ref, wgcv_ref, whc_ref, wz_ref, caw_ref,
                   ya_ref, zs_ref, h_ref, tail_ref, carry_ref):
    i, j = pl.program_id(0), pl.program_id(1)
    tm = x_ref.shape[0]

    @pl.when(j == 0)
    def _():
        def norm_rows(rows):
            h_ref[rows, :] = _rms(x_ref[rows, :], pre_ref[...]).astype(BF16)
        _for_row_chunks(tm, norm_rows)

    @pl.when((i == 0) & (j == 0))
    def _():
        carry_ref[...] = jnp.zeros_like(carry_ref)

    seq_start = i % tiles_per_seq == 0
    h = h_ref[...]
    for c in range(TN_A // SUB_COLS):
        cols = slice(c * SUB_COLS, (c + 1) * SUB_COLS)
        u = _nt_dot(h, wgcv_ref[cols, :]) * _nt_dot(h, whc_ref[cols, :])
        tail_ref[0, :, cols] = u[tm - 8:tm, :]
        conv, carry = _causal_conv(u, caw_ref[:, cols], _load_carry(carry_ref, j, cols, 2, seq_start))
        _store_carry(carry_ref, j, cols, carry)
        ya_ref[:, cols] = (_nt_dot(h, wgb_ref[cols, :]) * conv).astype(BF16)
        zs_ref[:, cols] = _silu(_nt_dot(h, wz_ref[cols, :])).astype(BF16)


def _proj_a(x, pre, wt, caw, tm, tiles_per_seq):
    m = x.shape[0]
    n_col = D_CONV // TN_A
    assert DN_HEADS * DN_DV == D_CONV
    z_part = Z_BLOCK * TN_IN // D_CONV
    w_spec = lambda part: pl.BlockSpec((TN_A, D_MODEL), lambda i, j: (part * n_col + j, 0))
    cols = pl.BlockSpec((tm, TN_A), lambda i, j: (i, j))
    return pl.pallas_call(
        functools.partial(_proj_a_kernel, tiles_per_seq),
        grid=(m // tm, n_col),
        in_specs=[
            pl.BlockSpec((tm, D_MODEL), lambda i, j: (i, 0)),
            pl.BlockSpec((1, D_MODEL), lambda i, j: (0, 0)),
            w_spec(0), w_spec(1), w_spec(2), w_spec(z_part),
            pl.BlockSpec((3, TN_A), lambda i, j: (0, j)),
        ],
        out_specs=[
            cols, cols,
            pl.BlockSpec((tm, D_MODEL), lambda i, j: (i, 0)),
            pl.BlockSpec((1, 8, TN_A), lambda i, j: (i, 0, j)),
        ],
        out_shape=[
            jax.ShapeDtypeStruct((m, D_CONV), BF16),
            jax.ShapeDtypeStruct((m, DN_HEADS * DN_DV), BF16),
            jax.ShapeDtypeStruct((m, D_MODEL), BF16),
            jax.ShapeDtypeStruct((m // tm, 8, D_CONV), F32),
        ],
        scratch_shapes=[pltpu.VMEM((n_col, 8, TN_A), F32)],
        compiler_params=_params("arbitrary", "arbitrary"),
        name="proj_a",
    )(x, pre, wt, wt, wt, wt, caw)


def _l2_normalize(x, scale):
    return x * (lax.rsqrt(jnp.sum(x * x, axis=-1, keepdims=True) + EPS) * scale)


def _gate_columns(beta_all, gc_all):
    heads = range(DN_HEADS)
    gc_t = gc_all.T
    return ([beta_all[:, DN_HEADS + h:DN_HEADS + h + 1] for h in heads],
            [gc_all[:, h:h + 1] for h in heads], [gc_t[h:h + 1, :] for h in heads])


def _heads_intra(qn, kn, vh, beta_col, gc_col, gc_row, causal, strict, eye, squarings):
    heads = range(len(qn))
    decay = [jnp.exp(jnp.where(causal, gc_col[h] - gc_row[h], -jnp.inf)) for h in heads]
    kb = [kn[h] * beta_col[h] for h in heads]
    gram = [_mm_nt(jnp.concatenate([kb[h], qn[h]], axis=0), kn[h]) for h in heads]
    a_mat = [jnp.where(strict, gram[h][:CHUNK] * decay[h], 0.0) for h in heads]
    qk = [jnp.where(causal, gram[h][CHUNK:] * decay[h], 0.0) for h in heads]
    inv = [eye - a for a in a_mat]
    power = a_mat
    for _ in range(squarings):
        power = [_mm(x, x) for x in power]
        inv = [_mm(inv[h], eye + power[h]) for h in heads]
    egc = [jnp.exp(x) for x in gc_col]
    wu = [_mm(inv[h], jnp.concatenate([kb[h] * egc[h], vh[h] * beta_col[h]], axis=1)) for h in heads]
    w = [x[:, :DN_DK] for x in wu]
    u = [x[:, DN_DK:] for x in wu]
    qdec = [qn[h] * egc[h] for h in heads]
    return w, u, qdec, qk


def _gates(ab, alog, dtb):
    x = ab + dtb
    softplus = jnp.maximum(x, 0.0) + jnp.log1p(jnp.exp(-jnp.abs(x)))
    return -jnp.exp(alog) * softplus, jax.nn.sigmoid(ab)


def _masks(same_seq):
    row = lax.broadcasted_iota(jnp.int32, (CHUNK, CHUNK), 0)
    col = lax.broadcasted_iota(jnp.int32, (CHUNK, CHUNK), 1)
    causal = row >= col
    strict = row > col
    if same_seq:
        same = (row >> SEQ_SHIFT) == (col >> SEQ_SHIFT)
        causal = causal & same
        strict = strict & same
    eye = jnp.where(row == col, 1.0, 0.0).astype(F32)
    return row, col, causal, strict, causal.astype(F32), eye


def _proj_qkv_kernel(tiles_per_seq, h_ref, wq_ref, wk_ref, wv_ref, wabt_ref, cqw_ref, alog_ref, dtb_ref,
                     w_ref, qd_ref, kd_ref, u_ref, qk_ref, egl_ref, tail_ref, carry_ref, qkv_ref):
    i = pl.program_id(0)
    tm = h_ref.shape[0]

    @pl.when(i == 0)
    def _():
        carry_ref[...] = jnp.zeros_like(carry_ref)

    seq_start = i % tiles_per_seq == 0
    h = h_ref[...]
    g_all, beta_all = _gates(_nt_dot(h, wabt_ref[...]), alog_ref[...], dtb_ref[...])
    for t, wt_ref in enumerate((wq_ref, wk_ref, wv_ref)):
        for c in range(TN_IN // SUB_COLS):
            cols = slice(c * SUB_COLS, (c + 1) * SUB_COLS)
            lo = t * TN_IN + c * SUB_COLS
            p = _nt_dot(h, wt_ref[cols, :])
            tail_ref[0, :, lo:lo + SUB_COLS] = p[tm - 8:tm, :]
            p, carry = _causal_conv(p, cqw_ref[:, lo:lo + SUB_COLS],
                                    _load_carry(carry_ref, t, cols, 3, seq_start))
            _store_carry(carry_ref, t, cols, carry)
            p = _silu(p)
            if t == 2:
                qkv_ref[:, lo:lo + SUB_COLS] = p
            else:
                for k in range(SUB_COLS // DN_DK):
                    blk = p[:, k * DN_DK:(k + 1) * DN_DK]
                    qkv_ref[:, lo + k * DN_DK:lo + (k + 1) * DN_DK] = _l2_normalize(
                        blk, DN_DK ** -0.5 if t == 0 else 1.0)

    heads = range(DN_HEADS)
    _, _, causal, strict, low_f, eye = _masks(False)
    chunk_rows = [slice(g * CHUNK, (g + 1) * CHUNK) for g in range(tm // CHUNK)]
    qn, kn, vh, beta_col, gc_col, gc_row = [], [], [], [], [], []
    for rows in chunk_rows:
        gc_all = _mm_exact(low_f, g_all[rows])
        egl_ref[rows, :] = jnp.broadcast_to(jnp.exp(gc_all[CHUNK - 1:CHUNK, :]), (CHUNK, AB_LANES))
        cols = _gate_columns(beta_all[rows], gc_all)
        beta_col += cols[0]
        gc_col += cols[1]
        gc_row += cols[2]
        qn += [qkv_ref[rows, h * DN_DK:(h + 1) * DN_DK] for h in heads]
        kn += [qkv_ref[rows, (DN_HEADS + h) * DN_DK:(DN_HEADS + h + 1) * DN_DK] for h in heads]
        vh += [qkv_ref[rows, (2 * DN_HEADS + h) * DN_DV:(2 * DN_HEADS + h + 1) * DN_DV] for h in heads]
    w, u, qdec, qk = _heads_intra(qn, kn, vh, beta_col, gc_col, gc_row, causal, strict, eye, 5)
    for n in range(len(w)):
        rows, hd = chunk_rows[n // DN_HEADS], n % DN_HEADS
        lo = hd * DN_DK
        g_last = gc_col[n][CHUNK - 1:CHUNK, :]
        w_ref[rows, lo:lo + DN_DK] = w[n].astype(BF16)
        qd_ref[rows, lo:lo + DN_DK] = qdec[n].astype(BF16)
        kd_ref[rows, lo:lo + DN_DK] = (kn[n] * jnp.exp(g_last - gc_col[n])).astype(BF16)
        u_ref[rows, lo:lo + DN_DV] = u[n]
        qk_ref[rows, hd * CHUNK:(hd + 1) * CHUNK] = qk[n].astype(BF16)


def _proj_qkv(h, wt, wabt, cqw, alog, dtb, tiles_per_seq):
    m = h.shape[0]
    hk, hv = DN_HEADS * DN_DK, DN_HEADS * DN_DV
    rows = lambda width: pl.BlockSpec((TM, width), lambda i: (i, 0))
    whole = lambda shape: pl.BlockSpec(shape, lambda i: (0,) * len(shape))
    w_tile = lambda t: pl.BlockSpec((TN_IN, D_MODEL), lambda i: (3 + t, 0))
    return pl.pallas_call(
        functools.partial(_proj_qkv_kernel, tiles_per_seq),
        grid=(m // TM,),
        in_specs=[rows(D_MODEL), w_tile(0), w_tile(1), w_tile(2), whole((AB_LANES, D_MODEL)),
                  whole((4, DN_QKV)), whole((1, AB_LANES)), whole((1, AB_LANES))],
        out_specs=[rows(hk), rows(hk), rows(hk), rows(hv), rows(DN_HEADS * CHUNK), rows(AB_LANES),
                   pl.BlockSpec((1, 8, DN_QKV), lambda i: (i, 0, 0))],
        out_shape=[
            jax.ShapeDtypeStruct((m, hk), BF16), jax.ShapeDtypeStruct((m, hk), BF16),
            jax.ShapeDtypeStruct((m, hk), BF16), jax.ShapeDtypeStruct((m, hv), F32),
            jax.ShapeDtypeStruct((m, DN_HEADS * CHUNK), BF16), jax.ShapeDtypeStruct((m, AB_LANES), F32),
            jax.ShapeDtypeStruct((m // TM, 8, DN_QKV), F32),
        ],
        scratch_shapes=[pltpu.VMEM((3, 8, TN_IN), F32), pltpu.VMEM((TM, DN_QKV), F32)],
        compiler_params=_params("arbitrary"),
        name="proj_qkv",
    )(h, wt, wt, wt, wabt, cqw, alog, dtb)


def _delta_scan_kernel(w_ref, qd_ref, kd_ref, u_ref, qk_ref, egl_ref, zs_ref, dnn_ref,
                       y_ref, so_ref, s_ref):
    c = pl.program_id(1)

    @pl.when(c == 0)
    def _():
        s_ref[...] = jnp.zeros_like(s_ref)

    heads = range(DN_HEADS)
    dnn = dnn_ref[...]
    s = [s_ref[h] for h in heads]
    for g in range(w_ref.shape[0] // CHUNK):
        rows = slice(g * CHUNK, (g + 1) * CHUNK)
        col = [slice(h * DN_DK, (h + 1) * DN_DK) for h in heads]
        lhs = [jnp.concatenate([w_ref[rows, col[h]], qd_ref[rows, col[h]]], axis=0) for h in heads]
        r = [jnp.dot(lhs[h], s[h].astype(BF16), preferred_element_type=F32) for h in heads]
        v_new = [(u_ref[rows, col[h]] - r[h][:CHUNK]).astype(BF16) for h in heads]
        upd = [lax.dot_general(kd_ref[rows, col[h]], v_new[h], (((0,), (0,)), ((), ())),
                               preferred_element_type=F32) for h in heads]
        s = [s[h] * egl_ref[g * CHUNK:g * CHUNK + 1, h:h + 1] + upd[h] for h in heads]
        o = [r[h][CHUNK:] + jnp.dot(qk_ref[rows, h * CHUNK:(h + 1) * CHUNK], v_new[h],
                                    preferred_element_type=F32) for h in heads]
        for h in heads:
            y_ref[rows, col[h]] = (_rms(o[h], dnn) * zs_ref[rows, col[h]]).astype(y_ref.dtype)
    for h in heads:
        s_ref[h] = s[h]

    @pl.when(c == pl.num_programs(1) - 1)
    def _():
        so_ref[0] = s_ref[...]


def _delta_scan(w, qd, kd, u, qk, egl, zs, dnn, n_seq, seq_len):
    n_steps = seq_len // TM
    hk, hv = DN_HEADS * DN_DK, DN_HEADS * DN_DV
    rows = lambda width: pl.BlockSpec((TM, width), lambda b, c: (b * n_steps + c, 0))
    return pl.pallas_call(
        _delta_scan_kernel,
        grid=(n_seq, n_steps),
        in_specs=[rows(hk), rows(hk), rows(hk), rows(hv), rows(DN_HEADS * CHUNK), rows(AB_LANES), rows(hv),
                  pl.BlockSpec((1, DN_DV), lambda b, c: (0, 0))],
        out_specs=[rows(hv), pl.BlockSpec((1, DN_HEADS, DN_DK, DN_DV), lambda b, c: (b, 0, 0, 0))],
        out_shape=[jax.ShapeDtypeStruct((n_seq * seq_len, hv), BF16),
                   jax.ShapeDtypeStruct((n_seq, DN_HEADS, DN_DK, DN_DV), F32)],
        scratch_shapes=[pltpu.VMEM((DN_HEADS, DN_DK, DN_DV), F32)],
        compiler_params=_params("arbitrary", "arbitrary"),
        name="delta_scan",
    )(w, qd, kd, u, qk, egl, zs, dnn)


def _mix_sample_kernel(pa_ref, pq_ref, pz_ref, pab_ref, sa_ref, sq_ref, si_ref,
                       caw_ref, cqw_ref, alog_ref, dtb_ref, dnn_ref,
                       y_ref, ca_ref, cq_ref, so_ref,
                       gb_ref, ua_ref, xq_ref, z_ref, ab_ref, ybuf_ref):
    seqs = range(SEQ_PER_STEP)
    pad = SEQ_PAD - SAMPLE_LEN
    for b in seqs:
        r0 = SEQ_PAD * b
        nxt = (b + 1) % SEQ_PER_STEP
        src = slice(SAMPLE_LEN * b, SAMPLE_LEN * (b + 1))
        pa = pa_ref[src, :]
        gb_ref[r0:r0 + SAMPLE_LEN, :] = pa[:, :D_CONV]
        ua_ref[r0:r0 + SAMPLE_LEN, :] = pa[:, D_CONV:2 * D_CONV] * pa[:, 2 * D_CONV:]
        xq_ref[r0:r0 + SAMPLE_LEN, :] = pq_ref[src, :]
        z_ref[r0:r0 + SAMPLE_LEN, :] = pz_ref[src, :]
        ab_ref[r0:r0 + SAMPLE_LEN, :] = pab_ref[src, :]
        for ref in (gb_ref, ua_ref, xq_ref, z_ref, ab_ref):
            ref[r0 + SAMPLE_LEN:r0 + SEQ_PAD, :] = jnp.zeros((pad, ref.shape[1]), F32)
        ua_ref[r0 + SEQ_PAD - 2:r0 + SEQ_PAD, :] = sa_ref[nxt]
        xq_ref[r0 + SEQ_PAD - 3:r0 + SEQ_PAD, :] = sq_ref[nxt]
    ybuf_ref[:, :D_CONV] = gb_ref[...] * _causal_conv(ua_ref[...], caw_ref[...], None)[0]
    row, col, causal, strict, low_f, eye = _masks(True)
    valid = (row[:, 0:1] & (SEQ_PAD - 1)) < SAMPLE_LEN
    cqkv = jnp.where(valid, _silu(_causal_conv(xq_ref[...], cqw_ref[...], None)[0]), 0.0)
    for b in seqs:
        r0 = SEQ_PAD * b
        ca_ref[b] = ua_ref[r0 + SAMPLE_LEN - 2:r0 + SAMPLE_LEN, :]
        cq_ref[b] = xq_ref[r0 + SAMPLE_LEN - 3:r0 + SAMPLE_LEN, :]

    g_all, beta_all = _gates(ab_ref[...], alog_ref[...], dtb_ref[...])
    g_all = jnp.where(valid, g_all, 0.0)
    beta_all = jnp.where(valid, beta_all, 0.0)
    gc_all = _mm_exact(low_f, g_all)
    last_sel = (col == (row | (SEQ_PAD - 1))).astype(F32)
    gl_all = _mm_exact(last_sel, gc_all)
    z = z_ref[...]
    dnn = dnn_ref[...]
    heads = range(DN_HEADS)
    qn = [_l2_normalize(cqkv[:, h * DN_DK:(h + 1) * DN_DK], DN_DK ** -0.5) for h in heads]
    kn = [_l2_normalize(cqkv[:, 1024 + h * DN_DK:1024 + (h + 1) * DN_DK], 1.0) for h in heads]
    vh = [cqkv[:, 2048 + h * DN_DV:2048 + (h + 1) * DN_DV] for h in heads]
    beta_col, gc_col, gc_row = _gate_columns(beta_all, gc_all)
    w, u, qdec, qk = _heads_intra(qn, kn, vh, beta_col, gc_col, gc_row, causal, strict, eye, 1)
    gl_col = [gl_all[:, h:h + 1] for h in heads]
    kdec = [kn[h] * jnp.exp(gl_col[h] - gc_col[h]) for h in heads]
    s_scale = [jnp.exp(x) for x in gl_col]
    rows = [slice(SEQ_PAD * b, SEQ_PAD * (b + 1)) for b in seqs]
    pairs = [(h, b) for h in heads for b in seqs]
    s_old = [si_ref[b, h] for h, b in pairs]
    r = [_mm(jnp.concatenate([w[h][rows[b]], qdec[h][rows[b]]], axis=0), s_old[n])
         for n, (h, b) in enumerate(pairs)]
    v_new = [u[h][rows[b]] - r[n][:SEQ_PAD] for n, (h, b) in enumerate(pairs)]
    upd = [_mm_tn(kdec[h][rows[b]], v_new[n]) for n, (h, b) in enumerate(pairs)]
    for n, (h, b) in enumerate(pairs):
        so_ref[b, h] = s_old[n] * s_scale[h][SEQ_PAD * b:SEQ_PAD * b + 1, :] + upd[n]
    for h in heads:
        mine = range(h * SEQ_PER_STEP, (h + 1) * SEQ_PER_STEP)
        o = (jnp.concatenate([r[n][SEQ_PAD:] for n in mine], axis=0)
             + _mm(qk[h], jnp.concatenate([v_new[n] for n in mine], axis=0)))
        lo = h * DN_DV
        ybuf_ref[:, D_CONV + lo:D_CONV + lo + DN_DV] = _rms(o, dnn) * _silu(z[:, lo:lo + DN_DV])
    for b in seqs:
        y_ref[SAMPLE_LEN * b:SAMPLE_LEN * (b + 1), :] = ybuf_ref[SEQ_PAD * b:SEQ_PAD * b + SAMPLE_LEN, :]


def _mix_sample(proj, ab, state_a, state_q, state_s, caw, cqw, alog, dtb, dnn):
    n_seq = state_s.shape[0]
    full2 = lambda i: (0, 0)
    seq_rows = lambda width, col: pl.BlockSpec((SEQ_PER_STEP * SAMPLE_LEN, width), lambda i: (i, col))
    return pl.pallas_call(
        _mix_sample_kernel,
        grid=(n_seq // SEQ_PER_STEP,),
        in_specs=[
            seq_rows(3 * D_CONV, 0), seq_rows(DN_QKV, 1), seq_rows(1024, Z_BLOCK), seq_rows(AB_LANES, 0),
            pl.BlockSpec((SEQ_PER_STEP, 2, D_CONV), lambda i: (i, 0, 0)),
            pl.BlockSpec((SEQ_PER_STEP, 3, DN_QKV), lambda i: (i, 0, 0)),
            pl.BlockSpec((SEQ_PER_STEP, DN_HEADS, DN_DK, DN_DV), lambda i: (i, 0, 0, 0)),
            pl.BlockSpec((3, D_CONV), full2),
            pl.BlockSpec((4, DN_QKV), full2),
            pl.BlockSpec((1, AB_LANES), full2),
            pl.BlockSpec((1, AB_LANES), full2),
            pl.BlockSpec((1, DN_DV), full2),
        ],
        out_specs=[
            seq_rows(D_MODEL, 0),
            pl.BlockSpec((SEQ_PER_STEP, 2, D_CONV), lambda i: (i, 0, 0)),
            pl.BlockSpec((SEQ_PER_STEP, 3, DN_QKV), lambda i: (i, 0, 0)),
            pl.BlockSpec((SEQ_PER_STEP, DN_HEADS, DN_DK, DN_DV), lambda i: (i, 0, 0, 0)),
        ],
        out_shape=[
            jax.ShapeDtypeStruct((n_seq * SAMPLE_LEN, D_MODEL), F32),
            jax.ShapeDtypeStruct((n_seq, 2, D_CONV), F32),
            jax.ShapeDtypeStruct((n_seq, 3, DN_QKV), F32),
            jax.ShapeDtypeStruct((n_seq, DN_HEADS, DN_DK, DN_DV), F32),
        ],
        scratch_shapes=[
            pltpu.VMEM((CHUNK, D_CONV), F32),
            pltpu.VMEM((CHUNK, D_CONV), F32),
            pltpu.VMEM((CHUNK, DN_QKV), F32),
            pltpu.VMEM((CHUNK, 1024), F32),
            pltpu.VMEM((CHUNK, AB_LANES), F32),
            pltpu.VMEM((CHUNK, D_MODEL), F32),
        ],
        compiler_params=_params("parallel"),
        name="mix_sample",
    )(proj, proj, proj, ab, state_a, state_q, state_s, caw, cqw, alog, dtb, dnn)


def _outproj_kernel(ya_ref, yb_ref, wa_ref, wb_ref, post_ref, x_ref, o_ref):
    m = (jnp.dot(ya_ref[...].astype(BF16), wa_ref[...], preferred_element_type=F32)
         + jnp.dot(yb_ref[...].astype(BF16), wb_ref[...], preferred_element_type=F32))
    o_ref[...] = x_ref[...] + _rms(m, post_ref[...])


def _outproj(ya, yb, yb_block, w, post, x):
    m = x.shape[0]
    half = D_MODEL // 2
    return pl.pallas_call(
        _outproj_kernel,
        grid=(m // TM,),
        in_specs=[
            pl.BlockSpec((TM, half), lambda i: (i, 0)),
            pl.BlockSpec((TM, half), lambda i: (i, yb_block)),
            pl.BlockSpec((half, D_MODEL), lambda i: (0, 0)),
            pl.BlockSpec((half, D_MODEL), lambda i: (1, 0)),
            pl.BlockSpec((1, D_MODEL), lambda i: (0, 0)),
            pl.BlockSpec((TM, D_MODEL), lambda i: (i, 0)),
        ],
        out_specs=pl.BlockSpec((TM, D_MODEL), lambda i: (i, 0)),
        out_shape=jax.ShapeDtypeStruct((m, D_MODEL), F32),
        compiler_params=_params("parallel"),
        name="outproj",
    )(ya, yb, w, w, post, x)


def _ple_kernel(x_ref, p_ref, pre_ref, post_ref, wg_ref, wp_ref, o_ref):
    x = x_ref[...]
    h = _rms(x, pre_ref[...]).astype(BF16)
    gate = jax.nn.sigmoid(jnp.dot(h, wg_ref[...], preferred_element_type=F32))
    proj = jnp.dot(p_ref[...].astype(BF16), wp_ref[...], preferred_element_type=F32)
    o_ref[...] = x + _rms(gate * proj, post_ref[...])


def _ple(x, p, pre, post, wg, wp):
    m = x.shape[0]
    return pl.pallas_call(
        _ple_kernel,
        grid=(m // TM,),
        in_specs=[
            pl.BlockSpec((TM, D_MODEL), lambda i: (i, 0)),
            pl.BlockSpec((TM, D_PLE), lambda i: (i, 0)),
            pl.BlockSpec((1, D_MODEL), lambda i: (0, 0)),
            pl.BlockSpec((1, D_MODEL), lambda i: (0, 0)),
            pl.BlockSpec((D_MODEL, D_MODEL), lambda i: (0, 0)),
            pl.BlockSpec((D_PLE, D_MODEL), lambda i: (0, 0)),
        ],
        out_specs=pl.BlockSpec((TM, D_MODEL), lambda i: (i, 0)),
        out_shape=jax.ShapeDtypeStruct((m, D_MODEL), F32),
        compiler_params=_params("parallel"),
        name="ple",
    )(x, p, pre, post, wg, wp)


def _row(v):
    return v.reshape(1, -1).astype(F32)


def _pad_lanes(v):
    return jnp.pad(_row(v), ((0, 0), (0, AB_LANES - v.shape[-1])))


def _seq_tails(tails, tiles_per_seq, n_rows):
    return tails[tiles_per_seq - 1::tiles_per_seq, 8 - n_rows:, :]


def kernel(x_prompt, x_sample, state_conv_a, state_conv_qkv, state_delta, p_prompt, p_sample,
           f1_pre, f1_post, f1_wg, f1_wu, f1_wd,
           mix_pre, mix_post, w_in, conv_a_w, conv_qkv_w, a_log, dt_bias, dn_norm, w_out,
           f2_pre, f2_post, f2_wg, f2_wu, f2_wd,
           ple_pre, ple_post, w_ple_gate, w_ple_proj):
    depth = f1_pre.shape[0]
    n_p, seq_p, _ = x_prompt.shape
    n_s, seq_s, _ = x_sample.shape
    rows_p = n_p * seq_p
    rows_s = n_s * seq_s
    assert seq_p % TM_BIG == 0 and seq_s == SAMPLE_LEN and n_s % SEQ_PER_STEP == 0
    assert rows_s == TM
    assert w_in.shape[-1] == IN_MAIN + N_AB
    tiles_per_seq = seq_p // TM_BIG
    xp = x_prompt.reshape(rows_p, D_MODEL)
    xs = x_sample.reshape(rows_s, D_MODEL)
    outs = [[] for _ in range(6)]
    for i in range(depth):
        pre, post = _row(f1_pre[i]), _row(f1_post[i])
        xs, wg, wu, wd = _ffn(xs, pre, post, f1_wg[i], f1_wu[i], f1_wd[i], TM)
        xp = _ffn(xp, pre, post, wg, wu, wd, TM_BIG)[0]

        pre = _row(mix_pre[i])
        caw, cqw = conv_a_w[i], conv_qkv_w[i]
        proj_s, ab_s, wt, wabt = _inproj_cast(xs, pre, w_in[i].T)
        y_a, zs, h, tail_a = _proj_a(xp, pre, wt, caw, TM_BIG, tiles_per_seq)
        alog, dtb, dnn = _pad_lanes(a_log[i]), _pad_lanes(dt_bias[i]), _row(dn_norm[i])
        wy_w, qdec, kdec, wy_u, qk, egl, tail_q = _proj_qkv(h, wt, wabt, cqw, alog, dtb, seq_p // TM)
        y_b, s_p = _delta_scan(wy_w, qdec, kdec, wy_u, qk, egl, zs, dnn, n_p, seq_p)
        ca_p = _seq_tails(tail_a, tiles_per_seq, 2)
        cq_p = _seq_tails(tail_q, seq_p // TM, 3)
        y_s, ca_s, cq_s, s_s = _mix_sample(
            proj_s, ab_s, state_conv_a[i], state_conv_qkv[i], state_delta[i], caw, cqw, alog, dtb, dnn)

        post = _row(mix_post[i])
        w_out_bf16 = w_out[i].astype(BF16)
        xs = _outproj(y_s, y_s, 1, w_out_bf16, post, xs)
        xp = _outproj(y_a, y_b, 0, w_out_bf16, post, xp)

        pre, post = _row(f2_pre[i]), _row(f2_post[i])
        xs, wg, wu, wd = _ffn(xs, pre, post, f2_wg[i], f2_wu[i], f2_wd[i], TM)
        xp = _ffn(xp, pre, post, wg, wu, wd, TM_BIG)[0]

        pre, post = _row(ple_pre[i]), _row(ple_post[i])
        wg, wp = w_ple_gate[i].astype(BF16), w_ple_proj[i].astype(BF16)
        xs = _ple(xs, p_sample[i].reshape(rows_s, D_PLE), pre, post, wg, wp)
        xp = _ple(xp, p_prompt[i].reshape(rows_p, D_PLE), pre, post, wg, wp)
        for lst, val in zip(outs, (ca_p, cq_p, s_p, ca_s, cq_s, s_s)):
            lst.append(val)
    return (xp.reshape(n_p, seq_p, D_MODEL), xs.reshape(n_s, seq_s, D_MODEL),
            *[jnp.stack(lst) for lst in outs])
```

```python
import functools

import jax
import jax.numpy as jnp
from jax import lax
from jax.experimental import pallas as pl
from jax.experimental.pallas import tpu as pltpu

F32 = jnp.float32
BF16 = jnp.bfloat16

D_MODEL = 2048
D_CONV = 1024
DN_HEADS = 8
DN_DK = 128
DN_DV = 128
DN_QKV = 3072
D_FF = 5632
D_PLE = 256
EPS = 1e-6
CHUNK = 64
SAMPLE_LEN = 4
SEQ_PAD = 8
SEQ_SHIFT = 3
SEQ_PER_STEP = CHUNK // SEQ_PAD
IN_MAIN = 7168
Z_BLOCK = 6
N_AB = 16
AB_LANES = 128

TM = 512
TF = 512
TF_CAST = 256
TN_IN = 1024
TM_BIG = 1024
TN_A = 512
SUB_COLS = 512
ROW_CHUNK = 32
VMEM_LIMIT = 56 * 1024 * 1024


def _rms(x, g):
    ms = jnp.mean(x * x, axis=-1, keepdims=True)
    return x * lax.rsqrt(ms + EPS) * g


def _silu(x):
    return x * jax.nn.sigmoid(x)


def _mm(a, b):
    return jnp.dot(a.astype(BF16), b.astype(BF16), preferred_element_type=F32)


def _mm_nt(a, b):
    return lax.dot_general(a.astype(BF16), b.astype(BF16), (((1,), (1,)), ((), ())),
                           preferred_element_type=F32)


def _mm_tn(a, b):
    return jnp.dot(a.T.astype(BF16), b.astype(BF16), preferred_element_type=F32)


def _mm_exact(a, b):
    return jnp.dot(a, b, precision=lax.Precision.HIGHEST, preferred_element_type=F32)


def _nt_dot(a, bt):
    return lax.dot_general(a, bt, (((1,), (1,)), ((), ())), preferred_element_type=F32)


def _for_row_chunks(n_rows, body):
    def step(k, carry):
        body(pl.ds(pl.multiple_of(k * ROW_CHUNK, ROW_CHUNK), ROW_CHUNK))
        return carry
    lax.fori_loop(0, n_rows // ROW_CHUNK, step, 0, unroll=4)


def _params(*semantics):
    return pltpu.CompilerParams(dimension_semantics=semantics, vmem_limit_bytes=VMEM_LIMIT)


def _ffn_kernel(emit_bf16, x_ref, pre_ref, post_ref, wg_ref, wu_ref, wd_ref, o_ref, *rest):
    h_ref = rest[-1]
    j = pl.program_id(1)

    @pl.when(j == 0)
    def _():
        def norm_rows(rows):
            h_ref[rows, :] = _rms(x_ref[rows, :], pre_ref[...]).astype(BF16)
            o_ref[rows, :] = jnp.zeros((ROW_CHUNK, D_MODEL), F32)
        _for_row_chunks(x_ref.shape[0], norm_rows)

    wg, wu, wd = wg_ref[...], wu_ref[...], wd_ref[...]
    if emit_bf16:
        wg, wu, wd = wg.astype(BF16), wu.astype(BF16), wd.astype(BF16)
        rest[0][...] = wg
        rest[1][...] = wu
        rest[2][...] = wd
    h = h_ref[...]
    g = jnp.dot(h, wg, preferred_element_type=F32)
    u = jnp.dot(h, wu, preferred_element_type=F32)
    a = (_silu(g) * u).astype(BF16)
    o_ref[...] += jnp.dot(a, wd, preferred_element_type=F32)

    @pl.when(j == pl.num_programs(1) - 1)
    def _():
        group = 4 * ROW_CHUNK

        def finish_rows(k, carry):
            r0 = pl.multiple_of(k * group, group)
            chunks = [pl.ds(r0 + s * ROW_CHUNK, ROW_CHUNK) for s in range(group // ROW_CHUNK)]
            scale = [0.5 * lax.rsqrt(jnp.mean(jnp.square(o_ref[c, :]), axis=-1, keepdims=True) + EPS)
                     for c in chunks]
            for c, sc in zip(chunks, scale):
                o_ref[c, :] = x_ref[c, :] + o_ref[c, :] * sc * post_ref[...]
            return carry
        lax.fori_loop(0, x_ref.shape[0] // group, finish_rows, 0)


def _ffn(x, pre, post, wg, wu, wd, tm):
    m = x.shape[0]
    emit_bf16 = wg.dtype != BF16
    tf = TF_CAST if emit_bf16 else TF
    up_spec = pl.BlockSpec((D_MODEL, tf), lambda i, j: (0, j))
    down_spec = pl.BlockSpec((tf, D_MODEL), lambda i, j: (j, 0))
    out_specs = [pl.BlockSpec((tm, D_MODEL), lambda i, j: (i, 0))]
    out_shape = [jax.ShapeDtypeStruct((m, D_MODEL), F32)]
    if emit_bf16:
        assert m == tm, "weights are re-emitted once, by a single-row-tile call"
        out_specs += [up_spec, up_spec, down_spec]
        out_shape += [jax.ShapeDtypeStruct(w.shape, BF16) for w in (wg, wu, wd)]
    outs = pl.pallas_call(
        functools.partial(_ffn_kernel, emit_bf16),
        grid=(m // tm, D_FF // tf),
        in_specs=[
            pl.BlockSpec((tm, D_MODEL), lambda i, j: (i, 0)),
            pl.BlockSpec((1, D_MODEL), lambda i, j: (0, 0)),
            pl.BlockSpec((1, D_MODEL), lambda i, j: (0, 0)),
            up_spec, up_spec, down_spec,
        ],
        out_specs=out_specs,
        out_shape=out_shape,
        scratch_shapes=[pltpu.VMEM((tm, D_MODEL), BF16)],
        compiler_params=_params("parallel", "arbitrary"),
        name="ffn_cast" if emit_bf16 else "ffn",
    )(x, pre, post, wg, wu, wd)
    return outs if emit_bf16 else (outs[0], wg, wu, wd)


def _inproj_cast_kernel(x_ref, pre_ref, wt_ref, wabt_ref, o_ref, ab_ref, wt16_ref, wabt16_ref, h_ref):
    @pl.when(pl.program_id(1) == 0)
    def _():
        h = _rms(x_ref[...], pre_ref[...]).astype(BF16)
        h_ref[...] = h
        wabt = jnp.concatenate(
            [wabt_ref[...].astype(BF16), jnp.zeros((AB_LANES - N_AB, D_MODEL), BF16)], axis=0)
        wabt16_ref[...] = wabt
        ab_ref[...] = _nt_dot(h, wabt)

    wt = wt_ref[...].astype(BF16)
    wt16_ref[...] = wt
    o_ref[...] = _nt_dot(h_ref[...], wt)


def _inproj_cast(x, pre, wt):
    m = x.shape[0]
    assert m == TM, "weights are re-emitted once, by a single-row-tile call"
    wt_spec = pl.BlockSpec((TN_IN, D_MODEL), lambda i, j: (j, 0))
    return pl.pallas_call(
        _inproj_cast_kernel,
        grid=(1, IN_MAIN // TN_IN),
        in_specs=[
            pl.BlockSpec((TM, D_MODEL), lambda i, j: (0, 0)),
            pl.BlockSpec((1, D_MODEL), lambda i, j: (0, 0)),
            wt_spec,
            pl.BlockSpec((N_AB, D_MODEL), lambda i, j: (IN_MAIN // N_AB, 0)),
        ],
        out_specs=[
            pl.BlockSpec((TM, TN_IN), lambda i, j: (0, j)),
            pl.BlockSpec((TM, AB_LANES), lambda i, j: (0, 0)),
            wt_spec,
            pl.BlockSpec((AB_LANES, D_MODEL), lambda i, j: (0, 0)),
        ],
        out_shape=[
            jax.ShapeDtypeStruct((m, IN_MAIN), F32),
            jax.ShapeDtypeStruct((m, AB_LANES), F32),
            jax.ShapeDtypeStruct((IN_MAIN, D_MODEL), BF16),
            jax.ShapeDtypeStruct((AB_LANES, D_MODEL), BF16),
        ],
        scratch_shapes=[pltpu.VMEM((TM, D_MODEL), BF16)],
        compiler_params=_params("parallel", "arbitrary"),
        name="inproj_cast",
    )(x, pre, wt, wt)


def _causal_conv(x, w, carry_in):
    width = w.shape[0]
    rows = x.shape[0]
    is_row0 = lax.broadcasted_iota(jnp.int32, (8, 1), 0) == 0
    acc = x * w[0:1, :]
    carry_out = []
    for j in range(1, width):
        delayed = pltpu.roll(acc, 1, axis=0)
        if carry_in is not None:
            carry_out.append(acc[rows - 1:rows, :])
            head = jnp.where(is_row0, carry_in[j - 1], delayed[:8])
            delayed = jnp.concatenate([head, delayed[8:]], axis=0)
        acc = x * w[j:j + 1, :] + delayed
    return acc, carry_out


def _load_carry(carry_ref, j, cols, n, seq_start):
    return [jnp.where(seq_start, 0.0, carry_ref[j, r:r + 1, cols]) for r in range(n)]


def _store_carry(carry_ref, j, cols, rows):
    for r, row in enumerate(rows):
        carry_ref[j, r:r + 1, cols] = row


def _proj_a_kernel(tiles_per_seq, x_ref, pre_ref, wgb_ref, wgcv_ref, whc_ref, wz_ref, caw_ref,
                   ya_ref, zs_ref, h_ref, tail_ref, carry_ref):
    i, j = pl.program_id(0), pl.program_id(1)
    tm = x_ref.shape[0]

    @pl.when(j == 0)
    def _():
        def norm_rows(rows):
            h_ref[rows, :] = _rms(x_ref[rows, :], pre_ref[...]).astype(BF16)
        _for_row_chunks(tm, norm_rows)

    @pl.when((i == 0) & (j == 0))
    def _():
        carry_ref[...] = jnp.zeros_like(carry_ref)

    seq_start = i % tiles_per_seq == 0
    h = h_ref[...]
    for c in range(TN_A // SUB_COLS):
        cols = slice(c * SUB_COLS, (c + 1) * SUB_COLS)
        u = _nt_dot(h, wgcv_ref[cols, :]) * _nt_dot(h, whc_ref[cols, :])
        tail_ref[0, :, cols] = u[tm - 8:tm, :]
        conv, carry = _causal_conv(u, caw_ref[:, cols], _load_carry(carry_ref, j, cols, 2, seq_start))
        _store_carry(carry_ref, j, cols, carry)
        ya_ref[:, cols] = (_nt_dot(h, wgb_ref[cols, :]) * conv).astype(BF16)
        zs_ref[:, cols] = _silu(_nt_dot(h, wz_ref[cols, :])).astype(BF16)


def _proj_a(x, pre, wt, caw, tm, tiles_per_seq):
    m = x.shape[0]
    n_col = D_CONV // TN_A
    assert DN_HEADS * DN_DV == D_CONV
    z_part = Z_BLOCK * TN_IN // D_CONV
    w_spec = lambda part: pl.BlockSpec((TN_A, D_MODEL), lambda i, j: (part * n_col + j, 0))
    cols = pl.BlockSpec((tm, TN_A), lambda i, j: (i, j))
    return pl.pallas_call(
        functools.partial(_proj_a_kernel, tiles_per_seq),
        grid=(m // tm, n_col),
        in_specs=[
            pl.BlockSpec((tm, D_MODEL), lambda i, j: (i, 0)),
            pl.BlockSpec((1, D_MODEL), lambda i, j: (0, 0)),
            w_spec(0), w_spec(1), w_spec(2), w_spec(z_part),
            pl.BlockSpec((3, TN_A), lambda i, j: (0, j)),
        ],
        out_specs=[
            cols, cols,
            pl.BlockSpec((tm, D_MODEL), lambda i, j: (i, 0)),
            pl.BlockSpec((1, 8, TN_A), lambda i, j: (i, 0, j)),
        ],
        out_shape=[
            jax.ShapeDtypeStruct((m, D_CONV), BF16),
            jax.ShapeDtypeStruct((m, DN_HEADS * DN_DV), BF16),
            jax.ShapeDtypeStruct((m, D_MODEL), BF16),
            jax.ShapeDtypeStruct((m // tm, 8, D_CONV), F32),
        ],
        scratch_shapes=[pltpu.VMEM((n_col, 8, TN_A), F32)],
        compiler_params=_params("arbitrary", "arbitrary"),
        name="proj_a",
    )(x, pre, wt, wt, wt, wt, caw)


def _l2_normalize(x, scale):
    return x * (lax.rsqrt(jnp.sum(x * x, axis=-1, keepdims=True) + EPS) * scale)


def _gate_columns(beta_all, gc_all):
    heads = range(DN_HEADS)
    gc_t = gc_all.T
    return ([beta_all[:, DN_HEADS + h:DN_HEADS + h + 1] for h in heads],
            [gc_all[:, h:h + 1] for h in heads], [gc_t[h:h + 1, :] for h in heads])


def _heads_intra(qn, kn, vh, beta_col, gc_col, gc_row, causal, strict, eye, squarings):
    heads = range(len(qn))
    decay = [jnp.exp(jnp.where(causal, gc_col[h] - gc_row[h], -jnp.inf)) for h in heads]
    kb = [kn[h] * beta_col[h] for h in heads]
    gram = [_mm_nt(jnp.concatenate([kb[h], qn[h]], axis=0), kn[h]) for h in heads]
    a_mat = [jnp.where(strict, gram[h][:CHUNK] * decay[h], 0.0) for h in heads]
    qk = [jnp.where(causal, gram[h][CHUNK:] * decay[h], 0.0) for h in heads]
    inv = [eye - a for a in a_mat]
    power = a_mat
    for _ in range(squarings):
        power = [_mm(x, x) for x in power]
        inv = [_mm(inv[h], eye + power[h]) for h in heads]
    egc = [jnp.exp(x) for x in gc_col]
    wu = [_mm(inv[h], jnp.concatenate([kb[h] * egc[h], vh[h] * beta_col[h]], axis=1)) for h in heads]
    w = [x[:, :DN_DK] for x in wu]
    u = [x[:, DN_DK:] for x in wu]
    qdec = [qn[h] * egc[h] for h in heads]
    return w, u, qdec, qk


def _gates(ab, alog, dtb):
    x = ab + dtb
    softplus = jnp.maximum(x, 0.0) + jnp.log1p(jnp.exp(-jnp.abs(x)))
    return -jnp.exp(alog) * softplus, jax.nn.sigmoid(ab)


def _masks(same_seq):
    row = lax.broadcasted_iota(jnp.int32, (CHUNK, CHUNK), 0)
    col = lax.broadcasted_iota(jnp.int32, (CHUNK, CHUNK), 1)
    causal = row >= col
    strict = row > col
    if same_seq:
        same = (row >> SEQ_SHIFT) == (col >> SEQ_SHIFT)
        causal = causal & same
        strict = strict & same
    eye = jnp.where(row == col, 1.0, 0.0).astype(F32)
    return row, col, causal, strict, causal.astype(F32), eye


def _proj_qkv_kernel(tiles_per_seq, h_ref, wq_ref, wk_ref, wv_ref, wabt_ref, cqw_ref, alog_ref, dtb_ref,
                     w_ref, qd_ref, kd_ref, u_ref, qk_ref, egl_ref, tail_ref, carry_ref, qkv_ref):
    i = pl.program_id(0)
    tm = h_ref.shape[0]

    @pl.when(i == 0)
    def _():
        carry_ref[...] = jnp.zeros_like(carry_ref)

    seq_start = i % tiles_per_seq == 0
    h = h_ref[...]
    g_all, beta_all = _gates(_nt_dot(h, wabt_ref[...]), alog_ref[...], dtb_ref[...])
    for t, wt_ref in enumerate((wq_ref, wk_ref, wv_ref)):
        for c in range(TN_IN // SUB_COLS):
            cols = slice(c * SUB_COLS, (c + 1) * SUB_COLS)
            lo = t * TN_IN + c * SUB_COLS
            p = _nt_dot(h, wt_ref[cols, :])
            tail_ref[0, :, lo:lo + SUB_COLS] = p[tm - 8:tm, :]
            p, carry = _causal_conv(p, cqw_ref[:, lo:lo + SUB_COLS],
                                    _load_carry(carry_ref, t, cols, 3, seq_start))
            _store_carry(carry_ref, t, cols, carry)
            p = _silu(p)
            if t == 2:
                qkv_ref[:, lo:lo + SUB_COLS] = p
            else:
                for k in range(SUB_COLS // DN_DK):
                    blk = p[:, k * DN_DK:(k + 1) * DN_DK]
                    qkv_ref[:, lo + k * DN_DK:lo + (k + 1) * DN_DK] = _l2_normalize(
                        blk, DN_DK ** -0.5 if t == 0 else 1.0)

    heads = range(DN_HEADS)
    _, _, causal, strict, low_f, eye = _masks(False)
    chunk_rows = [slice(g * CHUNK, (g + 1) * CHUNK) for g in range(tm // CHUNK)]
    qn, kn, vh, beta_col, gc_col, gc_row = [], [], [], [], [], []
    for rows in chunk_rows:
        gc_all = _mm_exact(low_f, g_all[rows])
        egl_ref[rows, :] = jnp.broadcast_to(jnp.exp(gc_all[CHUNK - 1:CHUNK, :]), (CHUNK, AB_LANES))
        cols = _gate_columns(beta_all[rows], gc_all)
        beta_col += cols[0]
        gc_col += cols[1]
        gc_row += cols[2]
        qn += [qkv_ref[rows, h * DN_DK:(h + 1) * DN_DK] for h in heads]
        kn += [qkv_ref[rows, (DN_HEADS + h) * DN_DK:(DN_HEADS + h + 1) * DN_DK] for h in heads]
        vh += [qkv_ref[rows, (2 * DN_HEADS + h) * DN_DV:(2 * DN_HEADS + h + 1) * DN_DV] for h in heads]
    w, u, qdec, qk = _heads_intra(qn, kn, vh, beta_col, gc_col, gc_row, causal, strict, eye, 5)
    for n in range(len(w)):
        rows, hd = chunk_rows[n // DN_HEADS], n % DN_HEADS
        lo = hd * DN_DK
        g_last = gc_col[n][CHUNK - 1:CHUNK, :]
        w_ref[rows, lo:lo + DN_DK] = w[n].astype(BF16)
        qd_ref[rows, lo:lo + DN_DK] = qdec[n].astype(BF16)
        kd_ref[rows, lo:lo + DN_DK] = (kn[n] * jnp.exp(g_last - gc_col[n])).astype(BF16)
        u_ref[rows, lo:lo + DN_DV] = u[n]
        qk_ref[rows, hd * CHUNK:(hd + 1) * CHUNK] = qk[n].astype(BF16)


def _proj_qkv(h, wt, wabt, cqw, alog, dtb, tiles_per_seq):
    m = h.shape[0]
    hk, hv = DN_HEADS * DN_DK, DN_HEADS * DN_DV
    rows = lambda width: pl.BlockSpec((TM, width), lambda i: (i, 0))
    whole = lambda shape: pl.BlockSpec(shape, lambda i: (0,) * len(shape))
    w_tile = lambda t: pl.BlockSpec((TN_IN, D_MODEL), lambda i: (3 + t, 0))
    return pl.pallas_call(
        functools.partial(_proj_qkv_kernel, tiles_per_seq),
        grid=(m // TM,),
        in_specs=[rows(D_MODEL), w_tile(0), w_tile(1), w_tile(2), whole((AB_LANES, D_MODEL)),
                  whole((4, DN_QKV)), whole((1, AB_LANES)), whole((1, AB_LANES))],
        out_specs=[rows(hk), rows(hk), rows(hk), rows(hv), rows(DN_HEADS * CHUNK), rows(AB_LANES),
                   pl.BlockSpec((1, 8, DN_QKV), lambda i: (i, 0, 0))],
        out_shape=[
            jax.ShapeDtypeStruct((m, hk), BF16), jax.ShapeDtypeStruct((m, hk), BF16),
            jax.ShapeDtypeStruct((m, hk), BF16), jax.ShapeDtypeStruct((m, hv), F32),
            jax.ShapeDtypeStruct((m, DN_HEADS * CHUNK), BF16), jax.ShapeDtypeStruct((m, AB_LANES), F32),
            jax.ShapeDtypeStruct((m // TM, 8, DN_QKV), F32),
        ],
        scratch_shapes=[pltpu.VMEM((3, 8, TN_IN), F32), pltpu.VMEM((TM, DN_QKV), F32)],
        compiler_params=_params("arbitrary"),
        name="proj_qkv",
    )(h, wt, wt, wt, wabt, cqw, alog, dtb)


def _delta_scan_kernel(w_ref, qd_ref, kd_ref, u_ref, qk_ref, egl_ref, zs_ref, dnn_ref,
                       y_ref, so_ref, s_ref):
    c = pl.program_id(1)

    @pl.when(c == 0)
    def _():
        s_ref[...] = jnp.zeros_like(s_ref)

    heads = range(DN_HEADS)
    dnn = dnn_ref[...]
    s = [s_ref[h] for h in heads]
    for g in range(w_ref.shape[0] // CHUNK):
        rows = slice(g * CHUNK, (g + 1) * CHUNK)
        col = [slice(h * DN_DK, (h + 1) * DN_DK) for h in heads]
        lhs = [jnp.concatenate([w_ref[rows, col[h]], qd_ref[rows, col[h]]], axis=0) for h in heads]
        r = [jnp.dot(lhs[h], s[h].astype(BF16), preferred_element_type=F32) for h in heads]
        v_new = [(u_ref[rows, col[h]] - r[h][:CHUNK]).astype(BF16) for h in heads]
        upd = [lax.dot_general(kd_ref[rows, col[h]], v_new[h], (((0,), (0,)), ((), ())),
                               preferred_element_type=F32) for h in heads]
        s = [s[h] * egl_ref[g * CHUNK:g * CHUNK + 1, h:h + 1] + upd[h] for h in heads]
        o = [r[h][CHUNK:] + jnp.dot(qk_ref[rows, h * CHUNK:(h + 1) * CHUNK], v_new[h],
                                    preferred_element_type=F32) for h in heads]
        for h in heads:
            y_ref[rows, col[h]] = (_rms(o[h], dnn) * zs_ref[rows, col[h]]).astype(y_ref.dtype)
    for h in heads:
        s_ref[h] = s[h]

    @pl.when(c == pl.num_programs(1) - 1)
    def _():
        so_ref[0] = s_ref[...]


def _delta_scan(w, qd, kd, u, qk, egl, zs, dnn, n_seq, seq_len):
    n_steps = seq_len // TM
    hk, hv = DN_HEADS * DN_DK, DN_HEADS * DN_DV
    rows = lambda width: pl.BlockSpec((TM, width), lambda b, c: (b * n_steps + c, 0))
    return pl.pallas_call(
        _delta_scan_kernel,
        grid=(n_seq, n_steps),
        in_specs=[rows(hk), rows(hk), rows(hk), rows(hv), rows(DN_HEADS * CHUNK), rows(AB_LANES), rows(hv),
                  pl.BlockSpec((1, DN_DV), lambda b, c: (0, 0))],
        out_specs=[rows(hv), pl.BlockSpec((1, DN_HEADS, DN_DK, DN_DV), lambda b, c: (b, 0, 0, 0))],
        out_shape=[jax.ShapeDtypeStruct((n_seq * seq_len, hv), BF16),
                   jax.ShapeDtypeStruct((n_seq, DN_HEADS, DN_DK, DN_DV), F32)],
        scratch_shapes=[pltpu.VMEM((DN_HEADS, DN_DK, DN_DV), F32)],
        compiler_params=_params("arbitrary", "arbitrary"),
        name="delta_scan",
    )(w, qd, kd, u, qk, egl, zs, dnn)


def _mix_sample_kernel(pa_ref, pq_ref, pz_ref, pab_ref, sa_ref, sq_ref, si_ref,
                       caw_ref, cqw_ref, alog_ref, dtb_ref, dnn_ref,
                       y_ref, ca_ref, cq_ref, so_ref,
                       gb_ref, ua_ref, xq_ref, z_ref, ab_ref, ybuf_ref):
    seqs = range(SEQ_PER_STEP)
    pad = SEQ_PAD - SAMPLE_LEN
    for b in seqs:
        r0 = SEQ_PAD * b
        nxt = (b + 1) % SEQ_PER_STEP
        src = slice(SAMPLE_LEN * b, SAMPLE_LEN * (b + 1))
        pa = pa_ref[src, :]
        gb_ref[r0:r0 + SAMPLE_LEN, :] = pa[:, :D_CONV]
        ua_ref[r0:r0 + SAMPLE_LEN, :] = pa[:, D_CONV:2 * D_CONV] * pa[:, 2 * D_CONV:]
        xq_ref[r0:r0 + SAMPLE_LEN, :] = pq_ref[src, :]
        z_ref[r0:r0 + SAMPLE_LEN, :] = pz_ref[src, :]
        ab_ref[r0:r0 + SAMPLE_LEN, :] = pab_ref[src, :]
        for ref in (gb_ref, ua_ref, xq_ref, z_ref, ab_ref):
            ref[r0 + SAMPLE_LEN:r0 + SEQ_PAD, :] = jnp.zeros((pad, ref.shape[1]), F32)
        ua_ref[r0 + SEQ_PAD - 2:r0 + SEQ_PAD, :] = sa_ref[nxt]
        for r in range(3):
            xq_ref[r0 + SEQ_PAD - 3 + r:r0 + SEQ_PAD - 2 + r, :] = sq_ref[r, nxt:nxt + 1, :]
    ybuf_ref[:, :D_CONV] = gb_ref[...] * _causal_conv(ua_ref[...], caw_ref[...], None)[0]
    row, col, causal, strict, low_f, eye = _masks(True)
    valid = (row[:, 0:1] & (SEQ_PAD - 1)) < SAMPLE_LEN
    cqkv = jnp.where(valid, _silu(_causal_conv(xq_ref[...], cqw_ref[...], None)[0]), 0.0)
    for b in seqs:
        r0 = SEQ_PAD * b
        ca_ref[b] = ua_ref[r0 + SAMPLE_LEN - 2:r0 + SAMPLE_LEN, :]
        for r in range(3):
            cq_ref[r, b:b + 1, :] = xq_ref[r0 + SAMPLE_LEN - 3 + r:r0 + SAMPLE_LEN - 2 + r, :]

    g_all, beta_all = _gates(ab_ref[...], alog_ref[...], dtb_ref[...])
    g_all = jnp.where(valid, g_all, 0.0)
    beta_all = jnp.where(valid, beta_all, 0.0)
    gc_all = _mm_exact(low_f, g_all)
    last_sel = (col == (row | (SEQ_PAD - 1))).astype(F32)
    gl_all = _mm_exact(last_sel, gc_all)
    z = z_ref[...]
    dnn = dnn_ref[...]
    heads = range(DN_HEADS)
    qn = [_l2_normalize(cqkv[:, h * DN_DK:(h + 1) * DN_DK], DN_DK ** -0.5) for h in heads]
    kn = [_l2_normalize(cqkv[:, 1024 + h * DN_DK:1024 + (h + 1) * DN_DK], 1.0) for h in heads]
    vh = [cqkv[:, 2048 + h * DN_DV:2048 + (h + 1) * DN_DV] for h in heads]
    beta_col, gc_col, gc_row = _gate_columns(beta_all, gc_all)
    w, u, qdec, qk = _heads_intra(qn, kn, vh, beta_col, gc_col, gc_row, causal, strict, eye, 1)
    gl_col = [gl_all[:, h:h + 1] for h in heads]
    kdec = [kn[h] * jnp.exp(gl_col[h] - gc_col[h]) for h in heads]
    s_scale = [jnp.exp(x) for x in gl_col]
    rows = [slice(SEQ_PAD * b, SEQ_PAD * (b + 1)) for b in seqs]
    pairs = [(h, b) for h in heads for b in seqs]
    s_old = [si_ref[b, h] for h, b in pairs]
    r = [_mm(jnp.concatenate([w[h][rows[b]], qdec[h][rows[b]]], axis=0), s_old[n])
         for n, (h, b) in enumerate(pairs)]
    v_new = [u[h][rows[b]] - r[n][:SEQ_PAD] for n, (h, b) in enumerate(pairs)]
    upd = [_mm_tn(kdec[h][rows[b]], v_new[n]) for n, (h, b) in enumerate(pairs)]
    for n, (h, b) in enumerate(pairs):
        so_ref[b, h] = s_old[n] * s_scale[h][SEQ_PAD * b:SEQ_PAD * b + 1, :] + upd[n]
    for h in heads:
        mine = range(h * SEQ_PER_STEP, (h + 1) * SEQ_PER_STEP)
        o = (jnp.concatenate([r[n][SEQ_PAD:] for n in mine], axis=0)
             + _mm(qk[h], jnp.concatenate([v_new[n] for n in mine], axis=0)))
        lo = h * DN_DV
        ybuf_ref[:, D_CONV + lo:D_CONV + lo + DN_DV] = _rms(o, dnn) * _silu(z[:, lo:lo + DN_DV])
    for b in seqs:
        y_ref[SAMPLE_LEN * b:SAMPLE_LEN * (b + 1), :] = ybuf_ref[SEQ_PAD * b:SEQ_PAD * b + SAMPLE_LEN, :]


def _mix_sample(proj, ab, state_a, state_q, state_s, caw, cqw, alog, dtb, dnn):
    n_seq = state_s.shape[0]
    full2 = lambda i: (0, 0)
    seq_rows = lambda width, col: pl.BlockSpec((SEQ_PER_STEP * SAMPLE_LEN, width), lambda i: (i, col))
    return pl.pallas_call(
        _mix_sample_kernel,
        grid=(n_seq // SEQ_PER_STEP,),
        in_specs=[
            seq_rows(3 * D_CONV, 0), seq_rows(DN_QKV, 1), seq_rows(1024, Z_BLOCK), seq_rows(AB_LANES, 0),
            pl.BlockSpec((SEQ_PER_STEP, 2, D_CONV), lambda i: (i, 0, 0)),
            pl.BlockSpec((3, SEQ_PER_STEP, DN_QKV), lambda i: (0, i, 0)),
            pl.BlockSpec((SEQ_PER_STEP, DN_HEADS, DN_DK, DN_DV), lambda i: (i, 0, 0, 0)),
            pl.BlockSpec((3, D_CONV), full2),
            pl.BlockSpec((4, DN_QKV), full2),
            pl.BlockSpec((1, AB_LANES), full2),
            pl.BlockSpec((1, AB_LANES), full2),
            pl.BlockSpec((1, DN_DV), full2),
        ],
        out_specs=[
            seq_rows(D_MODEL, 0),
            pl.BlockSpec((SEQ_PER_STEP, 2, D_CONV), lambda i: (i, 0, 0)),
            pl.BlockSpec((3, SEQ_PER_STEP, DN_QKV), lambda i: (0, i, 0)),
            pl.BlockSpec((SEQ_PER_STEP, DN_HEADS, DN_DK, DN_DV), lambda i: (i, 0, 0, 0)),
        ],
        out_shape=[
            jax.ShapeDtypeStruct((n_seq * SAMPLE_LEN, D_MODEL), F32),
            jax.ShapeDtypeStruct((n_seq, 2, D_CONV), F32),
            jax.ShapeDtypeStruct((3, n_seq, DN_QKV), F32),
            jax.ShapeDtypeStruct((n_seq, DN_HEADS, DN_DK, DN_DV), F32),
        ],
        scratch_shapes=[
            pltpu.VMEM((CHUNK, D_CONV), F32),
            pltpu.VMEM((CHUNK, D_CONV), F32),
            pltpu.VMEM((CHUNK, DN_QKV), F32),
            pltpu.VMEM((CHUNK, 1024), F32),
            pltpu.VMEM((CHUNK, AB_LANES), F32),
            pltpu.VMEM((CHUNK, D_MODEL), F32),
        ],
        compiler_params=_params("parallel"),
        name="mix_sample",
    )(proj, proj, proj, ab, state_a, state_q, state_s, caw, cqw, alog, dtb, dnn)


def _outproj_kernel(ya_ref, yb_ref, wa_ref, wb_ref, post_ref, x_ref, o_ref):
    m = (jnp.dot(ya_ref[...].astype(BF16), wa_ref[...], preferred_element_type=F32)
         + jnp.dot(yb_ref[...].astype(BF16), wb_ref[...], preferred_element_type=F32))
    o_ref[...] = x_ref[...] + _rms(m, post_ref[...])


def _outproj(ya, yb, yb_block, w, post, x):
    m = x.shape[0]
    half = D_MODEL // 2
    return pl.pallas_call(
        _outproj_kernel,
        grid=(m // TM,),
        in_specs=[
            pl.BlockSpec((TM, half), lambda i: (i, 0)),
            pl.BlockSpec((TM, half), lambda i: (i, yb_block)),
            pl.BlockSpec((half, D_MODEL), lambda i: (0, 0)),
            pl.BlockSpec((half, D_MODEL), lambda i: (1, 0)),
            pl.BlockSpec((1, D_MODEL), lambda i: (0, 0)),
            pl.BlockSpec((TM, D_MODEL), lambda i: (i, 0)),
        ],
        out_specs=pl.BlockSpec((TM, D_MODEL), lambda i: (i, 0)),
        out_shape=jax.ShapeDtypeStruct((m, D_MODEL), F32),
        compiler_params=_params("parallel"),
        name="outproj",
    )(ya, yb, w, w, post, x)


def _ple_kernel(x_ref, p_ref, pre_ref, post_ref, wg_ref, wp_ref, o_ref):
    x = x_ref[...]
    h = _rms(x, pre_ref[...]).astype(BF16)
    gate = jax.nn.sigmoid(jnp.dot(h, wg_ref[...], preferred_element_type=F32))
    proj = jnp.dot(p_ref[...].astype(BF16), wp_ref[...], preferred_element_type=F32)
    o_ref[...] = x + _rms(gate * proj, post_ref[...])


def _ple(x, p, pre, post, wg, wp):
    m = x.shape[0]
    return pl.pallas_call(
        _ple_kernel,
        grid=(m // TM,),
        in_specs=[
            pl.BlockSpec((TM, D_MODEL), lambda i: (i, 0)),
            pl.BlockSpec((TM, D_PLE), lambda i: (i, 0)),
            pl.BlockSpec((1, D_MODEL), lambda i: (0, 0)),
            pl.BlockSpec((1, D_MODEL), lambda i: (0, 0)),
            pl.BlockSpec((D_MODEL, D_MODEL), lambda i: (0, 0)),
            pl.BlockSpec((D_PLE, D_MODEL), lambda i: (0, 0)),
        ],
        out_specs=pl.BlockSpec((TM, D_MODEL), lambda i: (i, 0)),
        out_shape=jax.ShapeDtypeStruct((m, D_MODEL), F32),
        compiler_params=_params("parallel"),
        name="ple",
    )(x, p, pre, post, wg, wp)


def _row(v):
    return v.reshape(1, -1).astype(F32)


def _pad_lanes(v):
    return jnp.pad(_row(v), ((0, 0), (0, AB_LANES - v.shape[-1])))


def _seq_tails(tails, tiles_per_seq, n_rows):
    return tails[tiles_per_seq - 1::tiles_per_seq, 8 - n_rows:, :]


def kernel(x_prompt, x_sample, state_conv_a, state_conv_qkv, state_delta, p_prompt, p_sample,
           f1_pre, f1_post, f1_wg, f1_wu, f1_wd,
           mix_pre, mix_post, w_in, conv_a_w, conv_qkv_w, a_log, dt_bias, dn_norm, w_out,
           f2_pre, f2_post, f2_wg, f2_wu, f2_wd,
           ple_pre, ple_post, w_ple_gate, w_ple_proj):
    depth = f1_pre.shape[0]
    n_p, seq_p, _ = x_prompt.shape
    n_s, seq_s, _ = x_sample.shape
    rows_p = n_p * seq_p
    rows_s = n_s * seq_s
    assert seq_p % TM_BIG == 0 and seq_s == SAMPLE_LEN and n_s % SEQ_PER_STEP == 0
    assert rows_s == TM
    assert w_in.shape[-1] == IN_MAIN + N_AB
    tiles_per_seq = seq_p // TM_BIG
    xp = x_prompt.reshape(rows_p, D_MODEL)
    xs = x_sample.reshape(rows_s, D_MODEL)
    outs = [[] for _ in range(6)]
    for i in range(depth):
        pre, post = _row(f1_pre[i]), _row(f1_post[i])
        xs, wg, wu, wd = _ffn(xs, pre, post, f1_wg[i], f1_wu[i], f1_wd[i], TM)
        xp = _ffn(xp, pre, post, wg, wu, wd, TM_BIG)[0]

        pre = _row(mix_pre[i])
        caw, cqw = conv_a_w[i], conv_qkv_w[i]
        proj_s, ab_s, wt, wabt = _inproj_cast(xs, pre, w_in[i].T)
        y_a, zs, h, tail_a = _proj_a(xp, pre, wt, caw, TM_BIG, tiles_per_seq)
        alog, dtb, dnn = _pad_lanes(a_log[i]), _pad_lanes(dt_bias[i]), _row(dn_norm[i])
        wy_w, qdec, kdec, wy_u, qk, egl, tail_q = _proj_qkv(h, wt, wabt, cqw, alog, dtb, seq_p // TM)
        y_b, s_p = _delta_scan(wy_w, qdec, kdec, wy_u, qk, egl, zs, dnn, n_p, seq_p)
        ca_p = _seq_tails(tail_a, tiles_per_seq, 2)
        cq_p = _seq_tails(tail_q, seq_p // TM, 3)
        y_s, ca_s, cq_s, s_s = _mix_sample(
            proj_s, ab_s, state_conv_a[i], jnp.swapaxes(state_conv_qkv[i], 0, 1), state_delta[i],
            caw, cqw, alog, dtb, dnn)
        cq_s = jnp.swapaxes(cq_s, 0, 1)

        post = _row(mix_post[i])
        w_out_bf16 = w_out[i].astype(BF16)
        xs = _outproj(y_s, y_s, 1, w_out_bf16, post, xs)
        xp = _outproj(y_a, y_b, 0, w_out_bf16, post, xp)

        pre, post = _row(f2_pre[i]), _row(f2_post[i])
        xs, wg, wu, wd = _ffn(xs, pre, post, f2_wg[i], f2_wu[i], f2_wd[i], TM)
        xp = _ffn(xp, pre, post, wg, wu, wd, TM_BIG)[0]

        pre, post = _row(ple_pre[i]), _row(ple_post[i])
        wg, wp = w_ple_gate[i].astype(BF16), w_ple_proj[i].astype(BF16)
        xs = _ple(xs, p_sample[i].reshape(rows_s, D_PLE), pre, post, wg, wp)
        xp = _ple(xp, p_prompt[i].reshape(rows_p, D_PLE), pre, post, wg, wp)
        for lst, val in zip(outs, (ca_p, cq_p, s_p, ca_s, cq_s, s_s)):
            lst.append(val)
    return (xp.reshape(n_p, seq_p, D_MODEL), xs.reshape(n_s, seq_s, D_MODEL),
            *[jnp.stack(lst) for lst in outs])
```

```python
import functools

import jax
import jax.numpy as jnp
from jax import lax
from jax.experimental import pallas as pl
from jax.experimental.pallas import tpu as pltpu

F32 = jnp.float32
BF16 = jnp.bfloat16

D_MODEL = 2048
D_CONV = 1024
DN_HEADS = 8
DN_DK = 128
DN_DV = 128
DN_QKV = 3072
D_FF = 5632
D_PLE = 256
EPS = 1e-6
CHUNK = 64
SAMPLE_LEN = 4
SEQ_PAD = 8
SEQ_SHIFT = 3
SEQ_PER_STEP = CHUNK // SEQ_PAD
IN_MAIN = 7168
QKV_BLOCK = 3
Z_BLOCK = 6
HIST_A = 2
HIST_QKV = 3
TAIL_ROWS = 8
N_AB = 16
AB_LANES = 128

TM = 512
TF = 512
TF_CAST = 256
TN_IN = 1024
TM_BIG = 1024
TN_A = 512
SUB_COLS = 512
ROW_CHUNK = 32
VMEM_LIMIT = 56 * 1024 * 1024


def _rms(x, g):
    ms = jnp.mean(x * x, axis=-1, keepdims=True)
    return x * lax.rsqrt(ms + EPS) * g


def _silu(x):
    return x * jax.nn.sigmoid(x)


def _mm(a, b):
    return jnp.dot(a.astype(BF16), b.astype(BF16), preferred_element_type=F32)


def _mm_nt(a, b):
    return lax.dot_general(a.astype(BF16), b.astype(BF16), (((1,), (1,)), ((), ())),
                           preferred_element_type=F32)


def _mm_tn(a, b):
    return jnp.dot(a.T.astype(BF16), b.astype(BF16), preferred_element_type=F32)


def _mm_exact(a, b):
    return jnp.dot(a, b, precision=lax.Precision.HIGHEST, preferred_element_type=F32)


def _nt_dot(a, bt):
    return lax.dot_general(a, bt, (((1,), (1,)), ((), ())), preferred_element_type=F32)


def _for_row_chunks(n_rows, body):
    def step(k, carry):
        body(pl.ds(pl.multiple_of(k * ROW_CHUNK, ROW_CHUNK), ROW_CHUNK))
        return carry
    lax.fori_loop(0, n_rows // ROW_CHUNK, step, 0, unroll=4)


def _params(*semantics):
    return pltpu.CompilerParams(dimension_semantics=semantics, vmem_limit_bytes=VMEM_LIMIT)


def _ffn_kernel(emit_bf16, x_ref, pre_ref, post_ref, wg_ref, wu_ref, wd_ref, o_ref, *rest):
    h_ref = rest[-1]
    j = pl.program_id(1)

    @pl.when(j == 0)
    def _():
        def norm_rows(rows):
            h_ref[rows, :] = _rms(x_ref[rows, :], pre_ref[...]).astype(BF16)
            o_ref[rows, :] = jnp.zeros((ROW_CHUNK, D_MODEL), F32)
        _for_row_chunks(x_ref.shape[0], norm_rows)

    wg, wu, wd = wg_ref[...], wu_ref[...], wd_ref[...]
    if emit_bf16:
        wg, wu, wd = wg.astype(BF16), wu.astype(BF16), wd.astype(BF16)
        rest[0][...] = wg
        rest[1][...] = wu
        rest[2][...] = wd
    h = h_ref[...]
    g = jnp.dot(h, wg, preferred_element_type=F32)
    u = jnp.dot(h, wu, preferred_element_type=F32)
    a = (_silu(g) * u).astype(BF16)
    o_ref[...] += jnp.dot(a, wd, preferred_element_type=F32)

    @pl.when(j == pl.num_programs(1) - 1)
    def _():
        group = 4 * ROW_CHUNK

        def finish_rows(k, carry):
            r0 = pl.multiple_of(k * group, group)
            chunks = [pl.ds(r0 + s * ROW_CHUNK, ROW_CHUNK) for s in range(group // ROW_CHUNK)]
            scale = [0.5 * lax.rsqrt(jnp.mean(jnp.square(o_ref[c, :]), axis=-1, keepdims=True) + EPS)
                     for c in chunks]
            for c, sc in zip(chunks, scale):
                o_ref[c, :] = x_ref[c, :] + o_ref[c, :] * sc * post_ref[...]
            return carry
        lax.fori_loop(0, x_ref.shape[0] // group, finish_rows, 0)


def _ffn(x, pre, post, wg, wu, wd, tm):
    m = x.shape[0]
    emit_bf16 = wg.dtype != BF16
    tf = TF_CAST if emit_bf16 else TF
    up_spec = pl.BlockSpec((D_MODEL, tf), lambda i, j: (0, j))
    down_spec = pl.BlockSpec((tf, D_MODEL), lambda i, j: (j, 0))
    out_specs = [pl.BlockSpec((tm, D_MODEL), lambda i, j: (i, 0))]
    out_shape = [jax.ShapeDtypeStruct((m, D_MODEL), F32)]
    if emit_bf16:
        assert m == tm, "weights are re-emitted once, by a single-row-tile call"
        out_specs += [up_spec, up_spec, down_spec]
        out_shape += [jax.ShapeDtypeStruct(w.shape, BF16) for w in (wg, wu, wd)]
    outs = pl.pallas_call(
        functools.partial(_ffn_kernel, emit_bf16),
        grid=(m // tm, D_FF // tf),
        in_specs=[
            pl.BlockSpec((tm, D_MODEL), lambda i, j: (i, 0)),
            pl.BlockSpec((1, D_MODEL), lambda i, j: (0, 0)),
            pl.BlockSpec((1, D_MODEL), lambda i, j: (0, 0)),
            up_spec, up_spec, down_spec,
        ],
        out_specs=out_specs,
        out_shape=out_shape,
        scratch_shapes=[pltpu.VMEM((tm, D_MODEL), BF16)],
        compiler_params=_params("parallel", "arbitrary"),
        name="ffn_cast" if emit_bf16 else "ffn",
    )(x, pre, post, wg, wu, wd)
    return outs if emit_bf16 else (outs[0], wg, wu, wd)


def _inproj_cast_kernel(x_ref, pre_ref, wt_ref, wabt_ref, o_ref, ab_ref, wt16_ref, wabt16_ref, h_ref):
    @pl.when(pl.program_id(1) == 0)
    def _():
        h = _rms(x_ref[...], pre_ref[...]).astype(BF16)
        h_ref[...] = h
        wabt = jnp.concatenate(
            [wabt_ref[...].astype(BF16), jnp.zeros((AB_LANES - N_AB, D_MODEL), BF16)], axis=0)
        wabt16_ref[...] = wabt
        ab_ref[...] = _nt_dot(h, wabt)

    wt = wt_ref[...].astype(BF16)
    wt16_ref[...] = wt
    o_ref[...] = _nt_dot(h_ref[...], wt)


def _inproj_cast(x, pre, wt):
    m = x.shape[0]
    assert m == TM, "weights are re-emitted once, by a single-row-tile call"
    wt_spec = pl.BlockSpec((TN_IN, D_MODEL), lambda i, j: (j, 0))
    return pl.pallas_call(
        _inproj_cast_kernel,
        grid=(1, IN_MAIN // TN_IN),
        in_specs=[
            pl.BlockSpec((TM, D_MODEL), lambda i, j: (0, 0)),
            pl.BlockSpec((1, D_MODEL), lambda i, j: (0, 0)),
            wt_spec,
            pl.BlockSpec((N_AB, D_MODEL), lambda i, j: (IN_MAIN // N_AB, 0)),
        ],
        out_specs=[
            pl.BlockSpec((TM, TN_IN), lambda i, j: (0, j)),
            pl.BlockSpec((TM, AB_LANES), lambda i, j: (0, 0)),
            wt_spec,
            pl.BlockSpec((AB_LANES, D_MODEL), lambda i, j: (0, 0)),
        ],
        out_shape=[
            jax.ShapeDtypeStruct((m, IN_MAIN), F32),
            jax.ShapeDtypeStruct((m, AB_LANES), F32),
            jax.ShapeDtypeStruct((IN_MAIN, D_MODEL), BF16),
            jax.ShapeDtypeStruct((AB_LANES, D_MODEL), BF16),
        ],
        scratch_shapes=[pltpu.VMEM((TM, D_MODEL), BF16)],
        compiler_params=_params("parallel", "arbitrary"),
        name="inproj_cast",
    )(x, pre, wt, wt)


def _causal_conv(x, w, carry_in):
    width = w.shape[0]
    rows = x.shape[0]
    is_row0 = lax.broadcasted_iota(jnp.int32, (8, 1), 0) == 0
    acc = x * w[0:1, :]
    carry_out = []
    for j in range(1, width):
        delayed = pltpu.roll(acc, 1, axis=0)
        if carry_in is not None:
            carry_out.append(acc[rows - 1:rows, :])
            head = jnp.where(is_row0, carry_in[j - 1], delayed[:8])
            delayed = jnp.concatenate([head, delayed[8:]], axis=0)
        acc = x * w[j:j + 1, :] + delayed
    return acc, carry_out


def _load_carry(carry_ref, j, cols, n, seq_start):
    return [jnp.where(seq_start, 0.0, carry_ref[j, r:r + 1, cols]) for r in range(n)]


def _store_carry(carry_ref, j, cols, rows):
    for r, row in enumerate(rows):
        carry_ref[j, r:r + 1, cols] = row


def _proj_a_kernel(tiles_per_seq, x_ref, pre_ref, wgb_ref, wgcv_ref, whc_ref, wz_ref, caw_ref,
                   ya_ref, zs_ref, h_ref, tail_ref, carry_ref):
    i, j = pl.program_id(0), pl.program_id(1)
    tm = x_ref.shape[0]

    @pl.when(j == 0)
    def _():
        def norm_rows(rows):
            h_ref[rows, :] = _rms(x_ref[rows, :], pre_ref[...]).astype(BF16)
        _for_row_chunks(tm, norm_rows)

    @pl.when((i == 0) & (j == 0))
    def _():
        carry_ref[...] = jnp.zeros_like(carry_ref)

    seq_start = i % tiles_per_seq == 0
    h = h_ref[...]
    for c in range(TN_A // SUB_COLS):
        cols = slice(c * SUB_COLS, (c + 1) * SUB_COLS)
        u = _nt_dot(h, wgcv_ref[cols, :]) * _nt_dot(h, whc_ref[cols, :])
        tail_ref[0, :, cols] = u[tm - TAIL_ROWS:tm, :]
        conv, carry = _causal_conv(u, caw_ref[:, cols], _load_carry(carry_ref, j, cols, HIST_A, seq_start))
        _store_carry(carry_ref, j, cols, carry)
        ya_ref[:, cols] = (_nt_dot(h, wgb_ref[cols, :]) * conv).astype(BF16)
        zs_ref[:, cols] = _silu(_nt_dot(h, wz_ref[cols, :])).astype(BF16)


def _proj_a(x, pre, wt, caw, tm, tiles_per_seq):
    m = x.shape[0]
    n_col = D_CONV // TN_A
    assert DN_HEADS * DN_DV == D_CONV
    z_part = Z_BLOCK * TN_IN // D_CONV
    w_spec = lambda part: pl.BlockSpec((TN_A, D_MODEL), lambda i, j: (part * n_col + j, 0))
    cols = pl.BlockSpec((tm, TN_A), lambda i, j: (i, j))
    return pl.pallas_call(
        functools.partial(_proj_a_kernel, tiles_per_seq),
        grid=(m // tm, n_col),
        in_specs=[
            pl.BlockSpec((tm, D_MODEL), lambda i, j: (i, 0)),
            pl.BlockSpec((1, D_MODEL), lambda i, j: (0, 0)),
            w_spec(0), w_spec(1), w_spec(2), w_spec(z_part),
            pl.BlockSpec((3, TN_A), lambda i, j: (0, j)),
        ],
        out_specs=[
            cols, cols,
            pl.BlockSpec((tm, D_MODEL), lambda i, j: (i, 0)),
            pl.BlockSpec((1, TAIL_ROWS, TN_A), lambda i, j: (i, 0, j)),
        ],
        out_shape=[
            jax.ShapeDtypeStruct((m, D_CONV), BF16),
            jax.ShapeDtypeStruct((m, DN_HEADS * DN_DV), BF16),
            jax.ShapeDtypeStruct((m, D_MODEL), BF16),
            jax.ShapeDtypeStruct((m // tm, TAIL_ROWS, D_CONV), F32),
        ],
        scratch_shapes=[pltpu.VMEM((n_col, HIST_A, TN_A), F32)],
        compiler_params=_params("arbitrary", "arbitrary"),
        name="proj_a",
    )(x, pre, wt, wt, wt, wt, caw)


def _l2_normalize(x, scale):
    return x * (lax.rsqrt(jnp.sum(x * x, axis=-1, keepdims=True) + EPS) * scale)


def _gate_columns(beta_all, gc_all):
    heads = range(DN_HEADS)
    gc_t = gc_all.T
    return ([beta_all[:, DN_HEADS + h:DN_HEADS + h + 1] for h in heads],
            [gc_all[:, h:h + 1] for h in heads], [gc_t[h:h + 1, :] for h in heads])


def _heads_intra(qn, kn, vh, beta_col, gc_col, gc_row, causal, strict, eye, squarings):
    heads = range(len(qn))
    decay = [jnp.exp(jnp.where(causal, gc_col[h] - gc_row[h], -jnp.inf)) for h in heads]
    kb = [kn[h] * beta_col[h] for h in heads]
    gram = [_mm_nt(jnp.concatenate([kb[h], qn[h]], axis=0), kn[h]) for h in heads]
    a_mat = [jnp.where(strict, gram[h][:CHUNK] * decay[h], 0.0) for h in heads]
    qk = [jnp.where(causal, gram[h][CHUNK:] * decay[h], 0.0) for h in heads]
    inv = [eye - a for a in a_mat]
    power = a_mat
    for _ in range(squarings):
        power = [_mm(x, x) for x in power]
        inv = [_mm(inv[h], eye + power[h]) for h in heads]
    egc = [jnp.exp(x) for x in gc_col]
    wu = [_mm(inv[h], jnp.concatenate([kb[h] * egc[h], vh[h] * beta_col[h]], axis=1)) for h in heads]
    w = [x[:, :DN_DK] for x in wu]
    u = [x[:, DN_DK:] for x in wu]
    qdec = [qn[h] * egc[h] for h in heads]
    return w, u, qdec, qk


def _gates(ab, alog, dtb):
    x = ab + dtb
    softplus = jnp.maximum(x, 0.0) + jnp.log1p(jnp.exp(-jnp.abs(x)))
    return -jnp.exp(alog) * softplus, jax.nn.sigmoid(ab)


def _masks(same_seq):
    row = lax.broadcasted_iota(jnp.int32, (CHUNK, CHUNK), 0)
    col = lax.broadcasted_iota(jnp.int32, (CHUNK, CHUNK), 1)
    causal = row >= col
    strict = row > col
    if same_seq:
        same = (row >> SEQ_SHIFT) == (col >> SEQ_SHIFT)
        causal = causal & same
        strict = strict & same
    eye = jnp.where(row == col, 1.0, 0.0).astype(F32)
    return row, col, causal, strict, causal.astype(F32), eye


def _proj_qkv_kernel(tiles_per_seq, h_ref, wq_ref, wk_ref, wv_ref, wabt_ref, cqw_ref, alog_ref, dtb_ref,
                     w_ref, qd_ref, kd_ref, u_ref, qk_ref, egl_ref, tail_ref, carry_ref, qkv_ref):
    i = pl.program_id(0)
    tm = h_ref.shape[0]

    @pl.when(i == 0)
    def _():
        carry_ref[...] = jnp.zeros_like(carry_ref)

    seq_start = i % tiles_per_seq == 0
    h = h_ref[...]
    g_all, beta_all = _gates(_nt_dot(h, wabt_ref[...]), alog_ref[...], dtb_ref[...])
    for t, wt_ref in enumerate((wq_ref, wk_ref, wv_ref)):
        for c in range(TN_IN // SUB_COLS):
            cols = slice(c * SUB_COLS, (c + 1) * SUB_COLS)
            lo = t * TN_IN + c * SUB_COLS
            p = _nt_dot(h, wt_ref[cols, :])
            tail_ref[0, :, lo:lo + SUB_COLS] = p[tm - TAIL_ROWS:tm, :]
            p, carry = _causal_conv(p, cqw_ref[:, lo:lo + SUB_COLS],
                                    _load_carry(carry_ref, t, cols, HIST_QKV, seq_start))
            _store_carry(carry_ref, t, cols, carry)
            p = _silu(p)
            if t == 2:
                qkv_ref[:, lo:lo + SUB_COLS] = p
            else:
                for k in range(SUB_COLS // DN_DK):
                    blk = p[:, k * DN_DK:(k + 1) * DN_DK]
                    qkv_ref[:, lo + k * DN_DK:lo + (k + 1) * DN_DK] = _l2_normalize(
                        blk, DN_DK ** -0.5 if t == 0 else 1.0)

    heads = range(DN_HEADS)
    _, _, causal, strict, low_f, eye = _masks(False)
    chunk_rows = [slice(g * CHUNK, (g + 1) * CHUNK) for g in range(tm // CHUNK)]
    qn, kn, vh, beta_col, gc_col, gc_row = [], [], [], [], [], []
    for rows in chunk_rows:
        gc_all = _mm_exact(low_f, g_all[rows])
        egl_ref[rows, :] = jnp.broadcast_to(jnp.exp(gc_all[CHUNK - 1:CHUNK, :]), (CHUNK, AB_LANES))
        cols = _gate_columns(beta_all[rows], gc_all)
        beta_col += cols[0]
        gc_col += cols[1]
        gc_row += cols[2]
        qn += [qkv_ref[rows, h * DN_DK:(h + 1) * DN_DK] for h in heads]
        kn += [qkv_ref[rows, (DN_HEADS + h) * DN_DK:(DN_HEADS + h + 1) * DN_DK] for h in heads]
        vh += [qkv_ref[rows, (2 * DN_HEADS + h) * DN_DV:(2 * DN_HEADS + h + 1) * DN_DV] for h in heads]
    w, u, qdec, qk = _heads_intra(qn, kn, vh, beta_col, gc_col, gc_row, causal, strict, eye, 5)
    for n in range(len(w)):
        rows, hd = chunk_rows[n // DN_HEADS], n % DN_HEADS
        lo = hd * DN_DK
        g_last = gc_col[n][CHUNK - 1:CHUNK, :]
        w_ref[rows, lo:lo + DN_DK] = w[n].astype(BF16)
        qd_ref[rows, lo:lo + DN_DK] = qdec[n].astype(BF16)
        kd_ref[rows, lo:lo + DN_DK] = (kn[n] * jnp.exp(g_last - gc_col[n])).astype(BF16)
        u_ref[rows, lo:lo + DN_DV] = u[n]
        qk_ref[rows, hd * CHUNK:(hd + 1) * CHUNK] = qk[n].astype(BF16)


def _proj_qkv(h, wt, wabt, cqw, alog, dtb, tiles_per_seq):
    m = h.shape[0]
    hk, hv = DN_HEADS * DN_DK, DN_HEADS * DN_DV
    rows = lambda width: pl.BlockSpec((TM, width), lambda i: (i, 0))
    whole = lambda shape: pl.BlockSpec(shape, lambda i: (0,) * len(shape))
    w_tile = lambda t: pl.BlockSpec((TN_IN, D_MODEL), lambda i: (QKV_BLOCK + t, 0))
    return pl.pallas_call(
        functools.partial(_proj_qkv_kernel, tiles_per_seq),
        grid=(m // TM,),
        in_specs=[rows(D_MODEL), w_tile(0), w_tile(1), w_tile(2), whole((AB_LANES, D_MODEL)),
                  whole((4, DN_QKV)), whole((1, AB_LANES)), whole((1, AB_LANES))],
        out_specs=[rows(hk), rows(hk), rows(hk), rows(hv), rows(DN_HEADS * CHUNK), rows(AB_LANES),
                   pl.BlockSpec((1, TAIL_ROWS, DN_QKV), lambda i: (i, 0, 0))],
        out_shape=[
            jax.ShapeDtypeStruct((m, hk), BF16), jax.ShapeDtypeStruct((m, hk), BF16),
            jax.ShapeDtypeStruct((m, hk), BF16), jax.ShapeDtypeStruct((m, hv), F32),
            jax.ShapeDtypeStruct((m, DN_HEADS * CHUNK), BF16), jax.ShapeDtypeStruct((m, AB_LANES), F32),
            jax.ShapeDtypeStruct((m // TM, TAIL_ROWS, DN_QKV), F32),
        ],
        scratch_shapes=[pltpu.VMEM((3, HIST_QKV, TN_IN), F32), pltpu.VMEM((TM, DN_QKV), F32)],
        compiler_params=_params("arbitrary"),
        name="proj_qkv",
    )(h, wt, wt, wt, wabt, cqw, alog, dtb)


def _delta_scan_kernel(w_ref, qd_ref, kd_ref, u_ref, qk_ref, egl_ref, zs_ref, dnn_ref,
                       y_ref, so_ref, s_ref):
    c = pl.program_id(1)

    @pl.when(c == 0)
    def _():
        s_ref[...] = jnp.zeros_like(s_ref)

    heads = range(DN_HEADS)
    dnn = dnn_ref[...]
    s = [s_ref[h] for h in heads]
    for g in range(w_ref.shape[0] // CHUNK):
        rows = slice(g * CHUNK, (g + 1) * CHUNK)
        col = [slice(h * DN_DK, (h + 1) * DN_DK) for h in heads]
        lhs = [jnp.concatenate([w_ref[rows, col[h]], qd_ref[rows, col[h]]], axis=0) for h in heads]
        r = [jnp.dot(lhs[h], s[h].astype(BF16), preferred_element_type=F32) for h in heads]
        v_new = [(u_ref[rows, col[h]] - r[h][:CHUNK]).astype(BF16) for h in heads]
        upd = [lax.dot_general(kd_ref[rows, col[h]], v_new[h], (((0,), (0,)), ((), ())),
                               preferred_element_type=F32) for h in heads]
        s = [s[h] * egl_ref[g * CHUNK:g * CHUNK + 1, h:h + 1] + upd[h] for h in heads]
        o = [r[h][CHUNK:] + jnp.dot(qk_ref[rows, h * CHUNK:(h + 1) * CHUNK], v_new[h],
                                    preferred_element_type=F32) for h in heads]
        for h in heads:
            y_ref[rows, col[h]] = (_rms(o[h], dnn) * zs_ref[rows, col[h]]).astype(y_ref.dtype)
    for h in heads:
        s_ref[h] = s[h]

    @pl.when(c == pl.num_programs(1) - 1)
    def _():
        so_ref[0] = s_ref[...]


def _delta_scan(w, qd, kd, u, qk, egl, zs, dnn, n_seq, seq_len):
    n_steps = seq_len // TM
    hk, hv = DN_HEADS * DN_DK, DN_HEADS * DN_DV
    rows = lambda width: pl.BlockSpec((TM, width), lambda b, c: (b * n_steps + c, 0))
    return pl.pallas_call(
        _delta_scan_kernel,
        grid=(n_seq, n_steps),
        in_specs=[rows(hk), rows(hk), rows(hk), rows(hv), rows(DN_HEADS * CHUNK), rows(AB_LANES), rows(hv),
                  pl.BlockSpec((1, DN_DV), lambda b, c: (0, 0))],
        out_specs=[rows(hv), pl.BlockSpec((1, DN_HEADS, DN_DK, DN_DV), lambda b, c: (b, 0, 0, 0))],
        out_shape=[jax.ShapeDtypeStruct((n_seq * seq_len, hv), BF16),
                   jax.ShapeDtypeStruct((n_seq, DN_HEADS, DN_DK, DN_DV), F32)],
        scratch_shapes=[pltpu.VMEM((DN_HEADS, DN_DK, DN_DV), F32)],
        compiler_params=_params("arbitrary", "arbitrary"),
        name="delta_scan",
    )(w, qd, kd, u, qk, egl, zs, dnn)


def _mix_sample_kernel(pa_ref, pq_ref, pz_ref, pab_ref, sa_ref, sq_ref, si_ref,
                       caw_ref, cqw_ref, alog_ref, dtb_ref, dnn_ref,
                       y_ref, ca_ref, cq_ref, so_ref,
                       gb_ref, ua_ref, xq_ref, z_ref, ab_ref, ybuf_ref):
    seqs = range(SEQ_PER_STEP)
    pad = SEQ_PAD - SAMPLE_LEN
    for b in seqs:
        r0 = SEQ_PAD * b
        nxt = (b + 1) % SEQ_PER_STEP
        src = slice(SAMPLE_LEN * b, SAMPLE_LEN * (b + 1))
        pa = pa_ref[src, :]
        gb_ref[r0:r0 + SAMPLE_LEN, :] = pa[:, :D_CONV]
        ua_ref[r0:r0 + SAMPLE_LEN, :] = pa[:, D_CONV:2 * D_CONV] * pa[:, 2 * D_CONV:]
        xq_ref[r0:r0 + SAMPLE_LEN, :] = pq_ref[src, :]
        z_ref[r0:r0 + SAMPLE_LEN, :] = pz_ref[src, :]
        ab_ref[r0:r0 + SAMPLE_LEN, :] = pab_ref[src, :]
        for ref in (gb_ref, ua_ref, xq_ref, z_ref, ab_ref):
            ref[r0 + SAMPLE_LEN:r0 + SEQ_PAD, :] = jnp.zeros((pad, ref.shape[1]), F32)
        ua_ref[r0 + SEQ_PAD - HIST_A:r0 + SEQ_PAD, :] = sa_ref[nxt]
        for r in range(HIST_QKV):
            dst = r0 + SEQ_PAD - HIST_QKV + r
            xq_ref[dst:dst + 1, :] = sq_ref[r, nxt:nxt + 1, :]
    ybuf_ref[:, :D_CONV] = gb_ref[...] * _causal_conv(ua_ref[...], caw_ref[...], None)[0]
    row, col, causal, strict, low_f, eye = _masks(True)
    valid = (row[:, 0:1] & (SEQ_PAD - 1)) < SAMPLE_LEN
    cqkv = jnp.where(valid, _silu(_causal_conv(xq_ref[...], cqw_ref[...], None)[0]), 0.0)
    for b in seqs:
        r0 = SEQ_PAD * b
        ca_ref[b] = ua_ref[r0 + SAMPLE_LEN - HIST_A:r0 + SAMPLE_LEN, :]
        for r in range(HIST_QKV):
            src = r0 + SAMPLE_LEN - HIST_QKV + r
            cq_ref[r, b:b + 1, :] = xq_ref[src:src + 1, :]

    g_all, beta_all = _gates(ab_ref[...], alog_ref[...], dtb_ref[...])
    g_all = jnp.where(valid, g_all, 0.0)
    beta_all = jnp.where(valid, beta_all, 0.0)
    gc_all = _mm_exact(low_f, g_all)
    last_sel = (col == (row | (SEQ_PAD - 1))).astype(F32)
    gl_all = _mm_exact(last_sel, gc_all)
    z = z_ref[...]
    dnn = dnn_ref[...]
    heads = range(DN_HEADS)
    qn = [_l2_normalize(cqkv[:, h * DN_DK:(h + 1) * DN_DK], DN_DK ** -0.5) for h in heads]
    kn = [_l2_normalize(cqkv[:, (DN_HEADS + h) * DN_DK:(DN_HEADS + h + 1) * DN_DK], 1.0) for h in heads]
    vh = [cqkv[:, (2 * DN_HEADS + h) * DN_DV:(2 * DN_HEADS + h + 1) * DN_DV] for h in heads]
    beta_col, gc_col, gc_row = _gate_columns(beta_all, gc_all)
    w, u, qdec, qk = _heads_intra(qn, kn, vh, beta_col, gc_col, gc_row, causal, strict, eye, 1)
    gl_col = [gl_all[:, h:h + 1] for h in heads]
    kdec = [kn[h] * jnp.exp(gl_col[h] - gc_col[h]) for h in heads]
    s_scale = [jnp.exp(x) for x in gl_col]
    rows = [slice(SEQ_PAD * b, SEQ_PAD * (b + 1)) for b in seqs]
    pairs = [(h, b) for h in heads for b in seqs]
    s_old = [si_ref[b, h] for h, b in pairs]
    r = [_mm(jnp.concatenate([w[h][rows[b]], qdec[h][rows[b]]], axis=0), s_old[n])
         for n, (h, b) in enumerate(pairs)]
    v_new = [u[h][rows[b]] - r[n][:SEQ_PAD] for n, (h, b) in enumerate(pairs)]
    upd = [_mm_tn(kdec[h][rows[b]], v_new[n]) for n, (h, b) in enumerate(pairs)]
    for n, (h, b) in enumerate(pairs):
        so_ref[b, h] = s_old[n] * s_scale[h][SEQ_PAD * b:SEQ_PAD * b + 1, :] + upd[n]
    for h in heads:
        mine = range(h * SEQ_PER_STEP, (h + 1) * SEQ_PER_STEP)
        o = (jnp.concatenate([r[n][SEQ_PAD:] for n in mine], axis=0)
             + _mm(qk[h], jnp.concatenate([v_new[n] for n in mine], axis=0)))
        lo = h * DN_DV
        ybuf_ref[:, D_CONV + lo:D_CONV + lo + DN_DV] = _rms(o, dnn) * _silu(z[:, lo:lo + DN_DV])
    for b in seqs:
        y_ref[SAMPLE_LEN * b:SAMPLE_LEN * (b + 1), :] = ybuf_ref[SEQ_PAD * b:SEQ_PAD * b + SAMPLE_LEN, :]


def _mix_sample(proj, ab, state_a, state_q, state_s, caw, cqw, alog, dtb, dnn):
    n_seq = state_s.shape[0]
    full2 = lambda i: (0, 0)
    seq_rows = lambda width, col: pl.BlockSpec((SEQ_PER_STEP * SAMPLE_LEN, width), lambda i: (i, col))
    return pl.pallas_call(
        _mix_sample_kernel,
        grid=(n_seq // SEQ_PER_STEP,),
        in_specs=[
            seq_rows(3 * D_CONV, 0), seq_rows(DN_QKV, 1), seq_rows(DN_HEADS * DN_DV, Z_BLOCK), seq_rows(AB_LANES, 0),
            pl.BlockSpec((SEQ_PER_STEP, 2, D_CONV), lambda i: (i, 0, 0)),
            pl.BlockSpec((3, SEQ_PER_STEP, DN_QKV), lambda i: (0, i, 0)),
            pl.BlockSpec((SEQ_PER_STEP, DN_HEADS, DN_DK, DN_DV), lambda i: (i, 0, 0, 0)),
            pl.BlockSpec((3, D_CONV), full2),
            pl.BlockSpec((4, DN_QKV), full2),
            pl.BlockSpec((1, AB_LANES), full2),
            pl.BlockSpec((1, AB_LANES), full2),
            pl.BlockSpec((1, DN_DV), full2),
        ],
        out_specs=[
            seq_rows(D_MODEL, 0),
            pl.BlockSpec((SEQ_PER_STEP, 2, D_CONV), lambda i: (i, 0, 0)),
            pl.BlockSpec((3, SEQ_PER_STEP, DN_QKV), lambda i: (0, i, 0)),
            pl.BlockSpec((SEQ_PER_STEP, DN_HEADS, DN_DK, DN_DV), lambda i: (i, 0, 0, 0)),
        ],
        out_shape=[
            jax.ShapeDtypeStruct((n_seq * SAMPLE_LEN, D_MODEL), F32),
            jax.ShapeDtypeStruct((n_seq, 2, D_CONV), F32),
            jax.ShapeDtypeStruct((3, n_seq, DN_QKV), F32),
            jax.ShapeDtypeStruct((n_seq, DN_HEADS, DN_DK, DN_DV), F32),
        ],
        scratch_shapes=[
            pltpu.VMEM((CHUNK, D_CONV), F32),
            pltpu.VMEM((CHUNK, D_CONV), F32),
            pltpu.VMEM((CHUNK, DN_QKV), F32),
            pltpu.VMEM((CHUNK, DN_HEADS * DN_DV), F32),
            pltpu.VMEM((CHUNK, AB_LANES), F32),
            pltpu.VMEM((CHUNK, D_MODEL), F32),
        ],
        compiler_params=_params("parallel"),
        name="mix_sample",
    )(proj, proj, proj, ab, state_a, state_q, state_s, caw, cqw, alog, dtb, dnn)


def _outproj_kernel(ya_ref, yb_ref, wa_ref, wb_ref, post_ref, x_ref, o_ref):
    m = (jnp.dot(ya_ref[...].astype(BF16), wa_ref[...], preferred_element_type=F32)
         + jnp.dot(yb_ref[...].astype(BF16), wb_ref[...], preferred_element_type=F32))
    o_ref[...] = x_ref[...] + _rms(m, post_ref[...])


def _outproj(ya, yb, yb_block, w, post, x):
    m = x.shape[0]
    half = D_MODEL // 2
    return pl.pallas_call(
        _outproj_kernel,
        grid=(m // TM,),
        in_specs=[
            pl.BlockSpec((TM, half), lambda i: (i, 0)),
            pl.BlockSpec((TM, half), lambda i: (i, yb_block)),
            pl.BlockSpec((half, D_MODEL), lambda i: (0, 0)),
            pl.BlockSpec((half, D_MODEL), lambda i: (1, 0)),
            pl.BlockSpec((1, D_MODEL), lambda i: (0, 0)),
            pl.BlockSpec((TM, D_MODEL), lambda i: (i, 0)),
        ],
        out_specs=pl.BlockSpec((TM, D_MODEL), lambda i: (i, 0)),
        out_shape=jax.ShapeDtypeStruct((m, D_MODEL), F32),
        compiler_params=_params("parallel"),
        name="outproj",
    )(ya, yb, w, w, post, x)


def _ple_kernel(x_ref, p_ref, pre_ref, post_ref, wg_ref, wp_ref, o_ref):
    x = x_ref[...]
    h = _rms(x, pre_ref[...]).astype(BF16)
    gate = jax.nn.sigmoid(jnp.dot(h, wg_ref[...], preferred_element_type=F32))
    proj = jnp.dot(p_ref[...].astype(BF16), wp_ref[...], preferred_element_type=F32)
    o_ref[...] = x + _rms(gate * proj, post_ref[...])


def _ple(x, p, pre, post, wg, wp):
    m = x.shape[0]
    return pl.pallas_call(
        _ple_kernel,
        grid=(m // TM,),
        in_specs=[
            pl.BlockSpec((TM, D_MODEL), lambda i: (i, 0)),
            pl.BlockSpec((TM, D_PLE), lambda i: (i, 0)),
            pl.BlockSpec((1, D_MODEL), lambda i: (0, 0)),
            pl.BlockSpec((1, D_MODEL), lambda i: (0, 0)),
            pl.BlockSpec((D_MODEL, D_MODEL), lambda i: (0, 0)),
            pl.BlockSpec((D_PLE, D_MODEL), lambda i: (0, 0)),
        ],
        out_specs=pl.BlockSpec((TM, D_MODEL), lambda i: (i, 0)),
        out_shape=jax.ShapeDtypeStruct((m, D_MODEL), F32),
        compiler_params=_params("parallel"),
        name="ple",
    )(x, p, pre, post, wg, wp)


def _row(v):
    return v.reshape(1, -1).astype(F32)


def _pad_lanes(v):
    return jnp.pad(_row(v), ((0, 0), (0, AB_LANES - v.shape[-1])))


def _seq_tails(tails, tiles_per_seq, n_rows):
    return tails[tiles_per_seq - 1::tiles_per_seq, TAIL_ROWS - n_rows:, :]


def kernel(x_prompt, x_sample, state_conv_a, state_conv_qkv, state_delta, p_prompt, p_sample,
           f1_pre, f1_post, f1_wg, f1_wu, f1_wd,
           mix_pre, mix_post, w_in, conv_a_w, conv_qkv_w, a_log, dt_bias, dn_norm, w_out,
           f2_pre, f2_post, f2_wg, f2_wu, f2_wd,
           ple_pre, ple_post, w_ple_gate, w_ple_proj):
    depth = f1_pre.shape[0]
    n_p, seq_p, _ = x_prompt.shape
    n_s, seq_s, _ = x_sample.shape
    rows_p = n_p * seq_p
    rows_s = n_s * seq_s
    assert seq_p % TM_BIG == 0 and seq_s == SAMPLE_LEN and n_s % SEQ_PER_STEP == 0
    assert rows_s == TM
    assert w_in.shape[-1] == IN_MAIN + N_AB
    tiles_per_seq = seq_p // TM_BIG
    xp = x_prompt.reshape(rows_p, D_MODEL)
    xs = x_sample.reshape(rows_s, D_MODEL)
    outs = [[] for _ in range(6)]
    for i in range(depth):
        pre, post = _row(f1_pre[i]), _row(f1_post[i])
        xs, wg, wu, wd = _ffn(xs, pre, post, f1_wg[i], f1_wu[i], f1_wd[i], TM)
        xp = _ffn(xp, pre, post, wg, wu, wd, TM_BIG)[0]

        pre = _row(mix_pre[i])
        caw, cqw = conv_a_w[i], conv_qkv_w[i]
        proj_s, ab_s, wt, wabt = _inproj_cast(xs, pre, w_in[i].T)
        y_a, zs, h, tail_a = _proj_a(xp, pre, wt, caw, TM_BIG, tiles_per_seq)
        alog, dtb, dnn = _pad_lanes(a_log[i]), _pad_lanes(dt_bias[i]), _row(dn_norm[i])
        wy_w, qdec, kdec, wy_u, qk, egl, tail_q = _proj_qkv(h, wt, wabt, cqw, alog, dtb, seq_p // TM)
        y_b, s_p = _delta_scan(wy_w, qdec, kdec, wy_u, qk, egl, zs, dnn, n_p, seq_p)
        ca_p = _seq_tails(tail_a, tiles_per_seq, HIST_A)
        cq_p = _seq_tails(tail_q, seq_p // TM, HIST_QKV)
        y_s, ca_s, cq_s, s_s = _mix_sample(
            proj_s, ab_s, state_conv_a[i], jnp.swapaxes(state_conv_qkv[i], 0, 1), state_delta[i],
            caw, cqw, alog, dtb, dnn)
        cq_s = jnp.swapaxes(cq_s, 0, 1)

        post = _row(mix_post[i])
        w_out_bf16 = w_out[i].astype(BF16)
        xs = _outproj(y_s, y_s, 1, w_out_bf16, post, xs)
        xp = _outproj(y_a, y_b, 0, w_out_bf16, post, xp)

        pre, post = _row(f2_pre[i]), _row(f2_post[i])
        xs, wg, wu, wd = _ffn(xs, pre, post, f2_wg[i], f2_wu[i], f2_wd[i], TM)
        xp = _ffn(xp, pre, post, wg, wu, wd, TM_BIG)[0]

        pre, post = _row(ple_pre[i]), _row(ple_post[i])
        wg, wp = w_ple_gate[i].astype(BF16), w_ple_proj[i].astype(BF16)
        xs = _ple(xs, p_sample[i].reshape(rows_s, D_PLE), pre, post, wg, wp)
        xp = _ple(xp, p_prompt[i].reshape(rows_p, D_PLE), pre, post, wg, wp)
        for lst, val in zip(outs, (ca_p, cq_p, s_p, ca_s, cq_s, s_s)):
            lst.append(val)
    return (xp.reshape(n_p, seq_p, D_MODEL), xs.reshape(n_s, seq_s, D_MODEL),
            *[jnp.stack(lst) for lst in outs])
```

```python
import functools

import jax
import jax.numpy as jnp
from jax import lax
from jax.experimental import pallas as pl
from jax.experimental.pallas import tpu as pltpu

F32 = jnp.float32
BF16 = jnp.bfloat16

D_MODEL = 2048
D_CONV = 1024
DN_HEADS = 8
DN_DK = 128
DN_DV = 128
DN_QKV = 3072
D_FF = 5632
D_PLE = 256
EPS = 1e-6
CHUNK = 64
SAMPLE_LEN = 4
SEQ_PAD = 8
SEQ_SHIFT = 3
SEQ_PER_STEP = CHUNK // SEQ_PAD
IN_MAIN = 7168
QKV_BLOCK = 3
Z_BLOCK = 6
HIST_A = 2
HIST_QKV = 3
TAIL_ROWS = 8
N_AB = 16
AB_LANES = 128

TM = 512
TF = 512
TF_CAST = 256
TN_IN = 1024
TM_BIG = 1024
TN_A = 512
SUB_COLS = 512
SCAN_ROWS = 256
ROW_CHUNK = 32
VMEM_LIMIT = 56 * 1024 * 1024


def _rms(x, g):
    ms = jnp.mean(x * x, axis=-1, keepdims=True)
    return x * lax.rsqrt(ms + EPS) * g


def _silu(x):
    return x * jax.nn.sigmoid(x)


def _mm(a, b):
    return jnp.dot(a.astype(BF16), b.astype(BF16), preferred_element_type=F32)


def _mm_nt(a, b):
    return lax.dot_general(a.astype(BF16), b.astype(BF16), (((1,), (1,)), ((), ())),
                           preferred_element_type=F32)


def _mm_tn(a, b):
    return jnp.dot(a.T.astype(BF16), b.astype(BF16), preferred_element_type=F32)


def _mm_exact(a, b):
    return jnp.dot(a, b, precision=lax.Precision.HIGHEST, preferred_element_type=F32)


def _nt_dot(a, bt):
    return lax.dot_general(a, bt, (((1,), (1,)), ((), ())), preferred_element_type=F32)


def _for_row_chunks(n_rows, body):
    def step(k, carry):
        body(pl.ds(pl.multiple_of(k * ROW_CHUNK, ROW_CHUNK), ROW_CHUNK))
        return carry
    lax.fori_loop(0, n_rows // ROW_CHUNK, step, 0, unroll=4)


def _params(*semantics):
    return pltpu.CompilerParams(dimension_semantics=semantics, vmem_limit_bytes=VMEM_LIMIT)


def _ffn_kernel(emit_bf16, x_ref, pre_ref, post_ref, wg_ref, wu_ref, wd_ref, o_ref, *rest):
    h_ref = rest[-1]
    j = pl.program_id(1)

    @pl.when(j == 0)
    def _():
        def norm_rows(rows):
            h_ref[rows, :] = _rms(x_ref[rows, :], pre_ref[...]).astype(BF16)
            o_ref[rows, :] = jnp.zeros((ROW_CHUNK, D_MODEL), F32)
        _for_row_chunks(x_ref.shape[0], norm_rows)

    wg, wu, wd = wg_ref[...], wu_ref[...], wd_ref[...]
    if emit_bf16:
        wg, wu, wd = wg.astype(BF16), wu.astype(BF16), wd.astype(BF16)
        rest[0][...] = wg
        rest[1][...] = wu
        rest[2][...] = wd
    h = h_ref[...]
    g = jnp.dot(h, wg, preferred_element_type=F32)
    u = jnp.dot(h, wu, preferred_element_type=F32)
    a = (_silu(g) * u).astype(BF16)
    o_ref[...] += jnp.dot(a, wd, preferred_element_type=F32)

    @pl.when(j == pl.num_programs(1) - 1)
    def _():
        group = 4 * ROW_CHUNK

        def finish_rows(k, carry):
            r0 = pl.multiple_of(k * group, group)
            chunks = [pl.ds(r0 + s * ROW_CHUNK, ROW_CHUNK) for s in range(group // ROW_CHUNK)]
            scale = [0.5 * lax.rsqrt(jnp.mean(jnp.square(o_ref[c, :]), axis=-1, keepdims=True) + EPS)
                     for c in chunks]
            for c, sc in zip(chunks, scale):
                o_ref[c, :] = x_ref[c, :] + o_ref[c, :] * sc * post_ref[...]
            return carry
        lax.fori_loop(0, x_ref.shape[0] // group, finish_rows, 0)


def _ffn(x, pre, post, wg, wu, wd, tm):
    m = x.shape[0]
    emit_bf16 = wg.dtype != BF16
    tf = TF_CAST if emit_bf16 else TF
    up_spec = pl.BlockSpec((D_MODEL, tf), lambda i, j: (0, j))
    down_spec = pl.BlockSpec((tf, D_MODEL), lambda i, j: (j, 0))
    out_specs = [pl.BlockSpec((tm, D_MODEL), lambda i, j: (i, 0))]
    out_shape = [jax.ShapeDtypeStruct((m, D_MODEL), F32)]
    if emit_bf16:
        assert m == tm, "weights are re-emitted once, by a single-row-tile call"
        out_specs += [up_spec, up_spec, down_spec]
        out_shape += [jax.ShapeDtypeStruct(w.shape, BF16) for w in (wg, wu, wd)]
    outs = pl.pallas_call(
        functools.partial(_ffn_kernel, emit_bf16),
        grid=(m // tm, D_FF // tf),
        in_specs=[
            pl.BlockSpec((tm, D_MODEL), lambda i, j: (i, 0)),
            pl.BlockSpec((1, D_MODEL), lambda i, j: (0, 0)),
            pl.BlockSpec((1, D_MODEL), lambda i, j: (0, 0)),
            up_spec, up_spec, down_spec,
        ],
        out_specs=out_specs,
        out_shape=out_shape,
        scratch_shapes=[pltpu.VMEM((tm, D_MODEL), BF16)],
        compiler_params=_params("parallel", "arbitrary"),
        name="ffn_cast" if emit_bf16 else "ffn",
    )(x, pre, post, wg, wu, wd)
    return outs if emit_bf16 else (outs[0], wg, wu, wd)


def _inproj_cast_kernel(x_ref, pre_ref, wt_ref, wabt_ref, o_ref, ab_ref, wt16_ref, wabt16_ref, h_ref):
    @pl.when(pl.program_id(1) == 0)
    def _():
        h = _rms(x_ref[...], pre_ref[...]).astype(BF16)
        h_ref[...] = h
        wabt = jnp.concatenate(
            [wabt_ref[...].astype(BF16), jnp.zeros((AB_LANES - N_AB, D_MODEL), BF16)], axis=0)
        wabt16_ref[...] = wabt
        ab_ref[...] = _nt_dot(h, wabt)

    wt = wt_ref[...].astype(BF16)
    wt16_ref[...] = wt
    o_ref[...] = _nt_dot(h_ref[...], wt)


def _inproj_cast(x, pre, wt):
    m = x.shape[0]
    assert m == TM, "weights are re-emitted once, by a single-row-tile call"
    wt_spec = pl.BlockSpec((TN_IN, D_MODEL), lambda i, j: (j, 0))
    return pl.pallas_call(
        _inproj_cast_kernel,
        grid=(1, IN_MAIN // TN_IN),
        in_specs=[
            pl.BlockSpec((TM, D_MODEL), lambda i, j: (0, 0)),
            pl.BlockSpec((1, D_MODEL), lambda i, j: (0, 0)),
            wt_spec,
            pl.BlockSpec((N_AB, D_MODEL), lambda i, j: (IN_MAIN // N_AB, 0)),
        ],
        out_specs=[
            pl.BlockSpec((TM, TN_IN), lambda i, j: (0, j)),
            pl.BlockSpec((TM, AB_LANES), lambda i, j: (0, 0)),
            wt_spec,
            pl.BlockSpec((AB_LANES, D_MODEL), lambda i, j: (0, 0)),
        ],
        out_shape=[
            jax.ShapeDtypeStruct((m, IN_MAIN), F32),
            jax.ShapeDtypeStruct((m, AB_LANES), F32),
            jax.ShapeDtypeStruct((IN_MAIN, D_MODEL), BF16),
            jax.ShapeDtypeStruct((AB_LANES, D_MODEL), BF16),
        ],
        scratch_shapes=[pltpu.VMEM((TM, D_MODEL), BF16)],
        compiler_params=_params("parallel", "arbitrary"),
        name="inproj_cast",
    )(x, pre, wt, wt)


def _causal_conv(x, w, carry_in):
    width = w.shape[0]
    rows = x.shape[0]
    is_row0 = lax.broadcasted_iota(jnp.int32, (8, 1), 0) == 0
    acc = x * w[0:1, :]
    carry_out = []
    for j in range(1, width):
        delayed = pltpu.roll(acc, 1, axis=0)
        if carry_in is not None:
            carry_out.append(acc[rows - 1:rows, :])
            head = jnp.where(is_row0, carry_in[j - 1], delayed[:8])
            delayed = jnp.concatenate([head, delayed[8:]], axis=0)
        acc = x * w[j:j + 1, :] + delayed
    return acc, carry_out


def _load_carry(carry_ref, j, cols, n, seq_start):
    return [jnp.where(seq_start, 0.0, carry_ref[j, r:r + 1, cols]) for r in range(n)]


def _store_carry(carry_ref, j, cols, rows):
    for r, row in enumerate(rows):
        carry_ref[j, r:r + 1, cols] = row


def _proj_a_kernel(tiles_per_seq, x_ref, pre_ref, wgb_ref, wgcv_ref, whc_ref, wz_ref, caw_ref,
                   ya_ref, zs_ref, h_ref, tail_ref, carry_ref):
    i, j = pl.program_id(0), pl.program_id(1)
    tm = x_ref.shape[0]

    @pl.when(j == 0)
    def _():
        def norm_rows(rows):
            h_ref[rows, :] = _rms(x_ref[rows, :], pre_ref[...]).astype(BF16)
        _for_row_chunks(tm, norm_rows)

    @pl.when((i == 0) & (j == 0))
    def _():
        carry_ref[...] = jnp.zeros_like(carry_ref)

    seq_start = i % tiles_per_seq == 0
    h = h_ref[...]
    for c in range(TN_A // SUB_COLS):
        cols = slice(c * SUB_COLS, (c + 1) * SUB_COLS)
        u = _nt_dot(h, wgcv_ref[cols, :]) * _nt_dot(h, whc_ref[cols, :])
        tail_ref[0, :, cols] = u[tm - TAIL_ROWS:tm, :]
        conv, carry = _causal_conv(u, caw_ref[:, cols], _load_carry(carry_ref, j, cols, HIST_A, seq_start))
        _store_carry(carry_ref, j, cols, carry)
        ya_ref[:, cols] = (_nt_dot(h, wgb_ref[cols, :]) * conv).astype(BF16)
        zs_ref[:, cols] = _silu(_nt_dot(h, wz_ref[cols, :])).astype(BF16)


def _proj_a(x, pre, wt, caw, tm, tiles_per_seq):
    m = x.shape[0]
    n_col = D_CONV // TN_A
    assert DN_HEADS * DN_DV == D_CONV
    z_part = Z_BLOCK * TN_IN // D_CONV
    w_spec = lambda part: pl.BlockSpec((TN_A, D_MODEL), lambda i, j: (part * n_col + j, 0))
    cols = pl.BlockSpec((tm, TN_A), lambda i, j: (i, j))
    return pl.pallas_call(
        functools.partial(_proj_a_kernel, tiles_per_seq),
        grid=(m // tm, n_col),
        in_specs=[
            pl.BlockSpec((tm, D_MODEL), lambda i, j: (i, 0)),
            pl.BlockSpec((1, D_MODEL), lambda i, j: (0, 0)),
            w_spec(0), w_spec(1), w_spec(2), w_spec(z_part),
            pl.BlockSpec((3, TN_A), lambda i, j: (0, j)),
        ],
        out_specs=[
            cols, cols,
            pl.BlockSpec((tm, D_MODEL), lambda i, j: (i, 0)),
            pl.BlockSpec((1, TAIL_ROWS, TN_A), lambda i, j: (i, 0, j)),
        ],
        out_shape=[
            jax.ShapeDtypeStruct((m, D_CONV), BF16),
            jax.ShapeDtypeStruct((m, DN_HEADS * DN_DV), BF16),
            jax.ShapeDtypeStruct((m, D_MODEL), BF16),
            jax.ShapeDtypeStruct((m // tm, TAIL_ROWS, D_CONV), F32),
        ],
        scratch_shapes=[pltpu.VMEM((n_col, HIST_A, TN_A), F32)],
        compiler_params=_params("arbitrary", "arbitrary"),
        name="proj_a",
    )(x, pre, wt, wt, wt, wt, caw)


def _l2_normalize(x, scale):
    return x * (lax.rsqrt(jnp.sum(x * x, axis=-1, keepdims=True) + EPS) * scale)


def _gate_columns(beta_all, gc_all):
    heads = range(DN_HEADS)
    gc_t = gc_all.T
    return ([beta_all[:, DN_HEADS + h:DN_HEADS + h + 1] for h in heads],
            [gc_all[:, h:h + 1] for h in heads], [gc_t[h:h + 1, :] for h in heads])


def _heads_intra(qn, kn, vh, beta_col, gc_col, gc_row, causal, strict, eye, squarings):
    heads = range(len(qn))
    decay = [jnp.exp(jnp.where(causal, gc_col[h] - gc_row[h], -jnp.inf)) for h in heads]
    kb = [kn[h] * beta_col[h] for h in heads]
    gram = [_mm_nt(jnp.concatenate([kb[h], qn[h]], axis=0), kn[h]) for h in heads]
    a_mat = [jnp.where(strict, gram[h][:CHUNK] * decay[h], 0.0) for h in heads]
    qk = [jnp.where(causal, gram[h][CHUNK:] * decay[h], 0.0) for h in heads]
    inv = [eye - a for a in a_mat]
    power = a_mat
    for _ in range(squarings):
        power = [_mm(x, x) for x in power]
        inv = [_mm(inv[h], eye + power[h]) for h in heads]
    egc = [jnp.exp(x) for x in gc_col]
    wu = [_mm(inv[h], jnp.concatenate([kb[h] * egc[h], vh[h] * beta_col[h]], axis=1)) for h in heads]
    w = [x[:, :DN_DK] for x in wu]
    u = [x[:, DN_DK:] for x in wu]
    qdec = [qn[h] * egc[h] for h in heads]
    return w, u, qdec, qk


def _gates(ab, alog, dtb):
    x = ab + dtb
    softplus = jnp.maximum(x, 0.0) + jnp.log1p(jnp.exp(-jnp.abs(x)))
    return -jnp.exp(alog) * softplus, jax.nn.sigmoid(ab)


def _masks(same_seq):
    row = lax.broadcasted_iota(jnp.int32, (CHUNK, CHUNK), 0)
    col = lax.broadcasted_iota(jnp.int32, (CHUNK, CHUNK), 1)
    causal = row >= col
    strict = row > col
    if same_seq:
        same = (row >> SEQ_SHIFT) == (col >> SEQ_SHIFT)
        causal = causal & same
        strict = strict & same
    eye = jnp.where(row == col, 1.0, 0.0).astype(F32)
    return row, col, causal, strict, causal.astype(F32), eye


def _proj_qkv_kernel(tiles_per_seq, h_ref, wq_ref, wk_ref, wv_ref, wabt_ref, cqw_ref, alog_ref, dtb_ref,
                     w_ref, qd_ref, kd_ref, u_ref, qk_ref, egl_ref, tail_ref, carry_ref, qkv_ref):
    i = pl.program_id(0)
    tm = h_ref.shape[0]

    @pl.when(i == 0)
    def _():
        carry_ref[...] = jnp.zeros_like(carry_ref)

    seq_start = i % tiles_per_seq == 0
    h = h_ref[...]
    g_all, beta_all = _gates(_nt_dot(h, wabt_ref[...]), alog_ref[...], dtb_ref[...])
    for t, wt_ref in enumerate((wq_ref, wk_ref, wv_ref)):
        for c in range(TN_IN // SUB_COLS):
            cols = slice(c * SUB_COLS, (c + 1) * SUB_COLS)
            lo = t * TN_IN + c * SUB_COLS
            p = _nt_dot(h, wt_ref[cols, :])
            tail_ref[0, :, lo:lo + SUB_COLS] = p[tm - TAIL_ROWS:tm, :]
            p, carry = _causal_conv(p, cqw_ref[:, lo:lo + SUB_COLS],
                                    _load_carry(carry_ref, t, cols, HIST_QKV, seq_start))
            _store_carry(carry_ref, t, cols, carry)
            p = _silu(p)
            if t == 2:
                qkv_ref[:, lo:lo + SUB_COLS] = p
            else:
                for k in range(SUB_COLS // DN_DK):
                    blk = p[:, k * DN_DK:(k + 1) * DN_DK]
                    qkv_ref[:, lo + k * DN_DK:lo + (k + 1) * DN_DK] = _l2_normalize(
                        blk, DN_DK ** -0.5 if t == 0 else 1.0)

    heads = range(DN_HEADS)
    _, _, causal, strict, low_f, eye = _masks(False)
    chunk_rows = [slice(g * CHUNK, (g + 1) * CHUNK) for g in range(tm // CHUNK)]
    qn, kn, vh, beta_col, gc_col, gc_row = [], [], [], [], [], []
    for rows in chunk_rows:
        gc_all = _mm_exact(low_f, g_all[rows])
        egl_ref[rows, :] = jnp.broadcast_to(jnp.exp(gc_all[CHUNK - 1:CHUNK, :]), (CHUNK, AB_LANES))
        cols = _gate_columns(beta_all[rows], gc_all)
        beta_col += cols[0]
        gc_col += cols[1]
        gc_row += cols[2]
        qn += [qkv_ref[rows, h * DN_DK:(h + 1) * DN_DK] for h in heads]
        kn += [qkv_ref[rows, (DN_HEADS + h) * DN_DK:(DN_HEADS + h + 1) * DN_DK] for h in heads]
        vh += [qkv_ref[rows, (2 * DN_HEADS + h) * DN_DV:(2 * DN_HEADS + h + 1) * DN_DV] for h in heads]
    w, u, qdec, qk = _heads_intra(qn, kn, vh, beta_col, gc_col, gc_row, causal, strict, eye, 5)
    for n in range(len(w)):
        rows, hd = chunk_rows[n // DN_HEADS], n % DN_HEADS
        lo = hd * DN_DK
        g_last = gc_col[n][CHUNK - 1:CHUNK, :]
        w_ref[rows, lo:lo + DN_DK] = w[n].astype(BF16)
        qd_ref[rows, lo:lo + DN_DK] = qdec[n].astype(BF16)
        kd_ref[rows, lo:lo + DN_DK] = (kn[n] * jnp.exp(g_last - gc_col[n])).astype(BF16)
        u_ref[rows, lo:lo + DN_DV] = u[n]
        qk_ref[rows, hd * CHUNK:(hd + 1) * CHUNK] = qk[n].astype(BF16)


def _proj_qkv(h, wt, wabt, cqw, alog, dtb, tiles_per_seq):
    m = h.shape[0]
    hk, hv = DN_HEADS * DN_DK, DN_HEADS * DN_DV
    rows = lambda width: pl.BlockSpec((TM, width), lambda i: (i, 0))
    whole = lambda shape: pl.BlockSpec(shape, lambda i: (0,) * len(shape))
    w_tile = lambda t: pl.BlockSpec((TN_IN, D_MODEL), lambda i: (QKV_BLOCK + t, 0))
    return pl.pallas_call(
        functools.partial(_proj_qkv_kernel, tiles_per_seq),
        grid=(m // TM,),
        in_specs=[rows(D_MODEL), w_tile(0), w_tile(1), w_tile(2), whole((AB_LANES, D_MODEL)),
                  whole((4, DN_QKV)), whole((1, AB_LANES)), whole((1, AB_LANES))],
        out_specs=[rows(hk), rows(hk), rows(hk), rows(hv), rows(DN_HEADS * CHUNK), rows(AB_LANES),
                   pl.BlockSpec((1, TAIL_ROWS, DN_QKV), lambda i: (i, 0, 0))],
        out_shape=[
            jax.ShapeDtypeStruct((m, hk), BF16), jax.ShapeDtypeStruct((m, hk), BF16),
            jax.ShapeDtypeStruct((m, hk), BF16), jax.ShapeDtypeStruct((m, hv), F32),
            jax.ShapeDtypeStruct((m, DN_HEADS * CHUNK), BF16), jax.ShapeDtypeStruct((m, AB_LANES), F32),
            jax.ShapeDtypeStruct((m // TM, TAIL_ROWS, DN_QKV), F32),
        ],
        scratch_shapes=[pltpu.VMEM((3, HIST_QKV, TN_IN), F32), pltpu.VMEM((TM, DN_QKV), F32)],
        compiler_params=_params("arbitrary"),
        name="proj_qkv",
    )(h, wt, wt, wt, wabt, cqw, alog, dtb)


def _delta_scan_kernel(w_ref, qd_ref, kd_ref, u_ref, qk_ref, egl_ref, zs_ref, dnn_ref,
                       y_ref, so_ref, s_ref):
    c = pl.program_id(0)

    @pl.when(c == 0)
    def _():
        s_ref[...] = jnp.zeros_like(s_ref)

    n_seq = w_ref.shape[0]
    units = [(b, h) for b in range(n_seq) for h in range(DN_HEADS)]
    col = [slice(h * DN_DK, (h + 1) * DN_DK) for h in range(DN_HEADS)]
    dnn = dnn_ref[...]
    s = [s_ref[b, h] for b, h in units]
    for g in range(w_ref.shape[1] // CHUNK):
        rows = slice(g * CHUNK, (g + 1) * CHUNK)
        lhs = [jnp.concatenate([w_ref[b, rows, col[h]], qd_ref[b, rows, col[h]]], axis=0) for b, h in units]
        r = [jnp.dot(lhs[n], s[n].astype(BF16), preferred_element_type=F32) for n in range(len(units))]
        v_new = [(u_ref[b, rows, col[h]] - r[n][:CHUNK]).astype(BF16) for n, (b, h) in enumerate(units)]
        upd = [lax.dot_general(kd_ref[b, rows, col[h]], v_new[n], (((0,), (0,)), ((), ())),
                               preferred_element_type=F32) for n, (b, h) in enumerate(units)]
        s = [s[n] * egl_ref[b, g * CHUNK:g * CHUNK + 1, h:h + 1] + upd[n] for n, (b, h) in enumerate(units)]
        o = [r[n][CHUNK:] + jnp.dot(qk_ref[b, rows, h * CHUNK:(h + 1) * CHUNK], v_new[n],
                                    preferred_element_type=F32) for n, (b, h) in enumerate(units)]
        for n, (b, h) in enumerate(units):
            y_ref[b, rows, col[h]] = (_rms(o[n], dnn) * zs_ref[b, rows, col[h]]).astype(y_ref.dtype)
    for n, (b, h) in enumerate(units):
        s_ref[b, h] = s[n]

    @pl.when(c == pl.num_programs(0) - 1)
    def _():
        so_ref[...] = s_ref[...]


def _delta_scan(w, qd, kd, u, qk, egl, zs, dnn, n_seq, seq_len):
    hk, hv = DN_HEADS * DN_DK, DN_HEADS * DN_DV
    per_seq = lambda a: a.reshape(n_seq, seq_len, a.shape[-1])
    rows = lambda width: pl.BlockSpec((n_seq, SCAN_ROWS, width), lambda c: (0, c, 0))
    state = pl.BlockSpec((n_seq, DN_HEADS, DN_DK, DN_DV), lambda c: (0, 0, 0, 0))
    y, s = pl.pallas_call(
        _delta_scan_kernel,
        grid=(seq_len // SCAN_ROWS,),
        in_specs=[rows(hk), rows(hk), rows(hk), rows(hv), rows(DN_HEADS * CHUNK), rows(AB_LANES), rows(hv),
                  pl.BlockSpec((1, DN_DV), lambda c: (0, 0))],
        out_specs=[rows(hv), state],
        out_shape=[jax.ShapeDtypeStruct((n_seq, seq_len, hv), BF16),
                   jax.ShapeDtypeStruct((n_seq, DN_HEADS, DN_DK, DN_DV), F32)],
        scratch_shapes=[pltpu.VMEM((n_seq, DN_HEADS, DN_DK, DN_DV), F32)],
        compiler_params=_params("arbitrary"),
        name="delta_scan",
    )(*map(per_seq, (w, qd, kd, u, qk, egl, zs)), dnn)
    return y.reshape(n_seq * seq_len, hv), s


def _mix_sample_kernel(pa_ref, pq_ref, pz_ref, pab_ref, sa_ref, sq_ref, si_ref,
                       caw_ref, cqw_ref, alog_ref, dtb_ref, dnn_ref,
                       y_ref, ca_ref, cq_ref, so_ref,
                       gb_ref, ua_ref, xq_ref, z_ref, ab_ref, ybuf_ref):
    seqs = range(SEQ_PER_STEP)
    pad = SEQ_PAD - SAMPLE_LEN
    for b in seqs:
        r0 = SEQ_PAD * b
        nxt = (b + 1) % SEQ_PER_STEP
        src = slice(SAMPLE_LEN * b, SAMPLE_LEN * (b + 1))
        pa = pa_ref[src, :]
        gb_ref[r0:r0 + SAMPLE_LEN, :] = pa[:, :D_CONV]
        ua_ref[r0:r0 + SAMPLE_LEN, :] = pa[:, D_CONV:2 * D_CONV] * pa[:, 2 * D_CONV:]
        xq_ref[r0:r0 + SAMPLE_LEN, :] = pq_ref[src, :]
        z_ref[r0:r0 + SAMPLE_LEN, :] = pz_ref[src, :]
        ab_ref[r0:r0 + SAMPLE_LEN, :] = pab_ref[src, :]
        for ref in (gb_ref, ua_ref, xq_ref, z_ref, ab_ref):
            ref[r0 + SAMPLE_LEN:r0 + SEQ_PAD, :] = jnp.zeros((pad, ref.shape[1]), F32)
        ua_ref[r0 + SEQ_PAD - HIST_A:r0 + SEQ_PAD, :] = sa_ref[nxt]
        for r in range(HIST_QKV):
            dst = r0 + SEQ_PAD - HIST_QKV + r
            xq_ref[dst:dst + 1, :] = sq_ref[r, nxt:nxt + 1, :]
    ybuf_ref[:, :D_CONV] = gb_ref[...] * _causal_conv(ua_ref[...], caw_ref[...], None)[0]
    row, col, causal, strict, low_f, eye = _masks(True)
    valid = (row[:, 0:1] & (SEQ_PAD - 1)) < SAMPLE_LEN
    cqkv = jnp.where(valid, _silu(_causal_conv(xq_ref[...], cqw_ref[...], None)[0]), 0.0)
    for b in seqs:
        r0 = SEQ_PAD * b
        ca_ref[b] = ua_ref[r0 + SAMPLE_LEN - HIST_A:r0 + SAMPLE_LEN, :]
        for r in range(HIST_QKV):
            src = r0 + SAMPLE_LEN - HIST_QKV + r
            cq_ref[r, b:b + 1, :] = xq_ref[src:src + 1, :]

    g_all, beta_all = _gates(ab_ref[...], alog_ref[...], dtb_ref[...])
    g_all = jnp.where(valid, g_all, 0.0)
    beta_all = jnp.where(valid, beta_all, 0.0)
    gc_all = _mm_exact(low_f, g_all)
    last_sel = (col == (row | (SEQ_PAD - 1))).astype(F32)
    gl_all = _mm_exact(last_sel, gc_all)
    z = z_ref[...]
    dnn = dnn_ref[...]
    heads = range(DN_HEADS)
    qn = [_l2_normalize(cqkv[:, h * DN_DK:(h + 1) * DN_DK], DN_DK ** -0.5) for h in heads]
    kn = [_l2_normalize(cqkv[:, (DN_HEADS + h) * DN_DK:(DN_HEADS + h + 1) * DN_DK], 1.0) for h in heads]
    vh = [cqkv[:, (2 * DN_HEADS + h) * DN_DV:(2 * DN_HEADS + h + 1) * DN_DV] for h in heads]
    beta_col, gc_col, gc_row = _gate_columns(beta_all, gc_all)
    w, u, qdec, qk = _heads_intra(qn, kn, vh, beta_col, gc_col, gc_row, causal, strict, eye, 1)
    gl_col = [gl_all[:, h:h + 1] for h in heads]
    kdec = [kn[h] * jnp.exp(gl_col[h] - gc_col[h]) for h in heads]
    s_scale = [jnp.exp(x) for x in gl_col]
    rows = [slice(SEQ_PAD * b, SEQ_PAD * (b + 1)) for b in seqs]
    pairs = [(h, b) for h in heads for b in seqs]
    s_old = [si_ref[b, h] for h, b in pairs]
    r = [_mm(jnp.concatenate([w[h][rows[b]], qdec[h][rows[b]]], axis=0), s_old[n])
         for n, (h, b) in enumerate(pairs)]
    v_new = [u[h][rows[b]] - r[n][:SEQ_PAD] for n, (h, b) in enumerate(pairs)]
    upd = [_mm_tn(kdec[h][rows[b]], v_new[n]) for n, (h, b) in enumerate(pairs)]
    for n, (h, b) in enumerate(pairs):
        so_ref[b, h] = s_old[n] * s_scale[h][SEQ_PAD * b:SEQ_PAD * b + 1, :] + upd[n]
    for h in heads:
        mine = range(h * SEQ_PER_STEP, (h + 1) * SEQ_PER_STEP)
        o = (jnp.concatenate([r[n][SEQ_PAD:] for n in mine], axis=0)
             + _mm(qk[h], jnp.concatenate([v_new[n] for n in mine], axis=0)))
        lo = h * DN_DV
        ybuf_ref[:, D_CONV + lo:D_CONV + lo + DN_DV] = _rms(o, dnn) * _silu(z[:, lo:lo + DN_DV])
    for b in seqs:
        y_ref[SAMPLE_LEN * b:SAMPLE_LEN * (b + 1), :] = ybuf_ref[SEQ_PAD * b:SEQ_PAD * b + SAMPLE_LEN, :]


def _mix_sample(proj, ab, state_a, state_q, state_s, caw, cqw, alog, dtb, dnn):
    n_seq = state_s.shape[0]
    full2 = lambda i: (0, 0)
    seq_rows = lambda width, col: pl.BlockSpec((SEQ_PER_STEP * SAMPLE_LEN, width), lambda i: (i, col))
    return pl.pallas_call(
        _mix_sample_kernel,
        grid=(n_seq // SEQ_PER_STEP,),
        in_specs=[
            seq_rows(3 * D_CONV, 0), seq_rows(DN_QKV, 1), seq_rows(DN_HEADS * DN_DV, Z_BLOCK), seq_rows(AB_LANES, 0),
            pl.BlockSpec((SEQ_PER_STEP, 2, D_CONV), lambda i: (i, 0, 0)),
            pl.BlockSpec((3, SEQ_PER_STEP, DN_QKV), lambda i: (0, i, 0)),
            pl.BlockSpec((SEQ_PER_STEP, DN_HEADS, DN_DK, DN_DV), lambda i: (i, 0, 0, 0)),
            pl.BlockSpec((3, D_CONV), full2),
            pl.BlockSpec((4, DN_QKV), full2),
            pl.BlockSpec((1, AB_LANES), full2),
            pl.BlockSpec((1, AB_LANES), full2),
            pl.BlockSpec((1, DN_DV), full2),
        ],
        out_specs=[
            seq_rows(D_MODEL, 0),
            pl.BlockSpec((SEQ_PER_STEP, 2, D_CONV), lambda i: (i, 0, 0)),
            pl.BlockSpec((3, SEQ_PER_STEP, DN_QKV), lambda i: (0, i, 0)),
            pl.BlockSpec((SEQ_PER_STEP, DN_HEADS, DN_DK, DN_DV), lambda i: (i, 0, 0, 0)),
        ],
        out_shape=[
            jax.ShapeDtypeStruct((n_seq * SAMPLE_LEN, D_MODEL), F32),
            jax.ShapeDtypeStruct((n_seq, 2, D_CONV), F32),
            jax.ShapeDtypeStruct((3, n_seq, DN_QKV), F32),
            jax.ShapeDtypeStruct((n_seq, DN_HEADS, DN_DK, DN_DV), F32),
        ],
        scratch_shapes=[
            pltpu.VMEM((CHUNK, D_CONV), F32),
            pltpu.VMEM((CHUNK, D_CONV), F32),
            pltpu.VMEM((CHUNK, DN_QKV), F32),
            pltpu.VMEM((CHUNK, DN_HEADS * DN_DV), F32),
            pltpu.VMEM((CHUNK, AB_LANES), F32),
            pltpu.VMEM((CHUNK, D_MODEL), F32),
        ],
        compiler_params=_params("parallel"),
        name="mix_sample",
    )(proj, proj, proj, ab, state_a, state_q, state_s, caw, cqw, alog, dtb, dnn)


def _outproj_kernel(ya_ref, yb_ref, wa_ref, wb_ref, post_ref, x_ref, o_ref):
    m = (jnp.dot(ya_ref[...].astype(BF16), wa_ref[...], preferred_element_type=F32)
         + jnp.dot(yb_ref[...].astype(BF16), wb_ref[...], preferred_element_type=F32))
    o_ref[...] = x_ref[...] + _rms(m, post_ref[...])


def _outproj(ya, yb, yb_block, w, post, x):
    m = x.shape[0]
    half = D_MODEL // 2
    return pl.pallas_call(
        _outproj_kernel,
        grid=(m // TM,),
        in_specs=[
            pl.BlockSpec((TM, half), lambda i: (i, 0)),
            pl.BlockSpec((TM, half), lambda i: (i, yb_block)),
            pl.BlockSpec((half, D_MODEL), lambda i: (0, 0)),
            pl.BlockSpec((half, D_MODEL), lambda i: (1, 0)),
            pl.BlockSpec((1, D_MODEL), lambda i: (0, 0)),
            pl.BlockSpec((TM, D_MODEL), lambda i: (i, 0)),
        ],
        out_specs=pl.BlockSpec((TM, D_MODEL), lambda i: (i, 0)),
        out_shape=jax.ShapeDtypeStruct((m, D_MODEL), F32),
        compiler_params=_params("parallel"),
        name="outproj",
    )(ya, yb, w, w, post, x)


def _ple_kernel(x_ref, p_ref, pre_ref, post_ref, wg_ref, wp_ref, o_ref):
    x = x_ref[...]
    h = _rms(x, pre_ref[...]).astype(BF16)
    gate = jax.nn.sigmoid(jnp.dot(h, wg_ref[...], preferred_element_type=F32))
    proj = jnp.dot(p_ref[...].astype(BF16), wp_ref[...], preferred_element_type=F32)
    o_ref[...] = x + _rms(gate * proj, post_ref[...])


def _ple(x, p, pre, post, wg, wp):
    m = x.shape[0]
    return pl.pallas_call(
        _ple_kernel,
        grid=(m // TM,),
        in_specs=[
            pl.BlockSpec((TM, D_MODEL), lambda i: (i, 0)),
            pl.BlockSpec((TM, D_PLE), lambda i: (i, 0)),
            pl.BlockSpec((1, D_MODEL), lambda i: (0, 0)),
            pl.BlockSpec((1, D_MODEL), lambda i: (0, 0)),
            pl.BlockSpec((D_MODEL, D_MODEL), lambda i: (0, 0)),
            pl.BlockSpec((D_PLE, D_MODEL), lambda i: (0, 0)),
        ],
        out_specs=pl.BlockSpec((TM, D_MODEL), lambda i: (i, 0)),
        out_shape=jax.ShapeDtypeStruct((m, D_MODEL), F32),
        compiler_params=_params("parallel"),
        name="ple",
    )(x, p, pre, post, wg, wp)


def _row(v):
    return v.reshape(1, -1).astype(F32)


def _pad_lanes(v):
    return jnp.pad(_row(v), ((0, 0), (0, AB_LANES - v.shape[-1])))


def _seq_tails(tails, tiles_per_seq, n_rows):
    return tails[tiles_per_seq - 1::tiles_per_seq, TAIL_ROWS - n_rows:, :]


def kernel(x_prompt, x_sample, state_conv_a, state_conv_qkv, state_delta, p_prompt, p_sample,
           f1_pre, f1_post, f1_wg, f1_wu, f1_wd,
           mix_pre, mix_post, w_in, conv_a_w, conv_qkv_w, a_log, dt_bias, dn_norm, w_out,
           f2_pre, f2_post, f2_wg, f2_wu, f2_wd,
           ple_pre, ple_post, w_ple_gate, w_ple_proj):
    depth = f1_pre.shape[0]
    n_p, seq_p, _ = x_prompt.shape
    n_s, seq_s, _ = x_sample.shape
    rows_p = n_p * seq_p
    rows_s = n_s * seq_s
    assert seq_p % TM_BIG == 0 and seq_s == SAMPLE_LEN and n_s % SEQ_PER_STEP == 0
    assert rows_s == TM
    assert w_in.shape[-1] == IN_MAIN + N_AB
    tiles_per_seq = seq_p // TM_BIG
    xp = x_prompt.reshape(rows_p, D_MODEL)
    xs = x_sample.reshape(rows_s, D_MODEL)
    outs = [[] for _ in range(6)]
    for i in range(depth):
        pre, post = _row(f1_pre[i]), _row(f1_post[i])
        xs, wg, wu, wd = _ffn(xs, pre, post, f1_wg[i], f1_wu[i], f1_wd[i], TM)
        xp = _ffn(xp, pre, post, wg, wu, wd, TM_BIG)[0]

        pre = _row(mix_pre[i])
        caw, cqw = conv_a_w[i], conv_qkv_w[i]
        proj_s, ab_s, wt, wabt = _inproj_cast(xs, pre, w_in[i].T)
        y_a, zs, h, tail_a = _proj_a(xp, pre, wt, caw, TM_BIG, tiles_per_seq)
        alog, dtb, dnn = _pad_lanes(a_log[i]), _pad_lanes(dt_bias[i]), _row(dn_norm[i])
        wy_w, qdec, kdec, wy_u, qk, egl, tail_q = _proj_qkv(h, wt, wabt, cqw, alog, dtb, seq_p // TM)
        y_b, s_p = _delta_scan(wy_w, qdec, kdec, wy_u, qk, egl, zs, dnn, n_p, seq_p)
        ca_p = _seq_tails(tail_a, tiles_per_seq, HIST_A)
        cq_p = _seq_tails(tail_q, seq_p // TM, HIST_QKV)
        y_s, ca_s, cq_s, s_s = _mix_sample(
            proj_s, ab_s, state_conv_a[i], jnp.swapaxes(state_conv_qkv[i], 0, 1), state_delta[i],
            caw, cqw, alog, dtb, dnn)
        cq_s = jnp.swapaxes(cq_s, 0, 1)

        post = _row(mix_post[i])
        w_out_bf16 = w_out[i].astype(BF16)
        xs = _outproj(y_s, y_s, 1, w_out_bf16, post, xs)
        xp = _outproj(y_a, y_b, 0, w_out_bf16, post, xp)

        pre, post = _row(f2_pre[i]), _row(f2_post[i])
        xs, wg, wu, wd = _ffn(xs, pre, post, f2_wg[i], f2_wu[i], f2_wd[i], TM)
        xp = _ffn(xp, pre, post, wg, wu, wd, TM_BIG)[0]

        pre, post = _row(ple_pre[i]), _row(ple_post[i])
        wg, wp = w_ple_gate[i].astype(BF16), w_ple_proj[i].astype(BF16)
        xs = _ple(xs, p_sample[i].reshape(rows_s, D_PLE), pre, post, wg, wp)
        xp = _ple(xp, p_prompt[i].reshape(rows_p, D_PLE), pre, post, wg, wp)
        for lst, val in zip(outs, (ca_p, cq_p, s_p, ca_s, cq_s, s_s)):
            lst.append(val)
    return (xp.reshape(n_p, seq_p, D_MODEL), xs.reshape(n_s, seq_s, D_MODEL),
            *[jnp.stack(lst) for lst in outs])
```

```python
import functools

import jax
import jax.numpy as jnp
from jax import lax
from jax.experimental import pallas as pl
from jax.experimental.pallas import tpu as pltpu

F32 = jnp.float32
BF16 = jnp.bfloat16

D_MODEL = 2048
D_CONV = 1024
DN_HEADS = 8
DN_DK = 128
DN_DV = 128
DN_QKV = 3072
D_FF = 5632
D_PLE = 256
EPS = 1e-6
CHUNK = 64
SAMPLE_LEN = 4
SEQ_PAD = 8
SEQ_SHIFT = 3
SEQ_PER_STEP = CHUNK // SEQ_PAD
IN_MAIN = 7168
QKV_BLOCK = 3
Z_BLOCK = 6
HIST_A = 2
HIST_QKV = 3
TAIL_ROWS = 8
N_AB = 16
AB_LANES = 128

TM = 512
TF = 512
TF_CAST = 256
RING = 3
TN_IN = 1024
TM_BIG = 1024
TN_A = 512
SUB_COLS = 512
SCAN_ROWS = 256
ROW_CHUNK = 32
VMEM_LIMIT = 56 * 1024 * 1024


def _rms(x, g):
    ms = jnp.mean(x * x, axis=-1, keepdims=True)
    return x * lax.rsqrt(ms + EPS) * g


def _silu(x):
    return x * jax.nn.sigmoid(x)


def _mm(a, b):
    return jnp.dot(a.astype(BF16), b.astype(BF16), preferred_element_type=F32)


def _mm_nt(a, b):
    return lax.dot_general(a.astype(BF16), b.astype(BF16), (((1,), (1,)), ((), ())),
                           preferred_element_type=F32)


def _mm_tn(a, b):
    return jnp.dot(a.T.astype(BF16), b.astype(BF16), preferred_element_type=F32)


def _mm_exact(a, b):
    return jnp.dot(a, b, precision=lax.Precision.HIGHEST, preferred_element_type=F32)


def _nt_dot(a, bt):
    return lax.dot_general(a, bt, (((1,), (1,)), ((), ())), preferred_element_type=F32)


def _for_row_chunks(n_rows, body):
    def step(k, carry):
        body(pl.ds(pl.multiple_of(k * ROW_CHUNK, ROW_CHUNK), ROW_CHUNK))
        return carry
    lax.fori_loop(0, n_rows // ROW_CHUNK, step, 0, unroll=4)


def _params(*semantics):
    return pltpu.CompilerParams(dimension_semantics=semantics, vmem_limit_bytes=VMEM_LIMIT)


def _ffn_kernel(emit_bf16, x_ref, pre_ref, post_ref, wg_ref, wu_ref, wd_ref, o_ref, *rest):
    j = pl.program_id(1)
    if emit_bf16:
        wg16_ref, wu16_ref, wd16_ref, h_ref, wg_buf, wu_buf, wd_buf, sem = rest
        n_tiles = pl.num_programs(1)
        tf = wg_buf.shape[2]

        def tile_copies(t):
            start = t * tf if isinstance(t, int) else pl.multiple_of(t * tf, tf)
            slot = t % RING
            return (pltpu.make_async_copy(wg_ref.at[:, pl.ds(start, tf)], wg_buf.at[slot], sem.at[0, slot]),
                    pltpu.make_async_copy(wu_ref.at[:, pl.ds(start, tf)], wu_buf.at[slot], sem.at[1, slot]),
                    pltpu.make_async_copy(wd_ref.at[pl.ds(start, tf), :], wd_buf.at[slot], sem.at[2, slot]))

        @pl.when(j == 0)
        def _():
            for t in range(RING - 1):
                for copy in tile_copies(t):
                    copy.start()

        @pl.when(j + (RING - 1) < n_tiles)
        def _():
            for copy in tile_copies(j + (RING - 1)):
                copy.start()
    else:
        h_ref = rest[-1]

    @pl.when(j == 0)
    def _():
        def norm_rows(rows):
            h_ref[rows, :] = _rms(x_ref[rows, :], pre_ref[...]).astype(BF16)
            o_ref[rows, :] = jnp.zeros((ROW_CHUNK, D_MODEL), F32)
        _for_row_chunks(x_ref.shape[0], norm_rows)

    if emit_bf16:
        for copy in tile_copies(j):
            copy.wait()
        slot = j % RING
        wg, wu, wd = wg_buf[slot].astype(BF16), wu_buf[slot].astype(BF16), wd_buf[slot].astype(BF16)
        wg16_ref[...] = wg
        wu16_ref[...] = wu
        wd16_ref[...] = wd
    else:
        wg, wu, wd = wg_ref[...], wu_ref[...], wd_ref[...]
    h = h_ref[...]
    g = jnp.dot(h, wg, preferred_element_type=F32)
    u = jnp.dot(h, wu, preferred_element_type=F32)
    a = (_silu(g) * u).astype(BF16)
    o_ref[...] += jnp.dot(a, wd, preferred_element_type=F32)

    @pl.when(j == pl.num_programs(1) - 1)
    def _():
        group = 4 * ROW_CHUNK

        def finish_rows(k, carry):
            r0 = pl.multiple_of(k * group, group)
            chunks = [pl.ds(r0 + s * ROW_CHUNK, ROW_CHUNK) for s in range(group // ROW_CHUNK)]
            scale = [0.5 * lax.rsqrt(jnp.mean(jnp.square(o_ref[c, :]), axis=-1, keepdims=True) + EPS)
                     for c in chunks]
            for c, sc in zip(chunks, scale):
                o_ref[c, :] = x_ref[c, :] + o_ref[c, :] * sc * post_ref[...]
            return carry
        lax.fori_loop(0, x_ref.shape[0] // group, finish_rows, 0)


def _ffn(x, pre, post, wg, wu, wd, tm):
    m = x.shape[0]
    emit_bf16 = wg.dtype != BF16
    tf = TF_CAST if emit_bf16 else TF
    up_spec = pl.BlockSpec((D_MODEL, tf), lambda i, j: (0, j))
    down_spec = pl.BlockSpec((tf, D_MODEL), lambda i, j: (j, 0))
    out_specs = [pl.BlockSpec((tm, D_MODEL), lambda i, j: (i, 0))]
    out_shape = [jax.ShapeDtypeStruct((m, D_MODEL), F32)]
    w_specs = [up_spec, up_spec, down_spec]
    scratch = [pltpu.VMEM((tm, D_MODEL), BF16)]
    if emit_bf16:
        assert m == tm, "weights are re-emitted once, by a single-row-tile call"
        assert D_FF // tf >= RING - 1
        out_specs += w_specs
        out_shape += [jax.ShapeDtypeStruct(w.shape, BF16) for w in (wg, wu, wd)]
        w_specs = [pl.BlockSpec(memory_space=pl.ANY)] * 3
        scratch += [pltpu.VMEM((RING, D_MODEL, tf), F32), pltpu.VMEM((RING, D_MODEL, tf), F32),
                    pltpu.VMEM((RING, tf, D_MODEL), F32), pltpu.SemaphoreType.DMA((3, RING))]
    outs = pl.pallas_call(
        functools.partial(_ffn_kernel, emit_bf16),
        grid=(m // tm, D_FF // tf),
        in_specs=[
            pl.BlockSpec((tm, D_MODEL), lambda i, j: (i, 0)),
            pl.BlockSpec((1, D_MODEL), lambda i, j: (0, 0)),
            pl.BlockSpec((1, D_MODEL), lambda i, j: (0, 0)),
            *w_specs,
        ],
        out_specs=out_specs,
        out_shape=out_shape,
        scratch_shapes=scratch,
        compiler_params=_params("arbitrary", "arbitrary"),
        name="ffn_cast" if emit_bf16 else "ffn",
    )(x, pre, post, wg, wu, wd)
    return outs if emit_bf16 else (outs[0], wg, wu, wd)


def _inproj_cast_kernel(x_ref, pre_ref, wt_ref, wabt_ref, o_ref, ab_ref, wt16_ref, wabt16_ref, h_ref):
    @pl.when(pl.program_id(1) == 0)
    def _():
        h = _rms(x_ref[...], pre_ref[...]).astype(BF16)
        h_ref[...] = h
        wabt = jnp.concatenate(
            [wabt_ref[...].astype(BF16), jnp.zeros((AB_LANES - N_AB, D_MODEL), BF16)], axis=0)
        wabt16_ref[...] = wabt
        ab_ref[...] = _nt_dot(h, wabt)

    wt = wt_ref[...].astype(BF16)
    wt16_ref[...] = wt
    o_ref[...] = _nt_dot(h_ref[...], wt)


def _inproj_cast(x, pre, wt):
    m = x.shape[0]
    assert m == TM, "weights are re-emitted once, by a single-row-tile call"
    wt_spec = pl.BlockSpec((TN_IN, D_MODEL), lambda i, j: (j, 0))
    return pl.pallas_call(
        _inproj_cast_kernel,
        grid=(1, IN_MAIN // TN_IN),
        in_specs=[
            pl.BlockSpec((TM, D_MODEL), lambda i, j: (0, 0)),
            pl.BlockSpec((1, D_MODEL), lambda i, j: (0, 0)),
            wt_spec,
            pl.BlockSpec((N_AB, D_MODEL), lambda i, j: (IN_MAIN // N_AB, 0)),
        ],
        out_specs=[
            pl.BlockSpec((TM, TN_IN), lambda i, j: (0, j)),
            pl.BlockSpec((TM, AB_LANES), lambda i, j: (0, 0)),
            wt_spec,
            pl.BlockSpec((AB_LANES, D_MODEL), lambda i, j: (0, 0)),
        ],
        out_shape=[
            jax.ShapeDtypeStruct((m, IN_MAIN), F32),
            jax.ShapeDtypeStruct((m, AB_LANES), F32),
            jax.ShapeDtypeStruct((IN_MAIN, D_MODEL), BF16),
            jax.ShapeDtypeStruct((AB_LANES, D_MODEL), BF16),
        ],
        scratch_shapes=[pltpu.VMEM((TM, D_MODEL), BF16)],
        compiler_params=_params("parallel", "arbitrary"),
        name="inproj_cast",
    )(x, pre, wt, wt)


def _causal_conv(x, w, carry_in):
    width = w.shape[0]
    rows = x.shape[0]
    is_row0 = lax.broadcasted_iota(jnp.int32, (8, 1), 0) == 0
    acc = x * w[0:1, :]
    carry_out = []
    for j in range(1, width):
        delayed = pltpu.roll(acc, 1, axis=0)
        if carry_in is not None:
            carry_out.append(acc[rows - 1:rows, :])
            head = jnp.where(is_row0, carry_in[j - 1], delayed[:8])
            delayed = jnp.concatenate([head, delayed[8:]], axis=0)
        acc = x * w[j:j + 1, :] + delayed
    return acc, carry_out


def _load_carry(carry_ref, j, cols, n, seq_start):
    return [jnp.where(seq_start, 0.0, carry_ref[j, r:r + 1, cols]) for r in range(n)]


def _store_carry(carry_ref, j, cols, rows):
    for r, row in enumerate(rows):
        carry_ref[j, r:r + 1, cols] = row


def _proj_a_kernel(tiles_per_seq, x_ref, pre_ref, wgb_ref, wgcv_ref, whc_ref, wz_ref, caw_ref,
                   ya_ref, zs_ref, h_ref, tail_ref, carry_ref):
    i, j = pl.program_id(0), pl.program_id(1)
    tm = x_ref.shape[0]

    @pl.when(j == 0)
    def _():
        def norm_rows(rows):
            h_ref[rows, :] = _rms(x_ref[rows, :], pre_ref[...]).astype(BF16)
        _for_row_chunks(tm, norm_rows)

    @pl.when((i == 0) & (j == 0))
    def _():
        carry_ref[...] = jnp.zeros_like(carry_ref)

    seq_start = i % tiles_per_seq == 0
    h = h_ref[...]
    for c in range(TN_A // SUB_COLS):
        cols = slice(c * SUB_COLS, (c + 1) * SUB_COLS)
        u = _nt_dot(h, wgcv_ref[cols, :]) * _nt_dot(h, whc_ref[cols, :])
        tail_ref[0, :, cols] = u[tm - TAIL_ROWS:tm, :]
        conv, carry = _causal_conv(u, caw_ref[:, cols], _load_carry(carry_ref, j, cols, HIST_A, seq_start))
        _store_carry(carry_ref, j, cols, carry)
        ya_ref[:, cols] = (_nt_dot(h, wgb_ref[cols, :]) * conv).astype(BF16)
        zs_ref[:, cols] = _silu(_nt_dot(h, wz_ref[cols, :])).astype(BF16)


def _proj_a(x, pre, wt, caw, tm, tiles_per_seq):
    m = x.shape[0]
    n_col = D_CONV // TN_A
    assert DN_HEADS * DN_DV == D_CONV
    z_part = Z_BLOCK * TN_IN // D_CONV
    w_spec = lambda part: pl.BlockSpec((TN_A, D_MODEL), lambda i, j: (part * n_col + j, 0))
    cols = pl.BlockSpec((tm, TN_A), lambda i, j: (i, j))
    return pl.pallas_call(
        functools.partial(_proj_a_kernel, tiles_per_seq),
        grid=(m // tm, n_col),
        in_specs=[
            pl.BlockSpec((tm, D_MODEL), lambda i, j: (i, 0)),
            pl.BlockSpec((1, D_MODEL), lambda i, j: (0, 0)),
            w_spec(0), w_spec(1), w_spec(2), w_spec(z_part),
            pl.BlockSpec((3, TN_A), lambda i, j: (0, j)),
        ],
        out_specs=[
            cols, cols,
            pl.BlockSpec((tm, D_MODEL), lambda i, j: (i, 0)),
            pl.BlockSpec((1, TAIL_ROWS, TN_A), lambda i, j: (i, 0, j)),
        ],
        out_shape=[
            jax.ShapeDtypeStruct((m, D_CONV), BF16),
            jax.ShapeDtypeStruct((m, DN_HEADS * DN_DV), BF16),
            jax.ShapeDtypeStruct((m, D_MODEL), BF16),
            jax.ShapeDtypeStruct((m // tm, TAIL_ROWS, D_CONV), F32),
        ],
        scratch_shapes=[pltpu.VMEM((n_col, HIST_A, TN_A), F32)],
        compiler_params=_params("arbitrary", "arbitrary"),
        name="proj_a",
    )(x, pre, wt, wt, wt, wt, caw)


def _l2_normalize(x, scale):
    return x * (lax.rsqrt(jnp.sum(x * x, axis=-1, keepdims=True) + EPS) * scale)


def _gate_columns(beta_all, gc_all):
    heads = range(DN_HEADS)
    gc_t = gc_all.T
    return ([beta_all[:, DN_HEADS + h:DN_HEADS + h + 1] for h in heads],
            [gc_all[:, h:h + 1] for h in heads], [gc_t[h:h + 1, :] for h in heads])


def _heads_intra(qn, kn, vh, beta_col, gc_col, gc_row, causal, strict, eye, squarings):
    heads = range(len(qn))
    decay = [jnp.exp(jnp.where(causal, gc_col[h] - gc_row[h], -jnp.inf)) for h in heads]
    kb = [kn[h] * beta_col[h] for h in heads]
    gram = [_mm_nt(jnp.concatenate([kb[h], qn[h]], axis=0), kn[h]) for h in heads]
    a_mat = [jnp.where(strict, gram[h][:CHUNK] * decay[h], 0.0) for h in heads]
    qk = [jnp.where(causal, gram[h][CHUNK:] * decay[h], 0.0) for h in heads]
    inv = [eye - a for a in a_mat]
    power = a_mat
    for _ in range(squarings):
        power = [_mm(x, x) for x in power]
        inv = [_mm(inv[h], eye + power[h]) for h in heads]
    egc = [jnp.exp(x) for x in gc_col]
    wu = [_mm(inv[h], jnp.concatenate([kb[h] * egc[h], vh[h] * beta_col[h]], axis=1)) for h in heads]
    w = [x[:, :DN_DK] for x in wu]
    u = [x[:, DN_DK:] for x in wu]
    qdec = [qn[h] * egc[h] for h in heads]
    return w, u, qdec, qk


def _gates(ab, alog, dtb):
    x = ab + dtb
    softplus = jnp.maximum(x, 0.0) + jnp.log1p(jnp.exp(-jnp.abs(x)))
    return -jnp.exp(alog) * softplus, jax.nn.sigmoid(ab)


def _masks(same_seq):
    row = lax.broadcasted_iota(jnp.int32, (CHUNK, CHUNK), 0)
    col = lax.broadcasted_iota(jnp.int32, (CHUNK, CHUNK), 1)
    causal = row >= col
    strict = row > col
    if same_seq:
        same = (row >> SEQ_SHIFT) == (col >> SEQ_SHIFT)
        causal = causal & same
        strict = strict & same
    eye = jnp.where(row == col, 1.0, 0.0).astype(F32)
    return row, col, causal, strict, causal.astype(F32), eye


def _proj_qkv_kernel(tiles_per_seq, h_ref, wq_ref, wk_ref, wv_ref, wabt_ref, cqw_ref, alog_ref, dtb_ref,
                     w_ref, qd_ref, kd_ref, u_ref, qk_ref, egl_ref, tail_ref, carry_ref, qkv_ref):
    i = pl.program_id(0)
    tm = h_ref.shape[0]

    @pl.when(i == 0)
    def _():
        carry_ref[...] = jnp.zeros_like(carry_ref)

    seq_start = i % tiles_per_seq == 0
    h = h_ref[...]
    g_all, beta_all = _gates(_nt_dot(h, wabt_ref[...]), alog_ref[...], dtb_ref[...])
    for t, wt_ref in enumerate((wq_ref, wk_ref, wv_ref)):
        for c in range(TN_IN // SUB_COLS):
            cols = slice(c * SUB_COLS, (c + 1) * SUB_COLS)
            lo = t * TN_IN + c * SUB_COLS
            p = _nt_dot(h, wt_ref[cols, :])
            tail_ref[0, :, lo:lo + SUB_COLS] = p[tm - TAIL_ROWS:tm, :]
            p, carry = _causal_conv(p, cqw_ref[:, lo:lo + SUB_COLS],
                                    _load_carry(carry_ref, t, cols, HIST_QKV, seq_start))
            _store_carry(carry_ref, t, cols, carry)
            p = _silu(p)
            if t == 2:
                qkv_ref[:, lo:lo + SUB_COLS] = p
            else:
                for k in range(SUB_COLS // DN_DK):
                    blk = p[:, k * DN_DK:(k + 1) * DN_DK]
                    qkv_ref[:, lo + k * DN_DK:lo + (k + 1) * DN_DK] = _l2_normalize(
                        blk, DN_DK ** -0.5 if t == 0 else 1.0)

    heads = range(DN_HEADS)
    _, _, causal, strict, low_f, eye = _masks(False)
    chunk_rows = [slice(g * CHUNK, (g + 1) * CHUNK) for g in range(tm // CHUNK)]
    qn, kn, vh, beta_col, gc_col, gc_row = [], [], [], [], [], []
    for rows in chunk_rows:
        gc_all = _mm_exact(low_f, g_all[rows])
        egl_ref[rows, :] = jnp.broadcast_to(jnp.exp(gc_all[CHUNK - 1:CHUNK, :]), (CHUNK, AB_LANES))
        cols = _gate_columns(beta_all[rows], gc_all)
        beta_col += cols[0]
        gc_col += cols[1]
        gc_row += cols[2]
        qn += [qkv_ref[rows, h * DN_DK:(h + 1) * DN_DK] for h in heads]
        kn += [qkv_ref[rows, (DN_HEADS + h) * DN_DK:(DN_HEADS + h + 1) * DN_DK] for h in heads]
        vh += [qkv_ref[rows, (2 * DN_HEADS + h) * DN_DV:(2 * DN_HEADS + h + 1) * DN_DV] for h in heads]
    w, u, qdec, qk = _heads_intra(qn, kn, vh, beta_col, gc_col, gc_row, causal, strict, eye, 5)
    for n in range(len(w)):
        rows, hd = chunk_rows[n // DN_HEADS], n % DN_HEADS
        lo = hd * DN_DK
        g_last = gc_col[n][CHUNK - 1:CHUNK, :]
        w_ref[rows, lo:lo + DN_DK] = w[n].astype(BF16)
        qd_ref[rows, lo:lo + DN_DK] = qdec[n].astype(BF16)
        kd_ref[rows, lo:lo + DN_DK] = (kn[n] * jnp.exp(g_last - gc_col[n])).astype(BF16)
        u_ref[rows, lo:lo + DN_DV] = u[n]
        qk_ref[rows, hd * CHUNK:(hd + 1) * CHUNK] = qk[n].astype(BF16)


def _proj_qkv(h, wt, wabt, cqw, alog, dtb, tiles_per_seq):
    m = h.shape[0]
    hk, hv = DN_HEADS * DN_DK, DN_HEADS * DN_DV
    rows = lambda width: pl.BlockSpec((TM, width), lambda i: (i, 0))
    whole = lambda shape: pl.BlockSpec(shape, lambda i: (0,) * len(shape))
    w_tile = lambda t: pl.BlockSpec((TN_IN, D_MODEL), lambda i: (QKV_BLOCK + t, 0))
    return pl.pallas_call(
        functools.partial(_proj_qkv_kernel, tiles_per_seq),
        grid=(m // TM,),
        in_specs=[rows(D_MODEL), w_tile(0), w_tile(1), w_tile(2), whole((AB_LANES, D_MODEL)),
                  whole((4, DN_QKV)), whole((1, AB_LANES)), whole((1, AB_LANES))],
        out_specs=[rows(hk), rows(hk), rows(hk), rows(hv), rows(DN_HEADS * CHUNK), rows(AB_LANES),
                   pl.BlockSpec((1, TAIL_ROWS, DN_QKV), lambda i: (i, 0, 0))],
        out_shape=[
            jax.ShapeDtypeStruct((m, hk), BF16), jax.ShapeDtypeStruct((m, hk), BF16),
            jax.ShapeDtypeStruct((m, hk), BF16), jax.ShapeDtypeStruct((m, hv), F32),
            jax.ShapeDtypeStruct((m, DN_HEADS * CHUNK), BF16), jax.ShapeDtypeStruct((m, AB_LANES), F32),
            jax.ShapeDtypeStruct((m // TM, TAIL_ROWS, DN_QKV), F32),
        ],
        scratch_shapes=[pltpu.VMEM((3, HIST_QKV, TN_IN), F32), pltpu.VMEM((TM, DN_QKV), F32)],
        compiler_params=_params("arbitrary"),
        name="proj_qkv",
    )(h, wt, wt, wt, wabt, cqw, alog, dtb)


def _delta_scan_kernel(w_ref, qd_ref, kd_ref, u_ref, qk_ref, egl_ref, zs_ref, dnn_ref,
                       y_ref, so_ref, s_ref):
    c = pl.program_id(0)

    @pl.when(c == 0)
    def _():
        s_ref[...] = jnp.zeros_like(s_ref)

    n_seq = w_ref.shape[0]
    units = [(b, h) for b in range(n_seq) for h in range(DN_HEADS)]
    col = [slice(h * DN_DK, (h + 1) * DN_DK) for h in range(DN_HEADS)]
    dnn = dnn_ref[...]
    s = [s_ref[b, h] for b, h in units]
    for g in range(w_ref.shape[1] // CHUNK):
        rows = slice(g * CHUNK, (g + 1) * CHUNK)
        lhs = [jnp.concatenate([w_ref[b, rows, col[h]], qd_ref[b, rows, col[h]]], axis=0) for b, h in units]
        r = [jnp.dot(lhs[n], s[n].astype(BF16), preferred_element_type=F32) for n in range(len(units))]
        v_new = [(u_ref[b, rows, col[h]] - r[n][:CHUNK]).astype(BF16) for n, (b, h) in enumerate(units)]
        upd = [lax.dot_general(kd_ref[b, rows, col[h]], v_new[n], (((0,), (0,)), ((), ())),
                               preferred_element_type=F32) for n, (b, h) in enumerate(units)]
        s = [s[n] * egl_ref[b, g * CHUNK:g * CHUNK + 1, h:h + 1] + upd[n] for n, (b, h) in enumerate(units)]
        o = [r[n][CHUNK:] + jnp.dot(qk_ref[b, rows, h * CHUNK:(h + 1) * CHUNK], v_new[n],
                                    preferred_element_type=F32) for n, (b, h) in enumerate(units)]
        for n, (b, h) in enumerate(units):
            y_ref[b, rows, col[h]] = (_rms(o[n], dnn) * zs_ref[b, rows, col[h]]).astype(y_ref.dtype)
    for n, (b, h) in enumerate(units):
        s_ref[b, h] = s[n]

    @pl.when(c == pl.num_programs(0) - 1)
    def _():
        so_ref[...] = s_ref[...]


def _delta_scan(w, qd, kd, u, qk, egl, zs, dnn, n_seq, seq_len):
    hk, hv = DN_HEADS * DN_DK, DN_HEADS * DN_DV
    per_seq = lambda a: a.reshape(n_seq, seq_len, a.shape[-1])
    rows = lambda width: pl.BlockSpec((n_seq, SCAN_ROWS, width), lambda c: (0, c, 0))
    state = pl.BlockSpec((n_seq, DN_HEADS, DN_DK, DN_DV), lambda c: (0, 0, 0, 0))
    y, s = pl.pallas_call(
        _delta_scan_kernel,
        grid=(seq_len // SCAN_ROWS,),
        in_specs=[rows(hk), rows(hk), rows(hk), rows(hv), rows(DN_HEADS * CHUNK), rows(AB_LANES), rows(hv),
                  pl.BlockSpec((1, DN_DV), lambda c: (0, 0))],
        out_specs=[rows(hv), state],
        out_shape=[jax.ShapeDtypeStruct((n_seq, seq_len, hv), BF16),
                   jax.ShapeDtypeStruct((n_seq, DN_HEADS, DN_DK, DN_DV), F32)],
        scratch_shapes=[pltpu.VMEM((n_seq, DN_HEADS, DN_DK, DN_DV), F32)],
        compiler_params=_params("arbitrary"),
        name="delta_scan",
    )(*map(per_seq, (w, qd, kd, u, qk, egl, zs)), dnn)
    return y.reshape(n_seq * seq_len, hv), s


def _mix_sample_kernel(pa_ref, pq_ref, pz_ref, pab_ref, sa_ref, sq_ref, si_ref,
                       caw_ref, cqw_ref, alog_ref, dtb_ref, dnn_ref,
                       y_ref, ca_ref, cq_ref, so_ref,
                       gb_ref, ua_ref, xq_ref, z_ref, ab_ref, ybuf_ref):
    seqs = range(SEQ_PER_STEP)
    pad = SEQ_PAD - SAMPLE_LEN
    for b in seqs:
        r0 = SEQ_PAD * b
        nxt = (b + 1) % SEQ_PER_STEP
        src = slice(SAMPLE_LEN * b, SAMPLE_LEN * (b + 1))
        pa = pa_ref[src, :]
        gb_ref[r0:r0 + SAMPLE_LEN, :] = pa[:, :D_CONV]
        ua_ref[r0:r0 + SAMPLE_LEN, :] = pa[:, D_CONV:2 * D_CONV] * pa[:, 2 * D_CONV:]
        xq_ref[r0:r0 + SAMPLE_LEN, :] = pq_ref[src, :]
        z_ref[r0:r0 + SAMPLE_LEN, :] = pz_ref[src, :]
        ab_ref[r0:r0 + SAMPLE_LEN, :] = pab_ref[src, :]
        for ref in (gb_ref, ua_ref, xq_ref, z_ref, ab_ref):
            ref[r0 + SAMPLE_LEN:r0 + SEQ_PAD, :] = jnp.zeros((pad, ref.shape[1]), F32)
        ua_ref[r0 + SEQ_PAD - HIST_A:r0 + SEQ_PAD, :] = sa_ref[nxt]
        for r in range(HIST_QKV):
            dst = r0 + SEQ_PAD - HIST_QKV + r
            xq_ref[dst:dst + 1, :] = sq_ref[r, nxt:nxt + 1, :]
    ybuf_ref[:, :D_CONV] = gb_ref[...] * _causal_conv(ua_ref[...], caw_ref[...], None)[0]
    row, col, causal, strict, low_f, eye = _masks(True)
    valid = (row[:, 0:1] & (SEQ_PAD - 1)) < SAMPLE_LEN
    cqkv = jnp.where(valid, _silu(_causal_conv(xq_ref[...], cqw_ref[...], None)[0]), 0.0)
    for b in seqs:
        r0 = SEQ_PAD * b
        ca_ref[b] = ua_ref[r0 + SAMPLE_LEN - HIST_A:r0 + SAMPLE_LEN, :]
        for r in range(HIST_QKV):
            src = r0 + SAMPLE_LEN - HIST_QKV + r
            cq_ref[r, b:b + 1, :] = xq_ref[src:src + 1, :]

    g_all, beta_all = _gates(ab_ref[...], alog_ref[...], dtb_ref[...])
    g_all = jnp.where(valid, g_all, 0.0)
    beta_all = jnp.where(valid, beta_all, 0.0)
    gc_all = _mm_exact(low_f, g_all)
    last_sel = (col == (row | (SEQ_PAD - 1))).astype(F32)
    gl_all = _mm_exact(last_sel, gc_all)
    z = z_ref[...]
    dnn = dnn_ref[...]
    heads = range(DN_HEADS)
    qn = [_l2_normalize(cqkv[:, h * DN_DK:(h + 1) * DN_DK], DN_DK ** -0.5) for h in heads]
    kn = [_l2_normalize(cqkv[:, (DN_HEADS + h) * DN_DK:(DN_HEADS + h + 1) * DN_DK], 1.0) for h in heads]
    vh = [cqkv[:, (2 * DN_HEADS + h) * DN_DV:(2 * DN_HEADS + h + 1) * DN_DV] for h in heads]
    beta_col, gc_col, gc_row = _gate_columns(beta_all, gc_all)
    w, u, qdec, qk = _heads_intra(qn, kn, vh, beta_col, gc_col, gc_row, causal, strict, eye, 1)
    gl_col = [gl_all[:, h:h + 1] for h in heads]
    kdec = [kn[h] * jnp.exp(gl_col[h] - gc_col[h]) for h in heads]
    s_scale = [jnp.exp(x) for x in gl_col]
    rows = [slice(SEQ_PAD * b, SEQ_PAD * (b + 1)) for b in seqs]
    pairs = [(h, b) for h in heads for b in seqs]
    s_old = [si_ref[b, h] for h, b in pairs]
    r = [_mm(jnp.concatenate([w[h][rows[b]], qdec[h][rows[b]]], axis=0), s_old[n])
         for n, (h, b) in enumerate(pairs)]
    v_new = [u[h][rows[b]] - r[n][:SEQ_PAD] for n, (h, b) in enumerate(pairs)]
    upd = [_mm_tn(kdec[h][rows[b]], v_new[n]) for n, (h, b) in enumerate(pairs)]
    for n, (h, b) in enumerate(pairs):
        so_ref[b, h] = s_old[n] * s_scale[h][SEQ_PAD * b:SEQ_PAD * b + 1, :] + upd[n]
    for h in heads:
        mine = range(h * SEQ_PER_STEP, (h + 1) * SEQ_PER_STEP)
        o = (jnp.concatenate([r[n][SEQ_PAD:] for n in mine], axis=0)
             + _mm(qk[h], jnp.concatenate([v_new[n] for n in mine], axis=0)))
        lo = h * DN_DV
        ybuf_ref[:, D_CONV + lo:D_CONV + lo + DN_DV] = _rms(o, dnn) * _silu(z[:, lo:lo + DN_DV])
    for b in seqs:
        y_ref[SAMPLE_LEN * b:SAMPLE_LEN * (b + 1), :] = ybuf_ref[SEQ_PAD * b:SEQ_PAD * b + SAMPLE_LEN, :]


def _mix_sample(proj, ab, state_a, state_q, state_s, caw, cqw, alog, dtb, dnn):
    n_seq = state_s.shape[0]
    full2 = lambda i: (0, 0)
    seq_rows = lambda width, col: pl.BlockSpec((SEQ_PER_STEP * SAMPLE_LEN, width), lambda i: (i, col))
    return pl.pallas_call(
        _mix_sample_kernel,
        grid=(n_seq // SEQ_PER_STEP,),
        in_specs=[
            seq_rows(3 * D_CONV, 0), seq_rows(DN_QKV, 1), seq_rows(DN_HEADS * DN_DV, Z_BLOCK), seq_rows(AB_LANES, 0),
            pl.BlockSpec((SEQ_PER_STEP, 2, D_CONV), lambda i: (i, 0, 0)),
            pl.BlockSpec((3, SEQ_PER_STEP, DN_QKV), lambda i: (0, i, 0)),
            pl.BlockSpec((SEQ_PER_STEP, DN_HEADS, DN_DK, DN_DV), lambda i: (i, 0, 0, 0)),
            pl.BlockSpec((3, D_CONV), full2),
            pl.BlockSpec((4, DN_QKV), full2),
            pl.BlockSpec((1, AB_LANES), full2),
            pl.BlockSpec((1, AB_LANES), full2),
            pl.BlockSpec((1, DN_DV), full2),
        ],
        out_specs=[
            seq_rows(D_MODEL, 0),
            pl.BlockSpec((SEQ_PER_STEP, 2, D_CONV), lambda i: (i, 0, 0)),
            pl.BlockSpec((3, SEQ_PER_STEP, DN_QKV), lambda i: (0, i, 0)),
            pl.BlockSpec((SEQ_PER_STEP, DN_HEADS, DN_DK, DN_DV), lambda i: (i, 0, 0, 0)),
        ],
        out_shape=[
            jax.ShapeDtypeStruct((n_seq * SAMPLE_LEN, D_MODEL), F32),
            jax.ShapeDtypeStruct((n_seq, 2, D_CONV), F32),
            jax.ShapeDtypeStruct((3, n_seq, DN_QKV), F32),
            jax.ShapeDtypeStruct((n_seq, DN_HEADS, DN_DK, DN_DV), F32),
        ],
        scratch_shapes=[
            pltpu.VMEM((CHUNK, D_CONV), F32),
            pltpu.VMEM((CHUNK, D_CONV), F32),
            pltpu.VMEM((CHUNK, DN_QKV), F32),
            pltpu.VMEM((CHUNK, DN_HEADS * DN_DV), F32),
            pltpu.VMEM((CHUNK, AB_LANES), F32),
            pltpu.VMEM((CHUNK, D_MODEL), F32),
        ],
        compiler_params=_params("parallel"),
        name="mix_sample",
    )(proj, proj, proj, ab, state_a, state_q, state_s, caw, cqw, alog, dtb, dnn)


def _outproj_kernel(ya_ref, yb_ref, wa_ref, wb_ref, post_ref, x_ref, o_ref):
    m = (jnp.dot(ya_ref[...].astype(BF16), wa_ref[...], preferred_element_type=F32)
         + jnp.dot(yb_ref[...].astype(BF16), wb_ref[...], preferred_element_type=F32))
    o_ref[...] = x_ref[...] + _rms(m, post_ref[...])


def _outproj(ya, yb, yb_block, w, post, x):
    m = x.shape[0]
    half = D_MODEL // 2
    return pl.pallas_call(
        _outproj_kernel,
        grid=(m // TM,),
        in_specs=[
            pl.BlockSpec((TM, half), lambda i: (i, 0)),
            pl.BlockSpec((TM, half), lambda i: (i, yb_block)),
            pl.BlockSpec((half, D_MODEL), lambda i: (0, 0)),
            pl.BlockSpec((half, D_MODEL), lambda i: (1, 0)),
            pl.BlockSpec((1, D_MODEL), lambda i: (0, 0)),
            pl.BlockSpec((TM, D_MODEL), lambda i: (i, 0)),
        ],
        out_specs=pl.BlockSpec((TM, D_MODEL), lambda i: (i, 0)),
        out_shape=jax.ShapeDtypeStruct((m, D_MODEL), F32),
        compiler_params=_params("parallel"),
        name="outproj",
    )(ya, yb, w, w, post, x)


def _ple_kernel(x_ref, p_ref, pre_ref, post_ref, wg_ref, wp_ref, o_ref):
    x = x_ref[...]
    h = _rms(x, pre_ref[...]).astype(BF16)
    gate = jax.nn.sigmoid(jnp.dot(h, wg_ref[...], preferred_element_type=F32))
    proj = jnp.dot(p_ref[...].astype(BF16), wp_ref[...], preferred_element_type=F32)
    o_ref[...] = x + _rms(gate * proj, post_ref[...])


def _ple(x, p, pre, post, wg, wp):
    m = x.shape[0]
    return pl.pallas_call(
        _ple_kernel,
        grid=(m // TM,),
        in_specs=[
            pl.BlockSpec((TM, D_MODEL), lambda i: (i, 0)),
            pl.BlockSpec((TM, D_PLE), lambda i: (i, 0)),
            pl.BlockSpec((1, D_MODEL), lambda i: (0, 0)),
            pl.BlockSpec((1, D_MODEL), lambda i: (0, 0)),
            pl.BlockSpec((D_MODEL, D_MODEL), lambda i: (0, 0)),
            pl.BlockSpec((D_PLE, D_MODEL), lambda i: (0, 0)),
        ],
        out_specs=pl.BlockSpec((TM, D_MODEL), lambda i: (i, 0)),
        out_shape=jax.ShapeDtypeStruct((m, D_MODEL), F32),
        compiler_params=_params("parallel"),
        name="ple",
    )(x, p, pre, post, wg, wp)


def _row(v):
    return v.reshape(1, -1).astype(F32)


def _pad_lanes(v):
    return jnp.pad(_row(v), ((0, 0), (0, AB_LANES - v.shape[-1])))


def _seq_tails(tails, tiles_per_seq, n_rows):
    return tails[tiles_per_seq - 1::tiles_per_seq, TAIL_ROWS - n_rows:, :]


def kernel(x_prompt, x_sample, state_conv_a, state_conv_qkv, state_delta, p_prompt, p_sample,
           f1_pre, f1_post, f1_wg, f1_wu, f1_wd,
           mix_pre, mix_post, w_in, conv_a_w, conv_qkv_w, a_log, dt_bias, dn_norm, w_out,
           f2_pre, f2_post, f2_wg, f2_wu, f2_wd,
           ple_pre, ple_post, w_ple_gate, w_ple_proj):
    depth = f1_pre.shape[0]
    n_p, seq_p, _ = x_prompt.shape
    n_s, seq_s, _ = x_sample.shape
    rows_p = n_p * seq_p
    rows_s = n_s * seq_s
    assert seq_p % TM_BIG == 0 and seq_s == SAMPLE_LEN and n_s % SEQ_PER_STEP == 0
    assert rows_s == TM
    assert w_in.shape[-1] == IN_MAIN + N_AB
    tiles_per_seq = seq_p // TM_BIG
    xp = x_prompt.reshape(rows_p, D_MODEL)
    xs = x_sample.reshape(rows_s, D_MODEL)
    outs = [[] for _ in range(6)]
    for i in range(depth):
        pre, post = _row(f1_pre[i]), _row(f1_post[i])
        xs, wg, wu, wd = _ffn(xs, pre, post, f1_wg[i], f1_wu[i], f1_wd[i], TM)
        xp = _ffn(xp, pre, post, wg, wu, wd, TM_BIG)[0]

        pre = _row(mix_pre[i])
        caw, cqw = conv_a_w[i], conv_qkv_w[i]
        proj_s, ab_s, wt, wabt = _inproj_cast(xs, pre, w_in[i].T)
        y_a, zs, h, tail_a = _proj_a(xp, pre, wt, caw, TM_BIG, tiles_per_seq)
        alog, dtb, dnn = _pad_lanes(a_log[i]), _pad_lanes(dt_bias[i]), _row(dn_norm[i])
        wy_w, qdec, kdec, wy_u, qk, egl, tail_q = _proj_qkv(h, wt, wabt, cqw, alog, dtb, seq_p // TM)
        y_b, s_p = _delta_scan(wy_w, qdec, kdec, wy_u, qk, egl, zs, dnn, n_p, seq_p)
        ca_p = _seq_tails(tail_a, tiles_per_seq, HIST_A)
        cq_p = _seq_tails(tail_q, seq_p // TM, HIST_QKV)
        y_s, ca_s, cq_s, s_s = _mix_sample(
            proj_s, ab_s, state_conv_a[i], jnp.swapaxes(state_conv_qkv[i], 0, 1), state_delta[i],
            caw, cqw, alog, dtb, dnn)
        cq_s = jnp.swapaxes(cq_s, 0, 1)

        post = _row(mix_post[i])
        w_out_bf16 = w_out[i].astype(BF16)
        xs = _outproj(y_s, y_s, 1, w_out_bf16, post, xs)
        xp = _outproj(y_a, y_b, 0, w_out_bf16, post, xp)

        pre, post = _row(f2_pre[i]), _row(f2_post[i])
        xs, wg, wu, wd = _ffn(xs, pre, post, f2_wg[i], f2_wu[i], f2_wd[i], TM)
        xp = _ffn(xp, pre, post, wg, wu, wd, TM_BIG)[0]

        pre, post = _row(ple_pre[i]), _row(ple_post[i])
        wg, wp = w_ple_gate[i].astype(BF16), w_ple_proj[i].astype(BF16)
        xs = _ple(xs, p_sample[i].reshape(rows_s, D_PLE), pre, post, wg, wp)
        xp = _ple(xp, p_prompt[i].reshape(rows_p, D_PLE), pre, post, wg, wp)
        for lst, val in zip(outs, (ca_p, cq_p, s_p, ca_s, cq_s, s_s)):
            lst.append(val)
    return (xp.reshape(n_p, seq_p, D_MODEL), xs.reshape(n_s, seq_s, D_MODEL),
            *[jnp.stack(lst) for lst in outs])
```

```python
import functools

import jax
import jax.numpy as jnp
from jax import lax
from jax.experimental import pallas as pl
from jax.experimental.pallas import tpu as pltpu

F32 = jnp.float32
BF16 = jnp.bfloat16

D_MODEL = 2048
D_CONV = 1024
DN_HEADS = 8
DN_DK = 128
DN_DV = 128
DN_QKV = 3072
D_FF = 5632
D_PLE = 256
EPS = 1e-6
CHUNK = 64
SAMPLE_LEN = 4
SEQ_PAD = 8
SEQ_SHIFT = 3
SEQ_PER_STEP = CHUNK // SEQ_PAD
IN_MAIN = 7168
QKV_BLOCK = 3
Z_BLOCK = 6
HIST_A = 2
HIST_QKV = 3
TAIL_ROWS = 8
N_AB = 16
AB_LANES = 128

TM = 512
TF = 512
TF_CAST = 256
RING = 3
TN_IN = 1024
TM_BIG = 1024
TN_A = 512
SUB_COLS = 512
SCAN_ROWS = 256
ROW_CHUNK = 32
VMEM_LIMIT = 56 * 1024 * 1024


def _rms(x, g):
    ms = jnp.mean(x * x, axis=-1, keepdims=True)
    return x * lax.rsqrt(ms + EPS) * g


def _silu(x):
    return x * jax.nn.sigmoid(x)


def _mm(a, b):
    return jnp.dot(a.astype(BF16), b.astype(BF16), preferred_element_type=F32)


def _mm_nt(a, b):
    return lax.dot_general(a.astype(BF16), b.astype(BF16), (((1,), (1,)), ((), ())),
                           preferred_element_type=F32)


def _mm_tn(a, b):
    return jnp.dot(a.T.astype(BF16), b.astype(BF16), preferred_element_type=F32)


def _mm_exact(a, b):
    return jnp.dot(a, b, precision=lax.Precision.HIGHEST, preferred_element_type=F32)


def _nt_dot(a, bt):
    return lax.dot_general(a, bt, (((1,), (1,)), ((), ())), preferred_element_type=F32)


def _for_row_chunks(n_rows, body):
    def step(k, carry):
        body(pl.ds(pl.multiple_of(k * ROW_CHUNK, ROW_CHUNK), ROW_CHUNK))
        return carry
    lax.fori_loop(0, n_rows // ROW_CHUNK, step, 0, unroll=4)


def _params(*semantics):
    return pltpu.CompilerParams(dimension_semantics=semantics, vmem_limit_bytes=VMEM_LIMIT)


def _ffn_kernel(emit_bf16, x_ref, pre_ref, post_ref, wg_ref, wu_ref, wd_ref, o_ref, *rest):
    j = pl.program_id(1)
    if emit_bf16:
        wg16_ref, wu16_ref, wd16_ref, h_ref, wg_buf, wu_buf, wd_buf, sem = rest
        n_tiles = pl.num_programs(1)
        tf = wg_buf.shape[2]

        def tile_copies(t):
            start = t * tf if isinstance(t, int) else pl.multiple_of(t * tf, tf)
            slot = t % RING
            return (pltpu.make_async_copy(wg_ref.at[:, pl.ds(start, tf)], wg_buf.at[slot], sem.at[0, slot]),
                    pltpu.make_async_copy(wu_ref.at[:, pl.ds(start, tf)], wu_buf.at[slot], sem.at[1, slot]),
                    pltpu.make_async_copy(wd_ref.at[pl.ds(start, tf), :], wd_buf.at[slot], sem.at[2, slot]))

        @pl.when(j == 0)
        def _():
            for t in range(RING - 1):
                for copy in tile_copies(t):
                    copy.start()

        @pl.when(j + (RING - 1) < n_tiles)
        def _():
            for copy in tile_copies(j + (RING - 1)):
                copy.start()
    else:
        h_ref = rest[-1]

    @pl.when(j == 0)
    def _():
        def norm_rows(rows):
            h_ref[rows, :] = _rms(x_ref[rows, :], pre_ref[...]).astype(BF16)
            o_ref[rows, :] = jnp.zeros((ROW_CHUNK, D_MODEL), F32)
        _for_row_chunks(x_ref.shape[0], norm_rows)

    if emit_bf16:
        for copy in tile_copies(j):
            copy.wait()
        slot = j % RING
        wg, wu, wd = wg_buf[slot].astype(BF16), wu_buf[slot].astype(BF16), wd_buf[slot].astype(BF16)
        wg16_ref[...] = wg
        wu16_ref[...] = wu
        wd16_ref[...] = wd
    else:
        wg, wu, wd = wg_ref[...], wu_ref[...], wd_ref[...]
    h = h_ref[...]
    g = jnp.dot(h, wg, preferred_element_type=F32)
    u = jnp.dot(h, wu, preferred_element_type=F32)
    a = (_silu(g) * u).astype(BF16)
    o_ref[...] += jnp.dot(a, wd, preferred_element_type=F32)

    @pl.when(j == pl.num_programs(1) - 1)
    def _():
        group = 4 * ROW_CHUNK

        def finish_rows(k, carry):
            r0 = pl.multiple_of(k * group, group)
            chunks = [pl.ds(r0 + s * ROW_CHUNK, ROW_CHUNK) for s in range(group // ROW_CHUNK)]
            scale = [0.5 * lax.rsqrt(jnp.mean(jnp.square(o_ref[c, :]), axis=-1, keepdims=True) + EPS)
                     for c in chunks]
            for c, sc in zip(chunks, scale):
                o_ref[c, :] = x_ref[c, :] + o_ref[c, :] * sc * post_ref[...]
            return carry
        lax.fori_loop(0, x_ref.shape[0] // group, finish_rows, 0)


def _ffn(x, pre, post, wg, wu, wd, tm):
    m = x.shape[0]
    emit_bf16 = wg.dtype != BF16
    tf = TF_CAST if emit_bf16 else TF
    up_spec = pl.BlockSpec((D_MODEL, tf), lambda i, j: (0, j))
    down_spec = pl.BlockSpec((tf, D_MODEL), lambda i, j: (j, 0))
    out_specs = [pl.BlockSpec((tm, D_MODEL), lambda i, j: (i, 0))]
    out_shape = [jax.ShapeDtypeStruct((m, D_MODEL), F32)]
    w_specs = [up_spec, up_spec, down_spec]
    scratch = [pltpu.VMEM((tm, D_MODEL), BF16)]
    if emit_bf16:
        assert m == tm, "weights are re-emitted once, by a single-row-tile call"
        assert D_FF // tf >= RING - 1
        out_specs += w_specs
        out_shape += [jax.ShapeDtypeStruct(w.shape, BF16) for w in (wg, wu, wd)]
        w_specs = [pl.BlockSpec(memory_space=pl.ANY)] * 3
        scratch += [pltpu.VMEM((RING, D_MODEL, tf), F32), pltpu.VMEM((RING, D_MODEL, tf), F32),
                    pltpu.VMEM((RING, tf, D_MODEL), F32), pltpu.SemaphoreType.DMA((3, RING))]
    outs = pl.pallas_call(
        functools.partial(_ffn_kernel, emit_bf16),
        grid=(m // tm, D_FF // tf),
        in_specs=[
            pl.BlockSpec((tm, D_MODEL), lambda i, j: (i, 0)),
            pl.BlockSpec((1, D_MODEL), lambda i, j: (0, 0)),
            pl.BlockSpec((1, D_MODEL), lambda i, j: (0, 0)),
            *w_specs,
        ],
        out_specs=out_specs,
        out_shape=out_shape,
        scratch_shapes=scratch,
        compiler_params=_params("arbitrary", "arbitrary"),
        name="ffn_cast" if emit_bf16 else "ffn",
    )(x, pre, post, wg, wu, wd)
    return outs if emit_bf16 else (outs[0], wg, wu, wd)


def _inproj_cast_kernel(x_ref, pre_ref, wt_ref, wabt_ref, o_ref, ab_ref, wt16_ref, wabt16_ref,
                        h_ref, wt_buf, sem):
    j = pl.program_id(1)

    def tile_copy(t):
        start = t * TN_IN if isinstance(t, int) else pl.multiple_of(t * TN_IN, TN_IN)
        slot = t % RING
        return pltpu.make_async_copy(wt_ref.at[pl.ds(start, TN_IN), :], wt_buf.at[slot], sem.at[slot])

    @pl.when(j == 0)
    def _():
        for t in range(RING - 1):
            tile_copy(t).start()

    @pl.when(j + (RING - 1) < pl.num_programs(1))
    def _():
        tile_copy(j + (RING - 1)).start()

    @pl.when(j == 0)
    def _():
        h = _rms(x_ref[...], pre_ref[...]).astype(BF16)
        h_ref[...] = h
        wabt = jnp.concatenate(
            [wabt_ref[...].astype(BF16), jnp.zeros((AB_LANES - N_AB, D_MODEL), BF16)], axis=0)
        wabt16_ref[...] = wabt
        ab_ref[...] = _nt_dot(h, wabt)

    tile_copy(j).wait()
    wt = wt_buf[j % RING].astype(BF16)
    wt16_ref[...] = wt
    o_ref[...] = _nt_dot(h_ref[...], wt)


def _inproj_cast(x, pre, wt):
    m = x.shape[0]
    assert m == TM, "weights are re-emitted once, by a single-row-tile call"
    assert IN_MAIN // TN_IN >= RING - 1
    wt_spec = pl.BlockSpec((TN_IN, D_MODEL), lambda i, j: (j, 0))
    return pl.pallas_call(
        _inproj_cast_kernel,
        grid=(1, IN_MAIN // TN_IN),
        in_specs=[
            pl.BlockSpec((TM, D_MODEL), lambda i, j: (0, 0)),
            pl.BlockSpec((1, D_MODEL), lambda i, j: (0, 0)),
            pl.BlockSpec(memory_space=pl.ANY),
            pl.BlockSpec((N_AB, D_MODEL), lambda i, j: (IN_MAIN // N_AB, 0)),
        ],
        out_specs=[
            pl.BlockSpec((TM, TN_IN), lambda i, j: (0, j)),
            pl.BlockSpec((TM, AB_LANES), lambda i, j: (0, 0)),
            wt_spec,
            pl.BlockSpec((AB_LANES, D_MODEL), lambda i, j: (0, 0)),
        ],
        out_shape=[
            jax.ShapeDtypeStruct((m, IN_MAIN), F32),
            jax.ShapeDtypeStruct((m, AB_LANES), F32),
            jax.ShapeDtypeStruct((IN_MAIN, D_MODEL), BF16),
            jax.ShapeDtypeStruct((AB_LANES, D_MODEL), BF16),
        ],
        scratch_shapes=[pltpu.VMEM((TM, D_MODEL), BF16), pltpu.VMEM((RING, TN_IN, D_MODEL), F32),
                        pltpu.SemaphoreType.DMA((RING,))],
        compiler_params=_params("arbitrary", "arbitrary"),
        name="inproj_cast",
    )(x, pre, wt, wt)


def _causal_conv(x, w, carry_in):
    width = w.shape[0]
    rows = x.shape[0]
    is_row0 = lax.broadcasted_iota(jnp.int32, (8, 1), 0) == 0
    acc = x * w[0:1, :]
    carry_out = []
    for j in range(1, width):
        delayed = pltpu.roll(acc, 1, axis=0)
        if carry_in is not None:
            carry_out.append(acc[rows - 1:rows, :])
            head = jnp.where(is_row0, carry_in[j - 1], delayed[:8])
            delayed = jnp.concatenate([head, delayed[8:]], axis=0)
        acc = x * w[j:j + 1, :] + delayed
    return acc, carry_out


def _load_carry(carry_ref, j, cols, n, seq_start):
    return [jnp.where(seq_start, 0.0, carry_ref[j, r:r + 1, cols]) for r in range(n)]


def _store_carry(carry_ref, j, cols, rows):
    for r, row in enumerate(rows):
        carry_ref[j, r:r + 1, cols] = row


def _proj_a_kernel(tiles_per_seq, x_ref, pre_ref, wgb_ref, wgcv_ref, whc_ref, wz_ref, caw_ref,
                   ya_ref, zs_ref, h_ref, tail_ref, carry_ref):
    i, j = pl.program_id(0), pl.program_id(1)
    tm = x_ref.shape[0]

    @pl.when(j == 0)
    def _():
        def norm_rows(rows):
            h_ref[rows, :] = _rms(x_ref[rows, :], pre_ref[...]).astype(BF16)
        _for_row_chunks(tm, norm_rows)

    @pl.when((i == 0) & (j == 0))
    def _():
        carry_ref[...] = jnp.zeros_like(carry_ref)

    seq_start = i % tiles_per_seq == 0
    h = h_ref[...]
    for c in range(TN_A // SUB_COLS):
        cols = slice(c * SUB_COLS, (c + 1) * SUB_COLS)
        u = _nt_dot(h, wgcv_ref[cols, :]) * _nt_dot(h, whc_ref[cols, :])
        tail_ref[0, :, cols] = u[tm - TAIL_ROWS:tm, :]
        conv, carry = _causal_conv(u, caw_ref[:, cols], _load_carry(carry_ref, j, cols, HIST_A, seq_start))
        _store_carry(carry_ref, j, cols, carry)
        ya_ref[:, cols] = (_nt_dot(h, wgb_ref[cols, :]) * conv).astype(BF16)
        zs_ref[:, cols] = _silu(_nt_dot(h, wz_ref[cols, :])).astype(BF16)


def _proj_a(x, pre, wt, caw, tm, tiles_per_seq):
    m = x.shape[0]
    n_col = D_CONV // TN_A
    assert DN_HEADS * DN_DV == D_CONV
    z_part = Z_BLOCK * TN_IN // D_CONV
    w_spec = lambda part: pl.BlockSpec((TN_A, D_MODEL), lambda i, j: (part * n_col + j, 0))
    cols = pl.BlockSpec((tm, TN_A), lambda i, j: (i, j))
    return pl.pallas_call(
        functools.partial(_proj_a_kernel, tiles_per_seq),
        grid=(m // tm, n_col),
        in_specs=[
            pl.BlockSpec((tm, D_MODEL), lambda i, j: (i, 0)),
            pl.BlockSpec((1, D_MODEL), lambda i, j: (0, 0)),
            w_spec(0), w_spec(1), w_spec(2), w_spec(z_part),
            pl.BlockSpec((3, TN_A), lambda i, j: (0, j)),
        ],
        out_specs=[
            cols, cols,
            pl.BlockSpec((tm, D_MODEL), lambda i, j: (i, 0)),
            pl.BlockSpec((1, TAIL_ROWS, TN_A), lambda i, j: (i, 0, j)),
        ],
        out_shape=[
            jax.ShapeDtypeStruct((m, D_CONV), BF16),
            jax.ShapeDtypeStruct((m, DN_HEADS * DN_DV), BF16),
            jax.ShapeDtypeStruct((m, D_MODEL), BF16),
            jax.ShapeDtypeStruct((m // tm, TAIL_ROWS, D_CONV), F32),
        ],
        scratch_shapes=[pltpu.VMEM((n_col, HIST_A, TN_A), F32)],
        compiler_params=_params("arbitrary", "arbitrary"),
        name="proj_a",
    )(x, pre, wt, wt, wt, wt, caw)


def _l2_normalize(x, scale):
    return x * (lax.rsqrt(jnp.sum(x * x, axis=-1, keepdims=True) + EPS) * scale)


def _gate_columns(beta_all, gc_all):
    heads = range(DN_HEADS)
    gc_t = gc_all.T
    return ([beta_all[:, DN_HEADS + h:DN_HEADS + h + 1] for h in heads],
            [gc_all[:, h:h + 1] for h in heads], [gc_t[h:h + 1, :] for h in heads])


def _heads_intra(qn, kn, vh, beta_col, gc_col, gc_row, causal, strict, eye, squarings):
    heads = range(len(qn))
    decay = [jnp.exp(jnp.where(causal, gc_col[h] - gc_row[h], -jnp.inf)) for h in heads]
    kb = [kn[h] * beta_col[h] for h in heads]
    gram = [_mm_nt(jnp.concatenate([kb[h], qn[h]], axis=0), kn[h]) for h in heads]
    a_mat = [jnp.where(strict, gram[h][:CHUNK] * decay[h], 0.0) for h in heads]
    qk = [jnp.where(causal, gram[h][CHUNK:] * decay[h], 0.0) for h in heads]
    inv = [eye - a for a in a_mat]
    power = a_mat
    for _ in range(squarings):
        power = [_mm(x, x) for x in power]
        inv = [_mm(inv[h], eye + power[h]) for h in heads]
    egc = [jnp.exp(x) for x in gc_col]
    wu = [_mm(inv[h], jnp.concatenate([kb[h] * egc[h], vh[h] * beta_col[h]], axis=1)) for h in heads]
    w = [x[:, :DN_DK] for x in wu]
    u = [x[:, DN_DK:] for x in wu]
    qdec = [qn[h] * egc[h] for h in heads]
    return w, u, qdec, qk


def _gates(ab, alog, dtb):
    x = ab + dtb
    softplus = jnp.maximum(x, 0.0) + jnp.log1p(jnp.exp(-jnp.abs(x)))
    return -jnp.exp(alog) * softplus, jax.nn.sigmoid(ab)


def _masks(same_seq):
    row = lax.broadcasted_iota(jnp.int32, (CHUNK, CHUNK), 0)
    col = lax.broadcasted_iota(jnp.int32, (CHUNK, CHUNK), 1)
    causal = row >= col
    strict = row > col
    if same_seq:
        same = (row >> SEQ_SHIFT) == (col >> SEQ_SHIFT)
        causal = causal & same
        strict = strict & same
    eye = jnp.where(row == col, 1.0, 0.0).astype(F32)
    return row, col, causal, strict, causal.astype(F32), eye


def _proj_qkv_kernel(tiles_per_seq, h_ref, wq_ref, wk_ref, wv_ref, wabt_ref, cqw_ref, alog_ref, dtb_ref,
                     w_ref, qd_ref, kd_ref, u_ref, qk_ref, egl_ref, tail_ref, carry_ref, qkv_ref):
    i = pl.program_id(0)
    tm = h_ref.shape[0]

    @pl.when(i == 0)
    def _():
        carry_ref[...] = jnp.zeros_like(carry_ref)

    seq_start = i % tiles_per_seq == 0
    h = h_ref[...]
    g_all, beta_all = _gates(_nt_dot(h, wabt_ref[...]), alog_ref[...], dtb_ref[...])
    for t, wt_ref in enumerate((wq_ref, wk_ref, wv_ref)):
        for c in range(TN_IN // SUB_COLS):
            cols = slice(c * SUB_COLS, (c + 1) * SUB_COLS)
            lo = t * TN_IN + c * SUB_COLS
            p = _nt_dot(h, wt_ref[cols, :])
            tail_ref[0, :, lo:lo + SUB_COLS] = p[tm - TAIL_ROWS:tm, :]
            p, carry = _causal_conv(p, cqw_ref[:, lo:lo + SUB_COLS],
                                    _load_carry(carry_ref, t, cols, HIST_QKV, seq_start))
            _store_carry(carry_ref, t, cols, carry)
            p = _silu(p)
            if t == 2:
                qkv_ref[:, lo:lo + SUB_COLS] = p
            else:
                for k in range(SUB_COLS // DN_DK):
                    blk = p[:, k * DN_DK:(k + 1) * DN_DK]
                    qkv_ref[:, lo + k * DN_DK:lo + (k + 1) * DN_DK] = _l2_normalize(
                        blk, DN_DK ** -0.5 if t == 0 else 1.0)

    heads = range(DN_HEADS)
    _, _, causal, strict, low_f, eye = _masks(False)
    chunk_rows = [slice(g * CHUNK, (g + 1) * CHUNK) for g in range(tm // CHUNK)]
    qn, kn, vh, beta_col, gc_col, gc_row = [], [], [], [], [], []
    for rows in chunk_rows:
        gc_all = _mm_exact(low_f, g_all[rows])
        egl_ref[rows, :] = jnp.broadcast_to(jnp.exp(gc_all[CHUNK - 1:CHUNK, :]), (CHUNK, AB_LANES))
        cols = _gate_columns(beta_all[rows], gc_all)
        beta_col += cols[0]
        gc_col += cols[1]
        gc_row += cols[2]
        qn += [qkv_ref[rows, h * DN_DK:(h + 1) * DN_DK] for h in heads]
        kn += [qkv_ref[rows, (DN_HEADS + h) * DN_DK:(DN_HEADS + h + 1) * DN_DK] for h in heads]
        vh += [qkv_ref[rows, (2 * DN_HEADS + h) * DN_DV:(2 * DN_HEADS + h + 1) * DN_DV] for h in heads]
    w, u, qdec, qk = _heads_intra(qn, kn, vh, beta_col, gc_col, gc_row, causal, strict, eye, 5)
    for n in range(len(w)):
        rows, hd = chunk_rows[n // DN_HEADS], n % DN_HEADS
        lo = hd * DN_DK
        g_last = gc_col[n][CHUNK - 1:CHUNK, :]
        w_ref[rows, lo:lo + DN_DK] = w[n].astype(BF16)
        qd_ref[rows, lo:lo + DN_DK] = qdec[n].astype(BF16)
        kd_ref[rows, lo:lo + DN_DK] = (kn[n] * jnp.exp(g_last - gc_col[n])).astype(BF16)
        u_ref[rows, lo:lo + DN_DV] = u[n]
        qk_ref[rows, hd * CHUNK:(hd + 1) * CHUNK] = qk[n].astype(BF16)


def _proj_qkv(h, wt, wabt, cqw, alog, dtb, tiles_per_seq):
    m = h.shape[0]
    hk, hv = DN_HEADS * DN_DK, DN_HEADS * DN_DV
    rows = lambda width: pl.BlockSpec((TM, width), lambda i: (i, 0))
    whole = lambda shape: pl.BlockSpec(shape, lambda i: (0,) * len(shape))
    w_tile = lambda t: pl.BlockSpec((TN_IN, D_MODEL), lambda i: (QKV_BLOCK + t, 0))
    return pl.pallas_call(
        functools.partial(_proj_qkv_kernel, tiles_per_seq),
        grid=(m // TM,),
        in_specs=[rows(D_MODEL), w_tile(0), w_tile(1), w_tile(2), whole((AB_LANES, D_MODEL)),
                  whole((4, DN_QKV)), whole((1, AB_LANES)), whole((1, AB_LANES))],
        out_specs=[rows(hk), rows(hk), rows(hk), rows(hv), rows(DN_HEADS * CHUNK), rows(AB_LANES),
                   pl.BlockSpec((1, TAIL_ROWS, DN_QKV), lambda i: (i, 0, 0))],
        out_shape=[
            jax.ShapeDtypeStruct((m, hk), BF16), jax.ShapeDtypeStruct((m, hk), BF16),
            jax.ShapeDtypeStruct((m, hk), BF16), jax.ShapeDtypeStruct((m, hv), F32),
            jax.ShapeDtypeStruct((m, DN_HEADS * CHUNK), BF16), jax.ShapeDtypeStruct((m, AB_LANES), F32),
            jax.ShapeDtypeStruct((m // TM, TAIL_ROWS, DN_QKV), F32),
        ],
        scratch_shapes=[pltpu.VMEM((3, HIST_QKV, TN_IN), F32), pltpu.VMEM((TM, DN_QKV), F32)],
        compiler_params=_params("arbitrary"),
        name="proj_qkv",
    )(h, wt, wt, wt, wabt, cqw, alog, dtb)


def _delta_scan_kernel(w_ref, qd_ref, kd_ref, u_ref, qk_ref, egl_ref, zs_ref, dnn_ref,
                       y_ref, so_ref, s_ref):
    c = pl.program_id(0)

    @pl.when(c == 0)
    def _():
        s_ref[...] = jnp.zeros_like(s_ref)

    n_seq = w_ref.shape[0]
    units = [(b, h) for b in range(n_seq) for h in range(DN_HEADS)]
    col = [slice(h * DN_DK, (h + 1) * DN_DK) for h in range(DN_HEADS)]
    dnn = dnn_ref[...]
    s = [s_ref[b, h] for b, h in units]
    for g in range(w_ref.shape[1] // CHUNK):
        rows = slice(g * CHUNK, (g + 1) * CHUNK)
        lhs = [jnp.concatenate([w_ref[b, rows, col[h]], qd_ref[b, rows, col[h]]], axis=0) for b, h in units]
        r = [jnp.dot(lhs[n], s[n].astype(BF16), preferred_element_type=F32) for n in range(len(units))]
        v_new = [(u_ref[b, rows, col[h]] - r[n][:CHUNK]).astype(BF16) for n, (b, h) in enumerate(units)]
        upd = [lax.dot_general(kd_ref[b, rows, col[h]], v_new[n], (((0,), (0,)), ((), ())),
                               preferred_element_type=F32) for n, (b, h) in enumerate(units)]
        s = [s[n] * egl_ref[b, g * CHUNK:g * CHUNK + 1, h:h + 1] + upd[n] for n, (b, h) in enumerate(units)]
        o = [r[n][CHUNK:] + jnp.dot(qk_ref[b, rows, h * CHUNK:(h + 1) * CHUNK], v_new[n],
                                    preferred_element_type=F32) for n, (b, h) in enumerate(units)]
        for n, (b, h) in enumerate(units):
            y_ref[b, rows, col[h]] = (_rms(o[n], dnn) * zs_ref[b, rows, col[h]]).astype(y_ref.dtype)
    for n, (b, h) in enumerate(units):
        s_ref[b, h] = s[n]

    @pl.when(c == pl.num_programs(0) - 1)
    def _():
        so_ref[...] = s_ref[...]


def _delta_scan(w, qd, kd, u, qk, egl, zs, dnn, n_seq, seq_len):
    hk, hv = DN_HEADS * DN_DK, DN_HEADS * DN_DV
    per_seq = lambda a: a.reshape(n_seq, seq_len, a.shape[-1])
    rows = lambda width: pl.BlockSpec((n_seq, SCAN_ROWS, width), lambda c: (0, c, 0))
    state = pl.BlockSpec((n_seq, DN_HEADS, DN_DK, DN_DV), lambda c: (0, 0, 0, 0))
    y, s = pl.pallas_call(
        _delta_scan_kernel,
        grid=(seq_len // SCAN_ROWS,),
        in_specs=[rows(hk), rows(hk), rows(hk), rows(hv), rows(DN_HEADS * CHUNK), rows(AB_LANES), rows(hv),
                  pl.BlockSpec((1, DN_DV), lambda c: (0, 0))],
        out_specs=[rows(hv), state],
        out_shape=[jax.ShapeDtypeStruct((n_seq, seq_len, hv), BF16),
                   jax.ShapeDtypeStruct((n_seq, DN_HEADS, DN_DK, DN_DV), F32)],
        scratch_shapes=[pltpu.VMEM((n_seq, DN_HEADS, DN_DK, DN_DV), F32)],
        compiler_params=_params("arbitrary"),
        name="delta_scan",
    )(*map(per_seq, (w, qd, kd, u, qk, egl, zs)), dnn)
    return y.reshape(n_seq * seq_len, hv), s


def _mix_sample_kernel(pa_ref, pq_ref, pz_ref, pab_ref, sa_ref, sq_ref, si_ref,
                       caw_ref, cqw_ref, alog_ref, dtb_ref, dnn_ref,
                       y_ref, ca_ref, cq_ref, so_ref,
                       gb_ref, ua_ref, xq_ref, z_ref, ab_ref, ybuf_ref, si_buf, sem):
    step = pl.program_id(0)

    def state_copy(t):
        start = t * SEQ_PER_STEP if isinstance(t, int) else pl.multiple_of(t * SEQ_PER_STEP, SEQ_PER_STEP)
        slot = t % RING
        return pltpu.make_async_copy(si_ref.at[pl.ds(start, SEQ_PER_STEP)], si_buf.at[slot], sem.at[slot])

    @pl.when(step == 0)
    def _():
        for t in range(RING - 1):
            state_copy(t).start()

    @pl.when(step + (RING - 1) < pl.num_programs(0))
    def _():
        state_copy(step + (RING - 1)).start()

    seqs = range(SEQ_PER_STEP)
    pad = SEQ_PAD - SAMPLE_LEN
    for b in seqs:
        r0 = SEQ_PAD * b
        nxt = (b + 1) % SEQ_PER_STEP
        src = slice(SAMPLE_LEN * b, SAMPLE_LEN * (b + 1))
        pa = pa_ref[src, :]
        gb_ref[r0:r0 + SAMPLE_LEN, :] = pa[:, :D_CONV]
        ua_ref[r0:r0 + SAMPLE_LEN, :] = pa[:, D_CONV:2 * D_CONV] * pa[:, 2 * D_CONV:]
        xq_ref[r0:r0 + SAMPLE_LEN, :] = pq_ref[src, :]
        z_ref[r0:r0 + SAMPLE_LEN, :] = pz_ref[src, :]
        ab_ref[r0:r0 + SAMPLE_LEN, :] = pab_ref[src, :]
        for ref in (gb_ref, ua_ref, xq_ref, z_ref, ab_ref):
            ref[r0 + SAMPLE_LEN:r0 + SEQ_PAD, :] = jnp.zeros((pad, ref.shape[1]), F32)
        ua_ref[r0 + SEQ_PAD - HIST_A:r0 + SEQ_PAD, :] = sa_ref[nxt]
        for r in range(HIST_QKV):
            dst = r0 + SEQ_PAD - HIST_QKV + r
            xq_ref[dst:dst + 1, :] = sq_ref[r, nxt:nxt + 1, :]
    ybuf_ref[:, :D_CONV] = gb_ref[...] * _causal_conv(ua_ref[...], caw_ref[...], None)[0]
    row, col, causal, strict, low_f, eye = _masks(True)
    valid = (row[:, 0:1] & (SEQ_PAD - 1)) < SAMPLE_LEN
    cqkv = jnp.where(valid, _silu(_causal_conv(xq_ref[...], cqw_ref[...], None)[0]), 0.0)
    for b in seqs:
        r0 = SEQ_PAD * b
        ca_ref[b] = ua_ref[r0 + SAMPLE_LEN - HIST_A:r0 + SAMPLE_LEN, :]
        for r in range(HIST_QKV):
            src = r0 + SAMPLE_LEN - HIST_QKV + r
            cq_ref[r, b:b + 1, :] = xq_ref[src:src + 1, :]

    g_all, beta_all = _gates(ab_ref[...], alog_ref[...], dtb_ref[...])
    g_all = jnp.where(valid, g_all, 0.0)
    beta_all = jnp.where(valid, beta_all, 0.0)
    gc_all = _mm_exact(low_f, g_all)
    last_sel = (col == (row | (SEQ_PAD - 1))).astype(F32)
    gl_all = _mm_exact(last_sel, gc_all)
    z = z_ref[...]
    dnn = dnn_ref[...]
    heads = range(DN_HEADS)
    qn = [_l2_normalize(cqkv[:, h * DN_DK:(h + 1) * DN_DK], DN_DK ** -0.5) for h in heads]
    kn = [_l2_normalize(cqkv[:, (DN_HEADS + h) * DN_DK:(DN_HEADS + h + 1) * DN_DK], 1.0) for h in heads]
    vh = [cqkv[:, (2 * DN_HEADS + h) * DN_DV:(2 * DN_HEADS + h + 1) * DN_DV] for h in heads]
    beta_col, gc_col, gc_row = _gate_columns(beta_all, gc_all)
    w, u, qdec, qk = _heads_intra(qn, kn, vh, beta_col, gc_col, gc_row, causal, strict, eye, 1)
    gl_col = [gl_all[:, h:h + 1] for h in heads]
    kdec = [kn[h] * jnp.exp(gl_col[h] - gc_col[h]) for h in heads]
    s_scale = [jnp.exp(x) for x in gl_col]
    rows = [slice(SEQ_PAD * b, SEQ_PAD * (b + 1)) for b in seqs]
    pairs = [(h, b) for h in heads for b in seqs]
    state_copy(step).wait()
    slot = step % RING
    s_old = [si_buf[slot, b, h] for h, b in pairs]
    r = [_mm(jnp.concatenate([w[h][rows[b]], qdec[h][rows[b]]], axis=0), s_old[n])
         for n, (h, b) in enumerate(pairs)]
    v_new = [u[h][rows[b]] - r[n][:SEQ_PAD] for n, (h, b) in enumerate(pairs)]
    upd = [_mm_tn(kdec[h][rows[b]], v_new[n]) for n, (h, b) in enumerate(pairs)]
    for n, (h, b) in enumerate(pairs):
        so_ref[b, h] = s_old[n] * s_scale[h][SEQ_PAD * b:SEQ_PAD * b + 1, :] + upd[n]
    for h in heads:
        mine = range(h * SEQ_PER_STEP, (h + 1) * SEQ_PER_STEP)
        o = (jnp.concatenate([r[n][SEQ_PAD:] for n in mine], axis=0)
             + _mm(qk[h], jnp.concatenate([v_new[n] for n in mine], axis=0)))
        lo = h * DN_DV
        ybuf_ref[:, D_CONV + lo:D_CONV + lo + DN_DV] = _rms(o, dnn) * _silu(z[:, lo:lo + DN_DV])
    for b in seqs:
        y_ref[SAMPLE_LEN * b:SAMPLE_LEN * (b + 1), :] = ybuf_ref[SEQ_PAD * b:SEQ_PAD * b + SAMPLE_LEN, :]


def _mix_sample(proj, ab, state_a, state_q, state_s, caw, cqw, alog, dtb, dnn):
    n_seq = state_s.shape[0]
    assert n_seq // SEQ_PER_STEP >= RING - 1
    full2 = lambda i: (0, 0)
    seq_rows = lambda width, col: pl.BlockSpec((SEQ_PER_STEP * SAMPLE_LEN, width), lambda i: (i, col))
    return pl.pallas_call(
        _mix_sample_kernel,
        grid=(n_seq // SEQ_PER_STEP,),
        in_specs=[
            seq_rows(3 * D_CONV, 0), seq_rows(DN_QKV, 1), seq_rows(DN_HEADS * DN_DV, Z_BLOCK), seq_rows(AB_LANES, 0),
            pl.BlockSpec((SEQ_PER_STEP, 2, D_CONV), lambda i: (i, 0, 0)),
            pl.BlockSpec((3, SEQ_PER_STEP, DN_QKV), lambda i: (0, i, 0)),
            pl.BlockSpec(memory_space=pl.ANY),
            pl.BlockSpec((3, D_CONV), full2),
            pl.BlockSpec((4, DN_QKV), full2),
            pl.BlockSpec((1, AB_LANES), full2),
            pl.BlockSpec((1, AB_LANES), full2),
            pl.BlockSpec((1, DN_DV), full2),
        ],
        out_specs=[
            seq_rows(D_MODEL, 0),
            pl.BlockSpec((SEQ_PER_STEP, 2, D_CONV), lambda i: (i, 0, 0)),
            pl.BlockSpec((3, SEQ_PER_STEP, DN_QKV), lambda i: (0, i, 0)),
            pl.BlockSpec((SEQ_PER_STEP, DN_HEADS, DN_DK, DN_DV), lambda i: (i, 0, 0, 0)),
        ],
        out_shape=[
            jax.ShapeDtypeStruct((n_seq * SAMPLE_LEN, D_MODEL), F32),
            jax.ShapeDtypeStruct((n_seq, 2, D_CONV), F32),
            jax.ShapeDtypeStruct((3, n_seq, DN_QKV), F32),
            jax.ShapeDtypeStruct((n_seq, DN_HEADS, DN_DK, DN_DV), F32),
        ],
        scratch_shapes=[
            pltpu.VMEM((CHUNK, D_CONV), F32),
            pltpu.VMEM((CHUNK, D_CONV), F32),
            pltpu.VMEM((CHUNK, DN_QKV), F32),
            pltpu.VMEM((CHUNK, DN_HEADS * DN_DV), F32),
            pltpu.VMEM((CHUNK, AB_LANES), F32),
            pltpu.VMEM((CHUNK, D_MODEL), F32),
            pltpu.VMEM((RING, SEQ_PER_STEP, DN_HEADS, DN_DK, DN_DV), F32),
            pltpu.SemaphoreType.DMA((RING,)),
        ],
        compiler_params=_params("arbitrary"),
        name="mix_sample",
    )(proj, proj, proj, ab, state_a, state_q, state_s, caw, cqw, alog, dtb, dnn)


def _outproj_kernel(ya_ref, yb_ref, wa_ref, wb_ref, post_ref, x_ref, o_ref):
    m = (jnp.dot(ya_ref[...].astype(BF16), wa_ref[...], preferred_element_type=F32)
         + jnp.dot(yb_ref[...].astype(BF16), wb_ref[...], preferred_element_type=F32))
    o_ref[...] = x_ref[...] + _rms(m, post_ref[...])


def _outproj(ya, yb, yb_block, w, post, x):
    m = x.shape[0]
    half = D_MODEL // 2
    return pl.pallas_call(
        _outproj_kernel,
        grid=(m // TM,),
        in_specs=[
            pl.BlockSpec((TM, half), lambda i: (i, 0)),
            pl.BlockSpec((TM, half), lambda i: (i, yb_block)),
            pl.BlockSpec((half, D_MODEL), lambda i: (0, 0)),
            pl.BlockSpec((half, D_MODEL), lambda i: (1, 0)),
            pl.BlockSpec((1, D_MODEL), lambda i: (0, 0)),
            pl.BlockSpec((TM, D_MODEL), lambda i: (i, 0)),
        ],
        out_specs=pl.BlockSpec((TM, D_MODEL), lambda i: (i, 0)),
        out_shape=jax.ShapeDtypeStruct((m, D_MODEL), F32),
        compiler_params=_params("parallel"),
        name="outproj",
    )(ya, yb, w, w, post, x)


def _ple_kernel(x_ref, p_ref, pre_ref, post_ref, wg_ref, wp_ref, o_ref):
    x = x_ref[...]
    h = _rms(x, pre_ref[...]).astype(BF16)
    gate = jax.nn.sigmoid(jnp.dot(h, wg_ref[...], preferred_element_type=F32))
    proj = jnp.dot(p_ref[...].astype(BF16), wp_ref[...], preferred_element_type=F32)
    o_ref[...] = x + _rms(gate * proj, post_ref[...])


def _ple(x, p, pre, post, wg, wp):
    m = x.shape[0]
    return pl.pallas_call(
        _ple_kernel,
        grid=(m // TM,),
        in_specs=[
            pl.BlockSpec((TM, D_MODEL), lambda i: (i, 0)),
            pl.BlockSpec((TM, D_PLE), lambda i: (i, 0)),
            pl.BlockSpec((1, D_MODEL), lambda i: (0, 0)),
            pl.BlockSpec((1, D_MODEL), lambda i: (0, 0)),
            pl.BlockSpec((D_MODEL, D_MODEL), lambda i: (0, 0)),
            pl.BlockSpec((D_PLE, D_MODEL), lambda i: (0, 0)),
        ],
        out_specs=pl.BlockSpec((TM, D_MODEL), lambda i: (i, 0)),
        out_shape=jax.ShapeDtypeStruct((m, D_MODEL), F32),
        compiler_params=_params("parallel"),
        name="ple",
    )(x, p, pre, post, wg, wp)


def _row(v):
    return v.reshape(1, -1).astype(F32)


def _pad_lanes(v):
    return jnp.pad(_row(v), ((0, 0), (0, AB_LANES - v.shape[-1])))


def _seq_tails(tails, tiles_per_seq, n_rows):
    return tails[tiles_per_seq - 1::tiles_per_seq, TAIL_ROWS - n_rows:, :]


def kernel(x_prompt, x_sample, state_conv_a, state_conv_qkv, state_delta, p_prompt, p_sample,
           f1_pre, f1_post, f1_wg, f1_wu, f1_wd,
           mix_pre, mix_post, w_in, conv_a_w, conv_qkv_w, a_log, dt_bias, dn_norm, w_out,
           f2_pre, f2_post, f2_wg, f2_wu, f2_wd,
           ple_pre, ple_post, w_ple_gate, w_ple_proj):
    depth = f1_pre.shape[0]
    n_p, seq_p, _ = x_prompt.shape
    n_s, seq_s, _ = x_sample.shape
    rows_p = n_p * seq_p
    rows_s = n_s * seq_s
    assert seq_p % TM_BIG == 0 and seq_s == SAMPLE_LEN and n_s % SEQ_PER_STEP == 0
    assert rows_s == TM
    assert w_in.shape[-1] == IN_MAIN + N_AB
    tiles_per_seq = seq_p // TM_BIG
    xp = x_prompt.reshape(rows_p, D_MODEL)
    xs = x_sample.reshape(rows_s, D_MODEL)
    outs = [[] for _ in range(6)]
    for i in range(depth):
        pre, post = _row(f1_pre[i]), _row(f1_post[i])
        xs, wg, wu, wd = _ffn(xs, pre, post, f1_wg[i], f1_wu[i], f1_wd[i], TM)
        xp = _ffn(xp, pre, post, wg, wu, wd, TM_BIG)[0]

        pre = _row(mix_pre[i])
        caw, cqw = conv_a_w[i], conv_qkv_w[i]
        proj_s, ab_s, wt, wabt = _inproj_cast(xs, pre, w_in[i].T)
        y_a, zs, h, tail_a = _proj_a(xp, pre, wt, caw, TM_BIG, tiles_per_seq)
        alog, dtb, dnn = _pad_lanes(a_log[i]), _pad_lanes(dt_bias[i]), _row(dn_norm[i])
        wy_w, qdec, kdec, wy_u, qk, egl, tail_q = _proj_qkv(h, wt, wabt, cqw, alog, dtb, seq_p // TM)
        y_b, s_p = _delta_scan(wy_w, qdec, kdec, wy_u, qk, egl, zs, dnn, n_p, seq_p)
        ca_p = _seq_tails(tail_a, tiles_per_seq, HIST_A)
        cq_p = _seq_tails(tail_q, seq_p // TM, HIST_QKV)
        y_s, ca_s, cq_s, s_s = _mix_sample(
            proj_s, ab_s, state_conv_a[i], jnp.swapaxes(state_conv_qkv[i], 0, 1), state_delta[i],
            caw, cqw, alog, dtb, dnn)
        cq_s = jnp.swapaxes(cq_s, 0, 1)

        post = _row(mix_post[i])
        w_out_bf16 = w_out[i].astype(BF16)
        xs = _outproj(y_s, y_s, 1, w_out_bf16, post, xs)
        xp = _outproj(y_a, y_b, 0, w_out_bf16, post, xp)

        pre, post = _row(f2_pre[i]), _row(f2_post[i])
        xs, wg, wu, wd = _ffn(xs, pre, post, f2_wg[i], f2_wu[i], f2_wd[i], TM)
        xp = _ffn(xp, pre, post, wg, wu, wd, TM_BIG)[0]

        pre, post = _row(ple_pre[i]), _row(ple_post[i])
        wg, wp = w_ple_gate[i].astype(BF16), w_ple_proj[i].astype(BF16)
        xs = _ple(xs, p_sample[i].reshape(rows_s, D_PLE), pre, post, wg, wp)
        xp = _ple(xp, p_prompt[i].reshape(rows_p, D_PLE), pre, post, wg, wp)
        for lst, val in zip(outs, (ca_p, cq_p, s_p, ca_s, cq_s, s_s)):
            lst.append(val)
    return (xp.reshape(n_p, seq_p, D_MODEL), xs.reshape(n_s, seq_s, D_MODEL),
            *[jnp.stack(lst) for lst in outs])
```

```python
import functools

import jax
import jax.numpy as jnp
from jax import lax
from jax.experimental import pallas as pl
from jax.experimental.pallas import tpu as pltpu

F32 = jnp.float32
BF16 = jnp.bfloat16

D_MODEL = 2048
D_CONV = 1024
DN_HEADS = 8
DN_DK = 128
DN_DV = 128
DN_QKV = 3072
D_FF = 5632
D_PLE = 256
EPS = 1e-6
CHUNK = 64
SAMPLE_LEN = 4
SEQ_PAD = 8
SEQ_SHIFT = 3
SEQ_PER_STEP = CHUNK // SEQ_PAD
IN_MAIN = 7168
QKV_BLOCK = 3
Z_BLOCK = 6
HIST_A = 2
HIST_QKV = 3
TAIL_ROWS = 8
N_AB = 16
AB_LANES = 128

TM = 512
TF = 512
TF_CAST = 256
RING = 3
RING_PRIORITY = 1
TN_IN = 1024
TM_BIG = 1024
TN_A = 512
SUB_COLS = 512
SCAN_ROWS = 256
ROW_CHUNK = 32
VMEM_LIMIT = 56 * 1024 * 1024


def _rms(x, g):
    ms = jnp.mean(x * x, axis=-1, keepdims=True)
    return x * lax.rsqrt(ms + EPS) * g


def _silu(x):
    return x * jax.nn.sigmoid(x)


def _mm(a, b):
    return jnp.dot(a.astype(BF16), b.astype(BF16), preferred_element_type=F32)


def _mm_nt(a, b):
    return lax.dot_general(a.astype(BF16), b.astype(BF16), (((1,), (1,)), ((), ())),
                           preferred_element_type=F32)


def _mm_tn(a, b):
    return jnp.dot(a.T.astype(BF16), b.astype(BF16), preferred_element_type=F32)


def _mm_exact(a, b):
    return jnp.dot(a, b, precision=lax.Precision.HIGHEST, preferred_element_type=F32)


def _nt_dot(a, bt):
    return lax.dot_general(a, bt, (((1,), (1,)), ((), ())), preferred_element_type=F32)


def _for_row_chunks(n_rows, body):
    def step(k, carry):
        body(pl.ds(pl.multiple_of(k * ROW_CHUNK, ROW_CHUNK), ROW_CHUNK))
        return carry
    lax.fori_loop(0, n_rows // ROW_CHUNK, step, 0, unroll=4)


def _params(*semantics):
    return pltpu.CompilerParams(dimension_semantics=semantics, vmem_limit_bytes=VMEM_LIMIT)


def _ffn_kernel(emit_bf16, x_ref, pre_ref, post_ref, wg_ref, wu_ref, wd_ref, o_ref, *rest):
    j = pl.program_id(1)
    if emit_bf16:
        wg16_ref, wu16_ref, wd16_ref, h_ref, wg_buf, wu_buf, wd_buf, sem = rest
        n_tiles = pl.num_programs(1)
        tf = wg_buf.shape[2]

        def tile_copies(t):
            start = t * tf if isinstance(t, int) else pl.multiple_of(t * tf, tf)
            slot = t % RING
            return (pltpu.make_async_copy(wg_ref.at[:, pl.ds(start, tf)], wg_buf.at[slot], sem.at[0, slot]),
                    pltpu.make_async_copy(wu_ref.at[:, pl.ds(start, tf)], wu_buf.at[slot], sem.at[1, slot]),
                    pltpu.make_async_copy(wd_ref.at[pl.ds(start, tf), :], wd_buf.at[slot], sem.at[2, slot]))

        @pl.when(j == 0)
        def _():
            for t in range(RING - 1):
                for copy in tile_copies(t):
                    copy.start(priority=RING_PRIORITY)

        @pl.when(j + (RING - 1) < n_tiles)
        def _():
            for copy in tile_copies(j + (RING - 1)):
                copy.start(priority=RING_PRIORITY)
    else:
        h_ref = rest[-1]

    @pl.when(j == 0)
    def _():
        def norm_rows(rows):
            h_ref[rows, :] = _rms(x_ref[rows, :], pre_ref[...]).astype(BF16)
            o_ref[rows, :] = jnp.zeros((ROW_CHUNK, D_MODEL), F32)
        _for_row_chunks(x_ref.shape[0], norm_rows)

    if emit_bf16:
        for copy in tile_copies(j):
            copy.wait()
        slot = j % RING
        wg, wu, wd = wg_buf[slot].astype(BF16), wu_buf[slot].astype(BF16), wd_buf[slot].astype(BF16)
        wg16_ref[...] = wg
        wu16_ref[...] = wu
        wd16_ref[...] = wd
    else:
        wg, wu, wd = wg_ref[...], wu_ref[...], wd_ref[...]
    h = h_ref[...]
    g = jnp.dot(h, wg, preferred_element_type=F32)
    u = jnp.dot(h, wu, preferred_element_type=F32)
    a = (_silu(g) * u).astype(BF16)
    o_ref[...] += jnp.dot(a, wd, preferred_element_type=F32)

    @pl.when(j == pl.num_programs(1) - 1)
    def _():
        group = 4 * ROW_CHUNK

        def finish_rows(k, carry):
            r0 = pl.multiple_of(k * group, group)
            chunks = [pl.ds(r0 + s * ROW_CHUNK, ROW_CHUNK) for s in range(group // ROW_CHUNK)]
            scale = [0.5 * lax.rsqrt(jnp.mean(jnp.square(o_ref[c, :]), axis=-1, keepdims=True) + EPS)
                     for c in chunks]
            for c, sc in zip(chunks, scale):
                o_ref[c, :] = x_ref[c, :] + o_ref[c, :] * sc * post_ref[...]
            return carry
        lax.fori_loop(0, x_ref.shape[0] // group, finish_rows, 0)


def _ffn(x, pre, post, wg, wu, wd, tm):
    m = x.shape[0]
    emit_bf16 = wg.dtype != BF16
    tf = TF_CAST if emit_bf16 else TF
    up_spec = pl.BlockSpec((D_MODEL, tf), lambda i, j: (0, j))
    down_spec = pl.BlockSpec((tf, D_MODEL), lambda i, j: (j, 0))
    out_specs = [pl.BlockSpec((tm, D_MODEL), lambda i, j: (i, 0))]
    out_shape = [jax.ShapeDtypeStruct((m, D_MODEL), F32)]
    w_specs = [up_spec, up_spec, down_spec]
    scratch = [pltpu.VMEM((tm, D_MODEL), BF16)]
    if emit_bf16:
        assert m == tm, "weights are re-emitted once, by a single-row-tile call"
        assert D_FF // tf >= RING - 1
        out_specs += w_specs
        out_shape += [jax.ShapeDtypeStruct(w.shape, BF16) for w in (wg, wu, wd)]
        w_specs = [pl.BlockSpec(memory_space=pl.ANY)] * 3
        scratch += [pltpu.VMEM((RING, D_MODEL, tf), F32), pltpu.VMEM((RING, D_MODEL, tf), F32),
                    pltpu.VMEM((RING, tf, D_MODEL), F32), pltpu.SemaphoreType.DMA((3, RING))]
    outs = pl.pallas_call(
        functools.partial(_ffn_kernel, emit_bf16),
        grid=(m // tm, D_FF // tf),
        in_specs=[
            pl.BlockSpec((tm, D_MODEL), lambda i, j: (i, 0)),
            pl.BlockSpec((1, D_MODEL), lambda i, j: (0, 0)),
            pl.BlockSpec((1, D_MODEL), lambda i, j: (0, 0)),
            *w_specs,
        ],
        out_specs=out_specs,
        out_shape=out_shape,
        scratch_shapes=scratch,
        compiler_params=_params("arbitrary", "arbitrary"),
        name="ffn_cast" if emit_bf16 else "ffn",
    )(x, pre, post, wg, wu, wd)
    return outs if emit_bf16 else (outs[0], wg, wu, wd)


def _inproj_cast_kernel(x_ref, pre_ref, wt_ref, wabt_ref, o_ref, ab_ref, wt16_ref, wabt16_ref,
                        h_ref, wt_buf, sem):
    j = pl.program_id(1)

    def tile_copy(t):
        start = t * TN_IN if isinstance(t, int) else pl.multiple_of(t * TN_IN, TN_IN)
        slot = t % RING
        return pltpu.make_async_copy(wt_ref.at[pl.ds(start, TN_IN), :], wt_buf.at[slot], sem.at[slot])

    @pl.when(j == 0)
    def _():
        for t in range(RING - 1):
            tile_copy(t).start(priority=RING_PRIORITY)

    @pl.when(j + (RING - 1) < pl.num_programs(1))
    def _():
        tile_copy(j + (RING - 1)).start(priority=RING_PRIORITY)

    @pl.when(j == 0)
    def _():
        h = _rms(x_ref[...], pre_ref[...]).astype(BF16)
        h_ref[...] = h
        wabt = jnp.concatenate(
            [wabt_ref[...].astype(BF16), jnp.zeros((AB_LANES - N_AB, D_MODEL), BF16)], axis=0)
        wabt16_ref[...] = wabt
        ab_ref[...] = _nt_dot(h, wabt)

    tile_copy(j).wait()
    wt = wt_buf[j % RING].astype(BF16)
    wt16_ref[...] = wt
    o_ref[...] = _nt_dot(h_ref[...], wt)


def _inproj_cast(x, pre, wt):
    m = x.shape[0]
    assert m == TM, "weights are re-emitted once, by a single-row-tile call"
    assert IN_MAIN // TN_IN >= RING - 1
    wt_spec = pl.BlockSpec((TN_IN, D_MODEL), lambda i, j: (j, 0))
    return pl.pallas_call(
        _inproj_cast_kernel,
        grid=(1, IN_MAIN // TN_IN),
        in_specs=[
            pl.BlockSpec((TM, D_MODEL), lambda i, j: (0, 0)),
            pl.BlockSpec((1, D_MODEL), lambda i, j: (0, 0)),
            pl.BlockSpec(memory_space=pl.ANY),
            pl.BlockSpec((N_AB, D_MODEL), lambda i, j: (IN_MAIN // N_AB, 0)),
        ],
        out_specs=[
            pl.BlockSpec((TM, TN_IN), lambda i, j: (0, j)),
            pl.BlockSpec((TM, AB_LANES), lambda i, j: (0, 0)),
            wt_spec,
            pl.BlockSpec((AB_LANES, D_MODEL), lambda i, j: (0, 0)),
        ],
        out_shape=[
            jax.ShapeDtypeStruct((m, IN_MAIN), F32),
            jax.ShapeDtypeStruct((m, AB_LANES), F32),
            jax.ShapeDtypeStruct((IN_MAIN, D_MODEL), BF16),
            jax.ShapeDtypeStruct((AB_LANES, D_MODEL), BF16),
        ],
        scratch_shapes=[pltpu.VMEM((TM, D_MODEL), BF16), pltpu.VMEM((RING, TN_IN, D_MODEL), F32),
                        pltpu.SemaphoreType.DMA((RING,))],
        compiler_params=_params("arbitrary", "arbitrary"),
        name="inproj_cast",
    )(x, pre, wt, wt)


def _causal_conv(x, w, carry_in):
    width = w.shape[0]
    rows = x.shape[0]
    is_row0 = lax.broadcasted_iota(jnp.int32, (8, 1), 0) == 0
    acc = x * w[0:1, :]
    carry_out = []
    for j in range(1, width):
        delayed = pltpu.roll(acc, 1, axis=0)
        if carry_in is not None:
            carry_out.append(acc[rows - 1:rows, :])
            head = jnp.where(is_row0, carry_in[j - 1], delayed[:8])
            delayed = jnp.concatenate([head, delayed[8:]], axis=0)
        acc = x * w[j:j + 1, :] + delayed
    return acc, carry_out


def _load_carry(carry_ref, j, cols, n, seq_start):
    return [jnp.where(seq_start, 0.0, carry_ref[j, r:r + 1, cols]) for r in range(n)]


def _store_carry(carry_ref, j, cols, rows):
    for r, row in enumerate(rows):
        carry_ref[j, r:r + 1, cols] = row


def _proj_a_kernel(tiles_per_seq, x_ref, pre_ref, wgb_ref, wgcv_ref, whc_ref, wz_ref, caw_ref,
                   ya_ref, zs_ref, h_ref, tail_ref, carry_ref):
    i, j = pl.program_id(0), pl.program_id(1)
    tm = x_ref.shape[0]

    @pl.when(j == 0)
    def _():
        def norm_rows(rows):
            h_ref[rows, :] = _rms(x_ref[rows, :], pre_ref[...]).astype(BF16)
        _for_row_chunks(tm, norm_rows)

    @pl.when((i == 0) & (j == 0))
    def _():
        carry_ref[...] = jnp.zeros_like(carry_ref)

    seq_start = i % tiles_per_seq == 0
    h = h_ref[...]
    for c in range(TN_A // SUB_COLS):
        cols = slice(c * SUB_COLS, (c + 1) * SUB_COLS)
        u = _nt_dot(h, wgcv_ref[cols, :]) * _nt_dot(h, whc_ref[cols, :])
        tail_ref[0, :, cols] = u[tm - TAIL_ROWS:tm, :]
        conv, carry = _causal_conv(u, caw_ref[:, cols], _load_carry(carry_ref, j, cols, HIST_A, seq_start))
        _store_carry(carry_ref, j, cols, carry)
        ya_ref[:, cols] = (_nt_dot(h, wgb_ref[cols, :]) * conv).astype(BF16)
        zs_ref[:, cols] = _silu(_nt_dot(h, wz_ref[cols, :])).astype(BF16)


def _proj_a(x, pre, wt, caw, tm, tiles_per_seq):
    m = x.shape[0]
    n_col = D_CONV // TN_A
    assert DN_HEADS * DN_DV == D_CONV
    z_part = Z_BLOCK * TN_IN // D_CONV
    w_spec = lambda part: pl.BlockSpec((TN_A, D_MODEL), lambda i, j: (part * n_col + j, 0))
    cols = pl.BlockSpec((tm, TN_A), lambda i, j: (i, j))
    return pl.pallas_call(
        functools.partial(_proj_a_kernel, tiles_per_seq),
        grid=(m // tm, n_col),
        in_specs=[
            pl.BlockSpec((tm, D_MODEL), lambda i, j: (i, 0)),
            pl.BlockSpec((1, D_MODEL), lambda i, j: (0, 0)),
            w_spec(0), w_spec(1), w_spec(2), w_spec(z_part),
            pl.BlockSpec((3, TN_A), lambda i, j: (0, j)),
        ],
        out_specs=[
            cols, cols,
            pl.BlockSpec((tm, D_MODEL), lambda i, j: (i, 0)),
            pl.BlockSpec((1, TAIL_ROWS, TN_A), lambda i, j: (i, 0, j)),
        ],
        out_shape=[
            jax.ShapeDtypeStruct((m, D_CONV), BF16),
            jax.ShapeDtypeStruct((m, DN_HEADS * DN_DV), BF16),
            jax.ShapeDtypeStruct((m, D_MODEL), BF16),
            jax.ShapeDtypeStruct((m // tm, TAIL_ROWS, D_CONV), F32),
        ],
        scratch_shapes=[pltpu.VMEM((n_col, HIST_A, TN_A), F32)],
        compiler_params=_params("arbitrary", "arbitrary"),
        name="proj_a",
    )(x, pre, wt, wt, wt, wt, caw)


def _l2_normalize(x, scale):
    return x * (lax.rsqrt(jnp.sum(x * x, axis=-1, keepdims=True) + EPS) * scale)


def _gate_columns(beta_all, gc_all):
    heads = range(DN_HEADS)
    gc_t = gc_all.T
    return ([beta_all[:, DN_HEADS + h:DN_HEADS + h + 1] for h in heads],
            [gc_all[:, h:h + 1] for h in heads], [gc_t[h:h + 1, :] for h in heads])


def _heads_intra(qn, kn, vh, beta_col, gc_col, gc_row, causal, strict, eye, squarings):
    heads = range(len(qn))
    decay = [jnp.exp(jnp.where(causal, gc_col[h] - gc_row[h], -jnp.inf)) for h in heads]
    kb = [kn[h] * beta_col[h] for h in heads]
    gram = [_mm_nt(jnp.concatenate([kb[h], qn[h]], axis=0), kn[h]) for h in heads]
    a_mat = [jnp.where(strict, gram[h][:CHUNK] * decay[h], 0.0) for h in heads]
    qk = [jnp.where(causal, gram[h][CHUNK:] * decay[h], 0.0) for h in heads]
    inv = [eye - a for a in a_mat]
    power = a_mat
    for _ in range(squarings):
        power = [_mm(x, x) for x in power]
        inv = [_mm(inv[h], eye + power[h]) for h in heads]
    egc = [jnp.exp(x) for x in gc_col]
    wu = [_mm(inv[h], jnp.concatenate([kb[h] * egc[h], vh[h] * beta_col[h]], axis=1)) for h in heads]
    w = [x[:, :DN_DK] for x in wu]
    u = [x[:, DN_DK:] for x in wu]
    qdec = [qn[h] * egc[h] for h in heads]
    return w, u, qdec, qk


def _gates(ab, alog, dtb):
    x = ab + dtb
    softplus = jnp.maximum(x, 0.0) + jnp.log1p(jnp.exp(-jnp.abs(x)))
    return -jnp.exp(alog) * softplus, jax.nn.sigmoid(ab)


def _masks(same_seq):
    row = lax.broadcasted_iota(jnp.int32, (CHUNK, CHUNK), 0)
    col = lax.broadcasted_iota(jnp.int32, (CHUNK, CHUNK), 1)
    causal = row >= col
    strict = row > col
    if same_seq:
        same = (row >> SEQ_SHIFT) == (col >> SEQ_SHIFT)
        causal = causal & same
        strict = strict & same
    eye = jnp.where(row == col, 1.0, 0.0).astype(F32)
    return row, col, causal, strict, causal.astype(F32), eye


def _proj_qkv_kernel(tiles_per_seq, h_ref, wq_ref, wk_ref, wv_ref, wabt_ref, cqw_ref, alog_ref, dtb_ref,
                     w_ref, qd_ref, kd_ref, u_ref, qk_ref, egl_ref, tail_ref, carry_ref, qkv_ref):
    i = pl.program_id(0)
    tm = h_ref.shape[0]

    @pl.when(i == 0)
    def _():
        carry_ref[...] = jnp.zeros_like(carry_ref)

    seq_start = i % tiles_per_seq == 0
    h = h_ref[...]
    g_all, beta_all = _gates(_nt_dot(h, wabt_ref[...]), alog_ref[...], dtb_ref[...])
    for t, wt_ref in enumerate((wq_ref, wk_ref, wv_ref)):
        for c in range(TN_IN // SUB_COLS):
            cols = slice(c * SUB_COLS, (c + 1) * SUB_COLS)
            lo = t * TN_IN + c * SUB_COLS
            p = _nt_dot(h, wt_ref[cols, :])
            tail_ref[0, :, lo:lo + SUB_COLS] = p[tm - TAIL_ROWS:tm, :]
            p, carry = _causal_conv(p, cqw_ref[:, lo:lo + SUB_COLS],
                                    _load_carry(carry_ref, t, cols, HIST_QKV, seq_start))
            _store_carry(carry_ref, t, cols, carry)
            p = _silu(p)
            if t == 2:
                qkv_ref[:, lo:lo + SUB_COLS] = p
            else:
                for k in range(SUB_COLS // DN_DK):
                    blk = p[:, k * DN_DK:(k + 1) * DN_DK]
                    qkv_ref[:, lo + k * DN_DK:lo + (k + 1) * DN_DK] = _l2_normalize(
                        blk, DN_DK ** -0.5 if t == 0 else 1.0)

    heads = range(DN_HEADS)
    _, _, causal, strict, low_f, eye = _masks(False)
    chunk_rows = [slice(g * CHUNK, (g + 1) * CHUNK) for g in range(tm // CHUNK)]
    qn, kn, vh, beta_col, gc_col, gc_row = [], [], [], [], [], []
    for rows in chunk_rows:
        gc_all = _mm_exact(low_f, g_all[rows])
        egl_ref[rows, :] = jnp.broadcast_to(jnp.exp(gc_all[CHUNK - 1:CHUNK, :]), (CHUNK, AB_LANES))
        cols = _gate_columns(beta_all[rows], gc_all)
        beta_col += cols[0]
        gc_col += cols[1]
        gc_row += cols[2]
        qn += [qkv_ref[rows, h * DN_DK:(h + 1) * DN_DK] for h in heads]
        kn += [qkv_ref[rows, (DN_HEADS + h) * DN_DK:(DN_HEADS + h + 1) * DN_DK] for h in heads]
        vh += [qkv_ref[rows, (2 * DN_HEADS + h) * DN_DV:(2 * DN_HEADS + h + 1) * DN_DV] for h in heads]
    w, u, qdec, qk = _heads_intra(qn, kn, vh, beta_col, gc_col, gc_row, causal, strict, eye, 5)
    for n in range(len(w)):
        rows, hd = chunk_rows[n // DN_HEADS], n % DN_HEADS
        lo = hd * DN_DK
        g_last = gc_col[n][CHUNK - 1:CHUNK, :]
        w_ref[rows, lo:lo + DN_DK] = w[n].astype(BF16)
        qd_ref[rows, lo:lo + DN_DK] = qdec[n].astype(BF16)
        kd_ref[rows, lo:lo + DN_DK] = (kn[n] * jnp.exp(g_last - gc_col[n])).astype(BF16)
        u_ref[rows, lo:lo + DN_DV] = u[n]
        qk_ref[rows, hd * CHUNK:(hd + 1) * CHUNK] = qk[n].astype(BF16)


def _proj_qkv(h, wt, wabt, cqw, alog, dtb, tiles_per_seq):
    m = h.shape[0]
    hk, hv = DN_HEADS * DN_DK, DN_HEADS * DN_DV
    rows = lambda width: pl.BlockSpec((TM, width), lambda i: (i, 0))
    whole = lambda shape: pl.BlockSpec(shape, lambda i: (0,) * len(shape))
    w_tile = lambda t: pl.BlockSpec((TN_IN, D_MODEL), lambda i: (QKV_BLOCK + t, 0))
    return pl.pallas_call(
        functools.partial(_proj_qkv_kernel, tiles_per_seq),
        grid=(m // TM,),
        in_specs=[rows(D_MODEL), w_tile(0), w_tile(1), w_tile(2), whole((AB_LANES, D_MODEL)),
                  whole((4, DN_QKV)), whole((1, AB_LANES)), whole((1, AB_LANES))],
        out_specs=[rows(hk), rows(hk), rows(hk), rows(hv), rows(DN_HEADS * CHUNK), rows(AB_LANES),
                   pl.BlockSpec((1, TAIL_ROWS, DN_QKV), lambda i: (i, 0, 0))],
        out_shape=[
            jax.ShapeDtypeStruct((m, hk), BF16), jax.ShapeDtypeStruct((m, hk), BF16),
            jax.ShapeDtypeStruct((m, hk), BF16), jax.ShapeDtypeStruct((m, hv), F32),
            jax.ShapeDtypeStruct((m, DN_HEADS * CHUNK), BF16), jax.ShapeDtypeStruct((m, AB_LANES), F32),
            jax.ShapeDtypeStruct((m // TM, TAIL_ROWS, DN_QKV), F32),
        ],
        scratch_shapes=[pltpu.VMEM((3, HIST_QKV, TN_IN), F32), pltpu.VMEM((TM, DN_QKV), F32)],
        compiler_params=_params("arbitrary"),
        name="proj_qkv",
    )(h, wt, wt, wt, wabt, cqw, alog, dtb)


def _delta_scan_kernel(w_ref, qd_ref, kd_ref, u_ref, qk_ref, egl_ref, zs_ref, dnn_ref,
                       y_ref, so_ref, s_ref):
    c = pl.program_id(0)

    @pl.when(c == 0)
    def _():
        s_ref[...] = jnp.zeros_like(s_ref)

    n_seq = w_ref.shape[0]
    units = [(b, h) for b in range(n_seq) for h in range(DN_HEADS)]
    col = [slice(h * DN_DK, (h + 1) * DN_DK) for h in range(DN_HEADS)]
    dnn = dnn_ref[...]
    s = [s_ref[b, h] for b, h in units]
    for g in range(w_ref.shape[1] // CHUNK):
        rows = slice(g * CHUNK, (g + 1) * CHUNK)
        lhs = [jnp.concatenate([w_ref[b, rows, col[h]], qd_ref[b, rows, col[h]]], axis=0) for b, h in units]
        r = [jnp.dot(lhs[n], s[n].astype(BF16), preferred_element_type=F32) for n in range(len(units))]
        v_new = [(u_ref[b, rows, col[h]] - r[n][:CHUNK]).astype(BF16) for n, (b, h) in enumerate(units)]
        upd = [lax.dot_general(kd_ref[b, rows, col[h]], v_new[n], (((0,), (0,)), ((), ())),
                               preferred_element_type=F32) for n, (b, h) in enumerate(units)]
        s = [s[n] * egl_ref[b, g * CHUNK:g * CHUNK + 1, h:h + 1] + upd[n] for n, (b, h) in enumerate(units)]
        o = [r[n][CHUNK:] + jnp.dot(qk_ref[b, rows, h * CHUNK:(h + 1) * CHUNK], v_new[n],
                                    preferred_element_type=F32) for n, (b, h) in enumerate(units)]
        for n, (b, h) in enumerate(units):
            y_ref[b, rows, col[h]] = (_rms(o[n], dnn) * zs_ref[b, rows, col[h]]).astype(y_ref.dtype)
    for n, (b, h) in enumerate(units):
        s_ref[b, h] = s[n]

    @pl.when(c == pl.num_programs(0) - 1)
    def _():
        so_ref[...] = s_ref[...]


def _delta_scan(w, qd, kd, u, qk, egl, zs, dnn, n_seq, seq_len):
    hk, hv = DN_HEADS * DN_DK, DN_HEADS * DN_DV
    per_seq = lambda a: a.reshape(n_seq, seq_len, a.shape[-1])
    rows = lambda width: pl.BlockSpec((n_seq, SCAN_ROWS, width), lambda c: (0, c, 0))
    state = pl.BlockSpec((n_seq, DN_HEADS, DN_DK, DN_DV), lambda c: (0, 0, 0, 0))
    y, s = pl.pallas_call(
        _delta_scan_kernel,
        grid=(seq_len // SCAN_ROWS,),
        in_specs=[rows(hk), rows(hk), rows(hk), rows(hv), rows(DN_HEADS * CHUNK), rows(AB_LANES), rows(hv),
                  pl.BlockSpec((1, DN_DV), lambda c: (0, 0))],
        out_specs=[rows(hv), state],
        out_shape=[jax.ShapeDtypeStruct((n_seq, seq_len, hv), BF16),
                   jax.ShapeDtypeStruct((n_seq, DN_HEADS, DN_DK, DN_DV), F32)],
        scratch_shapes=[pltpu.VMEM((n_seq, DN_HEADS, DN_DK, DN_DV), F32)],
        compiler_params=_params("arbitrary"),
        name="delta_scan",
    )(*map(per_seq, (w, qd, kd, u, qk, egl, zs)), dnn)
    return y.reshape(n_seq * seq_len, hv), s


def _mix_sample_kernel(pa_ref, pq_ref, pz_ref, pab_ref, sa_ref, sq_ref, si_ref,
                       caw_ref, cqw_ref, alog_ref, dtb_ref, dnn_ref,
                       y_ref, ca_ref, cq_ref, so_ref,
                       gb_ref, ua_ref, xq_ref, z_ref, ab_ref, ybuf_ref, si_buf, sem):
    step = pl.program_id(0)

    def state_copy(t):
        start = t * SEQ_PER_STEP if isinstance(t, int) else pl.multiple_of(t * SEQ_PER_STEP, SEQ_PER_STEP)
        slot = t % RING
        return pltpu.make_async_copy(si_ref.at[pl.ds(start, SEQ_PER_STEP)], si_buf.at[slot], sem.at[slot])

    @pl.when(step == 0)
    def _():
        for t in range(RING - 1):
            state_copy(t).start(priority=RING_PRIORITY)

    @pl.when(step + (RING - 1) < pl.num_programs(0))
    def _():
        state_copy(step + (RING - 1)).start(priority=RING_PRIORITY)

    seqs = range(SEQ_PER_STEP)
    pad = SEQ_PAD - SAMPLE_LEN
    for b in seqs:
        r0 = SEQ_PAD * b
        nxt = (b + 1) % SEQ_PER_STEP
        src = slice(SAMPLE_LEN * b, SAMPLE_LEN * (b + 1))
        pa = pa_ref[src, :]
        gb_ref[r0:r0 + SAMPLE_LEN, :] = pa[:, :D_CONV]
        ua_ref[r0:r0 + SAMPLE_LEN, :] = pa[:, D_CONV:2 * D_CONV] * pa[:, 2 * D_CONV:]
        xq_ref[r0:r0 + SAMPLE_LEN, :] = pq_ref[src, :]
        z_ref[r0:r0 + SAMPLE_LEN, :] = pz_ref[src, :]
        ab_ref[r0:r0 + SAMPLE_LEN, :] = pab_ref[src, :]
        for ref in (gb_ref, ua_ref, xq_ref, z_ref, ab_ref):
            ref[r0 + SAMPLE_LEN:r0 + SEQ_PAD, :] = jnp.zeros((pad, ref.shape[1]), F32)
        ua_ref[r0 + SEQ_PAD - HIST_A:r0 + SEQ_PAD, :] = sa_ref[nxt]
        for r in range(HIST_QKV):
            dst = r0 + SEQ_PAD - HIST_QKV + r
            xq_ref[dst:dst + 1, :] = sq_ref[r, nxt:nxt + 1, :]
    ybuf_ref[:, :D_CONV] = gb_ref[...] * _causal_conv(ua_ref[...], caw_ref[...], None)[0]
    row, col, causal, strict, low_f, eye = _masks(True)
    valid = (row[:, 0:1] & (SEQ_PAD - 1)) < SAMPLE_LEN
    cqkv = jnp.where(valid, _silu(_causal_conv(xq_ref[...], cqw_ref[...], None)[0]), 0.0)
    for b in seqs:
        r0 = SEQ_PAD * b
        ca_ref[b] = ua_ref[r0 + SAMPLE_LEN - HIST_A:r0 + SAMPLE_LEN, :]
        for r in range(HIST_QKV):
            src = r0 + SAMPLE_LEN - HIST_QKV + r
            cq_ref[r, b:b + 1, :] = xq_ref[src:src + 1, :]

    g_all, beta_all = _gates(ab_ref[...], alog_ref[...], dtb_ref[...])
    g_all = jnp.where(valid, g_all, 0.0)
    beta_all = jnp.where(valid, beta_all, 0.0)
    gc_all = _mm_exact(low_f, g_all)
    last_sel = (col == (row | (SEQ_PAD - 1))).astype(F32)
    gl_all = _mm_exact(last_sel, gc_all)
    z = z_ref[...]
    dnn = dnn_ref[...]
    heads = range(DN_HEADS)
    qn = [_l2_normalize(cqkv[:, h * DN_DK:(h + 1) * DN_DK], DN_DK ** -0.5) for h in heads]
    kn = [_l2_normalize(cqkv[:, (DN_HEADS + h) * DN_DK:(DN_HEADS + h + 1) * DN_DK], 1.0) for h in heads]
    vh = [cqkv[:, (2 * DN_HEADS + h) * DN_DV:(2 * DN_HEADS + h + 1) * DN_DV] for h in heads]
    beta_col, gc_col, gc_row = _gate_columns(beta_all, gc_all)
    w, u, qdec, qk = _heads_intra(qn, kn, vh, beta_col, gc_col, gc_row, causal, strict, eye, 1)
    gl_col = [gl_all[:, h:h + 1] for h in heads]
    kdec = [kn[h] * jnp.exp(gl_col[h] - gc_col[h]) for h in heads]
    s_scale = [jnp.exp(x) for x in gl_col]
    rows = [slice(SEQ_PAD * b, SEQ_PAD * (b + 1)) for b in seqs]
    pairs = [(h, b) for h in heads for b in seqs]
    state_copy(step).wait()
    slot = step % RING
    s_old = [si_buf[slot, b, h] for h, b in pairs]
    r = [_mm(jnp.concatenate([w[h][rows[b]], qdec[h][rows[b]]], axis=0), s_old[n])
         for n, (h, b) in enumerate(pairs)]
    v_new = [u[h][rows[b]] - r[n][:SEQ_PAD] for n, (h, b) in enumerate(pairs)]
    upd = [_mm_tn(kdec[h][rows[b]], v_new[n]) for n, (h, b) in enumerate(pairs)]
    for n, (h, b) in enumerate(pairs):
        so_ref[b, h] = s_old[n] * s_scale[h][SEQ_PAD * b:SEQ_PAD * b + 1, :] + upd[n]
    for h in heads:
        mine = range(h * SEQ_PER_STEP, (h + 1) * SEQ_PER_STEP)
        o = (jnp.concatenate([r[n][SEQ_PAD:] for n in mine], axis=0)
             + _mm(qk[h], jnp.concatenate([v_new[n] for n in mine], axis=0)))
        lo = h * DN_DV
        ybuf_ref[:, D_CONV + lo:D_CONV + lo + DN_DV] = _rms(o, dnn) * _silu(z[:, lo:lo + DN_DV])
    for b in seqs:
        y_ref[SAMPLE_LEN * b:SAMPLE_LEN * (b + 1), :] = ybuf_ref[SEQ_PAD * b:SEQ_PAD * b + SAMPLE_LEN, :]


def _mix_sample(proj, ab, state_a, state_q, state_s, caw, cqw, alog, dtb, dnn):
    n_seq = state_s.shape[0]
    assert n_seq // SEQ_PER_STEP >= RING - 1
    full2 = lambda i: (0, 0)
    seq_rows = lambda width, col: pl.BlockSpec((SEQ_PER_STEP * SAMPLE_LEN, width), lambda i: (i, col))
    return pl.pallas_call(
        _mix_sample_kernel,
        grid=(n_seq // SEQ_PER_STEP,),
        in_specs=[
            seq_rows(3 * D_CONV, 0), seq_rows(DN_QKV, 1), seq_rows(DN_HEADS * DN_DV, Z_BLOCK), seq_rows(AB_LANES, 0),
            pl.BlockSpec((SEQ_PER_STEP, 2, D_CONV), lambda i: (i, 0, 0)),
            pl.BlockSpec((3, SEQ_PER_STEP, DN_QKV), lambda i: (0, i, 0)),
            pl.BlockSpec(memory_space=pl.ANY),
            pl.BlockSpec((3, D_CONV), full2),
            pl.BlockSpec((4, DN_QKV), full2),
            pl.BlockSpec((1, AB_LANES), full2),
            pl.BlockSpec((1, AB_LANES), full2),
            pl.BlockSpec((1, DN_DV), full2),
        ],
        out_specs=[
            seq_rows(D_MODEL, 0),
            pl.BlockSpec((SEQ_PER_STEP, 2, D_CONV), lambda i: (i, 0, 0)),
            pl.BlockSpec((3, SEQ_PER_STEP, DN_QKV), lambda i: (0, i, 0)),
            pl.BlockSpec((SEQ_PER_STEP, DN_HEADS, DN_DK, DN_DV), lambda i: (i, 0, 0, 0)),
        ],
        out_shape=[
            jax.ShapeDtypeStruct((n_seq * SAMPLE_LEN, D_MODEL), F32),
            jax.ShapeDtypeStruct((n_seq, 2, D_CONV), F32),
            jax.ShapeDtypeStruct((3, n_seq, DN_QKV), F32),
            jax.ShapeDtypeStruct((n_seq, DN_HEADS, DN_DK, DN_DV), F32),
        ],
        scratch_shapes=[
            pltpu.VMEM((CHUNK, D_CONV), F32),
            pltpu.VMEM((CHUNK, D_CONV), F32),
            pltpu.VMEM((CHUNK, DN_QKV), F32),
            pltpu.VMEM((CHUNK, DN_HEADS * DN_DV), F32),
            pltpu.VMEM((CHUNK, AB_LANES), F32),
            pltpu.VMEM((CHUNK, D_MODEL), F32),
            pltpu.VMEM((RING, SEQ_PER_STEP, DN_HEADS, DN_DK, DN_DV), F32),
            pltpu.SemaphoreType.DMA((RING,)),
        ],
        compiler_params=_params("arbitrary"),
        name="mix_sample",
    )(proj, proj, proj, ab, state_a, state_q, state_s, caw, cqw, alog, dtb, dnn)


def _outproj_kernel(ya_ref, yb_ref, wa_ref, wb_ref, post_ref, x_ref, o_ref):
    m = (jnp.dot(ya_ref[...].astype(BF16), wa_ref[...], preferred_element_type=F32)
         + jnp.dot(yb_ref[...].astype(BF16), wb_ref[...], preferred_element_type=F32))
    o_ref[...] = x_ref[...] + _rms(m, post_ref[...])


def _outproj(ya, yb, yb_block, w, post, x):
    m = x.shape[0]
    half = D_MODEL // 2
    return pl.pallas_call(
        _outproj_kernel,
        grid=(m // TM,),
        in_specs=[
            pl.BlockSpec((TM, half), lambda i: (i, 0)),
            pl.BlockSpec((TM, half), lambda i: (i, yb_block)),
            pl.BlockSpec((half, D_MODEL), lambda i: (0, 0)),
            pl.BlockSpec((half, D_MODEL), lambda i: (1, 0)),
            pl.BlockSpec((1, D_MODEL), lambda i: (0, 0)),
            pl.BlockSpec((TM, D_MODEL), lambda i: (i, 0)),
        ],
        out_specs=pl.BlockSpec((TM, D_MODEL), lambda i: (i, 0)),
        out_shape=jax.ShapeDtypeStruct((m, D_MODEL), F32),
        compiler_params=_params("parallel"),
        name="outproj",
    )(ya, yb, w, w, post, x)


def _ple_kernel(x_ref, p_ref, pre_ref, post_ref, wg_ref, wp_ref, o_ref):
    x = x_ref[...]
    h = _rms(x, pre_ref[...]).astype(BF16)
    gate = jax.nn.sigmoid(jnp.dot(h, wg_ref[...], preferred_element_type=F32))
    proj = jnp.dot(p_ref[...].astype(BF16), wp_ref[...], preferred_element_type=F32)
    o_ref[...] = x + _rms(gate * proj, post_ref[...])


def _ple(x, p, pre, post, wg, wp):
    m = x.shape[0]
    return pl.pallas_call(
        _ple_kernel,
        grid=(m // TM,),
        in_specs=[
            pl.BlockSpec((TM, D_MODEL), lambda i: (i, 0)),
            pl.BlockSpec((TM, D_PLE), lambda i: (i, 0)),
            pl.BlockSpec((1, D_MODEL), lambda i: (0, 0)),
            pl.BlockSpec((1, D_MODEL), lambda i: (0, 0)),
            pl.BlockSpec((D_MODEL, D_MODEL), lambda i: (0, 0)),
            pl.BlockSpec((D_PLE, D_MODEL), lambda i: (0, 0)),
        ],
        out_specs=pl.BlockSpec((TM, D_MODEL), lambda i: (i, 0)),
        out_shape=jax.ShapeDtypeStruct((m, D_MODEL), F32),
        compiler_params=_params("parallel"),
        name="ple",
    )(x, p, pre, post, wg, wp)


def _row(v):
    return v.reshape(1, -1).astype(F32)


def _pad_lanes(v):
    return jnp.pad(_row(v), ((0, 0), (0, AB_LANES - v.shape[-1])))


def _seq_tails(tails, tiles_per_seq, n_rows):
    return tails[tiles_per_seq - 1::tiles_per_seq, TAIL_ROWS - n_rows:, :]


def kernel(x_prompt, x_sample, state_conv_a, state_conv_qkv, state_delta, p_prompt, p_sample,
           f1_pre, f1_post, f1_wg, f1_wu, f1_wd,
           mix_pre, mix_post, w_in, conv_a_w, conv_qkv_w, a_log, dt_bias, dn_norm, w_out,
           f2_pre, f2_post, f2_wg, f2_wu, f2_wd,
           ple_pre, ple_post, w_ple_gate, w_ple_proj):
    depth = f1_pre.shape[0]
    n_p, seq_p, _ = x_prompt.shape
    n_s, seq_s, _ = x_sample.shape
    rows_p = n_p * seq_p
    rows_s = n_s * seq_s
    assert seq_p % TM_BIG == 0 and seq_s == SAMPLE_LEN and n_s % SEQ_PER_STEP == 0
    assert rows_s == TM
    assert w_in.shape[-1] == IN_MAIN + N_AB
    tiles_per_seq = seq_p // TM_BIG
    xp = x_prompt.reshape(rows_p, D_MODEL)
    xs = x_sample.reshape(rows_s, D_MODEL)
    outs = [[] for _ in range(6)]
    for i in range(depth):
        pre, post = _row(f1_pre[i]), _row(f1_post[i])
        xs, wg, wu, wd = _ffn(xs, pre, post, f1_wg[i], f1_wu[i], f1_wd[i], TM)
        xp = _ffn(xp, pre, post, wg, wu, wd, TM_BIG)[0]

        pre = _row(mix_pre[i])
        caw, cqw = conv_a_w[i], conv_qkv_w[i]
        proj_s, ab_s, wt, wabt = _inproj_cast(xs, pre, w_in[i].T)
        y_a, zs, h, tail_a = _proj_a(xp, pre, wt, caw, TM_BIG, tiles_per_seq)
        alog, dtb, dnn = _pad_lanes(a_log[i]), _pad_lanes(dt_bias[i]), _row(dn_norm[i])
        wy_w, qdec, kdec, wy_u, qk, egl, tail_q = _proj_qkv(h, wt, wabt, cqw, alog, dtb, seq_p // TM)
        y_b, s_p = _delta_scan(wy_w, qdec, kdec, wy_u, qk, egl, zs, dnn, n_p, seq_p)
        ca_p = _seq_tails(tail_a, tiles_per_seq, HIST_A)
        cq_p = _seq_tails(tail_q, seq_p // TM, HIST_QKV)
        y_s, ca_s, cq_s, s_s = _mix_sample(
            proj_s, ab_s, state_conv_a[i], jnp.swapaxes(state_conv_qkv[i], 0, 1), state_delta[i],
            caw, cqw, alog, dtb, dnn)
        cq_s = jnp.swapaxes(cq_s, 0, 1)

        post = _row(mix_post[i])
        w_out_bf16 = w_out[i].astype(BF16)
        xs = _outproj(y_s, y_s, 1, w_out_bf16, post, xs)
        xp = _outproj(y_a, y_b, 0, w_out_bf16, post, xp)

        pre, post = _row(f2_pre[i]), _row(f2_post[i])
        xs, wg, wu, wd = _ffn(xs, pre, post, f2_wg[i], f2_wu[i], f2_wd[i], TM)
        xp = _ffn(xp, pre, post, wg, wu, wd, TM_BIG)[0]

        pre, post = _row(ple_pre[i]), _row(ple_post[i])
        wg, wp = w_ple_gate[i].astype(BF16), w_ple_proj[i].astype(BF16)
        xs = _ple(xs, p_sample[i].reshape(rows_s, D_PLE), pre, post, wg, wp)
        xp = _ple(xp, p_prompt[i].reshape(rows_p, D_PLE), pre, post, wg, wp)
        for lst, val in zip(outs, (ca_p, cq_p, s_p, ca_s, cq_s, s_s)):
            lst.append(val)
    return (xp.reshape(n_p, seq_p, D_MODEL), xs.reshape(n_s, seq_s, D_MODEL),
            *[jnp.stack(lst) for lst in outs])
```

```python
import functools

import jax
import jax.numpy as jnp
from jax import lax
from jax.experimental import pallas as pl
from jax.experimental.pallas import tpu as pltpu

F32 = jnp.float32
BF16 = jnp.bfloat16

D_MODEL = 2048
D_CONV = 1024
DN_HEADS = 8
DN_DK = 128
DN_DV = 128
DN_QKV = 3072
D_FF = 5632
D_PLE = 256
EPS = 1e-6
CHUNK = 64
SAMPLE_LEN = 4
SEQ_PAD = 8
SEQ_SHIFT = 3
SEQ_PER_STEP = CHUNK // SEQ_PAD
IN_MAIN = 7168
QKV_BLOCK = 3
Z_BLOCK = 6
HIST_A = 2
HIST_QKV = 3
TAIL_ROWS = 8
N_AB = 16
AB_LANES = 128

TM = 512
TF = 512
TF_CAST = 256
RING = 3
TN_IN = 1024
TM_BIG = 1024
TN_A = 512
SUB_COLS = 512
SCAN_ROWS = 256
ROW_CHUNK = 32
VMEM_LIMIT = 56 * 1024 * 1024


def _rms(x, g):
    ms = jnp.mean(x * x, axis=-1, keepdims=True)
    return x * lax.rsqrt(ms + EPS) * g


def _silu(x):
    return x * jax.nn.sigmoid(x)


def _mm(a, b):
    return jnp.dot(a.astype(BF16), b.astype(BF16), preferred_element_type=F32)


def _mm_nt(a, b):
    return lax.dot_general(a.astype(BF16), b.astype(BF16), (((1,), (1,)), ((), ())),
                           preferred_element_type=F32)


def _mm_tn(a, b):
    return jnp.dot(a.T.astype(BF16), b.astype(BF16), preferred_element_type=F32)


def _mm_exact(a, b):
    return jnp.dot(a, b, precision=lax.Precision.HIGHEST, preferred_element_type=F32)


def _nt_dot(a, bt):
    return lax.dot_general(a, bt, (((1,), (1,)), ((), ())), preferred_element_type=F32)


def _for_row_chunks(n_rows, body):
    def step(k, carry):
        body(pl.ds(pl.multiple_of(k * ROW_CHUNK, ROW_CHUNK), ROW_CHUNK))
        return carry
    lax.fori_loop(0, n_rows // ROW_CHUNK, step, 0, unroll=4)


def _params(*semantics):
    return pltpu.CompilerParams(dimension_semantics=semantics, vmem_limit_bytes=VMEM_LIMIT)


def _ffn_kernel(emit_bf16, x_ref, pre_ref, post_ref, wg_ref, wu_ref, wd_ref, o_ref, *rest):
    j = pl.program_id(1)
    if emit_bf16:
        wg16_ref, wu16_ref, wd16_ref, h_ref, wg_buf, wu_buf, wd_buf, sem = rest
        n_tiles = pl.num_programs(1)
        tf = wg_buf.shape[2]

        def tile_copies(t):
            start = t * tf if isinstance(t, int) else pl.multiple_of(t * tf, tf)
            slot = t % RING
            return (pltpu.make_async_copy(wg_ref.at[:, pl.ds(start, tf)], wg_buf.at[slot], sem.at[0, slot]),
                    pltpu.make_async_copy(wu_ref.at[:, pl.ds(start, tf)], wu_buf.at[slot], sem.at[1, slot]),
                    pltpu.make_async_copy(wd_ref.at[pl.ds(start, tf), :], wd_buf.at[slot], sem.at[2, slot]))

        @pl.when(j == 0)
        def _():
            for t in range(RING - 1):
                for copy in tile_copies(t):
                    copy.start()

        @pl.when(j + (RING - 1) < n_tiles)
        def _():
            for copy in tile_copies(j + (RING - 1)):
                copy.start()
    else:
        h_ref = rest[-1]

    @pl.when(j == 0)
    def _():
        def norm_rows(rows):
            h_ref[rows, :] = _rms(x_ref[rows, :], pre_ref[...]).astype(BF16)
            o_ref[rows, :] = jnp.zeros((ROW_CHUNK, D_MODEL), F32)
        _for_row_chunks(x_ref.shape[0], norm_rows)

    if emit_bf16:
        copy_g, copy_u, copy_d = tile_copies(j)
        slot = j % RING
        copy_g.wait()
        copy_u.wait()
        wg, wu = wg_buf[slot].astype(BF16), wu_buf[slot].astype(BF16)
        wg16_ref[...] = wg
        wu16_ref[...] = wu
    else:
        wg, wu = wg_ref[...], wu_ref[...]
    h = h_ref[...]
    g = jnp.dot(h, wg, preferred_element_type=F32)
    u = jnp.dot(h, wu, preferred_element_type=F32)
    a = (_silu(g) * u).astype(BF16)
    if emit_bf16:
        copy_d.wait()
        wd = wd_buf[slot].astype(BF16)
        wd16_ref[...] = wd
    else:
        wd = wd_ref[...]
    o_ref[...] += jnp.dot(a, wd, preferred_element_type=F32)

    @pl.when(j == pl.num_programs(1) - 1)
    def _():
        group = 4 * ROW_CHUNK

        def finish_rows(k, carry):
            r0 = pl.multiple_of(k * group, group)
            chunks = [pl.ds(r0 + s * ROW_CHUNK, ROW_CHUNK) for s in range(group // ROW_CHUNK)]
            scale = [0.5 * lax.rsqrt(jnp.mean(jnp.square(o_ref[c, :]), axis=-1, keepdims=True) + EPS)
                     for c in chunks]
            for c, sc in zip(chunks, scale):
                o_ref[c, :] = x_ref[c, :] + o_ref[c, :] * sc * post_ref[...]
            return carry
        lax.fori_loop(0, x_ref.shape[0] // group, finish_rows, 0)


def _ffn(x, pre, post, wg, wu, wd, tm):
    m = x.shape[0]
    emit_bf16 = wg.dtype != BF16
    tf = TF_CAST if emit_bf16 else TF
    up_spec = pl.BlockSpec((D_MODEL, tf), lambda i, j: (0, j))
    down_spec = pl.BlockSpec((tf, D_MODEL), lambda i, j: (j, 0))
    out_specs = [pl.BlockSpec((tm, D_MODEL), lambda i, j: (i, 0))]
    out_shape = [jax.ShapeDtypeStruct((m, D_MODEL), F32)]
    w_specs = [up_spec, up_spec, down_spec]
    scratch = [pltpu.VMEM((tm, D_MODEL), BF16)]
    if emit_bf16:
        assert m == tm, "weights are re-emitted once, by a single-row-tile call"
        assert D_FF // tf >= RING - 1
        out_specs += w_specs
        out_shape += [jax.ShapeDtypeStruct(w.shape, BF16) for w in (wg, wu, wd)]
        w_specs = [pl.BlockSpec(memory_space=pl.ANY)] * 3
        scratch += [pltpu.VMEM((RING, D_MODEL, tf), F32), pltpu.VMEM((RING, D_MODEL, tf), F32),
                    pltpu.VMEM((RING, tf, D_MODEL), F32), pltpu.SemaphoreType.DMA((3, RING))]
    outs = pl.pallas_call(
        functools.partial(_ffn_kernel, emit_bf16),
        grid=(m // tm, D_FF // tf),
        in_specs=[
            pl.BlockSpec((tm, D_MODEL), lambda i, j: (i, 0)),
            pl.BlockSpec((1, D_MODEL), lambda i, j: (0, 0)),
            pl.BlockSpec((1, D_MODEL), lambda i, j: (0, 0)),
            *w_specs,
        ],
        out_specs=out_specs,
        out_shape=out_shape,
        scratch_shapes=scratch,
        compiler_params=_params("arbitrary", "arbitrary"),
        name="ffn_cast" if emit_bf16 else "ffn",
    )(x, pre, post, wg, wu, wd)
    return outs if emit_bf16 else (outs[0], wg, wu, wd)


def _inproj_cast_kernel(x_ref, pre_ref, wt_ref, wabt_ref, o_ref, ab_ref, wt16_ref, wabt16_ref,
                        h_ref, wt_buf, sem):
    j = pl.program_id(1)

    def tile_copy(t):
        start = t * TN_IN if isinstance(t, int) else pl.multiple_of(t * TN_IN, TN_IN)
        slot = t % RING
        return pltpu.make_async_copy(wt_ref.at[pl.ds(start, TN_IN), :], wt_buf.at[slot], sem.at[slot])

    @pl.when(j == 0)
    def _():
        for t in range(RING - 1):
            tile_copy(t).start()

    @pl.when(j + (RING - 1) < pl.num_programs(1))
    def _():
        tile_copy(j + (RING - 1)).start()

    @pl.when(j == 0)
    def _():
        h = _rms(x_ref[...], pre_ref[...]).astype(BF16)
        h_ref[...] = h
        wabt = jnp.concatenate(
            [wabt_ref[...].astype(BF16), jnp.zeros((AB_LANES - N_AB, D_MODEL), BF16)], axis=0)
        wabt16_ref[...] = wabt
        ab_ref[...] = _nt_dot(h, wabt)

    tile_copy(j).wait()
    wt = wt_buf[j % RING].astype(BF16)
    wt16_ref[...] = wt
    o_ref[...] = _nt_dot(h_ref[...], wt)


def _inproj_cast(x, pre, wt):
    m = x.shape[0]
    assert m == TM, "weights are re-emitted once, by a single-row-tile call"
    assert IN_MAIN // TN_IN >= RING - 1
    wt_spec = pl.BlockSpec((TN_IN, D_MODEL), lambda i, j: (j, 0))
    return pl.pallas_call(
        _inproj_cast_kernel,
        grid=(1, IN_MAIN // TN_IN),
        in_specs=[
            pl.BlockSpec((TM, D_MODEL), lambda i, j: (0, 0)),
            pl.BlockSpec((1, D_MODEL), lambda i, j: (0, 0)),
            pl.BlockSpec(memory_space=pl.ANY),
            pl.BlockSpec((N_AB, D_MODEL), lambda i, j: (IN_MAIN // N_AB, 0)),
        ],
        out_specs=[
            pl.BlockSpec((TM, TN_IN), lambda i, j: (0, j)),
            pl.BlockSpec((TM, AB_LANES), lambda i, j: (0, 0)),
            wt_spec,
            pl.BlockSpec((AB_LANES, D_MODEL), lambda i, j: (0, 0)),
        ],
        out_shape=[
            jax.ShapeDtypeStruct((m, IN_MAIN), F32),
            jax.ShapeDtypeStruct((m, AB_LANES), F32),
            jax.ShapeDtypeStruct((IN_MAIN, D_MODEL), BF16),
            jax.ShapeDtypeStruct((AB_LANES, D_MODEL), BF16),
        ],
        scratch_shapes=[pltpu.VMEM((TM, D_MODEL), BF16), pltpu.VMEM((RING, TN_IN, D_MODEL), F32),
                        pltpu.SemaphoreType.DMA((RING,))],
        compiler_params=_params("arbitrary", "arbitrary"),
        name="inproj_cast",
    )(x, pre, wt, wt)


def _causal_conv(x, w, carry_in):
    width = w.shape[0]
    rows = x.shape[0]
    is_row0 = lax.broadcasted_iota(jnp.int32, (8, 1), 0) == 0
    acc = x * w[0:1, :]
    carry_out = []
    for j in range(1, width):
        delayed = pltpu.roll(acc, 1, axis=0)
        if carry_in is not None:
            carry_out.append(acc[rows - 1:rows, :])
            head = jnp.where(is_row0, carry_in[j - 1], delayed[:8])
            delayed = jnp.concatenate([head, delayed[8:]], axis=0)
        acc = x * w[j:j + 1, :] + delayed
    return acc, carry_out


def _load_carry(carry_ref, j, cols, n, seq_start):
    return [jnp.where(seq_start, 0.0, carry_ref[j, r:r + 1, cols]) for r in range(n)]


def _store_carry(carry_ref, j, cols, rows):
    for r, row in enumerate(rows):
        carry_ref[j, r:r + 1, cols] = row


def _proj_a_kernel(tiles_per_seq, x_ref, pre_ref, wgb_ref, wgcv_ref, whc_ref, wz_ref, caw_ref,
                   ya_ref, zs_ref, h_ref, tail_ref, carry_ref):
    i, j = pl.program_id(0), pl.program_id(1)
    tm = x_ref.shape[0]

    @pl.when(j == 0)
    def _():
        def norm_rows(rows):
            h_ref[rows, :] = _rms(x_ref[rows, :], pre_ref[...]).astype(BF16)
        _for_row_chunks(tm, norm_rows)

    @pl.when((i == 0) & (j == 0))
    def _():
        carry_ref[...] = jnp.zeros_like(carry_ref)

    seq_start = i % tiles_per_seq == 0
    h = h_ref[...]
    for c in range(TN_A // SUB_COLS):
        cols = slice(c * SUB_COLS, (c + 1) * SUB_COLS)
        u = _nt_dot(h, wgcv_ref[cols, :]) * _nt_dot(h, whc_ref[cols, :])
        tail_ref[0, :, cols] = u[tm - TAIL_ROWS:tm, :]
        conv, carry = _causal_conv(u, caw_ref[:, cols], _load_carry(carry_ref, j, cols, HIST_A, seq_start))
        _store_carry(carry_ref, j, cols, carry)
        ya_ref[:, cols] = (_nt_dot(h, wgb_ref[cols, :]) * conv).astype(BF16)
        zs_ref[:, cols] = _silu(_nt_dot(h, wz_ref[cols, :])).astype(BF16)


def _proj_a(x, pre, wt, caw, tm, tiles_per_seq):
    m = x.shape[0]
    n_col = D_CONV // TN_A
    assert DN_HEADS * DN_DV == D_CONV
    z_part = Z_BLOCK * TN_IN // D_CONV
    w_spec = lambda part: pl.BlockSpec((TN_A, D_MODEL), lambda i, j: (part * n_col + j, 0))
    cols = pl.BlockSpec((tm, TN_A), lambda i, j: (i, j))
    return pl.pallas_call(
        functools.partial(_proj_a_kernel, tiles_per_seq),
        grid=(m // tm, n_col),
        in_specs=[
            pl.BlockSpec((tm, D_MODEL), lambda i, j: (i, 0)),
            pl.BlockSpec((1, D_MODEL), lambda i, j: (0, 0)),
            w_spec(0), w_spec(1), w_spec(2), w_spec(z_part),
            pl.BlockSpec((3, TN_A), lambda i, j: (0, j)),
        ],
        out_specs=[
            cols, cols,
            pl.BlockSpec((tm, D_MODEL), lambda i, j: (i, 0)),
            pl.BlockSpec((1, TAIL_ROWS, TN_A), lambda i, j: (i, 0, j)),
        ],
        out_shape=[
            jax.ShapeDtypeStruct((m, D_CONV), BF16),
            jax.ShapeDtypeStruct((m, DN_HEADS * DN_DV), BF16),
            jax.ShapeDtypeStruct((m, D_MODEL), BF16),
            jax.ShapeDtypeStruct((m // tm, TAIL_ROWS, D_CONV), F32),
        ],
        scratch_shapes=[pltpu.VMEM((n_col, HIST_A, TN_A), F32)],
        compiler_params=_params("arbitrary", "arbitrary"),
        name="proj_a",
    )(x, pre, wt, wt, wt, wt, caw)


def _l2_normalize(x, scale):
    return x * (lax.rsqrt(jnp.sum(x * x, axis=-1, keepdims=True) + EPS) * scale)


def _gate_columns(beta_all, gc_all):
    heads = range(DN_HEADS)
    gc_t = gc_all.T
    return ([beta_all[:, DN_HEADS + h:DN_HEADS + h + 1] for h in heads],
            [gc_all[:, h:h + 1] for h in heads], [gc_t[h:h + 1, :] for h in heads])


def _heads_intra(qn, kn, vh, beta_col, gc_col, gc_row, causal, strict, eye, squarings):
    heads = range(len(qn))
    decay = [jnp.exp(jnp.where(causal, gc_col[h] - gc_row[h], -jnp.inf)) for h in heads]
    kb = [kn[h] * beta_col[h] for h in heads]
    gram = [_mm_nt(jnp.concatenate([kb[h], qn[h]], axis=0), kn[h]) for h in heads]
    a_mat = [jnp.where(strict, gram[h][:CHUNK] * decay[h], 0.0) for h in heads]
    qk = [jnp.where(causal, gram[h][CHUNK:] * decay[h], 0.0) for h in heads]
    inv = [eye - a for a in a_mat]
    power = a_mat
    for _ in range(squarings):
        power = [_mm(x, x) for x in power]
        inv = [_mm(inv[h], eye + power[h]) for h in heads]
    egc = [jnp.exp(x) for x in gc_col]
    wu = [_mm(inv[h], jnp.concatenate([kb[h] * egc[h], vh[h] * beta_col[h]], axis=1)) for h in heads]
    w = [x[:, :DN_DK] for x in wu]
    u = [x[:, DN_DK:] for x in wu]
    qdec = [qn[h] * egc[h] for h in heads]
    return w, u, qdec, qk


def _gates(ab, alog, dtb):
    x = ab + dtb
    softplus = jnp.maximum(x, 0.0) + jnp.log1p(jnp.exp(-jnp.abs(x)))
    return -jnp.exp(alog) * softplus, jax.nn.sigmoid(ab)


def _masks(same_seq):
    row = lax.broadcasted_iota(jnp.int32, (CHUNK, CHUNK), 0)
    col = lax.broadcasted_iota(jnp.int32, (CHUNK, CHUNK), 1)
    causal = row >= col
    strict = row > col
    if same_seq:
        same = (row >> SEQ_SHIFT) == (col >> SEQ_SHIFT)
        causal = causal & same
        strict = strict & same
    eye = jnp.where(row == col, 1.0, 0.0).astype(F32)
    return row, col, causal, strict, causal.astype(F32), eye


def _proj_qkv_kernel(tiles_per_seq, h_ref, wq_ref, wk_ref, wv_ref, wabt_ref, cqw_ref, alog_ref, dtb_ref,
                     w_ref, qd_ref, kd_ref, u_ref, qk_ref, egl_ref, tail_ref, carry_ref, qkv_ref):
    i = pl.program_id(0)
    tm = h_ref.shape[0]

    @pl.when(i == 0)
    def _():
        carry_ref[...] = jnp.zeros_like(carry_ref)

    seq_start = i % tiles_per_seq == 0
    h = h_ref[...]
    g_all, beta_all = _gates(_nt_dot(h, wabt_ref[...]), alog_ref[...], dtb_ref[...])
    for t, wt_ref in enumerate((wq_ref, wk_ref, wv_ref)):
        for c in range(TN_IN // SUB_COLS):
            cols = slice(c * SUB_COLS, (c + 1) * SUB_COLS)
            lo = t * TN_IN + c * SUB_COLS
            p = _nt_dot(h, wt_ref[cols, :])
            tail_ref[0, :, lo:lo + SUB_COLS] = p[tm - TAIL_ROWS:tm, :]
            p, carry = _causal_conv(p, cqw_ref[:, lo:lo + SUB_COLS],
                                    _load_carry(carry_ref, t, cols, HIST_QKV, seq_start))
            _store_carry(carry_ref, t, cols, carry)
            p = _silu(p)
            if t == 2:
                qkv_ref[:, lo:lo + SUB_COLS] = p
            else:
                for k in range(SUB_COLS // DN_DK):
                    blk = p[:, k * DN_DK:(k + 1) * DN_DK]
                    qkv_ref[:, lo + k * DN_DK:lo + (k + 1) * DN_DK] = _l2_normalize(
                        blk, DN_DK ** -0.5 if t == 0 else 1.0)

    heads = range(DN_HEADS)
    _, _, causal, strict, low_f, eye = _masks(False)
    chunk_rows = [slice(g * CHUNK, (g + 1) * CHUNK) for g in range(tm // CHUNK)]
    qn, kn, vh, beta_col, gc_col, gc_row = [], [], [], [], [], []
    for rows in chunk_rows:
        gc_all = _mm_exact(low_f, g_all[rows])
        egl_ref[rows, :] = jnp.broadcast_to(jnp.exp(gc_all[CHUNK - 1:CHUNK, :]), (CHUNK, AB_LANES))
        cols = _gate_columns(beta_all[rows], gc_all)
        beta_col += cols[0]
        gc_col += cols[1]
        gc_row += cols[2]
        qn += [qkv_ref[rows, h * DN_DK:(h + 1) * DN_DK] for h in heads]
        kn += [qkv_ref[rows, (DN_HEADS + h) * DN_DK:(DN_HEADS + h + 1) * DN_DK] for h in heads]
        vh += [qkv_ref[rows, (2 * DN_HEADS + h) * DN_DV:(2 * DN_HEADS + h + 1) * DN_DV] for h in heads]
    w, u, qdec, qk = _heads_intra(qn, kn, vh, beta_col, gc_col, gc_row, causal, strict, eye, 5)
    for n in range(len(w)):
        rows, hd = chunk_rows[n // DN_HEADS], n % DN_HEADS
        lo = hd * DN_DK
        g_last = gc_col[n][CHUNK - 1:CHUNK, :]
        w_ref[rows, lo:lo + DN_DK] = w[n].astype(BF16)
        qd_ref[rows, lo:lo + DN_DK] = qdec[n].astype(BF16)
        kd_ref[rows, lo:lo + DN_DK] = (kn[n] * jnp.exp(g_last - gc_col[n])).astype(BF16)
        u_ref[rows, lo:lo + DN_DV] = u[n]
        qk_ref[rows, hd * CHUNK:(hd + 1) * CHUNK] = qk[n].astype(BF16)


def _proj_qkv(h, wt, wabt, cqw, alog, dtb, tiles_per_seq):
    m = h.shape[0]
    hk, hv = DN_HEADS * DN_DK, DN_HEADS * DN_DV
    rows = lambda width: pl.BlockSpec((TM, width), lambda i: (i, 0))
    whole = lambda shape: pl.BlockSpec(shape, lambda i: (0,) * len(shape))
    w_tile = lambda t: pl.BlockSpec((TN_IN, D_MODEL), lambda i: (QKV_BLOCK + t, 0))
    return pl.pallas_call(
        functools.partial(_proj_qkv_kernel, tiles_per_seq),
        grid=(m // TM,),
        in_specs=[rows(D_MODEL), w_tile(0), w_tile(1), w_tile(2), whole((AB_LANES, D_MODEL)),
                  whole((4, DN_QKV)), whole((1, AB_LANES)), whole((1, AB_LANES))],
        out_specs=[rows(hk), rows(hk), rows(hk), rows(hv), rows(DN_HEADS * CHUNK), rows(AB_LANES),
                   pl.BlockSpec((1, TAIL_ROWS, DN_QKV), lambda i: (i, 0, 0))],
        out_shape=[
            jax.ShapeDtypeStruct((m, hk), BF16), jax.ShapeDtypeStruct((m, hk), BF16),
            jax.ShapeDtypeStruct((m, hk), BF16), jax.ShapeDtypeStruct((m, hv), F32),
            jax.ShapeDtypeStruct((m, DN_HEADS * CHUNK), BF16), jax.ShapeDtypeStruct((m, AB_LANES), F32),
            jax.ShapeDtypeStruct((m // TM, TAIL_ROWS, DN_QKV), F32),
        ],
        scratch_shapes=[pltpu.VMEM((3, HIST_QKV, TN_IN), F32), pltpu.VMEM((TM, DN_QKV), F32)],
        compiler_params=_params("arbitrary"),
        name="proj_qkv",
    )(h, wt, wt, wt, wabt, cqw, alog, dtb)


def _delta_scan_kernel(w_ref, qd_ref, kd_ref, u_ref, qk_ref, egl_ref, zs_ref, dnn_ref,
                       y_ref, so_ref, s_ref):
    c = pl.program_id(0)

    @pl.when(c == 0)
    def _():
        s_ref[...] = jnp.zeros_like(s_ref)

    n_seq = w_ref.shape[0]
    units = [(b, h) for b in range(n_seq) for h in range(DN_HEADS)]
    col = [slice(h * DN_DK, (h + 1) * DN_DK) for h in range(DN_HEADS)]
    dnn = dnn_ref[...]
    s = [s_ref[b, h] for b, h in units]
    for g in range(w_ref.shape[1] // CHUNK):
        rows = slice(g * CHUNK, (g + 1) * CHUNK)
        lhs = [jnp.concatenate([w_ref[b, rows, col[h]], qd_ref[b, rows, col[h]]], axis=0) for b, h in units]
        r = [jnp.dot(lhs[n], s[n].astype(BF16), preferred_element_type=F32) for n in range(len(units))]
        v_new = [(u_ref[b, rows, col[h]] - r[n][:CHUNK]).astype(BF16) for n, (b, h) in enumerate(units)]
        upd = [lax.dot_general(kd_ref[b, rows, col[h]], v_new[n], (((0,), (0,)), ((), ())),
                               preferred_element_type=F32) for n, (b, h) in enumerate(units)]
        s = [s[n] * egl_ref[b, g * CHUNK:g * CHUNK + 1, h:h + 1] + upd[n] for n, (b, h) in enumerate(units)]
        o = [r[n][CHUNK:] + jnp.dot(qk_ref[b, rows, h * CHUNK:(h + 1) * CHUNK], v_new[n],
                                    preferred_element_type=F32) for n, (b, h) in enumerate(units)]
        for n, (b, h) in enumerate(units):
            y_ref[b, rows, col[h]] = (_rms(o[n], dnn) * zs_ref[b, rows, col[h]]).astype(y_ref.dtype)
    for n, (b, h) in enumerate(units):
        s_ref[b, h] = s[n]

    @pl.when(c == pl.num_programs(0) - 1)
    def _():
        so_ref[...] = s_ref[...]


def _delta_scan(w, qd, kd, u, qk, egl, zs, dnn, n_seq, seq_len):
    hk, hv = DN_HEADS * DN_DK, DN_HEADS * DN_DV
    per_seq = lambda a: a.reshape(n_seq, seq_len, a.shape[-1])
    rows = lambda width: pl.BlockSpec((n_seq, SCAN_ROWS, width), lambda c: (0, c, 0))
    state = pl.BlockSpec((n_seq, DN_HEADS, DN_DK, DN_DV), lambda c: (0, 0, 0, 0))
    y, s = pl.pallas_call(
        _delta_scan_kernel,
        grid=(seq_len // SCAN_ROWS,),
        in_specs=[rows(hk), rows(hk), rows(hk), rows(hv), rows(DN_HEADS * CHUNK), rows(AB_LANES), rows(hv),
                  pl.BlockSpec((1, DN_DV), lambda c: (0, 0))],
        out_specs=[rows(hv), state],
        out_shape=[jax.ShapeDtypeStruct((n_seq, seq_len, hv), BF16),
                   jax.ShapeDtypeStruct((n_seq, DN_HEADS, DN_DK, DN_DV), F32)],
        scratch_shapes=[pltpu.VMEM((n_seq, DN_HEADS, DN_DK, DN_DV), F32)],
        compiler_params=_params("arbitrary"),
        name="delta_scan",
    )(*map(per_seq, (w, qd, kd, u, qk, egl, zs)), dnn)
    return y.reshape(n_seq * seq_len, hv), s


def _mix_sample_kernel(pa_ref, pq_ref, pz_ref, pab_ref, sa_ref, sq_ref, si_ref,
                       caw_ref, cqw_ref, alog_ref, dtb_ref, dnn_ref,
                       y_ref, ca_ref, cq_ref, so_ref,
                       gb_ref, ua_ref, xq_ref, z_ref, ab_ref, ybuf_ref, si_buf, sem):
    step = pl.program_id(0)

    def state_copy(t):
        start = t * SEQ_PER_STEP if isinstance(t, int) else pl.multiple_of(t * SEQ_PER_STEP, SEQ_PER_STEP)
        slot = t % RING
        return pltpu.make_async_copy(si_ref.at[pl.ds(start, SEQ_PER_STEP)], si_buf.at[slot], sem.at[slot])

    @pl.when(step == 0)
    def _():
        for t in range(RING - 1):
            state_copy(t).start()

    @pl.when(step + (RING - 1) < pl.num_programs(0))
    def _():
        state_copy(step + (RING - 1)).start()

    seqs = range(SEQ_PER_STEP)
    pad = SEQ_PAD - SAMPLE_LEN
    for b in seqs:
        r0 = SEQ_PAD * b
        nxt = (b + 1) % SEQ_PER_STEP
        src = slice(SAMPLE_LEN * b, SAMPLE_LEN * (b + 1))
        pa = pa_ref[src, :]
        gb_ref[r0:r0 + SAMPLE_LEN, :] = pa[:, :D_CONV]
        ua_ref[r0:r0 + SAMPLE_LEN, :] = pa[:, D_CONV:2 * D_CONV] * pa[:, 2 * D_CONV:]
        xq_ref[r0:r0 + SAMPLE_LEN, :] = pq_ref[src, :]
        z_ref[r0:r0 + SAMPLE_LEN, :] = pz_ref[src, :]
        ab_ref[r0:r0 + SAMPLE_LEN, :] = pab_ref[src, :]
        for ref in (gb_ref, ua_ref, xq_ref, z_ref, ab_ref):
            ref[r0 + SAMPLE_LEN:r0 + SEQ_PAD, :] = jnp.zeros((pad, ref.shape[1]), F32)
        ua_ref[r0 + SEQ_PAD - HIST_A:r0 + SEQ_PAD, :] = sa_ref[nxt]
        for r in range(HIST_QKV):
            dst = r0 + SEQ_PAD - HIST_QKV + r
            xq_ref[dst:dst + 1, :] = sq_ref[r, nxt:nxt + 1, :]
    ybuf_ref[:, :D_CONV] = gb_ref[...] * _causal_conv(ua_ref[...], caw_ref[...], None)[0]
    row, col, causal, strict, low_f, eye = _masks(True)
    valid = (row[:, 0:1] & (SEQ_PAD - 1)) < SAMPLE_LEN
    cqkv = jnp.where(valid, _silu(_causal_conv(xq_ref[...], cqw_ref[...], None)[0]), 0.0)
    for b in seqs:
        r0 = SEQ_PAD * b
        ca_ref[b] = ua_ref[r0 + SAMPLE_LEN - HIST_A:r0 + SAMPLE_LEN, :]
        for r in range(HIST_QKV):
            src = r0 + SAMPLE_LEN - HIST_QKV + r
            cq_ref[r, b:b + 1, :] = xq_ref[src:src + 1, :]

    g_all, beta_all = _gates(ab_ref[...], alog_ref[...], dtb_ref[...])
    g_all = jnp.where(valid, g_all, 0.0)
    beta_all = jnp.where(valid, beta_all, 0.0)
    gc_all = _mm_exact(low_f, g_all)
    last_sel = (col == (row | (SEQ_PAD - 1))).astype(F32)
    gl_all = _mm_exact(last_sel, gc_all)
    z = z_ref[...]
    dnn = dnn_ref[...]
    heads = range(DN_HEADS)
    qn = [_l2_normalize(cqkv[:, h * DN_DK:(h + 1) * DN_DK], DN_DK ** -0.5) for h in heads]
    kn = [_l2_normalize(cqkv[:, (DN_HEADS + h) * DN_DK:(DN_HEADS + h + 1) * DN_DK], 1.0) for h in heads]
    vh = [cqkv[:, (2 * DN_HEADS + h) * DN_DV:(2 * DN_HEADS + h + 1) * DN_DV] for h in heads]
    beta_col, gc_col, gc_row = _gate_columns(beta_all, gc_all)
    w, u, qdec, qk = _heads_intra(qn, kn, vh, beta_col, gc_col, gc_row, causal, strict, eye, 1)
    gl_col = [gl_all[:, h:h + 1] for h in heads]
    kdec = [kn[h] * jnp.exp(gl_col[h] - gc_col[h]) for h in heads]
    s_scale = [jnp.exp(x) for x in gl_col]
    rows = [slice(SEQ_PAD * b, SEQ_PAD * (b + 1)) for b in seqs]
    pairs = [(h, b) for h in heads for b in seqs]
    state_copy(step).wait()
    slot = step % RING
    s_old = [si_buf[slot, b, h] for h, b in pairs]
    r = [_mm(jnp.concatenate([w[h][rows[b]], qdec[h][rows[b]]], axis=0), s_old[n])
         for n, (h, b) in enumerate(pairs)]
    v_new = [u[h][rows[b]] - r[n][:SEQ_PAD] for n, (h, b) in enumerate(pairs)]
    upd = [_mm_tn(kdec[h][rows[b]], v_new[n]) for n, (h, b) in enumerate(pairs)]
    for n, (h, b) in enumerate(pairs):
        so_ref[b, h] = s_old[n] * s_scale[h][SEQ_PAD * b:SEQ_PAD * b + 1, :] + upd[n]
    for h in heads:
        mine = range(h * SEQ_PER_STEP, (h + 1) * SEQ_PER_STEP)
        o = (jnp.concatenate([r[n][SEQ_PAD:] for n in mine], axis=0)
             + _mm(qk[h], jnp.concatenate([v_new[n] for n in mine], axis=0)))
        lo = h * DN_DV
        ybuf_ref[:, D_CONV + lo:D_CONV + lo + DN_DV] = _rms(o, dnn) * _silu(z[:, lo:lo + DN_DV])
    for b in seqs:
        y_ref[SAMPLE_LEN * b:SAMPLE_LEN * (b + 1), :] = ybuf_ref[SEQ_PAD * b:SEQ_PAD * b + SAMPLE_LEN, :]


def _mix_sample(proj, ab, state_a, state_q, state_s, caw, cqw, alog, dtb, dnn):
    n_seq = state_s.shape[0]
    assert n_seq // SEQ_PER_STEP >= RING - 1
    full2 = lambda i: (0, 0)
    seq_rows = lambda width, col: pl.BlockSpec((SEQ_PER_STEP * SAMPLE_LEN, width), lambda i: (i, col))
    return pl.pallas_call(
        _mix_sample_kernel,
        grid=(n_seq // SEQ_PER_STEP,),
        in_specs=[
            seq_rows(3 * D_CONV, 0), seq_rows(DN_QKV, 1), seq_rows(DN_HEADS * DN_DV, Z_BLOCK), seq_rows(AB_LANES, 0),
            pl.BlockSpec((SEQ_PER_STEP, 2, D_CONV), lambda i: (i, 0, 0)),
            pl.BlockSpec((3, SEQ_PER_STEP, DN_QKV), lambda i: (0, i, 0)),
            pl.BlockSpec(memory_space=pl.ANY),
            pl.BlockSpec((3, D_CONV), full2),
            pl.BlockSpec((4, DN_QKV), full2),
            pl.BlockSpec((1, AB_LANES), full2),
            pl.BlockSpec((1, AB_LANES), full2),
            pl.BlockSpec((1, DN_DV), full2),
        ],
        out_specs=[
            seq_rows(D_MODEL, 0),
            pl.BlockSpec((SEQ_PER_STEP, 2, D_CONV), lambda i: (i, 0, 0)),
            pl.BlockSpec((3, SEQ_PER_STEP, DN_QKV), lambda i: (0, i, 0)),
            pl.BlockSpec((SEQ_PER_STEP, DN_HEADS, DN_DK, DN_DV), lambda i: (i, 0, 0, 0)),
        ],
        out_shape=[
            jax.ShapeDtypeStruct((n_seq * SAMPLE_LEN, D_MODEL), F32),
            jax.ShapeDtypeStruct((n_seq, 2, D_CONV), F32),
            jax.ShapeDtypeStruct((3, n_seq, DN_QKV), F32),
            jax.ShapeDtypeStruct((n_seq, DN_HEADS, DN_DK, DN_DV), F32),
        ],
        scratch_shapes=[
            pltpu.VMEM((CHUNK, D_CONV), F32),
            pltpu.VMEM((CHUNK, D_CONV), F32),
            pltpu.VMEM((CHUNK, DN_QKV), F32),
            pltpu.VMEM((CHUNK, DN_HEADS * DN_DV), F32),
            pltpu.VMEM((CHUNK, AB_LANES), F32),
            pltpu.VMEM((CHUNK, D_MODEL), F32),
            pltpu.VMEM((RING, SEQ_PER_STEP, DN_HEADS, DN_DK, DN_DV), F32),
            pltpu.SemaphoreType.DMA((RING,)),
        ],
        compiler_params=_params("arbitrary"),
        name="mix_sample",
    )(proj, proj, proj, ab, state_a, state_q, state_s, caw, cqw, alog, dtb, dnn)


def _outproj_kernel(ya_ref, yb_ref, wa_ref, wb_ref, post_ref, x_ref, o_ref):
    m = (jnp.dot(ya_ref[...].astype(BF16), wa_ref[...], preferred_element_type=F32)
         + jnp.dot(yb_ref[...].astype(BF16), wb_ref[...], preferred_element_type=F32))
    o_ref[...] = x_ref[...] + _rms(m, post_ref[...])


def _outproj(ya, yb, yb_block, w, post, x):
    m = x.shape[0]
    half = D_MODEL // 2
    return pl.pallas_call(
        _outproj_kernel,
        grid=(m // TM,),
        in_specs=[
            pl.BlockSpec((TM, half), lambda i: (i, 0)),
            pl.BlockSpec((TM, half), lambda i: (i, yb_block)),
            pl.BlockSpec((half, D_MODEL), lambda i: (0, 0)),
            pl.BlockSpec((half, D_MODEL), lambda i: (1, 0)),
            pl.BlockSpec((1, D_MODEL), lambda i: (0, 0)),
            pl.BlockSpec((TM, D_MODEL), lambda i: (i, 0)),
        ],
        out_specs=pl.BlockSpec((TM, D_MODEL), lambda i: (i, 0)),
        out_shape=jax.ShapeDtypeStruct((m, D_MODEL), F32),
        compiler_params=_params("parallel"),
        name="outproj",
    )(ya, yb, w, w, post, x)


def _ple_kernel(x_ref, p_ref, pre_ref, post_ref, wg_ref, wp_ref, o_ref):
    x = x_ref[...]
    h = _rms(x, pre_ref[...]).astype(BF16)
    gate = jax.nn.sigmoid(jnp.dot(h, wg_ref[...], preferred_element_type=F32))
    proj = jnp.dot(p_ref[...].astype(BF16), wp_ref[...], preferred_element_type=F32)
    o_ref[...] = x + _rms(gate * proj, post_ref[...])


def _ple(x, p, pre, post, wg, wp):
    m = x.shape[0]
    return pl.pallas_call(
        _ple_kernel,
        grid=(m // TM,),
        in_specs=[
            pl.BlockSpec((TM, D_MODEL), lambda i: (i, 0)),
            pl.BlockSpec((TM, D_PLE), lambda i: (i, 0)),
            pl.BlockSpec((1, D_MODEL), lambda i: (0, 0)),
            pl.BlockSpec((1, D_MODEL), lambda i: (0, 0)),
            pl.BlockSpec((D_MODEL, D_MODEL), lambda i: (0, 0)),
            pl.BlockSpec((D_PLE, D_MODEL), lambda i: (0, 0)),
        ],
        out_specs=pl.BlockSpec((TM, D_MODEL), lambda i: (i, 0)),
        out_shape=jax.ShapeDtypeStruct((m, D_MODEL), F32),
        compiler_params=_params("parallel"),
        name="ple",
    )(x, p, pre, post, wg, wp)


def _row(v):
    return v.reshape(1, -1).astype(F32)


def _pad_lanes(v):
    return jnp.pad(_row(v), ((0, 0), (0, AB_LANES - v.shape[-1])))


def _seq_tails(tails, tiles_per_seq, n_rows):
    return tails[tiles_per_seq - 1::tiles_per_seq, TAIL_ROWS - n_rows:, :]


def kernel(x_prompt, x_sample, state_conv_a, state_conv_qkv, state_delta, p_prompt, p_sample,
           f1_pre, f1_post, f1_wg, f1_wu, f1_wd,
           mix_pre, mix_post, w_in, conv_a_w, conv_qkv_w, a_log, dt_bias, dn_norm, w_out,
           f2_pre, f2_post, f2_wg, f2_wu, f2_wd,
           ple_pre, ple_post, w_ple_gate, w_ple_proj):
    depth = f1_pre.shape[0]
    n_p, seq_p, _ = x_prompt.shape
    n_s, seq_s, _ = x_sample.shape
    rows_p = n_p * seq_p
    rows_s = n_s * seq_s
    assert seq_p % TM_BIG == 0 and seq_s == SAMPLE_LEN and n_s % SEQ_PER_STEP == 0
    assert rows_s == TM
    assert w_in.shape[-1] == IN_MAIN + N_AB
    tiles_per_seq = seq_p // TM_BIG
    xp = x_prompt.reshape(rows_p, D_MODEL)
    xs = x_sample.reshape(rows_s, D_MODEL)
    outs = [[] for _ in range(6)]
    for i in range(depth):
        pre, post = _row(f1_pre[i]), _row(f1_post[i])
        xs, wg, wu, wd = _ffn(xs, pre, post, f1_wg[i], f1_wu[i], f1_wd[i], TM)
        xp = _ffn(xp, pre, post, wg, wu, wd, TM_BIG)[0]

        pre = _row(mix_pre[i])
        caw, cqw = conv_a_w[i], conv_qkv_w[i]
        proj_s, ab_s, wt, wabt = _inproj_cast(xs, pre, w_in[i].T)
        y_a, zs, h, tail_a = _proj_a(xp, pre, wt, caw, TM_BIG, tiles_per_seq)
        alog, dtb, dnn = _pad_lanes(a_log[i]), _pad_lanes(dt_bias[i]), _row(dn_norm[i])
        wy_w, qdec, kdec, wy_u, qk, egl, tail_q = _proj_qkv(h, wt, wabt, cqw, alog, dtb, seq_p // TM)
        y_b, s_p = _delta_scan(wy_w, qdec, kdec, wy_u, qk, egl, zs, dnn, n_p, seq_p)
        ca_p = _seq_tails(tail_a, tiles_per_seq, HIST_A)
        cq_p = _seq_tails(tail_q, seq_p // TM, HIST_QKV)
        y_s, ca_s, cq_s, s_s = _mix_sample(
            proj_s, ab_s, state_conv_a[i], jnp.swapaxes(state_conv_qkv[i], 0, 1), state_delta[i],
            caw, cqw, alog, dtb, dnn)
        cq_s = jnp.swapaxes(cq_s, 0, 1)

        post = _row(mix_post[i])
        w_out_bf16 = w_out[i].astype(BF16)
        xs = _outproj(y_s, y_s, 1, w_out_bf16, post, xs)
        xp = _outproj(y_a, y_b, 0, w_out_bf16, post, xp)

        pre, post = _row(f2_pre[i]), _row(f2_post[i])
        xs, wg, wu, wd = _ffn(xs, pre, post, f2_wg[i], f2_wu[i], f2_wd[i], TM)
        xp = _ffn(xp, pre, post, wg, wu, wd, TM_BIG)[0]

        pre, post = _row(ple_pre[i]), _row(ple_post[i])
        wg, wp = w_ple_gate[i].astype(BF16), w_ple_proj[i].astype(BF16)
        xs = _ple(xs, p_sample[i].reshape(rows_s, D_PLE), pre, post, wg, wp)
        xp = _ple(xp, p_prompt[i].reshape(rows_p, D_PLE), pre, post, wg, wp)
        for lst, val in zip(outs, (ca_p, cq_p, s_p, ca_s, cq_s, s_s)):
            lst.append(val)
    return (xp.reshape(n_p, seq_p, D_MODEL), xs.reshape(n_s, seq_s, D_MODEL),
            *[jnp.stack(lst) for lst in outs])
```

```python
import functools

import jax
import jax.numpy as jnp
from jax import lax
from jax.experimental import pallas as pl
from jax.experimental.pallas import tpu as pltpu

F32 = jnp.float32
BF16 = jnp.bfloat16

D_MODEL = 2048
D_CONV = 1024
DN_HEADS = 8
DN_DK = 128
DN_DV = 128
DN_QKV = 3072
D_FF = 5632
D_PLE = 256
EPS = 1e-6
CHUNK = 64
SAMPLE_LEN = 4
SEQ_PAD = 8
SEQ_SHIFT = 3
SEQ_PER_STEP = CHUNK // SEQ_PAD
IN_MAIN = 7168
QKV_BLOCK = 3
Z_BLOCK = 6
HIST_A = 2
HIST_QKV = 3
TAIL_ROWS = 8
N_AB = 16
AB_LANES = 128

TM = 512
TF = 512
TF_CAST = 256
RING = 4
TN_IN = 1024
TM_BIG = 1024
TN_A = 512
SUB_COLS = 512
SCAN_ROWS = 256
ROW_CHUNK = 32
VMEM_LIMIT = 56 * 1024 * 1024


def _rms(x, g):
    ms = jnp.mean(x * x, axis=-1, keepdims=True)
    return x * lax.rsqrt(ms + EPS) * g


def _silu(x):
    return x * jax.nn.sigmoid(x)


def _mm(a, b):
    return jnp.dot(a.astype(BF16), b.astype(BF16), preferred_element_type=F32)


def _mm_nt(a, b):
    return lax.dot_general(a.astype(BF16), b.astype(BF16), (((1,), (1,)), ((), ())),
                           preferred_element_type=F32)


def _mm_tn(a, b):
    return jnp.dot(a.T.astype(BF16), b.astype(BF16), preferred_element_type=F32)


def _mm_exact(a, b):
    return jnp.dot(a, b, precision=lax.Precision.HIGHEST, preferred_element_type=F32)


def _nt_dot(a, bt):
    return lax.dot_general(a, bt, (((1,), (1,)), ((), ())), preferred_element_type=F32)


def _for_row_chunks(n_rows, body):
    def step(k, carry):
        body(pl.ds(pl.multiple_of(k * ROW_CHUNK, ROW_CHUNK), ROW_CHUNK))
        return carry
    lax.fori_loop(0, n_rows // ROW_CHUNK, step, 0, unroll=4)


def _params(*semantics):
    return pltpu.CompilerParams(dimension_semantics=semantics, vmem_limit_bytes=VMEM_LIMIT)


def _ffn_kernel(emit_bf16, x_ref, pre_ref, post_ref, wg_ref, wu_ref, wd_ref, o_ref, *rest):
    j = pl.program_id(1)
    if emit_bf16:
        wg16_ref, wu16_ref, wd16_ref, h_ref, wg_buf, wu_buf, wd_buf, sem = rest
        n_tiles = pl.num_programs(1)
        tf = wg_buf.shape[2]

        def tile_copies(t):
            start = t * tf if isinstance(t, int) else pl.multiple_of(t * tf, tf)
            slot = t % RING
            return (pltpu.make_async_copy(wg_ref.at[:, pl.ds(start, tf)], wg_buf.at[slot], sem.at[0, slot]),
                    pltpu.make_async_copy(wu_ref.at[:, pl.ds(start, tf)], wu_buf.at[slot], sem.at[1, slot]),
                    pltpu.make_async_copy(wd_ref.at[pl.ds(start, tf), :], wd_buf.at[slot], sem.at[2, slot]))

        @pl.when(j == 0)
        def _():
            for t in range(RING - 1):
                for copy in tile_copies(t):
                    copy.start()

        @pl.when(j + (RING - 1) < n_tiles)
        def _():
            for copy in tile_copies(j + (RING - 1)):
                copy.start()
    else:
        h_ref = rest[-1]

    @pl.when(j == 0)
    def _():
        def norm_rows(rows):
            h_ref[rows, :] = _rms(x_ref[rows, :], pre_ref[...]).astype(BF16)
            o_ref[rows, :] = jnp.zeros((ROW_CHUNK, D_MODEL), F32)
        _for_row_chunks(x_ref.shape[0], norm_rows)

    if emit_bf16:
        for copy in tile_copies(j):
            copy.wait()
        slot = j % RING
        wg, wu, wd = wg_buf[slot].astype(BF16), wu_buf[slot].astype(BF16), wd_buf[slot].astype(BF16)
        wg16_ref[...] = wg
        wu16_ref[...] = wu
        wd16_ref[...] = wd
    else:
        wg, wu, wd = wg_ref[...], wu_ref[...], wd_ref[...]
    h = h_ref[...]
    g = jnp.dot(h, wg, preferred_element_type=F32)
    u = jnp.dot(h, wu, preferred_element_type=F32)
    a = (_silu(g) * u).astype(BF16)
    o_ref[...] += jnp.dot(a, wd, preferred_element_type=F32)

    @pl.when(j == pl.num_programs(1) - 1)
    def _():
        group = 4 * ROW_CHUNK

        def finish_rows(k, carry):
            r0 = pl.multiple_of(k * group, group)
            chunks = [pl.ds(r0 + s * ROW_CHUNK, ROW_CHUNK) for s in range(group // ROW_CHUNK)]
            scale = [0.5 * lax.rsqrt(jnp.mean(jnp.square(o_ref[c, :]), axis=-1, keepdims=True) + EPS)
                     for c in chunks]
            for c, sc in zip(chunks, scale):
                o_ref[c, :] = x_ref[c, :] + o_ref[c, :] * sc * post_ref[...]
            return carry
        lax.fori_loop(0, x_ref.shape[0] // group, finish_rows, 0)


def _ffn(x, pre, post, wg, wu, wd, tm):
    m = x.shape[0]
    emit_bf16 = wg.dtype != BF16
    tf = TF_CAST if emit_bf16 else TF
    up_spec = pl.BlockSpec((D_MODEL, tf), lambda i, j: (0, j))
    down_spec = pl.BlockSpec((tf, D_MODEL), lambda i, j: (j, 0))
    out_specs = [pl.BlockSpec((tm, D_MODEL), lambda i, j: (i, 0))]
    out_shape = [jax.ShapeDtypeStruct((m, D_MODEL), F32)]
    w_specs = [up_spec, up_spec, down_spec]
    scratch = [pltpu.VMEM((tm, D_MODEL), BF16)]
    if emit_bf16:
        assert m == tm, "weights are re-emitted once, by a single-row-tile call"
        assert D_FF // tf >= RING - 1
        out_specs += w_specs
        out_shape += [jax.ShapeDtypeStruct(w.shape, BF16) for w in (wg, wu, wd)]
        w_specs = [pl.BlockSpec(memory_space=pl.ANY)] * 3
        scratch += [pltpu.VMEM((RING, D_MODEL, tf), F32), pltpu.VMEM((RING, D_MODEL, tf), F32),
                    pltpu.VMEM((RING, tf, D_MODEL), F32), pltpu.SemaphoreType.DMA((3, RING))]
    outs = pl.pallas_call(
        functools.partial(_ffn_kernel, emit_bf16),
        grid=(m // tm, D_FF // tf),
        in_specs=[
            pl.BlockSpec((tm, D_MODEL), lambda i, j: (i, 0)),
            pl.BlockSpec((1, D_MODEL), lambda i, j: (0, 0)),
            pl.BlockSpec((1, D_MODEL), lambda i, j: (0, 0)),
            *w_specs,
        ],
        out_specs=out_specs,
        out_shape=out_shape,
        scratch_shapes=scratch,
        compiler_params=_params("arbitrary", "arbitrary"),
        name="ffn_cast" if emit_bf16 else "ffn",
    )(x, pre, post, wg, wu, wd)
    return outs if emit_bf16 else (outs[0], wg, wu, wd)


def _inproj_cast_kernel(x_ref, pre_ref, wt_ref, wabt_ref, o_ref, ab_ref, wt16_ref, wabt16_ref,
                        h_ref, wt_buf, sem):
    j = pl.program_id(1)

    def tile_copy(t):
        start = t * TN_IN if isinstance(t, int) else pl.multiple_of(t * TN_IN, TN_IN)
        slot = t % RING
        return pltpu.make_async_copy(wt_ref.at[pl.ds(start, TN_IN), :], wt_buf.at[slot], sem.at[slot])

    @pl.when(j == 0)
    def _():
        for t in range(RING - 1):
            tile_copy(t).start()

    @pl.when(j + (RING - 1) < pl.num_programs(1))
    def _():
        tile_copy(j + (RING - 1)).start()

    @pl.when(j == 0)
    def _():
        h = _rms(x_ref[...], pre_ref[...]).astype(BF16)
        h_ref[...] = h
        wabt = jnp.concatenate(
            [wabt_ref[...].astype(BF16), jnp.zeros((AB_LANES - N_AB, D_MODEL), BF16)], axis=0)
        wabt16_ref[...] = wabt
        ab_ref[...] = _nt_dot(h, wabt)

    tile_copy(j).wait()
    wt = wt_buf[j % RING].astype(BF16)
    wt16_ref[...] = wt
    o_ref[...] = _nt_dot(h_ref[...], wt)


def _inproj_cast(x, pre, wt):
    m = x.shape[0]
    assert m == TM, "weights are re-emitted once, by a single-row-tile call"
    assert IN_MAIN // TN_IN >= RING - 1
    wt_spec = pl.BlockSpec((TN_IN, D_MODEL), lambda i, j: (j, 0))
    return pl.pallas_call(
        _inproj_cast_kernel,
        grid=(1, IN_MAIN // TN_IN),
        in_specs=[
            pl.BlockSpec((TM, D_MODEL), lambda i, j: (0, 0)),
            pl.BlockSpec((1, D_MODEL), lambda i, j: (0, 0)),
            pl.BlockSpec(memory_space=pl.ANY),
            pl.BlockSpec((N_AB, D_MODEL), lambda i, j: (IN_MAIN // N_AB, 0)),
        ],
        out_specs=[
            pl.BlockSpec((TM, TN_IN), lambda i, j: (0, j)),
            pl.BlockSpec((TM, AB_LANES), lambda i, j: (0, 0)),
            wt_spec,
            pl.BlockSpec((AB_LANES, D_MODEL), lambda i, j: (0, 0)),
        ],
        out_shape=[
            jax.ShapeDtypeStruct((m, IN_MAIN), F32),
            jax.ShapeDtypeStruct((m, AB_LANES), F32),
            jax.ShapeDtypeStruct((IN_MAIN, D_MODEL), BF16),
            jax.ShapeDtypeStruct((AB_LANES, D_MODEL), BF16),
        ],
        scratch_shapes=[pltpu.VMEM((TM, D_MODEL), BF16), pltpu.VMEM((RING, TN_IN, D_MODEL), F32),
                        pltpu.SemaphoreType.DMA((RING,))],
        compiler_params=_params("arbitrary", "arbitrary"),
        name="inproj_cast",
    )(x, pre, wt, wt)


def _causal_conv(x, w, carry_in):
    width = w.shape[0]
    rows = x.shape[0]
    is_row0 = lax.broadcasted_iota(jnp.int32, (8, 1), 0) == 0
    acc = x * w[0:1, :]
    carry_out = []
    for j in range(1, width):
        delayed = pltpu.roll(acc, 1, axis=0)
        if carry_in is not None:
            carry_out.append(acc[rows - 1:rows, :])
            head = jnp.where(is_row0, carry_in[j - 1], delayed[:8])
            delayed = jnp.concatenate([head, delayed[8:]], axis=0)
        acc = x * w[j:j + 1, :] + delayed
    return acc, carry_out


def _load_carry(carry_ref, j, cols, n, seq_start):
    return [jnp.where(seq_start, 0.0, carry_ref[j, r:r + 1, cols]) for r in range(n)]


def _store_carry(carry_ref, j, cols, rows):
    for r, row in enumerate(rows):
        carry_ref[j, r:r + 1, cols] = row


def _proj_a_kernel(tiles_per_seq, x_ref, pre_ref, wgb_ref, wgcv_ref, whc_ref, wz_ref, caw_ref,
                   ya_ref, zs_ref, h_ref, tail_ref, carry_ref):
    i, j = pl.program_id(0), pl.program_id(1)
    tm = x_ref.shape[0]

    @pl.when(j == 0)
    def _():
        def norm_rows(rows):
            h_ref[rows, :] = _rms(x_ref[rows, :], pre_ref[...]).astype(BF16)
        _for_row_chunks(tm, norm_rows)

    @pl.when((i == 0) & (j == 0))
    def _():
        carry_ref[...] = jnp.zeros_like(carry_ref)

    seq_start = i % tiles_per_seq == 0
    h = h_ref[...]
    for c in range(TN_A // SUB_COLS):
        cols = slice(c * SUB_COLS, (c + 1) * SUB_COLS)
        u = _nt_dot(h, wgcv_ref[cols, :]) * _nt_dot(h, whc_ref[cols, :])
        tail_ref[0, :, cols] = u[tm - TAIL_ROWS:tm, :]
        conv, carry = _causal_conv(u, caw_ref[:, cols], _load_carry(carry_ref, j, cols, HIST_A, seq_start))
        _store_carry(carry_ref, j, cols, carry)
        ya_ref[:, cols] = (_nt_dot(h, wgb_ref[cols, :]) * conv).astype(BF16)
        zs_ref[:, cols] = _silu(_nt_dot(h, wz_ref[cols, :])).astype(BF16)


def _proj_a(x, pre, wt, caw, tm, tiles_per_seq):
    m = x.shape[0]
    n_col = D_CONV // TN_A
    assert DN_HEADS * DN_DV == D_CONV
    z_part = Z_BLOCK * TN_IN // D_CONV
    w_spec = lambda part: pl.BlockSpec((TN_A, D_MODEL), lambda i, j: (part * n_col + j, 0))
    cols = pl.BlockSpec((tm, TN_A), lambda i, j: (i, j))
    return pl.pallas_call(
        functools.partial(_proj_a_kernel, tiles_per_seq),
        grid=(m // tm, n_col),
        in_specs=[
            pl.BlockSpec((tm, D_MODEL), lambda i, j: (i, 0)),
            pl.BlockSpec((1, D_MODEL), lambda i, j: (0, 0)),
            w_spec(0), w_spec(1), w_spec(2), w_spec(z_part),
            pl.BlockSpec((3, TN_A), lambda i, j: (0, j)),
        ],
        out_specs=[
            cols, cols,
            pl.BlockSpec((tm, D_MODEL), lambda i, j: (i, 0)),
            pl.BlockSpec((1, TAIL_ROWS, TN_A), lambda i, j: (i, 0, j)),
        ],
        out_shape=[
            jax.ShapeDtypeStruct((m, D_CONV), BF16),
            jax.ShapeDtypeStruct((m, DN_HEADS * DN_DV), BF16),
            jax.ShapeDtypeStruct((m, D_MODEL), BF16),
            jax.ShapeDtypeStruct((m // tm, TAIL_ROWS, D_CONV), F32),
        ],
        scratch_shapes=[pltpu.VMEM((n_col, HIST_A, TN_A), F32)],
        compiler_params=_params("arbitrary", "arbitrary"),
        name="proj_a",
    )(x, pre, wt, wt, wt, wt, caw)


def _l2_normalize(x, scale):
    return x * (lax.rsqrt(jnp.sum(x * x, axis=-1, keepdims=True) + EPS) * scale)


def _gate_columns(beta_all, gc_all):
    heads = range(DN_HEADS)
    gc_t = gc_all.T
    return ([beta_all[:, DN_HEADS + h:DN_HEADS + h + 1] for h in heads],
            [gc_all[:, h:h + 1] for h in heads], [gc_t[h:h + 1, :] for h in heads])


def _heads_intra(qn, kn, vh, beta_col, gc_col, gc_row, causal, strict, eye, squarings):
    heads = range(len(qn))
    decay = [jnp.exp(jnp.where(causal, gc_col[h] - gc_row[h], -jnp.inf)) for h in heads]
    kb = [kn[h] * beta_col[h] for h in heads]
    gram = [_mm_nt(jnp.concatenate([kb[h], qn[h]], axis=0), kn[h]) for h in heads]
    a_mat = [jnp.where(strict, gram[h][:CHUNK] * decay[h], 0.0) for h in heads]
    qk = [jnp.where(causal, gram[h][CHUNK:] * decay[h], 0.0) for h in heads]
    inv = [eye - a for a in a_mat]
    power = a_mat
    for _ in range(squarings):
        power = [_mm(x, x) for x in power]
        inv = [_mm(inv[h], eye + power[h]) for h in heads]
    egc = [jnp.exp(x) for x in gc_col]
    wu = [_mm(inv[h], jnp.concatenate([kb[h] * egc[h], vh[h] * beta_col[h]], axis=1)) for h in heads]
    w = [x[:, :DN_DK] for x in wu]
    u = [x[:, DN_DK:] for x in wu]
    qdec = [qn[h] * egc[h] for h in heads]
    return w, u, qdec, qk


def _gates(ab, alog, dtb):
    x = ab + dtb
    softplus = jnp.maximum(x, 0.0) + jnp.log1p(jnp.exp(-jnp.abs(x)))
    return -jnp.exp(alog) * softplus, jax.nn.sigmoid(ab)


def _masks(same_seq):
    row = lax.broadcasted_iota(jnp.int32, (CHUNK, CHUNK), 0)
    col = lax.broadcasted_iota(jnp.int32, (CHUNK, CHUNK), 1)
    causal = row >= col
    strict = row > col
    if same_seq:
        same = (row >> SEQ_SHIFT) == (col >> SEQ_SHIFT)
        causal = causal & same
        strict = strict & same
    eye = jnp.where(row == col, 1.0, 0.0).astype(F32)
    return row, col, causal, strict, causal.astype(F32), eye


def _proj_qkv_kernel(tiles_per_seq, h_ref, wq_ref, wk_ref, wv_ref, wabt_ref, cqw_ref, alog_ref, dtb_ref,
                     w_ref, qd_ref, kd_ref, u_ref, qk_ref, egl_ref, tail_ref, carry_ref, qkv_ref):
    i = pl.program_id(0)
    tm = h_ref.shape[0]

    @pl.when(i == 0)
    def _():
        carry_ref[...] = jnp.zeros_like(carry_ref)

    seq_start = i % tiles_per_seq == 0
    h = h_ref[...]
    g_all, beta_all = _gates(_nt_dot(h, wabt_ref[...]), alog_ref[...], dtb_ref[...])
    for t, wt_ref in enumerate((wq_ref, wk_ref, wv_ref)):
        for c in range(TN_IN // SUB_COLS):
            cols = slice(c * SUB_COLS, (c + 1) * SUB_COLS)
            lo = t * TN_IN + c * SUB_COLS
            p = _nt_dot(h, wt_ref[cols, :])
            tail_ref[0, :, lo:lo + SUB_COLS] = p[tm - TAIL_ROWS:tm, :]
            p, carry = _causal_conv(p, cqw_ref[:, lo:lo + SUB_COLS],
                                    _load_carry(carry_ref, t, cols, HIST_QKV, seq_start))
            _store_carry(carry_ref, t, cols, carry)
            p = _silu(p)
            if t == 2:
                qkv_ref[:, lo:lo + SUB_COLS] = p
            else:
                for k in range(SUB_COLS // DN_DK):
                    blk = p[:, k * DN_DK:(k + 1) * DN_DK]
                    qkv_ref[:, lo + k * DN_DK:lo + (k + 1) * DN_DK] = _l2_normalize(
                        blk, DN_DK ** -0.5 if t == 0 else 1.0)

    heads = range(DN_HEADS)
    _, _, causal, strict, low_f, eye = _masks(False)
    chunk_rows = [slice(g * CHUNK, (g + 1) * CHUNK) for g in range(tm // CHUNK)]
    qn, kn, vh, beta_col, gc_col, gc_row = [], [], [], [], [], []
    for rows in chunk_rows:
        gc_all = _mm_exact(low_f, g_all[rows])
        egl_ref[rows, :] = jnp.broadcast_to(jnp.exp(gc_all[CHUNK - 1:CHUNK, :]), (CHUNK, AB_LANES))
        cols = _gate_columns(beta_all[rows], gc_all)
        beta_col += cols[0]
        gc_col += cols[1]
        gc_row += cols[2]
        qn += [qkv_ref[rows, h * DN_DK:(h + 1) * DN_DK] for h in heads]
        kn += [qkv_ref[rows, (DN_HEADS + h) * DN_DK:(DN_HEADS + h + 1) * DN_DK] for h in heads]
        vh += [qkv_ref[rows, (2 * DN_HEADS + h) * DN_DV:(2 * DN_HEADS + h + 1) * DN_DV] for h in heads]
    w, u, qdec, qk = _heads_intra(qn, kn, vh, beta_col, gc_col, gc_row, causal, strict, eye, 5)
    for n in range(len(w)):
        rows, hd = chunk_rows[n // DN_HEADS], n % DN_HEADS
        lo = hd * DN_DK
        g_last = gc_col[n][CHUNK - 1:CHUNK, :]
        w_ref[rows, lo:lo + DN_DK] = w[n].astype(BF16)
        qd_ref[rows, lo:lo + DN_DK] = qdec[n].astype(BF16)
        kd_ref[rows, lo:lo + DN_DK] = (kn[n] * jnp.exp(g_last - gc_col[n])).astype(BF16)
        u_ref[rows, lo:lo + DN_DV] = u[n]
        qk_ref[rows, hd * CHUNK:(hd + 1) * CHUNK] = qk[n].astype(BF16)


def _proj_qkv(h, wt, wabt, cqw, alog, dtb, tiles_per_seq):
    m = h.shape[0]
    hk, hv = DN_HEADS * DN_DK, DN_HEADS * DN_DV
    rows = lambda width: pl.BlockSpec((TM, width), lambda i: (i, 0))
    whole = lambda shape: pl.BlockSpec(shape, lambda i: (0,) * len(shape))
    w_tile = lambda t: pl.BlockSpec((TN_IN, D_MODEL), lambda i: (QKV_BLOCK + t, 0))
    return pl.pallas_call(
        functools.partial(_proj_qkv_kernel, tiles_per_seq),
        grid=(m // TM,),
        in_specs=[rows(D_MODEL), w_tile(0), w_tile(1), w_tile(2), whole((AB_LANES, D_MODEL)),
                  whole((4, DN_QKV)), whole((1, AB_LANES)), whole((1, AB_LANES))],
        out_specs=[rows(hk), rows(hk), rows(hk), rows(hv), rows(DN_HEADS * CHUNK), rows(AB_LANES),
                   pl.BlockSpec((1, TAIL_ROWS, DN_QKV), lambda i: (i, 0, 0))],
        out_shape=[
            jax.ShapeDtypeStruct((m, hk), BF16), jax.ShapeDtypeStruct((m, hk), BF16),
            jax.ShapeDtypeStruct((m, hk), BF16), jax.ShapeDtypeStruct((m, hv), F32),
            jax.ShapeDtypeStruct((m, DN_HEADS * CHUNK), BF16), jax.ShapeDtypeStruct((m, AB_LANES), F32),
            jax.ShapeDtypeStruct((m // TM, TAIL_ROWS, DN_QKV), F32),
        ],
        scratch_shapes=[pltpu.VMEM((3, HIST_QKV, TN_IN), F32), pltpu.VMEM((TM, DN_QKV), F32)],
        compiler_params=_params("arbitrary"),
        name="proj_qkv",
    )(h, wt, wt, wt, wabt, cqw, alog, dtb)


def _delta_scan_kernel(w_ref, qd_ref, kd_ref, u_ref, qk_ref, egl_ref, zs_ref, dnn_ref,
                       y_ref, so_ref, s_ref):
    c = pl.program_id(0)

    @pl.when(c == 0)
    def _():
        s_ref[...] = jnp.zeros_like(s_ref)

    n_seq = w_ref.shape[0]
    units = [(b, h) for b in range(n_seq) for h in range(DN_HEADS)]
    col = [slice(h * DN_DK, (h + 1) * DN_DK) for h in range(DN_HEADS)]
    dnn = dnn_ref[...]
    s = [s_ref[b, h] for b, h in units]
    for g in range(w_ref.shape[1] // CHUNK):
        rows = slice(g * CHUNK, (g + 1) * CHUNK)
        lhs = [jnp.concatenate([w_ref[b, rows, col[h]], qd_ref[b, rows, col[h]]], axis=0) for b, h in units]
        r = [jnp.dot(lhs[n], s[n].astype(BF16), preferred_element_type=F32) for n in range(len(units))]
        v_new = [(u_ref[b, rows, col[h]] - r[n][:CHUNK]).astype(BF16) for n, (b, h) in enumerate(units)]
        upd = [lax.dot_general(kd_ref[b, rows, col[h]], v_new[n], (((0,), (0,)), ((), ())),
                               preferred_element_type=F32) for n, (b, h) in enumerate(units)]
        s = [s[n] * egl_ref[b, g * CHUNK:g * CHUNK + 1, h:h + 1] + upd[n] for n, (b, h) in enumerate(units)]
        o = [r[n][CHUNK:] + jnp.dot(qk_ref[b, rows, h * CHUNK:(h + 1) * CHUNK], v_new[n],
                                    preferred_element_type=F32) for n, (b, h) in enumerate(units)]
        for n, (b, h) in enumerate(units):
            y_ref[b, rows, col[h]] = (_rms(o[n], dnn) * zs_ref[b, rows, col[h]]).astype(y_ref.dtype)
    for n, (b, h) in enumerate(units):
        s_ref[b, h] = s[n]

    @pl.when(c == pl.num_programs(0) - 1)
    def _():
        so_ref[...] = s_ref[...]


def _delta_scan(w, qd, kd, u, qk, egl, zs, dnn, n_seq, seq_len):
    hk, hv = DN_HEADS * DN_DK, DN_HEADS * DN_DV
    per_seq = lambda a: a.reshape(n_seq, seq_len, a.shape[-1])
    rows = lambda width: pl.BlockSpec((n_seq, SCAN_ROWS, width), lambda c: (0, c, 0))
    state = pl.BlockSpec((n_seq, DN_HEADS, DN_DK, DN_DV), lambda c: (0, 0, 0, 0))
    y, s = pl.pallas_call(
        _delta_scan_kernel,
        grid=(seq_len // SCAN_ROWS,),
        in_specs=[rows(hk), rows(hk), rows(hk), rows(hv), rows(DN_HEADS * CHUNK), rows(AB_LANES), rows(hv),
                  pl.BlockSpec((1, DN_DV), lambda c: (0, 0))],
        out_specs=[rows(hv), state],
        out_shape=[jax.ShapeDtypeStruct((n_seq, seq_len, hv), BF16),
                   jax.ShapeDtypeStruct((n_seq, DN_HEADS, DN_DK, DN_DV), F32)],
        scratch_shapes=[pltpu.VMEM((n_seq, DN_HEADS, DN_DK, DN_DV), F32)],
        compiler_params=_params("arbitrary"),
        name="delta_scan",
    )(*map(per_seq, (w, qd, kd, u, qk, egl, zs)), dnn)
    return y.reshape(n_seq * seq_len, hv), s


def _mix_sample_kernel(pa_ref, pq_ref, pz_ref, pab_ref, sa_ref, sq_ref, si_ref,
                       caw_ref, cqw_ref, alog_ref, dtb_ref, dnn_ref,
                       y_ref, ca_ref, cq_ref, so_ref,
                       gb_ref, ua_ref, xq_ref, z_ref, ab_ref, ybuf_ref, si_buf, sem):
    step = pl.program_id(0)

    def state_copy(t):
        start = t * SEQ_PER_STEP if isinstance(t, int) else pl.multiple_of(t * SEQ_PER_STEP, SEQ_PER_STEP)
        slot = t % RING
        return pltpu.make_async_copy(si_ref.at[pl.ds(start, SEQ_PER_STEP)], si_buf.at[slot], sem.at[slot])

    @pl.when(step == 0)
    def _():
        for t in range(RING - 1):
            state_copy(t).start()

    @pl.when(step + (RING - 1) < pl.num_programs(0))
    def _():
        state_copy(step + (RING - 1)).start()

    seqs = range(SEQ_PER_STEP)
    pad = SEQ_PAD - SAMPLE_LEN
    for b in seqs:
        r0 = SEQ_PAD * b
        nxt = (b + 1) % SEQ_PER_STEP
        src = slice(SAMPLE_LEN * b, SAMPLE_LEN * (b + 1))
        pa = pa_ref[src, :]
        gb_ref[r0:r0 + SAMPLE_LEN, :] = pa[:, :D_CONV]
        ua_ref[r0:r0 + SAMPLE_LEN, :] = pa[:, D_CONV:2 * D_CONV] * pa[:, 2 * D_CONV:]
        xq_ref[r0:r0 + SAMPLE_LEN, :] = pq_ref[src, :]
        z_ref[r0:r0 + SAMPLE_LEN, :] = pz_ref[src, :]
        ab_ref[r0:r0 + SAMPLE_LEN, :] = pab_ref[src, :]
        for ref in (gb_ref, ua_ref, xq_ref, z_ref, ab_ref):
            ref[r0 + SAMPLE_LEN:r0 + SEQ_PAD, :] = jnp.zeros((pad, ref.shape[1]), F32)
        ua_ref[r0 + SEQ_PAD - HIST_A:r0 + SEQ_PAD, :] = sa_ref[nxt]
        for r in range(HIST_QKV):
            dst = r0 + SEQ_PAD - HIST_QKV + r
            xq_ref[dst:dst + 1, :] = sq_ref[r, nxt:nxt + 1, :]
    ybuf_ref[:, :D_CONV] = gb_ref[...] * _causal_conv(ua_ref[...], caw_ref[...], None)[0]
    row, col, causal, strict, low_f, eye = _masks(True)
    valid = (row[:, 0:1] & (SEQ_PAD - 1)) < SAMPLE_LEN
    cqkv = jnp.where(valid, _silu(_causal_conv(xq_ref[...], cqw_ref[...], None)[0]), 0.0)
    for b in seqs:
        r0 = SEQ_PAD * b
        ca_ref[b] = ua_ref[r0 + SAMPLE_LEN - HIST_A:r0 + SAMPLE_LEN, :]
        for r in range(HIST_QKV):
            src = r0 + SAMPLE_LEN - HIST_QKV + r
            cq_ref[r, b:b + 1, :] = xq_ref[src:src + 1, :]

    g_all, beta_all = _gates(ab_ref[...], alog_ref[...], dtb_ref[...])
    g_all = jnp.where(valid, g_all, 0.0)
    beta_all = jnp.where(valid, beta_all, 0.0)
    gc_all = _mm_exact(low_f, g_all)
    last_sel = (col == (row | (SEQ_PAD - 1))).astype(F32)
    gl_all = _mm_exact(last_sel, gc_all)
    z = z_ref[...]
    dnn = dnn_ref[...]
    heads = range(DN_HEADS)
    qn = [_l2_normalize(cqkv[:, h * DN_DK:(h + 1) * DN_DK], DN_DK ** -0.5) for h in heads]
    kn = [_l2_normalize(cqkv[:, (DN_HEADS + h) * DN_DK:(DN_HEADS + h + 1) * DN_DK], 1.0) for h in heads]
    vh = [cqkv[:, (2 * DN_HEADS + h) * DN_DV:(2 * DN_HEADS + h + 1) * DN_DV] for h in heads]
    beta_col, gc_col, gc_row = _gate_columns(beta_all, gc_all)
    w, u, qdec, qk = _heads_intra(qn, kn, vh, beta_col, gc_col, gc_row, causal, strict, eye, 1)
    gl_col = [gl_all[:, h:h + 1] for h in heads]
    kdec = [kn[h] * jnp.exp(gl_col[h] - gc_col[h]) for h in heads]
    s_scale = [jnp.exp(x) for x in gl_col]
    rows = [slice(SEQ_PAD * b, SEQ_PAD * (b + 1)) for b in seqs]
    pairs = [(h, b) for h in heads for b in seqs]
    state_copy(step).wait()
    slot = step % RING
    s_old = [si_buf[slot, b, h] for h, b in pairs]
    r = [_mm(jnp.concatenate([w[h][rows[b]], qdec[h][rows[b]]], axis=0), s_old[n])
         for n, (h, b) in enumerate(pairs)]
    v_new = [u[h][rows[b]] - r[n][:SEQ_PAD] for n, (h, b) in enumerate(pairs)]
    upd = [_mm_tn(kdec[h][rows[b]], v_new[n]) for n, (h, b) in enumerate(pairs)]
    for n, (h, b) in enumerate(pairs):
        so_ref[b, h] = s_old[n] * s_scale[h][SEQ_PAD * b:SEQ_PAD * b + 1, :] + upd[n]
    for h in heads:
        mine = range(h * SEQ_PER_STEP, (h + 1) * SEQ_PER_STEP)
        o = (jnp.concatenate([r[n][SEQ_PAD:] for n in mine], axis=0)
             + _mm(qk[h], jnp.concatenate([v_new[n] for n in mine], axis=0)))
        lo = h * DN_DV
        ybuf_ref[:, D_CONV + lo:D_CONV + lo + DN_DV] = _rms(o, dnn) * _silu(z[:, lo:lo + DN_DV])
    for b in seqs:
        y_ref[SAMPLE_LEN * b:SAMPLE_LEN * (b + 1), :] = ybuf_ref[SEQ_PAD * b:SEQ_PAD * b + SAMPLE_LEN, :]


def _mix_sample(proj, ab, state_a, state_q, state_s, caw, cqw, alog, dtb, dnn):
    n_seq = state_s.shape[0]
    assert n_seq // SEQ_PER_STEP >= RING - 1
    full2 = lambda i: (0, 0)
    seq_rows = lambda width, col: pl.BlockSpec((SEQ_PER_STEP * SAMPLE_LEN, width), lambda i: (i, col))
    return pl.pallas_call(
        _mix_sample_kernel,
        grid=(n_seq // SEQ_PER_STEP,),
        in_specs=[
            seq_rows(3 * D_CONV, 0), seq_rows(DN_QKV, 1), seq_rows(DN_HEADS * DN_DV, Z_BLOCK), seq_rows(AB_LANES, 0),
            pl.BlockSpec((SEQ_PER_STEP, 2, D_CONV), lambda i: (i, 0, 0)),
            pl.BlockSpec((3, SEQ_PER_STEP, DN_QKV), lambda i: (0, i, 0)),
            pl.BlockSpec(memory_space=pl.ANY),
            pl.BlockSpec((3, D_CONV), full2),
            pl.BlockSpec((4, DN_QKV), full2),
            pl.BlockSpec((1, AB_LANES), full2),
            pl.BlockSpec((1, AB_LANES), full2),
            pl.BlockSpec((1, DN_DV), full2),
        ],
        out_specs=[
            seq_rows(D_MODEL, 0),
            pl.BlockSpec((SEQ_PER_STEP, 2, D_CONV), lambda i: (i, 0, 0)),
            pl.BlockSpec((3, SEQ_PER_STEP, DN_QKV), lambda i: (0, i, 0)),
            pl.BlockSpec((SEQ_PER_STEP, DN_HEADS, DN_DK, DN_DV), lambda i: (i, 0, 0, 0)),
        ],
        out_shape=[
            jax.ShapeDtypeStruct((n_seq * SAMPLE_LEN, D_MODEL), F32),
            jax.ShapeDtypeStruct((n_seq, 2, D_CONV), F32),
            jax.ShapeDtypeStruct((3, n_seq, DN_QKV), F32),
            jax.ShapeDtypeStruct((n_seq, DN_HEADS, DN_DK, DN_DV), F32),
        ],
        scratch_shapes=[
            pltpu.VMEM((CHUNK, D_CONV), F32),
            pltpu.VMEM((CHUNK, D_CONV), F32),
            pltpu.VMEM((CHUNK, DN_QKV), F32),
            pltpu.VMEM((CHUNK, DN_HEADS * DN_DV), F32),
            pltpu.VMEM((CHUNK, AB_LANES), F32),
            pltpu.VMEM((CHUNK, D_MODEL), F32),
            pltpu.VMEM((RING, SEQ_PER_STEP, DN_HEADS, DN_DK, DN_DV), F32),
            pltpu.SemaphoreType.DMA((RING,)),
        ],
        compiler_params=_params("arbitrary"),
        name="mix_sample",
    )(proj, proj, proj, ab, state_a, state_q, state_s, caw, cqw, alog, dtb, dnn)


def _outproj_kernel(ya_ref, yb_ref, wa_ref, wb_ref, post_ref, x_ref, o_ref):
    m = (jnp.dot(ya_ref[...].astype(BF16), wa_ref[...], preferred_element_type=F32)
         + jnp.dot(yb_ref[...].astype(BF16), wb_ref[...], preferred_element_type=F32))
    o_ref[...] = x_ref[...] + _rms(m, post_ref[...])


def _outproj(ya, yb, yb_block, w, post, x):
    m = x.shape[0]
    half = D_MODEL // 2
    return pl.pallas_call(
        _outproj_kernel,
        grid=(m // TM,),
        in_specs=[
            pl.BlockSpec((TM, half), lambda i: (i, 0)),
            pl.BlockSpec((TM, half), lambda i: (i, yb_block)),
            pl.BlockSpec((half, D_MODEL), lambda i: (0, 0)),
            pl.BlockSpec((half, D_MODEL), lambda i: (1, 0)),
            pl.BlockSpec((1, D_MODEL), lambda i: (0, 0)),
            pl.BlockSpec((TM, D_MODEL), lambda i: (i, 0)),
        ],
        out_specs=pl.BlockSpec((TM, D_MODEL), lambda i: (i, 0)),
        out_shape=jax.ShapeDtypeStruct((m, D_MODEL), F32),
        compiler_params=_params("parallel"),
        name="outproj",
    )(ya, yb, w, w, post, x)


def _ple_kernel(x_ref, p_ref, pre_ref, post_ref, wg_ref, wp_ref, o_ref):
    x = x_ref[...]
    h = _rms(x, pre_ref[...]).astype(BF16)
    gate = jax.nn.sigmoid(jnp.dot(h, wg_ref[...], preferred_element_type=F32))
    proj = jnp.dot(p_ref[...].astype(BF16), wp_ref[...], preferred_element_type=F32)
    o_ref[...] = x + _rms(gate * proj, post_ref[...])


def _ple(x, p, pre, post, wg, wp):
    m = x.shape[0]
    return pl.pallas_call(
        _ple_kernel,
        grid=(m // TM,),
        in_specs=[
            pl.BlockSpec((TM, D_MODEL), lambda i: (i, 0)),
            pl.BlockSpec((TM, D_PLE), lambda i: (i, 0)),
            pl.BlockSpec((1, D_MODEL), lambda i: (0, 0)),
            pl.BlockSpec((1, D_MODEL), lambda i: (0, 0)),
            pl.BlockSpec((D_MODEL, D_MODEL), lambda i: (0, 0)),
            pl.BlockSpec((D_PLE, D_MODEL), lambda i: (0, 0)),
        ],
        out_specs=pl.BlockSpec((TM, D_MODEL), lambda i: (i, 0)),
        out_shape=jax.ShapeDtypeStruct((m, D_MODEL), F32),
        compiler_params=_params("parallel"),
        name="ple",
    )(x, p, pre, post, wg, wp)


def _row(v):
    return v.reshape(1, -1).astype(F32)


def _pad_lanes(v):
    return jnp.pad(_row(v), ((0, 0), (0, AB_LANES - v.shape[-1])))


def _seq_tails(tails, tiles_per_seq, n_rows):
    return tails[tiles_per_seq - 1::tiles_per_seq, TAIL_ROWS - n_rows:, :]


def kernel(x_prompt, x_sample, state_conv_a, state_conv_qkv, state_delta, p_prompt, p_sample,
           f1_pre, f1_post, f1_wg, f1_wu, f1_wd,
           mix_pre, mix_post, w_in, conv_a_w, conv_qkv_w, a_log, dt_bias, dn_norm, w_out,
           f2_pre, f2_post, f2_wg, f2_wu, f2_wd,
           ple_pre, ple_post, w_ple_gate, w_ple_proj):
    depth = f1_pre.shape[0]
    n_p, seq_p, _ = x_prompt.shape
    n_s, seq_s, _ = x_sample.shape
    rows_p = n_p * seq_p
    rows_s = n_s * seq_s
    assert seq_p % TM_BIG == 0 and seq_s == SAMPLE_LEN and n_s % SEQ_PER_STEP == 0
    assert rows_s == TM
    assert w_in.shape[-1] == IN_MAIN + N_AB
    tiles_per_seq = seq_p // TM_BIG
    xp = x_prompt.reshape(rows_p, D_MODEL)
    xs = x_sample.reshape(rows_s, D_MODEL)
    outs = [[] for _ in range(6)]
    for i in range(depth):
        pre, post = _row(f1_pre[i]), _row(f1_post[i])
        xs, wg, wu, wd = _ffn(xs, pre, post, f1_wg[i], f1_wu[i], f1_wd[i], TM)
        xp = _ffn(xp, pre, post, wg, wu, wd, TM_BIG)[0]

        pre = _row(mix_pre[i])
        caw, cqw = conv_a_w[i], conv_qkv_w[i]
        proj_s, ab_s, wt, wabt = _inproj_cast(xs, pre, w_in[i].T)
        y_a, zs, h, tail_a = _proj_a(xp, pre, wt, caw, TM_BIG, tiles_per_seq)
        alog, dtb, dnn = _pad_lanes(a_log[i]), _pad_lanes(dt_bias[i]), _row(dn_norm[i])
        wy_w, qdec, kdec, wy_u, qk, egl, tail_q = _proj_qkv(h, wt, wabt, cqw, alog, dtb, seq_p // TM)
        y_b, s_p = _delta_scan(wy_w, qdec, kdec, wy_u, qk, egl, zs, dnn, n_p, seq_p)
        ca_p = _seq_tails(tail_a, tiles_per_seq, HIST_A)
        cq_p = _seq_tails(tail_q, seq_p // TM, HIST_QKV)
        y_s, ca_s, cq_s, s_s = _mix_sample(
            proj_s, ab_s, state_conv_a[i], jnp.swapaxes(state_conv_qkv[i], 0, 1), state_delta[i],
            caw, cqw, alog, dtb, dnn)
        cq_s = jnp.swapaxes(cq_s, 0, 1)

        post = _row(mix_post[i])
        w_out_bf16 = w_out[i].astype(BF16)
        xs = _outproj(y_s, y_s, 1, w_out_bf16, post, xs)
        xp = _outproj(y_a, y_b, 0, w_out_bf16, post, xp)

        pre, post = _row(f2_pre[i]), _row(f2_post[i])
        xs, wg, wu, wd = _ffn(xs, pre, post, f2_wg[i], f2_wu[i], f2_wd[i], TM)
        xp = _ffn(xp, pre, post, wg, wu, wd, TM_BIG)[0]

        pre, post = _row(ple_pre[i]), _row(ple_post[i])
        wg, wp = w_ple_gate[i].astype(BF16), w_ple_proj[i].astype(BF16)
        xs = _ple(xs, p_sample[i].reshape(rows_s, D_PLE), pre, post, wg, wp)
        xp = _ple(xp, p_prompt[i].reshape(rows_p, D_PLE), pre, post, wg, wp)
        for lst, val in zip(outs, (ca_p, cq_p, s_p, ca_s, cq_s, s_s)):
            lst.append(val)
    return (xp.reshape(n_p, seq_p, D_MODEL), xs.reshape(n_s, seq_s, D_MODEL),
            *[jnp.stack(lst) for lst in outs])
```
